```python
import jax, jax.numpy as jnp
from jax import lax
import numpy as np

D_MODEL = 1024
BATCH = 4
SEQ = 4096
DEPTH = 2
DEC_BATCH = 128
DEC_SEQ = 4
PAST_LEN = 16384
PAGE_SIZE = 128

HEAD_DIM = 64
D_CONV = 512
CONV_W = 3
SWA_HEADS = 8
SWA_KV_HEADS = 2
SWA_GROUP = SWA_HEADS // SWA_KV_HEADS
SWA_WINDOW = 128
DIL_PAIRS = ((128, 1), (512, 4), (2048, 16))
DIL_HEADS = 8
BLOCK = 128
D_FF = 2816
N_EXPERTS = 8
TOP_K = 2
D_EXPERT = 3584
N_EVEN = (DEPTH + 1) // 2
N_ODD = DEPTH // 2
EPS = 1e-5
IN0 = 3 * D_CONV + (SWA_HEADS + 2 * SWA_KV_HEADS) * HEAD_DIM
IN1 = len(DIL_PAIRS) * 3 * DIL_HEADS * HEAD_DIM
OUT0 = D_CONV + SWA_HEADS * HEAD_DIM
OUT1 = DIL_HEADS * HEAD_DIM

kernel_name = 'hybrid_shortconv_swa_dilated_moe_step'


def _rmsnorm(x, g):
    xf = x.astype(jnp.float32)
    y = xf * lax.rsqrt(jnp.mean(xf * xf, axis=-1, keepdims=True) + EPS)
    return (y * g.astype(jnp.float32)).astype(x.dtype)


def _swiglu(h, w_gate, w_up, w_down):
    return (jax.nn.silu(h @ w_gate) * (h @ w_up)) @ w_down


def _masked_softmax(scores, mask, sink):
    s = jnp.where(mask, scores, -jnp.inf)
    m = jnp.max(s, axis=-1, keepdims=True)
    if sink is not None:
        sk = sink.astype(jnp.float32)[:, :, None, None]
        m = jnp.maximum(m, sk)
    p = jnp.exp(s - m)
    denom = jnp.sum(p, axis=-1, keepdims=True)
    if sink is not None:
        denom = denom + jnp.exp(sk - m)
    return p / denom, (m + jnp.log(denom))[..., 0]


def _banded_attention(q, k, v, span, sink):
    n, t = q.shape[0], q.shape[1]
    nb = -(-t // BLOCK)
    pad = nb * BLOCK - t

    def blocks(a):
        a = jnp.pad(a, [(0, 0), (0, pad)] + [(0, 0)] * (a.ndim - 2))
        return a.reshape((n, nb, BLOCK) + a.shape[2:])

    def with_prev(a):
        prev = jnp.concatenate([jnp.zeros_like(a[:, :1]), a[:, :-1]], axis=1)
        return jnp.concatenate([prev, a], axis=2)

    qb = blocks(q)
    kw = with_prev(blocks(k))
    vw = with_prev(blocks(v))
    scores = jnp.einsum('nbqhgd,nbkhd->nbhgqk', qb, kw,
                        preferred_element_type=jnp.float32) * (HEAD_DIM ** -0.5)
    qi = np.arange(BLOCK)[:, None]
    kj = np.arange(2 * BLOCK)[None, :]
    dist = BLOCK + qi - kj
    k_pos = (np.arange(nb)[:, None, None] - 1) * BLOCK + kj[None]
    mask = (dist >= 0) & (dist <= span) & (k_pos >= 0)
    p, lse = _masked_softmax(scores, mask[None, :, None, None], sink)
    out = jnp.einsum('nbhgqk,nbkhd->nbqhgd', p, vw, preferred_element_type=jnp.float32)
    out = out.reshape((n, nb * BLOCK) + q.shape[2:])[:, :t]
    lse = lse.transpose(0, 1, 4, 2, 3).reshape((n, nb * BLOCK) + q.shape[2:4])[:, :t]
    return out, lse


def _window_step_attention(q, k_new, v_new, k_buf, v_buf, dilation, span, sink):
    s_len, buf_len = q.shape[1], k_buf.shape[1]
    kk = jnp.concatenate([k_buf.astype(k_new.dtype), k_new], axis=1)
    vv = jnp.concatenate([v_buf.astype(v_new.dtype), v_new], axis=1)
    idx = buf_len + np.arange(s_len)[:, None] - dilation * (span - np.arange(span + 1))[None, :]
    valid = idx >= 0
    idx = np.maximum(idx, 0)
    kg = kk[:, idx]
    vg = vv[:, idx]
    scores = jnp.einsum('nqhgd,nqkhd->nhgqk', q, kg,
                        preferred_element_type=jnp.float32) * (HEAD_DIM ** -0.5)
    p, lse = _masked_softmax(scores, valid, sink)
    out = jnp.einsum('nhgqk,nqkhd->nqhgd', p, vg, preferred_element_type=jnp.float32)
    return out, lse.transpose(0, 3, 1, 2)


def _causal_conv(u, prev, w):
    t = u.shape[1]
    up = jnp.concatenate([prev.astype(u.dtype), u], axis=1)
    y = w[0] * up[:, 0:t]
    for tap in range(1, CONV_W):
        y = y + w[tap] * up[:, tap:tap + t]
    return y, up[:, t:]


def _even_mixer(h, w_in, conv_w, sink, w_out, conv_prev, kv_buf):
    n, t, _ = h.shape
    proj = jnp.einsum('ntd,de->nte', h, w_in)
    cuts = [D_CONV, 2 * D_CONV, 3 * D_CONV, 3 * D_CONV + SWA_HEADS * HEAD_DIM,
            3 * D_CONV + (SWA_HEADS + SWA_KV_HEADS) * HEAD_DIM]
    gate_b, gate_c, xa, q, k, v = jnp.split(proj, cuts, axis=-1)
    conv_out, conv_state = _causal_conv(gate_c * xa, conv_prev, conv_w)
    a_out = gate_b * conv_out
    q = q.reshape(n, t, SWA_KV_HEADS, SWA_GROUP, HEAD_DIM)
    k = k.reshape(n, t, SWA_KV_HEADS, HEAD_DIM)
    v = v.reshape(n, t, SWA_KV_HEADS, HEAD_DIM)
    kv_new = jnp.stack([k, v], axis=2)
    if kv_buf is None:
        att, _ = _banded_attention(q, k, v, SWA_WINDOW, sink)
        kv_state = kv_new[:, -min(SWA_WINDOW, t):]
    else:
        att, _ = _window_step_attention(q, k, v, kv_buf[:, :, 0], kv_buf[:, :, 1], 1, SWA_WINDOW, sink)
        kv_state = jnp.concatenate([kv_buf.astype(kv_new.dtype), kv_new], axis=1)[:, -kv_buf.shape[1]:]
    mixed = jnp.concatenate([a_out, att.reshape(n, t, SWA_HEADS * HEAD_DIM).astype(h.dtype)], axis=-1)
    return jnp.einsum('nte,ed->ntd', mixed, w_out), conv_state, kv_state


def _odd_mixer(h, w_in, w_out, kv_bufs):
    n, t, _ = h.shape
    proj = jnp.einsum('ntd,de->nte', h, w_in).reshape(n, t, len(DIL_PAIRS), 3, DIL_HEADS, HEAD_DIM)
    outs, lses, states = [], [], []
    for g, (window, dil) in enumerate(DIL_PAIRS):
        span = window // dil
        q = proj[:, :, g, 0][:, :, :, None]
        k = proj[:, :, g, 1]
        v = proj[:, :, g, 2]
        kv_new = jnp.stack([k, v], axis=2)
        if kv_bufs is None:
            tc = t // dil

            def to_res(a):
                return a.reshape((n, tc, dil) + a.shape[2:]).swapaxes(1, 2).reshape((n * dil, tc) + a.shape[2:])

            def from_res(a):
                return a.reshape((n, dil, tc) + a.shape[2:]).swapaxes(1, 2).reshape((n, t) + a.shape[2:])

            o, l = _banded_attention(to_res(q), to_res(k), to_res(v), span, None)
            o, l = from_res(o), from_res(l)
            states.append(kv_new[:, -min(window, t):])
        else:
            buf = kv_bufs[g]
            o, l = _window_step_attention(q, k, v, buf[:, :, 0], buf[:, :, 1], dil, span, None)
            states.append(jnp.concatenate([buf.astype(kv_new.dtype), kv_new], axis=1)[:, -buf.shape[1]:])
        outs.append(o)
        lses.append(l)
    wts = jax.nn.softmax(jnp.stack(lses, axis=0), axis=0)
    comb = jnp.sum(wts[..., None] * jnp.stack(outs, axis=0), axis=0)
    y = jnp.einsum('nte,ed->ntd', comb.reshape(n, t, OUT1).astype(h.dtype), w_out)
    return y, states


def _moe(h, w_router, w_gate, w_up, w_down):
    logits = jnp.einsum('ntd,de->nte', h, w_router, preferred_element_type=jnp.float32)
    top_val, top_idx = lax.top_k(logits, TOP_K)
    gates = jax.nn.softmax(top_val, axis=-1)
    comb = jnp.sum(jax.nn.one_hot(top_idx, N_EXPERTS, dtype=jnp.float32) * gates[..., None], axis=-2).astype(h.dtype)
    y = jnp.zeros_like(h)
    for e in range(N_EXPERTS):
        y = y + comb[..., e:e + 1] * _swiglu(h, w_gate[e], w_up[e], w_down[e])
    return y


def setup_inputs(seed: int = 0) -> dict:
    key = jax.random.key(seed)
    ks = iter(jax.random.split(key, 32))

    def nrm(shape, scale=1.0):
        return jax.random.normal(next(ks), shape, jnp.float32) * scale

    def gain(shape):
        return 1.0 + nrm(shape, 0.02)

    d = D_MODEL
    swa_buf = min(SWA_WINDOW, PAST_LEN)
    return {
        'x_prompt': nrm((BATCH, SEQ, d)),
        'x_sample': nrm((DEC_BATCH, DEC_SEQ, d)),
        'cache_conv': nrm((N_EVEN, DEC_BATCH, CONV_W - 1, D_CONV)),
        'cache_swa_kv': nrm((N_EVEN, DEC_BATCH, swa_buf, 2, SWA_KV_HEADS, HEAD_DIM)),
        'cache_dil_kv0': nrm((N_ODD, DEC_BATCH, min(DIL_PAIRS[0][0], PAST_LEN), 2, DIL_HEADS, HEAD_DIM)),
        'cache_dil_kv1': nrm((N_ODD, DEC_BATCH, min(DIL_PAIRS[1][0], PAST_LEN), 2, DIL_HEADS, HEAD_DIM)),
        'cache_dil_kv2': nrm((N_ODD, DEC_BATCH, min(DIL_PAIRS[2][0], PAST_LEN), 2, DIL_HEADS, HEAD_DIM)),
        'norm_mix0': gain((N_EVEN, d)),
        'w_in0': nrm((N_EVEN, d, IN0), d ** -0.5),
        'conv_w': nrm((N_EVEN, CONV_W, D_CONV), CONV_W ** -0.5),
        'swa_sink': nrm((N_EVEN, SWA_KV_HEADS, SWA_GROUP)),
        'w_out0': nrm((N_EVEN, OUT0, d), OUT0 ** -0.5),
        'norm_ffn0': gain((N_EVEN, d)),
        'w_gate0': nrm((N_EVEN, d, D_FF), d ** -0.5),
        'w_up0': nrm((N_EVEN, d, D_FF), d ** -0.5),
        'w_down0': nrm((N_EVEN, D_FF, d), D_FF ** -0.5),
        'norm_mix1': gain((N_ODD, d)),
        'w_in1': nrm((N_ODD, d, IN1), d ** -0.5),
        'w_out1': nrm((N_ODD, OUT1, d), OUT1 ** -0.5),
        'norm_ffn1': gain((N_ODD, d)),
        'w_router': nrm((N_ODD, d, N_EXPERTS), d ** -0.5),
        'w_gate1': nrm((N_ODD, N_EXPERTS, d, D_EXPERT), d ** -0.5),
        'w_up1': nrm((N_ODD, N_EXPERTS, d, D_EXPERT), d ** -0.5),
        'w_down1': nrm((N_ODD, N_EXPERTS, D_EXPERT, d), D_EXPERT ** -0.5),
        'norm_final': gain((d,)),
    }


def reference(x_prompt, x_sample, cache_conv, cache_swa_kv, cache_dil_kv0, cache_dil_kv1, cache_dil_kv2,
              norm_mix0, w_in0, conv_w, swa_sink, w_out0, norm_ffn0, w_gate0, w_up0, w_down0,
              norm_mix1, w_in1, w_out1, norm_ffn1, w_router, w_gate1, w_up1, w_down1, norm_final):
    hp, hs = x_prompt, x_sample
    conv_p, conv_s, swa_p, swa_s = [], [], [], []
    dil_p, dil_s = ([], [], []), ([], [], [])
    dil_caches = (cache_dil_kv0, cache_dil_kv1, cache_dil_kv2)
    for layer in range(DEPTH):
        i = layer // 2
        if layer % 2 == 0:
            zero_conv = jnp.zeros((hp.shape[0], CONV_W - 1, D_CONV), hp.dtype)
            mp, cp, kvp = _even_mixer(_rmsnorm(hp, norm_mix0[i]), w_in0[i], conv_w[i], swa_sink[i], w_out0[i],
                                      zero_conv, None)
            ms, cs, kvs = _even_mixer(_rmsnorm(hs, norm_mix0[i]), w_in0[i], conv_w[i], swa_sink[i], w_out0[i],
                                      cache_conv[i], cache_swa_kv[i])
            hp = hp + mp
            hs = hs + ms
            conv_p.append(cp)
            conv_s.append(cs)
            swa_p.append(kvp)
            swa_s.append(kvs)
            hp = hp + _swiglu(_rmsnorm(hp, norm_ffn0[i]), w_gate0[i], w_up0[i], w_down0[i])
            hs = hs + _swiglu(_rmsnorm(hs, norm_ffn0[i]), w_gate0[i], w_up0[i], w_down0[i])
        else:
            mp, sp = _odd_mixer(_rmsnorm(hp, norm_mix1[i]), w_in1[i], w_out1[i], None)
            ms, ss = _odd_mixer(_rmsnorm(hs, norm_mix1[i]), w_in1[i], w_out1[i],
                                (dil_caches[0][i], dil_caches[1][i], dil_caches[2][i]))
            hp = hp + mp
            hs = hs + ms
            for g in range(len(DIL_PAIRS)):
                dil_p[g].append(sp[g])
                dil_s[g].append(ss[g])
            hp = hp + _moe(_rmsnorm(hp, norm_ffn1[i]), w_router[i], w_gate1[i], w_up1[i], w_down1[i])
            hs = hs + _moe(_rmsnorm(hs, norm_ffn1[i]), w_router[i], w_gate1[i], w_up1[i], w_down1[i])
    y_prompt = _rmsnorm(hp, norm_final)
    y_sample = _rmsnorm(hs, norm_final)
    new_conv_prompt = jnp.stack(conv_p)
    new_conv_sample = jnp.stack(conv_s)
    new_swa_kv_prompt = jnp.stack(swa_p)
    new_swa_kv_sample = jnp.stack(swa_s)
    new_dil_kv0_prompt = jnp.stack(dil_p[0])
    new_dil_kv0_sample = jnp.stack(dil_s[0])
    new_dil_kv1_prompt = jnp.stack(dil_p[1])
    new_dil_kv1_sample = jnp.stack(dil_s[1])
    new_dil_kv2_prompt = jnp.stack(dil_p[2])
    new_dil_kv2_sample = jnp.stack(dil_s[2])
    return (y_prompt, y_sample, new_conv_prompt, new_conv_sample, new_swa_kv_prompt, new_swa_kv_sample,
            new_dil_kv0_prompt, new_dil_kv0_sample, new_dil_kv1_prompt, new_dil_kv1_sample,
            new_dil_kv2_prompt, new_dil_kv2_sample)
```

```python
import functools

import jax
import jax.numpy as jnp
from jax import lax
from jax.experimental import pallas as pl
from jax.experimental.pallas import tpu as pltpu

F32 = jnp.float32
BF16 = jnp.bfloat16
I32 = jnp.int32

EPS = 1e-5
HEAD_DIM = 64
SPAN = 128
CONV_W = 3
DILATIONS = (1, 4, 16)
TOP_K = 2
LANES = 128
SUBLANES = 8
VMEM_LIMIT_BYTES = 56 * 1024 * 1024
NT_DIMS = (((1,), (1,)), ((), ()))


def _params(*sem):
    return pltpu.CompilerParams(dimension_semantics=sem, vmem_limit_bytes=VMEM_LIMIT_BYTES)


def _pick(n, candidates):
    for c in candidates:
        if n % c == 0:
            return c
    raise ValueError(f"no tile for {n} in {candidates}")


def _rms(x, g):
    y = x * lax.rsqrt(jnp.mean(x * x, axis=-1, keepdims=True) + EPS)
    return y * g


def _dense_body(*refs, n_in, has_gain, has_res):
    xs = refs[:n_in]
    pos = n_in
    g_ref = refs[pos] if has_gain else None
    pos += int(has_gain)
    ws = refs[pos:pos + n_in]
    pos += n_in
    res_ref = refs[pos] if has_res else None
    pos += int(has_res)
    o_ref = refs[pos]
    if has_gain:
        xn_ref = refs[pos + 1]

        @pl.when(pl.program_id(1) == 0)
        def _():
            xn_ref[...] = _rms(xs[0][...], g_ref[...]).astype(BF16)

        lhs = [xn_ref[...]]
    else:
        lhs = [x[...].astype(BF16) for x in xs]
    acc = None
    for a, w in zip(lhs, ws):
        d = jnp.dot(a, w[...].astype(BF16), preferred_element_type=F32)
        acc = d if acc is None else acc + d
    if has_res:
        acc = res_ref[...] + acc
    o_ref[...] = acc.astype(o_ref.dtype)


def _dense(xs, w, *, gain=None, res=None, out_dtype, name):
    t = xs[0].shape[0]
    n = w.shape[1]
    ks = [x.shape[1] for x in xs]
    assert sum(ks) == w.shape[0] and all(k == ks[0] for k in ks)
    tm = _pick(t, (1056, 1024, 768, 512, 256, 128, 64, 32, 16, 8))
    tn = _pick(n, (1152, 1024, 768, 512, 256, 128))
    in_specs = [pl.BlockSpec((tm, k), lambda i, j: (i, 0)) for k in ks]
    args = list(xs)
    if gain is not None:
        in_specs.append(pl.BlockSpec((1, ks[0]), lambda i, j: (0, 0)))
        args.append(gain.reshape(1, -1))
    for r, k in enumerate(ks):
        in_specs.append(pl.BlockSpec((k, tn), lambda i, j, r=r: (r, j)))
        args.append(w)
    if res is not None:
        in_specs.append(pl.BlockSpec((tm, tn), lambda i, j: (i, j)))
        args.append(res)
    scratch = [pltpu.VMEM((tm, ks[0]), BF16)] if gain is not None else []
    return pl.pallas_call(
        functools.partial(_dense_body, n_in=len(xs), has_gain=gain is not None, has_res=res is not None),
        out_shape=jax.ShapeDtypeStruct((t, n), out_dtype),
        grid=(t // tm, n // tn),
        in_specs=in_specs,
        out_specs=pl.BlockSpec((tm, tn), lambda i, j: (i, j)),
        scratch_shapes=scratch,
        compiler_params=_params("parallel", "arbitrary"),
        name=name,
    )(*args)


def _proj_t_body(w_ref, x_ref, g_ref, o_ref):
    xn = _rms(x_ref[...], g_ref[...]).astype(BF16)
    o_ref[...] = lax.dot_general(w_ref[...].astype(BF16), xn, NT_DIMS, preferred_element_type=F32)


def _proj_t(w_t, x, gain, name):
    c, k = w_t.shape
    rows = x.shape[0]
    tc = _pick(c, (512, 256, 128))
    return pl.pallas_call(
        _proj_t_body,
        out_shape=jax.ShapeDtypeStruct((c, rows), F32),
        grid=(c // tc,),
        in_specs=[pl.BlockSpec((tc, k), lambda i: (i, 0)),
                  pl.BlockSpec((rows, k), lambda i: (0, 0)),
                  pl.BlockSpec((1, k), lambda i: (0, 0))],
        out_specs=pl.BlockSpec((tc, rows), lambda i: (i, 0)),
        compiler_params=_params("parallel"),
        name=name,
    )(w_t, x, gain.reshape(1, -1))


def _ffn_body(x_ref, g_ref, wg_ref, wu_ref, wd_ref, o_ref, xn_ref, acc_ref, *, nj):
    j = pl.program_id(1)

    @pl.when(j == 0)
    def _():
        xn_ref[...] = _rms(x_ref[...], g_ref[...]).astype(BF16)
        acc_ref[...] = jnp.zeros_like(acc_ref)

    xn = xn_ref[...]
    gate = jnp.dot(xn, wg_ref[...].astype(BF16), preferred_element_type=F32)
    up = jnp.dot(xn, wu_ref[...].astype(BF16), preferred_element_type=F32)
    act = (gate * jax.nn.sigmoid(gate) * up).astype(BF16)
    acc_ref[...] += jnp.dot(act, wd_ref[...].astype(BF16), preferred_element_type=F32)

    @pl.when(j == nj - 1)
    def _():
        o_ref[...] = x_ref[...] + acc_ref[...]


def _ffn(x, gain, w_gate, w_up, w_down, name):
    t, d = x.shape
    f = w_gate.shape[1]
    tm = _pick(t, (1408, 1056, 1024, 768, 512, 256, 128, 64, 32, 16, 8))
    tf = _pick(f, (256, 128))
    nj = f // tf
    return pl.pallas_call(
        functools.partial(_ffn_body, nj=nj),
        out_shape=jax.ShapeDtypeStruct((t, d), F32),
        grid=(t // tm, nj),
        in_specs=[pl.BlockSpec((tm, d), lambda i, j: (i, 0)),
                  pl.BlockSpec((1, d), lambda i, j: (0, 0)),
                  pl.BlockSpec((d, tf), lambda i, j: (0, j)),
                  pl.BlockSpec((d, tf), lambda i, j: (0, j)),
                  pl.BlockSpec((tf, d), lambda i, j: (j, 0))],
        out_specs=pl.BlockSpec((tm, d), lambda i, j: (i, 0)),
        scratch_shapes=[pltpu.VMEM((tm, d), BF16), pltpu.VMEM((tm, d), F32)],
        compiler_params=_params("parallel", "arbitrary"),
        name=name,
    )(x, gain.reshape(1, -1), w_gate, w_up, w_down)


def _band_body(*refs, hq, hk, tq, has_sink, want_lse):
    q_ref, kc_ref, kp_ref, vc_ref, vp_ref = refs[:5]
    pos = 5
    sink_ref = refs[pos] if has_sink else None
    pos += int(has_sink)
    o_ref = refs[pos]
    lse_ref = refs[pos + 1] if want_lse else None
    t = pl.program_id(1)
    group = hq // hk
    scale = HEAD_DIM ** -0.5
    qi = lax.broadcasted_iota(I32, (SPAN, 2 * SPAN), 0)
    kj = lax.broadcasted_iota(I32, (SPAN, 2 * SPAN), 1)
    band = (kj >= qi) & (kj <= qi + SPAN)
    band_first = band & (kj >= jnp.where(t == 0, SPAN, 0))
    kall = jnp.concatenate([kp_ref[...], kc_ref[...]], axis=0).astype(BF16)
    vall = jnp.concatenate([vp_ref[...], vc_ref[...]], axis=0).astype(BF16)
    for h in range(hq):
        kh = h // group
        lanes = slice(h * HEAD_DIM, (h + 1) * HEAD_DIM)
        klanes = slice(kh * HEAD_DIM, (kh + 1) * HEAD_DIM)
        for b in range(tq // SPAN):
            rows = slice(b * SPAN, (b + 1) * SPAN)
            keys = slice(b * SPAN, (b + 2) * SPAN)
            q = (q_ref[rows, lanes] * scale).astype(BF16)
            s = lax.dot_general(q, kall[keys, klanes], NT_DIMS, preferred_element_type=F32)
            s = jnp.where(band_first if b == 0 else band, s, -jnp.inf)
            m = jnp.max(s, axis=-1, keepdims=True)
            if has_sink:
                m = jnp.maximum(m, sink_ref[h])
            p = jnp.exp(s - m)
            den = jnp.sum(p, axis=-1, keepdims=True)
            if has_sink:
                den = den + jnp.exp(sink_ref[h] - m)
            o = jnp.dot(p.astype(BF16), vall[keys, klanes], preferred_element_type=F32) / den
            o_ref[rows, lanes] = o.astype(o_ref.dtype)
            if want_lse:
                lse_ref[rows, lanes] = jnp.broadcast_to(m + jnp.log(den), (SPAN, HEAD_DIM))


def _band_attn(arr, *, n_seq, seq_len, hq, hk, q_col, k_col, v_col, sink=None, want_lse, out_dtype, name):
    cq, ck = hq * HEAD_DIM, hk * HEAD_DIM
    assert q_col % cq == 0 and k_col % ck == 0 and v_col % ck == 0
    tq = _pick(seq_len, (256, 128))
    nt = seq_len // tq
    nb = tq // SPAN

    def cur(col):
        return lambda n, t: (n * nt + t, col)

    def prev(col):
        return lambda n, t: (jnp.maximum((n * nt + t) * nb - 1, 0), col)

    in_specs = [pl.BlockSpec((tq, cq), cur(q_col // cq)),
                pl.BlockSpec((tq, ck), cur(k_col // ck)),
                pl.BlockSpec((SPAN, ck), prev(k_col // ck)),
                pl.BlockSpec((tq, ck), cur(v_col // ck)),
                pl.BlockSpec((SPAN, ck), prev(v_col // ck))]
    args = [arr] * 5
    if sink is not None:
        in_specs.append(pl.BlockSpec(memory_space=pltpu.SMEM))
        args.append(sink.reshape(hq).astype(F32))
    rows = n_seq * seq_len
    out_shape = [jax.ShapeDtypeStruct((rows, cq), out_dtype)]
    out_specs = [pl.BlockSpec((tq, cq), lambda n, t: (n * nt + t, 0))]
    if want_lse:
        out_shape.append(jax.ShapeDtypeStruct((rows, cq), F32))
        out_specs.append(pl.BlockSpec((tq, cq), lambda n, t: (n * nt + t, 0)))
    return pl.pallas_call(
        functools.partial(_band_body, hq=hq, hk=hk, tq=tq, has_sink=sink is not None, want_lse=want_lse),
        out_shape=out_shape,
        grid=(n_seq, nt),
        in_specs=in_specs,
        out_specs=out_specs,
        compiler_params=_params("parallel", "parallel"),
        name=name,
    )(*args)


def _combine_body(o0, l0, o1, l1, o2, l2, out_ref):
    a0, a1, a2 = l0[...], l1[...], l2[...]
    m = jnp.maximum(jnp.maximum(a0, a1), a2)
    e0, e1, e2 = jnp.exp(a0 - m), jnp.exp(a1 - m), jnp.exp(a2 - m)
    num = e0 * o0[...].astype(F32) + e1 * o1[...].astype(F32) + e2 * o2[...].astype(F32)
    out_ref[...] = (num / (e0 + e1 + e2)).astype(out_ref.dtype)


def _combine(pairs, name):
    rows, c = pairs[0][0].shape
    tm = _pick(rows, (1024, 512, 256, 128, 64, 32, 16, 8))
    spec = pl.BlockSpec((tm, c), lambda i: (i, 0))
    args = [a for pair in pairs for a in pair]
    return pl.pallas_call(
        _combine_body,
        out_shape=jax.ShapeDtypeStruct((rows, c), BF16),
        grid=(rows // tm,),
        in_specs=[spec] * 6,
        out_specs=spec,
        compiler_params=_params("parallel"),
        name=name,
    )(*args)


def _conv_prompt_body(gb_ref, gc_ref, xa_ref, gcp_ref, xap_ref, w_ref, a_ref, st_ref):
    t = pl.program_id(1)
    u = gc_ref[...] * xa_ref[...]
    up = jnp.where(t == 0, 0.0, gcp_ref[...] * xap_ref[...])
    ext = jnp.concatenate([up, u], axis=0)
    w = w_ref[...]
    y = (w[0:1] * pltpu.roll(ext, 2, 0)[SUBLANES:] + w[1:2] * pltpu.roll(ext, 1, 0)[SUBLANES:]) + w[2:3] * u
    a_ref[...] = (gb_ref[...] * y).astype(a_ref.dtype)
    st_ref[0] = u[u.shape[0] - SUBLANES:]


def _conv_prompt(proj, conv_w, *, n_seq, seq_len, c):
    tq = _pick(seq_len, (512, 256, 128))
    nt = seq_len // tq
    rb = tq // SUBLANES

    def cur(col):
        return lambda n, t: (n * nt + t, col)

    def prev(col):
        return lambda n, t: (jnp.maximum((n * nt + t) * rb - 1, 0), col)

    return pl.pallas_call(
        _conv_prompt_body,
        out_shape=[jax.ShapeDtypeStruct((n_seq * seq_len, c), BF16),
                   jax.ShapeDtypeStruct((n_seq, SUBLANES, c), F32)],
        grid=(n_seq, nt),
        in_specs=[pl.BlockSpec((tq, c), cur(0)), pl.BlockSpec((tq, c), cur(1)), pl.BlockSpec((tq, c), cur(2)),
                  pl.BlockSpec((SUBLANES, c), prev(1)), pl.BlockSpec((SUBLANES, c), prev(2)),
                  pl.BlockSpec((CONV_W, c), lambda n, t: (0, 0))],
        out_specs=[pl.BlockSpec((tq, c), lambda n, t: (n * nt + t, 0)),
                   pl.BlockSpec((1, SUBLANES, c), lambda n, t: (n, 0, 0))],
        compiler_params=_params("parallel", "arbitrary"),
        name="conv_prompt",
    )(proj, proj, proj, proj, proj, conv_w)


def _conv_step_body(p_ref, prev_ref, w_ref, a_ref, st_ref):
    s_len = p_ref.shape[1]
    w = w_ref[...]
    hist = [prev_ref[k] for k in range(CONV_W - 1)] + [p_ref[1, s] * p_ref[2, s] for s in range(s_len)]
    for s in range(s_len):
        y = (w[0:1] * hist[s] + w[1:2] * hist[s + 1]) + w[2:3] * hist[s + 2]
        a_ref[s] = p_ref[0, s] * y
    for k in range(CONV_W - 1):
        st_ref[k] = hist[s_len + k]


def _conv_step(p3, prev, conv_w):
    _, s_len, n, c = p3.shape
    return pl.pallas_call(
        _conv_step_body,
        out_shape=[jax.ShapeDtypeStruct((s_len, n, c), F32), jax.ShapeDtypeStruct((CONV_W - 1, n, c), F32)],
        name="conv_step",
        compiler_params=pltpu.CompilerParams(vmem_limit_bytes=VMEM_LIMIT_BYTES),
    )(p3, prev, conv_w)


def _step_body(*refs, n_blk, qr, tok_div, w, dil, cw, bsz, has_sink, want_lse):
    q_ref, kvt_ref, cache_ref = refs[:3]
    pos = 3
    sink_ref = refs[pos] if has_sink else None
    pos += int(has_sink)
    o_ref = refs[pos]
    pos += 1
    lse_ref = refs[pos] if want_lse else None
    pos += int(want_lse)
    cout_ref = refs[pos]
    r_dim = n_blk * HEAD_DIM
    nrb = n_blk * qr
    n_chunks = w // cw
    scale = HEAD_DIM ** -0.5
    step = pl.program_id(0)
    lane_r = lax.broadcasted_iota(I32, (1, r_dim), 1)
    blk_masks = [(lane_r >= j * HEAD_DIM) & (lane_r < (j + 1) * HEAD_DIM) for j in range(n_blk)]
    row = lax.broadcasted_iota(I32, (nrb, 1), 0)
    tok = (row % qr) // tok_div
    lane_c = lax.broadcasted_iota(I32, (1, cw), 1)
    lane_n = lax.broadcasted_iota(I32, (1, LANES), 1)

    def valid(pos_l):
        ok = (pos_l >= tok) & (pos_l <= w + tok)
        if dil > 1:
            ok = ok & (((pos_l - tok) & (dil - 1)) == 0)
        return ok

    for b in range(bsz):
        off = ((step * bsz + b) * 4) % LANES
        shift = (LANES - off) % LANES
        new_k = pltpu.roll(kvt_ref[0], shift, 1)
        new_v = pltpu.roll(kvt_ref[1], shift, 1)
        q = q_ref[b] * scale
        qbd = jnp.concatenate([jnp.where(mk, q, 0.0) for mk in blk_masks], axis=0).astype(BF16)
        scores = []
        for c in range(n_chunks):
            lo, hi = c * cw, (c + 1) * cw
            kc = cache_ref[b, 0, :, lo:hi]
            sc = jnp.dot(qbd, kc.astype(BF16), preferred_element_type=F32)
            scores.append(jnp.where(valid(lane_c + lo), sc, -jnp.inf))
            nxt = cache_ref[b, 0, :, hi:hi + LANES] if c + 1 < n_chunks else new_k
            ext = jnp.concatenate([kc, nxt], axis=1)
            cout_ref[b, 0, :, lo:hi] = pltpu.roll(ext, cw + LANES - 4, 1)[:, :cw]
        sc = jnp.dot(qbd, new_k.astype(BF16), preferred_element_type=F32)
        scores.append(jnp.where(valid(lane_n + w), sc, -jnp.inf))
        m = functools.reduce(jnp.maximum, [jnp.max(s, axis=-1, keepdims=True) for s in scores])
        if has_sink:
            m = jnp.maximum(m, sink_ref[...])
        probs = [jnp.exp(s - m) for s in scores]
        den = functools.reduce(lambda a, c: a + c, [jnp.sum(p, axis=-1, keepdims=True) for p in probs])
        if has_sink:
            den = den + jnp.exp(sink_ref[...] - m)
        pv = lax.dot_general(probs[-1].astype(BF16), new_v.astype(BF16), NT_DIMS, preferred_element_type=F32)
        for c in range(n_chunks):
            lo, hi = c * cw, (c + 1) * cw
            vc = cache_ref[b, 1, :, lo:hi]
            pv = pv + lax.dot_general(probs[c].astype(BF16), vc.astype(BF16), NT_DIMS,
                                      preferred_element_type=F32)
            nxt = cache_ref[b, 1, :, hi:hi + LANES] if c + 1 < n_chunks else new_v
            ext = jnp.concatenate([vc, nxt], axis=1)
            cout_ref[b, 1, :, lo:hi] = pltpu.roll(ext, cw + LANES - 4, 1)[:, :cw]
        o = jnp.zeros((qr, r_dim), F32)
        m_e = jnp.zeros((qr, r_dim), F32)
        den_e = jnp.zeros((qr, r_dim), F32)
        for j, mk in enumerate(blk_masks):
            rows = slice(j * qr, (j + 1) * qr)
            o = jnp.where(mk, pv[rows], o)
            m_e = jnp.where(mk, m[rows], m_e)
            den_e = jnp.where(mk, den[rows], den_e)
        o_ref[b] = o / den_e
        if want_lse:
            lse_ref[b] = m_e + jnp.log(den_e)


def _step_attn(q, kvt, cache, *, dil, tok_div, sink_col=None, want_lse, bsz, name):
    n, qr, r_dim = q.shape
    w = cache.shape[-1]
    n_blk = r_dim // HEAD_DIM
    cw = min(w, 512)
    in_specs = [pl.BlockSpec((bsz, qr, r_dim), lambda i: (i, 0, 0)),
                pl.BlockSpec((2, r_dim, LANES), lambda i: (0, 0, (i * bsz * 4) // LANES)),
                pl.BlockSpec((bsz, 2, r_dim, w), lambda i: (i, 0, 0, 0))]
    args = [q, kvt, cache]
    if sink_col is not None:
        in_specs.append(pl.BlockSpec((n_blk * qr, 1), lambda i: (0, 0)))
        args.append(sink_col)
    o_spec = pl.BlockSpec((bsz, qr, r_dim), lambda i: (i, 0, 0))
    out_shape = [jax.ShapeDtypeStruct((n, qr, r_dim), F32)]
    out_specs = [o_spec]
    if want_lse:
        out_shape.append(jax.ShapeDtypeStruct((n, qr, r_dim), F32))
        out_specs.append(o_spec)
    out_shape.append(jax.ShapeDtypeStruct(cache.shape, F32))
    out_specs.append(pl.BlockSpec((bsz, 2, r_dim, w), lambda i: (i, 0, 0, 0)))
    return pl.pallas_call(
        functools.partial(_step_body, n_blk=n_blk, qr=qr, tok_div=tok_div, w=w, dil=dil, cw=cw, bsz=bsz,
                          has_sink=sink_col is not None, want_lse=want_lse),
        out_shape=out_shape,
        grid=(n // bsz,),
        in_specs=in_specs,
        out_specs=out_specs,
        compiler_params=_params("parallel"),
        name=name,
    )(*args)


def _router_body(h_ref, g_ref, wr_ref, xn_ref, im_ref, gm_ref, cnt_ref, carry_ref, *, n_exp):
    i = pl.program_id(0)

    @pl.when(i == 0)
    def _():
        carry_ref[...] = jnp.zeros_like(carry_ref)

    xn = _rms(h_ref[...], g_ref[...])
    xn_ref[...] = xn
    tm = xn.shape[0]
    wr = wr_ref[...]
    xh = xn.astype(BF16)
    xl = (xn - xh.astype(F32)).astype(BF16)
    wh = wr.astype(BF16)
    wl = (wr - wh.astype(F32)).astype(BF16)
    lg = jnp.dot(xh, wh, preferred_element_type=F32) + (
        jnp.dot(xh, wl, preferred_element_type=F32) + jnp.dot(xl, wh, preferred_element_type=F32))
    lane = lax.broadcasted_iota(I32, (tm, LANES), 1)
    lane_f = lane.astype(F32)
    lg = jnp.where(lane < n_exp, lg, -jnp.inf)
    m1 = jnp.max(lg, axis=-1, keepdims=True)
    i1 = jnp.min(jnp.where(lg == m1, lane_f, float(LANES)), axis=-1, keepdims=True)
    lg2 = jnp.where(lane_f == i1, -jnp.inf, lg)
    m2 = jnp.max(lg2, axis=-1, keepdims=True)
    i2 = jnp.min(jnp.where(lg2 == m2, lane_f, float(LANES)), axis=-1, keepdims=True)
    e = jnp.exp(m2 - m1)
    g1 = 1.0 / (1.0 + e)
    g2 = e / (1.0 + e)
    sel1 = lane_f == i1
    sel2 = lane_f == i2
    onehot = jnp.where(sel1 | sel2, 1.0, 0.0)
    r_i = lax.broadcasted_iota(I32, (tm, tm), 0)
    c_i = lax.broadcasted_iota(I32, (tm, tm), 1)
    tri = jnp.where(c_i < r_i, 1.0, 0.0).astype(BF16)
    before = jnp.dot(tri, onehot.astype(BF16), preferred_element_type=F32) + carry_ref[0:1]
    r1 = jnp.sum(jnp.where(sel1, before, 0.0), axis=-1, keepdims=True)
    r2 = jnp.sum(jnp.where(sel2, before, 0.0), axis=-1, keepdims=True)
    total = carry_ref[0:1] + jnp.sum(onehot, axis=0, keepdims=True)
    carry_ref[...] = jnp.broadcast_to(total, carry_ref.shape)
    cnt_ref[...] = jnp.broadcast_to(total, cnt_ref.shape).astype(I32)
    meta = jnp.where(lane == 0, i1, jnp.where(lane == 1, i2, jnp.where(lane == 2, r1, jnp.where(lane == 3, r2, 0.0))))
    im_ref[...] = meta.astype(I32)
    gm_ref[...] = jnp.where(lane == 0, g1, jnp.where(lane == 1, g2, 0.0))


def _router(h, gain, w_router):
    t, d = h.shape
    n_exp = w_router.shape[1]
    tm = _pick(t, (512, 256, 128, 64, 32, 16, 8))
    wr = jnp.zeros((d, LANES), F32).at[:, :n_exp].set(w_router)
    return pl.pallas_call(
        functools.partial(_router_body, n_exp=n_exp),
        out_shape=[jax.ShapeDtypeStruct((t, d), F32), jax.ShapeDtypeStruct((t, LANES), I32),
                   jax.ShapeDtypeStruct((t, LANES), F32), jax.ShapeDtypeStruct((SUBLANES, LANES), I32)],
        grid=(t // tm,),
        in_specs=[pl.BlockSpec((tm, d), lambda i: (i, 0)),
                  pl.BlockSpec((1, d), lambda i: (0, 0)),
                  pl.BlockSpec((d, LANES), lambda i: (0, 0))],
        out_specs=[pl.BlockSpec((tm, d), lambda i: (i, 0)),
                   pl.BlockSpec((tm, LANES), lambda i: (i, 0)),
                   pl.BlockSpec((tm, LANES), lambda i: (i, 0)),
                   pl.BlockSpec((SUBLANES, LANES), lambda i: (0, 0))],
        scratch_shapes=[pltpu.VMEM((SUBLANES, LANES), F32)],
        compiler_params=_params("arbitrary"),
        name="moe_router",
    )(h, gain.reshape(1, -1), wr)


def _dispatch_body(s1_ref, s2_ref, x_ref, zero_ref, out_ref, sem, *, tm):
    del zero_ref

    def issue(r, carry):
        src = x_ref.at[pl.ds(r, 1)]
        pltpu.make_async_copy(src, out_ref.at[pl.ds(s1_ref[0, 0, r], 1)], sem.at[0]).start()
        pltpu.make_async_copy(src, out_ref.at[pl.ds(s2_ref[0, 0, r], 1)], sem.at[1]).start()
        return carry

    lax.fori_loop(0, tm, issue, 0, unroll=8)
    pltpu.make_async_copy(x_ref, out_ref.at[pl.ds(0, tm)], sem.at[0]).wait()
    pltpu.make_async_copy(x_ref, out_ref.at[pl.ds(0, tm)], sem.at[1]).wait()


def _dispatch(xn, slot1, slot2, n_slots):
    t, d = xn.shape
    tm = _pick(t, (512, 256, 128, 64, 32, 16, 8))
    nt = t // tm
    smem = pl.BlockSpec((1, 1, tm), lambda i: (i, 0, 0), memory_space=pltpu.SMEM)
    return pl.pallas_call(
        functools.partial(_dispatch_body, tm=tm),
        out_shape=jax.ShapeDtypeStruct((n_slots, d), F32),
        grid=(nt,),
        in_specs=[smem, smem, pl.BlockSpec((tm, d), lambda i: (i, 0)), pl.BlockSpec(memory_space=pl.ANY)],
        out_specs=pl.BlockSpec(memory_space=pl.ANY),
        scratch_shapes=[pltpu.SemaphoreType.DMA((2,))],
        input_output_aliases={3: 0},
        compiler_params=_params("arbitrary"),
        name="moe_dispatch",
    )(slot1.reshape(nt, 1, tm), slot2.reshape(nt, 1, tm), xn, jnp.zeros((n_slots, d), F32))


def _experts_body(te_ref, nu_ref, x_ref, wg_ref, wu_ref, wd_ref, o_ref, xb_ref, acc_ref, *, nj):
    del te_ref
    i = pl.program_id(0)
    j = pl.program_id(1)
    active = i < nu_ref[0]

    @pl.when(active & (j == 0))
    def _():
        xb_ref[...] = x_ref[...].astype(BF16)
        acc_ref[...] = jnp.zeros_like(acc_ref)

    @pl.when(active)
    def _():
        xb = xb_ref[...]
        gate = jnp.dot(xb, wg_ref[0].astype(BF16), preferred_element_type=F32)
        up = jnp.dot(xb, wu_ref[0].astype(BF16), preferred_element_type=F32)
        act = (gate * jax.nn.sigmoid(gate) * up).astype(BF16)
        acc_ref[...] += jnp.dot(act, wd_ref[0].astype(BF16), preferred_element_type=F32)

    @pl.when(active & (j == nj - 1))
    def _():
        o_ref[...] = acc_ref[...]

    @pl.when(jnp.logical_not(active) & (j == nj - 1))
    def _():
        o_ref[...] = jnp.zeros_like(o_ref)


def _experts(xs, tile_expert, n_used, w_gate, w_up, w_down, tm):
    n_slots, d = xs.shape
    f = w_gate.shape[2]
    tf = _pick(f, (512, 256, 128))
    nj = f // tf
    n_tiles = n_slots // tm

    def act_j(i, j, nu):
        return jnp.where(i < nu[0], j, nj - 1)

    def row(i, nu):
        return jnp.minimum(i, nu[0] - 1)

    grid_spec = pltpu.PrefetchScalarGridSpec(
        num_scalar_prefetch=2,
        grid=(n_tiles, nj),
        in_specs=[pl.BlockSpec((tm, d), lambda i, j, te, nu: (row(i, nu), 0)),
                  pl.BlockSpec((1, d, tf), lambda i, j, te, nu: (te[i], 0, act_j(i, j, nu))),
                  pl.BlockSpec((1, d, tf), lambda i, j, te, nu: (te[i], 0, act_j(i, j, nu))),
                  pl.BlockSpec((1, tf, d), lambda i, j, te, nu: (te[i], act_j(i, j, nu), 0))],
        out_specs=pl.BlockSpec((tm, d), lambda i, j, te, nu: (i, 0)),
        scratch_shapes=[pltpu.VMEM((tm, d), BF16), pltpu.VMEM((tm, d), F32)],
    )
    return pl.pallas_call(
        functools.partial(_experts_body, nj=nj),
        out_shape=jax.ShapeDtypeStruct((n_slots, d), F32),
        grid_spec=grid_spec,
        compiler_params=_params("arbitrary", "arbitrary"),
        name="moe_experts",
    )(tile_expert, n_used, xs, w_gate, w_up, w_down)


def _gather_norm_body(s1_ref, s2_ref, h_ref, gm_ref, g_ref, ys_ref, o_ref, ya_ref, yb_ref, sem, *, tm):
    def issue(r, carry):
        pltpu.make_async_copy(ys_ref.at[pl.ds(s1_ref[0, 0, r], 1)], ya_ref.at[pl.ds(r, 1)], sem.at[0]).start()
        pltpu.make_async_copy(ys_ref.at[pl.ds(s2_ref[0, 0, r], 1)], yb_ref.at[pl.ds(r, 1)], sem.at[1]).start()
        return carry

    lax.fori_loop(0, tm, issue, 0, unroll=8)
    pltpu.make_async_copy(ys_ref.at[pl.ds(0, tm)], ya_ref, sem.at[0]).wait()
    pltpu.make_async_copy(ys_ref.at[pl.ds(0, tm)], yb_ref, sem.at[1]).wait()
    gm = gm_ref[...]
    y = h_ref[...] + (gm[:, 0:1] * ya_ref[...] + gm[:, 1:2] * yb_ref[...])
    o_ref[...] = _rms(y, g_ref[...])


def _gather_norm(h, gates, slot1, slot2, ys, gain):
    t, d = h.shape
    tm = _pick(t, (512, 256, 128, 64, 32, 16, 8))
    nt = t // tm
    smem = pl.BlockSpec((1, 1, tm), lambda i: (i, 0, 0), memory_space=pltpu.SMEM)
    return pl.pallas_call(
        functools.partial(_gather_norm_body, tm=tm),
        out_shape=jax.ShapeDtypeStruct((t, d), F32),
        grid=(nt,),
        in_specs=[smem, smem, pl.BlockSpec((tm, d), lambda i: (i, 0)),
                  pl.BlockSpec((tm, LANES), lambda i: (i, 0)),
                  pl.BlockSpec((1, d), lambda i: (0, 0)),
                  pl.BlockSpec(memory_space=pl.ANY)],
        out_specs=pl.BlockSpec((tm, d), lambda i: (i, 0)),
        scratch_shapes=[pltpu.VMEM((tm, d), F32), pltpu.VMEM((tm, d), F32), pltpu.SemaphoreType.DMA((2,))],
        compiler_params=_params("arbitrary"),
        name="moe_gather_norm",
    )(slot1.reshape(nt, 1, tm), slot2.reshape(nt, 1, tm), h, gates, gain.reshape(1, -1), ys)


def _moe(h, gain, w_router, w_gate, w_up, w_down, final_gain):
    t, _ = h.shape
    n_exp = w_router.shape[1]
    tm_e = 1024
    xn, imeta, gates, counts = _router(h, gain, w_router)
    idx1, idx2, rank1, rank2 = imeta[:, 0], imeta[:, 1], imeta[:, 2], imeta[:, 3]
    cnt = counts[0, :n_exp]
    padded = ((cnt + tm_e - 1) // tm_e) * tm_e
    ends = jnp.cumsum(padded)
    starts = ends - padded
    slot1 = starts[idx1] + rank1
    slot2 = starts[idx2] + rank2
    n_tiles = (TOP_K * t + n_exp * (tm_e - 1)) // tm_e
    n_used = (ends[-1] // tm_e).astype(I32)
    tile_start = jnp.arange(n_tiles, dtype=I32) * tm_e
    tile_expert = jnp.minimum(jnp.searchsorted(ends, tile_start, side="right"), n_exp - 1).astype(I32)
    last = tile_expert[jnp.maximum(n_used - 1, 0)]
    tile_expert = jnp.where(jnp.arange(n_tiles) < n_used, tile_expert, last)
    xs = _dispatch(xn, slot1, slot2, n_tiles * tm_e)
    ys = _experts(xs, tile_expert, n_used.reshape(1), w_gate, w_up, w_down, tm_e)
    return _gather_norm(h, gates, slot1, slot2, ys, final_gain)


def _cache_to_slabs(cache):
    n, w, two, heads, hd = cache.shape
    return jnp.transpose(cache, (0, 2, 3, 4, 1)).reshape(n, two, heads * hd, w)


def _slabs_to_cache(slabs, heads):
    n, two, _, w = slabs.shape
    return jnp.transpose(slabs.reshape(n, two, heads, HEAD_DIM, w), (0, 4, 1, 2, 3))


def kernel(x_prompt, x_sample, cache_conv, cache_swa_kv, cache_dil_kv0, cache_dil_kv1, cache_dil_kv2, norm_mix0, w_in0, conv_w, swa_sink, w_out0, norm_ffn0, w_gate0, w_up0, w_down0, norm_mix1, w_in1, w_out1, norm_ffn1, w_router, w_gate1, w_up1, w_down1, norm_final):
    n_p, seq, d = x_prompt.shape
    n_s, s_len, _ = x_sample.shape
    tp, ts = n_p * seq, n_s * s_len
    d_conv = conv_w.shape[2]
    kvh, grp = swa_sink.shape[1], swa_sink.shape[2]
    hq0 = kvh * grp
    h1 = cache_dil_kv0.shape[4]
    c_q0, c_kv0 = hq0 * HEAD_DIM, kvh * HEAD_DIM
    q0_col = 3 * d_conv
    k0_col = q0_col + c_q0
    c_g = 3 * h1 * HEAD_DIM
    dil_caches = (cache_dil_kv0, cache_dil_kv1, cache_dil_kv2)

    h = jnp.concatenate([x_prompt.reshape(tp, d), x_sample.reshape(ts, d)], axis=0)

    proj0 = _dense([h], w_in0[0], gain=norm_mix0[0], out_dtype=F32, name="in_proj0")
    a_p, conv_tail = _conv_prompt(proj0, conv_w[0], n_seq=n_p, seq_len=seq, c=d_conv)
    att_p = _band_attn(proj0, n_seq=n_p, seq_len=seq, hq=hq0, hk=kvh, q_col=q0_col, k_col=k0_col,
                       v_col=k0_col + c_kv0, sink=swa_sink[0], want_lse=False, out_dtype=BF16, name="swa_prompt")[0]
    proj0_s = proj0[tp:]
    p3 = jnp.transpose(proj0_s[:, :q0_col].reshape(n_s, s_len, 3, d_conv), (2, 1, 0, 3))
    a_s, conv_new = _conv_step(p3, jnp.transpose(cache_conv[0], (1, 0, 2)), conv_w[0])
    a_s = jnp.transpose(a_s, (1, 0, 2)).reshape(ts, d_conv)
    kvt0 = _proj_t(jnp.transpose(w_in0[0][:, k0_col:]), h[tp:], norm_mix0[0], "kv_t0")
    q_s = proj0_s[:, q0_col:k0_col].reshape(n_s, s_len, kvh, grp, HEAD_DIM)
    q_s = jnp.transpose(q_s, (0, 1, 3, 2, 4)).reshape(n_s, s_len * grp, c_kv0)
    sink_col = jnp.broadcast_to(swa_sink[0][:, None, :], (kvh, s_len, grp)).reshape(kvh * s_len * grp, 1)
    o_s, swa_new = _step_attn(q_s, kvt0.reshape(2, c_kv0, ts), _cache_to_slabs(cache_swa_kv[0]), dil=1,
                              tok_div=grp, sink_col=sink_col, want_lse=False, bsz=8, name="swa_step")
    att_s = jnp.transpose(o_s.reshape(n_s, s_len, grp, kvh, HEAD_DIM), (0, 1, 3, 2, 4)).reshape(ts, c_q0)
    a_all = jnp.concatenate([a_p, a_s.astype(BF16)], axis=0)
    att_all = jnp.concatenate([att_p, att_s.astype(BF16)], axis=0)
    h = _dense([a_all, att_all], w_out0[0], res=h, out_dtype=F32, name="out_proj0")
    h = _ffn(h, norm_ffn0[0], w_gate0[0], w_up0[0], w_down0[0], "ffn0")

    proj1 = _dense([h], w_in1[0], gain=norm_mix1[0], out_dtype=F32, name="in_proj1")
    proj1_p = proj1[:tp].reshape(n_p, seq, 3, c_g)
    proj1_s = proj1[tp:]
    c_h = h1 * HEAD_DIM
    pairs_p, pairs_s, dil_p, dil_s = [], [], [], []
    for g, dil in enumerate(DILATIONS):
        window = dil_caches[g].shape[2]
        cls_len = seq // dil
        qkv = proj1_p[:, :, g].reshape(n_p, cls_len, dil, c_g)
        qkv = jnp.transpose(qkv, (0, 2, 1, 3)).reshape(n_p * dil * cls_len, c_g)
        o_g, l_g = _band_attn(qkv, n_seq=n_p * dil, seq_len=cls_len, hq=h1, hk=h1, q_col=0, k_col=c_h,
                              v_col=2 * c_h, want_lse=True, out_dtype=BF16, name=f"dil{g}_prompt")

        def natural(a):
            a = a.reshape(n_p, dil, cls_len, c_h)
            return jnp.transpose(a, (0, 2, 1, 3)).reshape(tp, c_h)

        pairs_p.append((natural(o_g), natural(l_g)))
        keep = min(window, seq)
        dil_p.append(proj1_p[:, seq - keep:, g, c_h:].reshape(1, n_p, keep, 2, h1, HEAD_DIM))
        lo = g * c_g
        q_g = proj1_s[:, lo:lo + c_h].reshape(n_s, s_len, c_h)
        q_g = jnp.concatenate([q_g, jnp.zeros((n_s, SUBLANES - s_len, c_h), F32)], axis=1)
        kvt = _proj_t(jnp.transpose(w_in1[0][:, lo + c_h:lo + c_g]), h[tp:], norm_mix1[0], f"kv_t1_{g}")
        o_sg, l_sg, cache_new = _step_attn(q_g, kvt.reshape(2, c_h, ts), _cache_to_slabs(dil_caches[g][0]),
                                           dil=dil, tok_div=1, want_lse=True,
                                           bsz=max(1, 2048 // window), name=f"dil{g}_step")
        pairs_s.append((o_sg.reshape(n_s * SUBLANES, c_h), l_sg.reshape(n_s * SUBLANES, c_h)))
        dil_s.append(_slabs_to_cache(cache_new, h1)[None])
    comb_p = _combine(pairs_p, "combine_prompt")
    comb_s = _combine(pairs_s, "combine_step").reshape(n_s, SUBLANES, c_h)[:, :s_len].reshape(ts, c_h)
    h = _dense([jnp.concatenate([comb_p, comb_s], axis=0)], w_out1[0], res=h, out_dtype=F32, name="out_proj1")
    y = _moe(h, norm_ffn1[0], w_router[0], w_gate1[0], w_up1[0], w_down1[0], norm_final)

    y_prompt = y[:tp].reshape(n_p, seq, d)
    y_sample = y[tp:].reshape(n_s, s_len, d)
    new_conv_prompt = conv_tail[:, SUBLANES - (CONV_W - 1):][None]
    new_conv_sample = jnp.transpose(conv_new, (1, 0, 2))[None]
    keep0 = min(SPAN, seq)
    new_swa_kv_prompt = proj0[:tp].reshape(n_p, seq, -1)[:, seq - keep0:, k0_col:].reshape(
        1, n_p, keep0, 2, kvh, HEAD_DIM)
    new_swa_kv_sample = _slabs_to_cache(swa_new, kvh)[None]
    return (y_prompt, y_sample, new_conv_prompt, new_conv_sample, new_swa_kv_prompt, new_swa_kv_sample,
            dil_p[0], dil_s[0], dil_p[1], dil_s[1], dil_p[2], dil_s[2])
```

```python
import functools

import jax
import jax.numpy as jnp
from jax import lax
from jax.experimental import pallas as pl
from jax.experimental.pallas import tpu as pltpu

F32 = jnp.float32
BF16 = jnp.bfloat16
I32 = jnp.int32

EPS = 1e-5
HEAD_DIM = 64
SPAN = 128
CONV_W = 3
DILATIONS = (1, 4, 16)
TOP_K = 2
LANES = 128
SUBLANES = 8
VMEM_LIMIT_BYTES = 56 * 1024 * 1024
NT_DIMS = (((1,), (1,)), ((), ()))


def _params(*sem):
    return pltpu.CompilerParams(dimension_semantics=sem, vmem_limit_bytes=VMEM_LIMIT_BYTES)


def _pick(n, candidates):
    for c in candidates:
        if n % c == 0:
            return c
    raise ValueError(f"no tile for {n} in {candidates}")


def _rms(x, g):
    y = x * lax.rsqrt(jnp.mean(x * x, axis=-1, keepdims=True) + EPS)
    return y * g


def _dense_body(*refs, n_in, has_gain, has_res, slab_out):
    xs = refs[:n_in]
    pos = n_in
    g_ref = refs[pos] if has_gain else None
    pos += int(has_gain)
    ws = refs[pos:pos + n_in]
    pos += n_in
    res_ref = refs[pos] if has_res else None
    pos += int(has_res)
    o_ref = refs[pos]
    if has_gain:
        xn_ref = refs[pos + 1]

        @pl.when(pl.program_id(1) == 0)
        def _():
            xn_ref[...] = _rms(xs[0][...], g_ref[...]).astype(BF16)

        lhs = [xn_ref[...]]
    else:
        lhs = [x[...].astype(BF16) for x in xs]
    acc = None
    for a, w in zip(lhs, ws):
        d = jnp.dot(a, w[...].astype(BF16), preferred_element_type=F32)
        acc = d if acc is None else acc + d
    if has_res:
        acc = res_ref[...] + acc
    if slab_out:
        for s in range(o_ref.shape[0]):
            o_ref[s] = acc[:, s * LANES:(s + 1) * LANES].astype(o_ref.dtype)
    else:
        o_ref[...] = acc.astype(o_ref.dtype)


def _dense(xs, w, *, gain=None, res=None, out_dtype, name, slab_out=False):
    t = xs[0].shape[0]
    n = w.shape[1]
    ks = [x.shape[1] for x in xs]
    assert sum(ks) == w.shape[0] and all(k == ks[0] for k in ks)
    tm = _pick(t, (1056, 1024, 768, 512, 256, 128, 64, 32, 16, 8))
    tn = _pick(n, (1152, 1024, 768, 512, 256, 128))
    in_specs = [pl.BlockSpec((tm, k), lambda i, j: (i, 0)) for k in ks]
    args = list(xs)
    if gain is not None:
        in_specs.append(pl.BlockSpec((1, ks[0]), lambda i, j: (0, 0)))
        args.append(gain.reshape(1, -1))
    for r, k in enumerate(ks):
        in_specs.append(pl.BlockSpec((k, tn), lambda i, j, r=r: (r, j)))
        args.append(w)
    if res is not None:
        in_specs.append(pl.BlockSpec((tm, tn), lambda i, j: (i, j)))
        args.append(res)
    scratch = [pltpu.VMEM((tm, ks[0]), BF16)] if gain is not None else []
    if slab_out:
        out_shape = jax.ShapeDtypeStruct((n // LANES, t, LANES), out_dtype)
        out_spec = pl.BlockSpec((tn // LANES, tm, LANES), lambda i, j: (j, i, 0))
    else:
        out_shape = jax.ShapeDtypeStruct((t, n), out_dtype)
        out_spec = pl.BlockSpec((tm, tn), lambda i, j: (i, j))
    return pl.pallas_call(
        functools.partial(_dense_body, n_in=len(xs), has_gain=gain is not None, has_res=res is not None,
                          slab_out=slab_out),
        out_shape=out_shape,
        grid=(t // tm, n // tn),
        in_specs=in_specs,
        out_specs=out_spec,
        scratch_shapes=scratch,
        compiler_params=_params("parallel", "arbitrary"),
        name=name,
    )(*args)


def _proj_t_body(w_ref, x_ref, g_ref, o_ref):
    xn = _rms(x_ref[...], g_ref[...]).astype(BF16)
    o_ref[...] = lax.dot_general(w_ref[...].astype(BF16), xn, NT_DIMS, preferred_element_type=F32)


def _proj_t(w_t, x, gain, name):
    c, k = w_t.shape
    rows = x.shape[0]
    tc = _pick(c, (512, 256, 128))
    return pl.pallas_call(
        _proj_t_body,
        out_shape=jax.ShapeDtypeStruct((c, rows), F32),
        grid=(c // tc,),
        in_specs=[pl.BlockSpec((tc, k), lambda i: (i, 0)),
                  pl.BlockSpec((rows, k), lambda i: (0, 0)),
                  pl.BlockSpec((1, k), lambda i: (0, 0))],
        out_specs=pl.BlockSpec((tc, rows), lambda i: (i, 0)),
        compiler_params=_params("parallel"),
        name=name,
    )(w_t, x, gain.reshape(1, -1))


def _ffn_body(x_ref, g_ref, wg_ref, wu_ref, wd_ref, o_ref, xn_ref, acc_ref, *, nj):
    j = pl.program_id(1)

    @pl.when(j == 0)
    def _():
        xn_ref[...] = _rms(x_ref[...], g_ref[...]).astype(BF16)
        acc_ref[...] = jnp.zeros_like(acc_ref)

    xn = xn_ref[...]
    gate = jnp.dot(xn, wg_ref[...].astype(BF16), preferred_element_type=F32)
    up = jnp.dot(xn, wu_ref[...].astype(BF16), preferred_element_type=F32)
    act = (gate * jax.nn.sigmoid(gate) * up).astype(BF16)
    acc_ref[...] += jnp.dot(act, wd_ref[...].astype(BF16), preferred_element_type=F32)

    @pl.when(j == nj - 1)
    def _():
        o_ref[...] = x_ref[...] + acc_ref[...]


def _ffn(x, gain, w_gate, w_up, w_down, name):
    t, d = x.shape
    f = w_gate.shape[1]
    tm = _pick(t, (1408, 1056, 1024, 768, 512, 256, 128, 64, 32, 16, 8))
    tf = _pick(f, (256, 128))
    nj = f // tf
    return pl.pallas_call(
        functools.partial(_ffn_body, nj=nj),
        out_shape=jax.ShapeDtypeStruct((t, d), F32),
        grid=(t // tm, nj),
        in_specs=[pl.BlockSpec((tm, d), lambda i, j: (i, 0)),
                  pl.BlockSpec((1, d), lambda i, j: (0, 0)),
                  pl.BlockSpec((d, tf), lambda i, j: (0, j)),
                  pl.BlockSpec((d, tf), lambda i, j: (0, j)),
                  pl.BlockSpec((tf, d), lambda i, j: (j, 0))],
        out_specs=pl.BlockSpec((tm, d), lambda i, j: (i, 0)),
        scratch_shapes=[pltpu.VMEM((tm, d), BF16), pltpu.VMEM((tm, d), F32)],
        compiler_params=_params("parallel", "arbitrary"),
        name=name,
    )(x, gain.reshape(1, -1), w_gate, w_up, w_down)


def _band_body(*refs, dil, mb, shared_kv, has_sink, want_lse):
    q_ref, kc_ref, kp_ref, vc_ref, vp_ref = refs[:5]
    pos = 5
    sink_ref = refs[pos] if has_sink else None
    pos += int(has_sink)
    o_ref = refs[pos]
    lse_ref = refs[pos + 1] if want_lse else None
    t = pl.program_id(1)
    slab = pl.program_id(2)
    scale = HEAD_DIM ** -0.5
    qi = lax.broadcasted_iota(I32, (SPAN, 2 * SPAN), 0)
    kj = lax.broadcasted_iota(I32, (SPAN, 2 * SPAN), 1)
    band = (kj >= qi) & (kj <= qi + SPAN)
    band_first = band & (kj >= jnp.where(t == 0, SPAN, 0))
    lane = lax.broadcasted_iota(I32, (1, LANES), 1)
    halves = [lane < HEAD_DIM, lane >= HEAD_DIM]
    if shared_kv:
        kv_half = slab // 2
        kv_mask = (lane >= kv_half * HEAD_DIM) & (lane < (kv_half + 1) * HEAD_DIM)

    def rows_of(ref, r, start, count):
        if dil == 1:
            return ref[0, start:start + count, :]
        return ref[0, pl.ds(r + dil * start, count, stride=dil), :]

    for r in range(dil):
        q_all = rows_of(q_ref, r, 0, mb * SPAN) * scale
        k_all = jnp.concatenate([rows_of(kp_ref, r, 0, SPAN), rows_of(kc_ref, r, 0, mb * SPAN)], axis=0).astype(BF16)
        v_all = jnp.concatenate([rows_of(vp_ref, r, 0, SPAN), rows_of(vc_ref, r, 0, mb * SPAN)], axis=0).astype(BF16)
        for b in range(mb):
            qb = q_all[b * SPAN:(b + 1) * SPAN]
            kw = k_all[b * SPAN:(b + 2) * SPAN]
            vw = v_all[b * SPAN:(b + 2) * SPAN]
            msk = band_first if b == 0 else band
            o_tile = jnp.zeros((SPAN, LANES), F32)
            lse_tile = jnp.zeros((SPAN, LANES), F32)
            for hh in range(2):
                qm = jnp.where(halves[hh], qb, 0.0)
                if shared_kv:
                    shift = HEAD_DIM * ((kv_half + hh) % 2)
                    qm = pltpu.roll(qm, shift, 1)
                    vm = jnp.where(kv_mask, vw, jnp.zeros_like(vw))
                else:
                    vm = jnp.where(halves[hh], vw, jnp.zeros_like(vw))
                s = lax.dot_general(qm.astype(BF16), kw, NT_DIMS, preferred_element_type=F32)
                s = jnp.where(msk, s, -jnp.inf)
                m = jnp.max(s, axis=-1, keepdims=True)
                if has_sink:
                    sink = sink_ref[slab * 2 + hh]
                    m = jnp.maximum(m, sink)
                p = jnp.exp(s - m)
                den = jnp.sum(p, axis=-1, keepdims=True)
                if has_sink:
                    den = den + jnp.exp(sink - m)
                o = jnp.dot(p.astype(BF16), vm, preferred_element_type=F32) / den
                if shared_kv:
                    o = pltpu.roll(o, shift, 1)
                o_tile = o_tile + o
                if want_lse:
                    lse_tile = jnp.where(halves[hh], m + jnp.log(den), lse_tile)
            if dil == 1:
                o_ref[b * SPAN:(b + 1) * SPAN, :] = o_tile.astype(o_ref.dtype)
                if want_lse:
                    lse_ref[b * SPAN:(b + 1) * SPAN, :] = lse_tile
            else:
                o_ref[0, pl.ds(r + dil * b * SPAN, SPAN, stride=dil), :] = o_tile
                if want_lse:
                    lse_ref[0, pl.ds(r + dil * b * SPAN, SPAN, stride=dil), :] = lse_tile


def _band_attn(slabs, *, n_seq, seq_len, dil, q_slab, k_slab, v_slab, n_q_slabs, shared_kv=False, sink=None,
               want_lse, name):
    mb = max(1, 512 // (dil * SPAN))
    tp = dil * SPAN * mb
    nt = seq_len // tp
    assert seq_len % tp == 0
    prev_rows = dil * SPAN

    def cur(base, per_slab):
        return lambda n, t, s: (base + (s if per_slab else 0), n * nt + t, 0)

    def prev(base, per_slab):
        return lambda n, t, s: (base + (s if per_slab else 0), jnp.maximum((n * nt + t) * mb - 1, 0), 0)

    kv_per_slab = not shared_kv
    in_specs = [pl.BlockSpec((1, tp, LANES), cur(q_slab, True)),
                pl.BlockSpec((1, tp, LANES), cur(k_slab, kv_per_slab)),
                pl.BlockSpec((1, prev_rows, LANES), prev(k_slab, kv_per_slab)),
                pl.BlockSpec((1, tp, LANES), cur(v_slab, kv_per_slab)),
                pl.BlockSpec((1, prev_rows, LANES), prev(v_slab, kv_per_slab))]
    args = [slabs] * 5
    if sink is not None:
        in_specs.append(pl.BlockSpec(memory_space=pltpu.SMEM))
        args.append(sink.reshape(-1).astype(F32))
    rows = n_seq * seq_len
    if dil == 1:
        spec = pl.BlockSpec((tp, LANES), lambda n, t, s: (n * nt + t, s))
        out_shape = [jax.ShapeDtypeStruct((rows, n_q_slabs * LANES), BF16)]
        lse_shape = jax.ShapeDtypeStruct((rows, n_q_slabs * LANES), F32)
    else:
        spec = pl.BlockSpec((1, tp, LANES), lambda n, t, s: (s, n * nt + t, 0))
        out_shape = [jax.ShapeDtypeStruct((n_q_slabs, rows, LANES), F32)]
        lse_shape = jax.ShapeDtypeStruct((n_q_slabs, rows, LANES), F32)
    out_specs = [spec]
    if want_lse:
        out_shape.append(lse_shape)
        out_specs.append(spec)
    return pl.pallas_call(
        functools.partial(_band_body, dil=dil, mb=mb, shared_kv=shared_kv, has_sink=sink is not None,
                          want_lse=want_lse),
        out_shape=out_shape,
        grid=(n_seq, nt, n_q_slabs),
        in_specs=in_specs,
        out_specs=out_specs,
        compiler_params=_params("parallel", "parallel", "parallel"),
        name=name,
    )(*args)


def _combine_body(o0, l0, o1, l1, o2, l2, out_ref):
    def tile(ref):
        return ref[...].reshape(ref.shape[-2:]).astype(F32)

    a0, a1, a2 = tile(l0), tile(l1), tile(l2)
    m = jnp.maximum(jnp.maximum(a0, a1), a2)
    e0, e1, e2 = jnp.exp(a0 - m), jnp.exp(a1 - m), jnp.exp(a2 - m)
    num = e0 * tile(o0) + e1 * tile(o1) + e2 * tile(o2)
    out_ref[...] = (num / (e0 + e1 + e2)).astype(out_ref.dtype)


def _combine(pairs, name):
    first = pairs[0][0]
    rows, c = first.shape if first.ndim == 2 else (first.shape[1], first.shape[0] * LANES)
    tm = _pick(rows, (1024, 512, 256, 128, 64, 32, 16, 8))
    flat = pl.BlockSpec((tm, LANES), lambda i, s: (i, s))
    slab = pl.BlockSpec((1, tm, LANES), lambda i, s: (s, i, 0))
    args = [a for pair in pairs for a in pair]
    return pl.pallas_call(
        _combine_body,
        out_shape=jax.ShapeDtypeStruct((rows, c), BF16),
        grid=(rows // tm, c // LANES),
        in_specs=[flat if a.ndim == 2 else slab for a in args],
        out_specs=flat,
        compiler_params=_params("parallel", "parallel"),
        name=name,
    )(*args)


def _state_body(x_ref, o_ref):
    o_ref[0, 0] = x_ref[0].T


def _state_slabs(slabs, *, first_slab, n_slabs, n_seq, seq_len, keep, name):
    assert seq_len % keep == 0
    per_seq = seq_len // keep
    return pl.pallas_call(
        _state_body,
        out_shape=jax.ShapeDtypeStruct((n_seq, n_slabs, LANES, keep), F32),
        grid=(n_seq, n_slabs),
        in_specs=[pl.BlockSpec((1, keep, LANES), lambda n, s: (first_slab + s, (n + 1) * per_seq - 1, 0))],
        out_specs=pl.BlockSpec((1, 1, LANES, keep), lambda n, s: (n, s, 0, 0)),
        compiler_params=_params("parallel", "parallel"),
        name=name,
    )(slabs)


def _conv_prompt_body(gb_ref, gc_ref, xa_ref, gcp_ref, xap_ref, w_ref, a_ref, st_ref):
    t = pl.program_id(1)
    w = w_ref[...]
    for s in range(gb_ref.shape[0]):
        lanes = slice(s * LANES, (s + 1) * LANES)
        u = gc_ref[s] * xa_ref[s]
        up = jnp.where(t == 0, 0.0, gcp_ref[s] * xap_ref[s])
        ext = jnp.concatenate([up, u], axis=0)
        y = (w[0:1, lanes] * pltpu.roll(ext, 2, 0)[SUBLANES:]
             + w[1:2, lanes] * pltpu.roll(ext, 1, 0)[SUBLANES:]) + w[2:3, lanes] * u
        a_ref[:, lanes] = (gb_ref[s] * y).astype(a_ref.dtype)
        st_ref[0, :, lanes] = u[u.shape[0] - SUBLANES:]


def _conv_prompt(slabs, conv_w, *, n_seq, seq_len, c):
    tq = _pick(seq_len, (512, 256, 128))
    nt = seq_len // tq
    rb = tq // SUBLANES
    ns = c // LANES

    def cur(part):
        return lambda n, t: (part, n * nt + t, 0)

    def prev(part):
        return lambda n, t: (part, jnp.maximum((n * nt + t) * rb - 1, 0), 0)

    return pl.pallas_call(
        _conv_prompt_body,
        out_shape=[jax.ShapeDtypeStruct((n_seq * seq_len, c), BF16),
                   jax.ShapeDtypeStruct((n_seq, SUBLANES, c), F32)],
        grid=(n_seq, nt),
        in_specs=[pl.BlockSpec((ns, tq, LANES), cur(0)), pl.BlockSpec((ns, tq, LANES), cur(1)),
                  pl.BlockSpec((ns, tq, LANES), cur(2)),
                  pl.BlockSpec((ns, SUBLANES, LANES), prev(1)), pl.BlockSpec((ns, SUBLANES, LANES), prev(2)),
                  pl.BlockSpec((CONV_W, c), lambda n, t: (0, 0))],
        out_specs=[pl.BlockSpec((tq, c), lambda n, t: (n * nt + t, 0)),
                   pl.BlockSpec((1, SUBLANES, c), lambda n, t: (n, 0, 0))],
        compiler_params=_params("parallel", "arbitrary"),
        name="conv_prompt",
    )(slabs, slabs, slabs, slabs, slabs, conv_w)


def _conv_step_body(p_ref, prev_ref, w_ref, a_ref, st_ref):
    s_len = p_ref.shape[1]
    w = w_ref[...]
    hist = [prev_ref[k] for k in range(CONV_W - 1)] + [p_ref[1, s] * p_ref[2, s] for s in range(s_len)]
    for s in range(s_len):
        y = (w[0:1] * hist[s] + w[1:2] * hist[s + 1]) + w[2:3] * hist[s + 2]
        a_ref[s] = p_ref[0, s] * y
    for k in range(CONV_W - 1):
        st_ref[k] = hist[s_len + k]


def _conv_step(p3, prev, conv_w):
    _, s_len, n, c = p3.shape
    return pl.pallas_call(
        _conv_step_body,
        out_shape=[jax.ShapeDtypeStruct((s_len, n, c), F32), jax.ShapeDtypeStruct((CONV_W - 1, n, c), F32)],
        name="conv_step",
        compiler_params=pltpu.CompilerParams(vmem_limit_bytes=VMEM_LIMIT_BYTES),
    )(p3, prev, conv_w)


def _step_body(*refs, n_blk, qr, tok_div, w, dil, cw, bsz, has_sink, want_lse):
    q_ref, kvt_ref, cache_ref = refs[:3]
    pos = 3
    sink_ref = refs[pos] if has_sink else None
    pos += int(has_sink)
    o_ref = refs[pos]
    pos += 1
    lse_ref = refs[pos] if want_lse else None
    pos += int(want_lse)
    cout_ref = refs[pos]
    r_dim = n_blk * HEAD_DIM
    nrb = n_blk * qr
    n_chunks = w // cw
    scale = HEAD_DIM ** -0.5
    step = pl.program_id(0)
    lane_r = lax.broadcasted_iota(I32, (1, r_dim), 1)
    blk_masks = [(lane_r >= j * HEAD_DIM) & (lane_r < (j + 1) * HEAD_DIM) for j in range(n_blk)]
    row = lax.broadcasted_iota(I32, (nrb, 1), 0)
    tok = (row % qr) // tok_div
    lane_c = lax.broadcasted_iota(I32, (1, cw), 1)
    lane_n = lax.broadcasted_iota(I32, (1, LANES), 1)

    def valid(pos_l):
        ok = (pos_l >= tok) & (pos_l <= w + tok)
        if dil > 1:
            ok = ok & (((pos_l - tok) & (dil - 1)) == 0)
        return ok

    for b in range(bsz):
        off = ((step * bsz + b) * 4) % LANES
        shift = (LANES - off) % LANES
        new_k = pltpu.roll(kvt_ref[0], shift, 1)
        new_v = pltpu.roll(kvt_ref[1], shift, 1)
        q = q_ref[b] * scale
        qbd = jnp.concatenate([jnp.where(mk, q, 0.0) for mk in blk_masks], axis=0).astype(BF16)
        scores = []
        for c in range(n_chunks):
            lo, hi = c * cw, (c + 1) * cw
            kc = cache_ref[b, 0, :, lo:hi]
            sc = jnp.dot(qbd, kc.astype(BF16), preferred_element_type=F32)
            scores.append(jnp.where(valid(lane_c + lo), sc, -jnp.inf))
            nxt = cache_ref[b, 0, :, hi:hi + LANES] if c + 1 < n_chunks else new_k
            ext = jnp.concatenate([kc, nxt], axis=1)
            cout_ref[b, 0, :, lo:hi] = pltpu.roll(ext, cw + LANES - 4, 1)[:, :cw]
        sc = jnp.dot(qbd, new_k.astype(BF16), preferred_element_type=F32)
        scores.append(jnp.where(valid(lane_n + w), sc, -jnp.inf))
        m = functools.reduce(jnp.maximum, [jnp.max(s, axis=-1, keepdims=True) for s in scores])
        if has_sink:
            m = jnp.maximum(m, sink_ref[...])
        probs = [jnp.exp(s - m) for s in scores]
        den = functools.reduce(lambda a, c: a + c, [jnp.sum(p, axis=-1, keepdims=True) for p in probs])
        if has_sink:
            den = den + jnp.exp(sink_ref[...] - m)
        pv = lax.dot_general(probs[-1].astype(BF16), new_v.astype(BF16), NT_DIMS, preferred_element_type=F32)
        for c in range(n_chunks):
            lo, hi = c * cw, (c + 1) * cw
            vc = cache_ref[b, 1, :, lo:hi]
            pv = pv + lax.dot_general(probs[c].astype(BF16), vc.astype(BF16), NT_DIMS,
                                      preferred_element_type=F32)
            nxt = cache_ref[b, 1, :, hi:hi + LANES] if c + 1 < n_chunks else new_v
            ext = jnp.concatenate([vc, nxt], axis=1)
            cout_ref[b, 1, :, lo:hi] = pltpu.roll(ext, cw + LANES - 4, 1)[:, :cw]
        o = jnp.zeros((qr, r_dim), F32)
        m_e = jnp.zeros((qr, r_dim), F32)
        den_e = jnp.zeros((qr, r_dim), F32)
        for j, mk in enumerate(blk_masks):
            rows = slice(j * qr, (j + 1) * qr)
            o = jnp.where(mk, pv[rows], o)
            m_e = jnp.where(mk, m[rows], m_e)
            den_e = jnp.where(mk, den[rows], den_e)
        o_ref[b] = o / den_e
        if want_lse:
            lse_ref[b] = m_e + jnp.log(den_e)


def _step_attn(q, kvt, cache, *, dil, tok_div, sink_col=None, want_lse, bsz, name):
    n, qr, r_dim = q.shape
    w = cache.shape[-1]
    n_blk = r_dim // HEAD_DIM
    cw = min(w, 512)
    in_specs = [pl.BlockSpec((bsz, qr, r_dim), lambda i: (i, 0, 0)),
                pl.BlockSpec((2, r_dim, LANES), lambda i: (0, 0, (i * bsz * 4) // LANES)),
                pl.BlockSpec((bsz, 2, r_dim, w), lambda i: (i, 0, 0, 0))]
    args = [q, kvt, cache]
    if sink_col is not None:
        in_specs.append(pl.BlockSpec((n_blk * qr, 1), lambda i: (0, 0)))
        args.append(sink_col)
    o_spec = pl.BlockSpec((bsz, qr, r_dim), lambda i: (i, 0, 0))
    out_shape = [jax.ShapeDtypeStruct((n, qr, r_dim), F32)]
    out_specs = [o_spec]
    if want_lse:
        out_shape.append(jax.ShapeDtypeStruct((n, qr, r_dim), F32))
        out_specs.append(o_spec)
    out_shape.append(jax.ShapeDtypeStruct(cache.shape, F32))
    out_specs.append(pl.BlockSpec((bsz, 2, r_dim, w), lambda i: (i, 0, 0, 0)))
    return pl.pallas_call(
        functools.partial(_step_body, n_blk=n_blk, qr=qr, tok_div=tok_div, w=w, dil=dil, cw=cw, bsz=bsz,
                          has_sink=sink_col is not None, want_lse=want_lse),
        out_shape=out_shape,
        grid=(n // bsz,),
        in_specs=in_specs,
        out_specs=out_specs,
        compiler_params=_params("parallel"),
        name=name,
    )(*args)


def _router_body(h_ref, g_ref, wr_ref, xn_ref, im_ref, gm_ref, cnt_ref, carry_ref, *, n_exp):
    i = pl.program_id(0)

    @pl.when(i == 0)
    def _():
        carry_ref[...] = jnp.zeros_like(carry_ref)

    xn = _rms(h_ref[...], g_ref[...])
    xn_ref[...] = xn
    tm = xn.shape[0]
    wr = wr_ref[...]
    xh = xn.astype(BF16)
    xl = (xn - xh.astype(F32)).astype(BF16)
    wh = wr.astype(BF16)
    wl = (wr - wh.astype(F32)).astype(BF16)
    lg = jnp.dot(xh, wh, preferred_element_type=F32) + (
        jnp.dot(xh, wl, preferred_element_type=F32) + jnp.dot(xl, wh, preferred_element_type=F32))
    lane = lax.broadcasted_iota(I32, (tm, LANES), 1)
    lane_f = lane.astype(F32)
    lg = jnp.where(lane < n_exp, lg, -jnp.inf)
    m1 = jnp.max(lg, axis=-1, keepdims=True)
    i1 = jnp.min(jnp.where(lg == m1, lane_f, float(LANES)), axis=-1, keepdims=True)
    lg2 = jnp.where(lane_f == i1, -jnp.inf, lg)
    m2 = jnp.max(lg2, axis=-1, keepdims=True)
    i2 = jnp.min(jnp.where(lg2 == m2, lane_f, float(LANES)), axis=-1, keepdims=True)
    e = jnp.exp(m2 - m1)
    g1 = 1.0 / (1.0 + e)
    g2 = e / (1.0 + e)
    sel1 = lane_f == i1
    sel2 = lane_f == i2
    onehot = jnp.where(sel1 | sel2, 1.0, 0.0)
    r_i = lax.broadcasted_iota(I32, (tm, tm), 0)
    c_i = lax.broadcasted_iota(I32, (tm, tm), 1)
    tri = jnp.where(c_i < r_i, 1.0, 0.0).astype(BF16)
    before = jnp.dot(tri, onehot.astype(BF16), preferred_element_type=F32) + carry_ref[0:1]
    r1 = jnp.sum(jnp.where(sel1, before, 0.0), axis=-1, keepdims=True)
    r2 = jnp.sum(jnp.where(sel2, before, 0.0), axis=-1, keepdims=True)
    total = carry_ref[0:1] + jnp.sum(onehot, axis=0, keepdims=True)
    carry_ref[...] = jnp.broadcast_to(total, carry_ref.shape)
    cnt_ref[...] = jnp.broadcast_to(total, cnt_ref.shape).astype(I32)
    meta = jnp.where(lane == 0, i1, jnp.where(lane == 1, i2, jnp.where(lane == 2, r1, jnp.where(lane == 3, r2, 0.0))))
    im_ref[...] = meta.astype(I32)
    gm_ref[...] = jnp.where(lane == 0, g1, jnp.where(lane == 1, g2, 0.0))


def _router(h, gain, w_router):
    t, d = h.shape
    n_exp = w_router.shape[1]
    tm = _pick(t, (512, 256, 128, 64, 32, 16, 8))
    wr = jnp.zeros((d, LANES), F32).at[:, :n_exp].set(w_router)
    return pl.pallas_call(
        functools.partial(_router_body, n_exp=n_exp),
        out_shape=[jax.ShapeDtypeStruct((t, d), F32), jax.ShapeDtypeStruct((t, LANES), I32),
                   jax.ShapeDtypeStruct((t, LANES), F32), jax.ShapeDtypeStruct((SUBLANES, LANES), I32)],
        grid=(t // tm,),
        in_specs=[pl.BlockSpec((tm, d), lambda i: (i, 0)),
                  pl.BlockSpec((1, d), lambda i: (0, 0)),
                  pl.BlockSpec((d, LANES), lambda i: (0, 0))],
        out_specs=[pl.BlockSpec((tm, d), lambda i: (i, 0)),
                   pl.BlockSpec((tm, LANES), lambda i: (i, 0)),
                   pl.BlockSpec((tm, LANES), lambda i: (i, 0)),
                   pl.BlockSpec((SUBLANES, LANES), lambda i: (0, 0))],
        scratch_shapes=[pltpu.VMEM((SUBLANES, LANES), F32)],
        compiler_params=_params("arbitrary"),
        name="moe_router",
    )(h, gain.reshape(1, -1), wr)


def _dispatch_body(s1_ref, s2_ref, x_ref, zero_ref, out_ref, sem, *, tm):
    del zero_ref

    def issue(r, carry):
        src = x_ref.at[pl.ds(r, 1)]
        pltpu.make_async_copy(src, out_ref.at[pl.ds(s1_ref[0, 0, r], 1)], sem.at[0]).start()
        pltpu.make_async_copy(src, out_ref.at[pl.ds(s2_ref[0, 0, r], 1)], sem.at[1]).start()
        return carry

    lax.fori_loop(0, tm, issue, 0, unroll=8)
    pltpu.make_async_copy(x_ref, out_ref.at[pl.ds(0, tm)], sem.at[0]).wait()
    pltpu.make_async_copy(x_ref, out_ref.at[pl.ds(0, tm)], sem.at[1]).wait()


def _dispatch(xn, slot1, slot2, n_slots):
    t, d = xn.shape
    tm = _pick(t, (512, 256, 128, 64, 32, 16, 8))
    nt = t // tm
    smem = pl.BlockSpec((1, 1, tm), lambda i: (i, 0, 0), memory_space=pltpu.SMEM)
    return pl.pallas_call(
        functools.partial(_dispatch_body, tm=tm),
        out_shape=jax.ShapeDtypeStruct((n_slots, d), F32),
        grid=(nt,),
        in_specs=[smem, smem, pl.BlockSpec((tm, d), lambda i: (i, 0)), pl.BlockSpec(memory_space=pl.ANY)],
        out_specs=pl.BlockSpec(memory_space=pl.ANY),
        scratch_shapes=[pltpu.SemaphoreType.DMA((2,))],
        input_output_aliases={3: 0},
        compiler_params=_params("arbitrary"),
        name="moe_dispatch",
    )(slot1.reshape(nt, 1, tm), slot2.reshape(nt, 1, tm), xn, jnp.zeros((n_slots, d), F32))


def _experts_body(te_ref, nu_ref, x_ref, wg_ref, wu_ref, wd_ref, o_ref, xb_ref, acc_ref, *, nj):
    del te_ref
    i = pl.program_id(0)
    j = pl.program_id(1)
    active = i < nu_ref[0]

    @pl.when(active & (j == 0))
    def _():
        xb_ref[...] = x_ref[...].astype(BF16)
        acc_ref[...] = jnp.zeros_like(acc_ref)

    @pl.when(active)
    def _():
        xb = xb_ref[...]
        gate = jnp.dot(xb, wg_ref[0].astype(BF16), preferred_element_type=F32)
        up = jnp.dot(xb, wu_ref[0].astype(BF16), preferred_element_type=F32)
        act = (gate * jax.nn.sigmoid(gate) * up).astype(BF16)
        acc_ref[...] += jnp.dot(act, wd_ref[0].astype(BF16), preferred_element_type=F32)

    @pl.when(active & (j == nj - 1))
    def _():
        o_ref[...] = acc_ref[...]

    @pl.when(jnp.logical_not(active) & (j == nj - 1))
    def _():
        o_ref[...] = jnp.zeros_like(o_ref)


def _experts(xs, tile_expert, n_used, w_gate, w_up, w_down, tm):
    n_slots, d = xs.shape
    f = w_gate.shape[2]
    tf = _pick(f, (512, 256, 128))
    nj = f // tf
    n_tiles = n_slots // tm

    def act_j(i, j, nu):
        return jnp.where(i < nu[0], j, nj - 1)

    def row(i, nu):
        return jnp.minimum(i, nu[0] - 1)

    grid_spec = pltpu.PrefetchScalarGridSpec(
        num_scalar_prefetch=2,
        grid=(n_tiles, nj),
        in_specs=[pl.BlockSpec((tm, d), lambda i, j, te, nu: (row(i, nu), 0)),
                  pl.BlockSpec((1, d, tf), lambda i, j, te, nu: (te[i], 0, act_j(i, j, nu))),
                  pl.BlockSpec((1, d, tf), lambda i, j, te, nu: (te[i], 0, act_j(i, j, nu))),
                  pl.BlockSpec((1, tf, d), lambda i, j, te, nu: (te[i], act_j(i, j, nu), 0))],
        out_specs=pl.BlockSpec((tm, d), lambda i, j, te, nu: (i, 0)),
        scratch_shapes=[pltpu.VMEM((tm, d), BF16), pltpu.VMEM((tm, d), F32)],
    )
    return pl.pallas_call(
        functools.partial(_experts_body, nj=nj),
        out_shape=jax.ShapeDtypeStruct((n_slots, d), F32),
        grid_spec=grid_spec,
        compiler_params=_params("arbitrary", "arbitrary"),
        name="moe_experts",
    )(tile_expert, n_used, xs, w_gate, w_up, w_down)


def _gather_norm_body(s1_ref, s2_ref, h_ref, gm_ref, g_ref, ys_ref, oa_ref, ob_ref, ya_ref, yb_ref, sem, *,
                      tm, head_tiles):
    def issue(r, carry):
        pltpu.make_async_copy(ys_ref.at[pl.ds(s1_ref[0, 0, r], 1)], ya_ref.at[pl.ds(r, 1)], sem.at[0]).start()
        pltpu.make_async_copy(ys_ref.at[pl.ds(s2_ref[0, 0, r], 1)], yb_ref.at[pl.ds(r, 1)], sem.at[1]).start()
        return carry

    lax.fori_loop(0, tm, issue, 0, unroll=8)
    pltpu.make_async_copy(ys_ref.at[pl.ds(0, tm)], ya_ref, sem.at[0]).wait()
    pltpu.make_async_copy(ys_ref.at[pl.ds(0, tm)], yb_ref, sem.at[1]).wait()
    gm = gm_ref[...]
    y = _rms(h_ref[...] + (gm[:, 0:1] * ya_ref[...] + gm[:, 1:2] * yb_ref[...]), g_ref[...])
    i = pl.program_id(0)

    @pl.when(i < head_tiles)
    def _():
        oa_ref[...] = y

    @pl.when(i >= head_tiles)
    def _():
        ob_ref[...] = y


def _gather_norm(h, gates, slot1, slot2, ys, gain, split):
    t, d = h.shape
    tm = _pick(split, (512, 256, 128, 64, 32, 16, 8))
    assert t % tm == 0
    nt = t // tm
    head_tiles = split // tm
    smem = pl.BlockSpec((1, 1, tm), lambda i: (i, 0, 0), memory_space=pltpu.SMEM)
    return pl.pallas_call(
        functools.partial(_gather_norm_body, tm=tm, head_tiles=head_tiles),
        out_shape=[jax.ShapeDtypeStruct((split, d), F32), jax.ShapeDtypeStruct((t - split, d), F32)],
        grid=(nt,),
        in_specs=[smem, smem, pl.BlockSpec((tm, d), lambda i: (i, 0)),
                  pl.BlockSpec((tm, LANES), lambda i: (i, 0)),
                  pl.BlockSpec((1, d), lambda i: (0, 0)),
                  pl.BlockSpec(memory_space=pl.ANY)],
        out_specs=[pl.BlockSpec((tm, d), lambda i: (jnp.minimum(i, head_tiles - 1), 0)),
                   pl.BlockSpec((tm, d), lambda i: (jnp.maximum(i - head_tiles, 0), 0))],
        scratch_shapes=[pltpu.VMEM((tm, d), F32), pltpu.VMEM((tm, d), F32), pltpu.SemaphoreType.DMA((2,))],
        compiler_params=_params("arbitrary"),
        name="moe_gather_norm",
    )(slot1.reshape(nt, 1, tm), slot2.reshape(nt, 1, tm), h, gates, gain.reshape(1, -1), ys)


def _moe(h, gain, w_router, w_gate, w_up, w_down, final_gain, split):
    t, _ = h.shape
    n_exp = w_router.shape[1]
    tm_e = 1024
    xn, imeta, gates, counts = _router(h, gain, w_router)
    idx1, idx2, rank1, rank2 = imeta[:, 0], imeta[:, 1], imeta[:, 2], imeta[:, 3]
    cnt = counts[0, :n_exp]
    padded = ((cnt + tm_e - 1) // tm_e) * tm_e
    ends = jnp.cumsum(padded)
    starts = ends - padded
    slot1 = starts[idx1] + rank1
    slot2 = starts[idx2] + rank2
    n_tiles = (TOP_K * t + n_exp * (tm_e - 1)) // tm_e
    n_used = (ends[-1] // tm_e).astype(I32)
    tile_start = jnp.arange(n_tiles, dtype=I32) * tm_e
    tile_expert = jnp.minimum(jnp.searchsorted(ends, tile_start, side="right"), n_exp - 1).astype(I32)
    last = tile_expert[jnp.maximum(n_used - 1, 0)]
    tile_expert = jnp.where(jnp.arange(n_tiles) < n_used, tile_expert, last)
    xs = _dispatch(xn, slot1, slot2, n_tiles * tm_e)
    ys = _experts(xs, tile_expert, n_used.reshape(1), w_gate, w_up, w_down, tm_e)
    return _gather_norm(h, gates, slot1, slot2, ys, final_gain, split)


def _cache_to_slabs(cache):
    n, w, two, heads, hd = cache.shape
    return jnp.transpose(cache, (0, 2, 3, 4, 1)).reshape(n, two, heads * hd, w)


def _slabs_to_cache(slabs, heads):
    n, two, _, w = slabs.shape
    return jnp.transpose(slabs.reshape(n, two, heads, HEAD_DIM, w), (0, 4, 1, 2, 3))


def kernel(x_prompt, x_sample, cache_conv, cache_swa_kv, cache_dil_kv0, cache_dil_kv1, cache_dil_kv2, norm_mix0, w_in0, conv_w, swa_sink, w_out0, norm_ffn0, w_gate0, w_up0, w_down0, norm_mix1, w_in1, w_out1, norm_ffn1, w_router, w_gate1, w_up1, w_down1, norm_final):
    n_p, seq, d = x_prompt.shape
    n_s, s_len, _ = x_sample.shape
    tp, ts = n_p * seq, n_s * s_len
    d_conv = conv_w.shape[2]
    kvh, grp = swa_sink.shape[1], swa_sink.shape[2]
    hq0 = kvh * grp
    h1 = cache_dil_kv0.shape[4]
    c_q0, c_kv0 = hq0 * HEAD_DIM, kvh * HEAD_DIM
    q0_col = 3 * d_conv
    k0_col = q0_col + c_q0
    c_g = 3 * h1 * HEAD_DIM
    dil_caches = (cache_dil_kv0, cache_dil_kv1, cache_dil_kv2)

    h = jnp.concatenate([x_prompt.reshape(tp, d), x_sample.reshape(ts, d)], axis=0)

    assert c_kv0 == LANES and grp % 2 == 0 and d_conv % LANES == 0
    nsc = d_conv // LANES
    q0_slab = 3 * nsc
    k0_slab = q0_slab + c_q0 // LANES
    proj0 = _dense([h], w_in0[0], gain=norm_mix0[0], out_dtype=F32, name="in_proj0", slab_out=True)
    a_p, conv_tail = _conv_prompt(proj0, conv_w[0], n_seq=n_p, seq_len=seq, c=d_conv)
    att_p = _band_attn(proj0, n_seq=n_p, seq_len=seq, dil=1, q_slab=q0_slab, k_slab=k0_slab, v_slab=k0_slab + 1,
                       n_q_slabs=c_q0 // LANES, shared_kv=True, sink=swa_sink[0], want_lse=False,
                       name="swa_prompt")[0]
    proj0_s = proj0[:, tp:]
    p3 = proj0_s[:q0_slab].reshape(3, nsc, n_s, s_len, LANES)
    p3 = jnp.transpose(p3, (0, 3, 2, 1, 4)).reshape(3, s_len, n_s, d_conv)
    a_s, conv_new = _conv_step(p3, jnp.transpose(cache_conv[0], (1, 0, 2)), conv_w[0])
    a_s = jnp.transpose(a_s, (1, 0, 2)).reshape(ts, d_conv)
    kvt0 = _proj_t(jnp.transpose(w_in0[0][:, k0_col:]), h[tp:], norm_mix0[0], "kv_t0")
    q_s = proj0_s[q0_slab:k0_slab].reshape(kvh, grp // 2, n_s, s_len, 2, HEAD_DIM)
    q_s = jnp.transpose(q_s, (2, 3, 1, 4, 0, 5)).reshape(n_s, s_len * grp, c_kv0)
    sink_col = jnp.broadcast_to(swa_sink[0][:, None, :], (kvh, s_len, grp)).reshape(kvh * s_len * grp, 1)
    o_s, swa_new = _step_attn(q_s, kvt0.reshape(2, c_kv0, ts), _cache_to_slabs(cache_swa_kv[0]), dil=1,
                              tok_div=grp, sink_col=sink_col, want_lse=False, bsz=8, name="swa_step")
    att_s = jnp.transpose(o_s.reshape(n_s, s_len, grp, kvh, HEAD_DIM), (0, 1, 3, 2, 4)).reshape(ts, c_q0)
    a_all = jnp.concatenate([a_p, a_s.astype(BF16)], axis=0)
    att_all = jnp.concatenate([att_p, att_s.astype(BF16)], axis=0)
    h = _dense([a_all, att_all], w_out0[0], res=h, out_dtype=F32, name="out_proj0")
    h = _ffn(h, norm_ffn0[0], w_gate0[0], w_up0[0], w_down0[0], "ffn0")

    proj1 = _dense([h], w_in1[0], gain=norm_mix1[0], out_dtype=F32, name="in_proj1", slab_out=True)
    proj1_s = proj1[:, tp:]
    c_h = h1 * HEAD_DIM
    nsh = c_h // LANES
    pairs_p, pairs_s, dil_p, dil_s = [], [], [], []
    for g, dil in enumerate(DILATIONS):
        window = dil_caches[g].shape[2]
        s0 = g * 3 * nsh
        pairs_p.append(tuple(_band_attn(proj1, n_seq=n_p, seq_len=seq, dil=dil, q_slab=s0, k_slab=s0 + nsh,
                                        v_slab=s0 + 2 * nsh, n_q_slabs=nsh, want_lse=True,
                                        name=f"dil{g}_prompt")))
        keep = min(window, seq)
        state = _state_slabs(proj1, first_slab=s0 + nsh, n_slabs=2 * nsh, n_seq=n_p, seq_len=seq, keep=keep,
                             name=f"dil{g}_state")
        dil_p.append(_slabs_to_cache(state.reshape(n_p, 2, c_h, keep), h1)[None])
        lo = g * c_g
        q_g = jnp.transpose(proj1_s[s0:s0 + nsh].reshape(nsh, n_s, s_len, LANES), (1, 2, 0, 3))
        q_g = q_g.reshape(n_s, s_len, c_h)
        q_g = jnp.concatenate([q_g, jnp.zeros((n_s, SUBLANES - s_len, c_h), F32)], axis=1)
        kvt = _proj_t(jnp.transpose(w_in1[0][:, lo + c_h:lo + c_g]), h[tp:], norm_mix1[0], f"kv_t1_{g}")
        o_sg, l_sg, cache_new = _step_attn(q_g, kvt.reshape(2, c_h, ts), _cache_to_slabs(dil_caches[g][0]),
                                           dil=dil, tok_div=1, want_lse=True,
                                           bsz=max(1, 2048 // window), name=f"dil{g}_step")
        pairs_s.append((o_sg.reshape(n_s * SUBLANES, c_h), l_sg.reshape(n_s * SUBLANES, c_h)))
        dil_s.append(_slabs_to_cache(cache_new, h1)[None])
    comb_p = _combine(pairs_p, "combine_prompt")
    comb_s = _combine(pairs_s, "combine_step").reshape(n_s, SUBLANES, c_h)[:, :s_len].reshape(ts, c_h)
    h = _dense([jnp.concatenate([comb_p, comb_s], axis=0)], w_out1[0], res=h, out_dtype=F32, name="out_proj1")
    y_p, y_s = _moe(h, norm_ffn1[0], w_router[0], w_gate1[0], w_up1[0], w_down1[0], norm_final, tp)

    y_prompt = y_p.reshape(n_p, seq, d)
    y_sample = y_s.reshape(n_s, s_len, d)
    new_conv_prompt = conv_tail[:, SUBLANES - (CONV_W - 1):][None]
    new_conv_sample = jnp.transpose(conv_new, (1, 0, 2))[None]
    keep0 = min(SPAN, seq)
    swa_state = _state_slabs(proj0, first_slab=k0_slab, n_slabs=2, n_seq=n_p, seq_len=seq, keep=keep0,
                             name="swa_state")
    new_swa_kv_prompt = _slabs_to_cache(swa_state, kvh)[None]
    new_swa_kv_sample = _slabs_to_cache(swa_new, kvh)[None]
    return (y_prompt, y_sample, new_conv_prompt, new_conv_sample, new_swa_kv_prompt, new_swa_kv_sample,
            dil_p[0], dil_s[0], dil_p[1], dil_s[1], dil_p[2], dil_s[2])
```

```python
import functools

import jax
import jax.numpy as jnp
from jax import lax
from jax.experimental import pallas as pl
from jax.experimental.pallas import tpu as pltpu

F32 = jnp.float32
BF16 = jnp.bfloat16
I32 = jnp.int32

EPS = 1e-5
HEAD_DIM = 64
SPAN = 128
CONV_W = 3
DILATIONS = (1, 4, 16)
TOP_K = 2
LANES = 128
SUBLANES = 8
VMEM_LIMIT_BYTES = 56 * 1024 * 1024
NT_DIMS = (((1,), (1,)), ((), ()))


def _params(*sem):
    return pltpu.CompilerParams(dimension_semantics=sem, vmem_limit_bytes=VMEM_LIMIT_BYTES)


def _pick(n, candidates):
    for c in candidates:
        if n % c == 0:
            return c
    raise ValueError(f"no tile for {n} in {candidates}")


def _rms(x, g):
    y = x * lax.rsqrt(jnp.mean(x * x, axis=-1, keepdims=True) + EPS)
    return y * g


def _load_bf16(w_hbm, w_vmem, stage, sem, *, axis, chunk):
    n_chunks = w_hbm.shape[axis] // chunk

    def window(ref, c):
        return ref.at[pl.ds(c * chunk, chunk), :] if axis == 0 else ref.at[:, pl.ds(c * chunk, chunk)]

    def copy(c):
        return pltpu.make_async_copy(window(w_hbm, c), stage.at[c % 2], sem.at[c % 2])

    copy(0).start()
    for c in range(n_chunks):
        if c + 1 < n_chunks:
            copy(c + 1).start()
        copy(c).wait()
        if axis == 0:
            w_vmem[c * chunk:(c + 1) * chunk, :] = stage[c % 2].astype(BF16)
        else:
            w_vmem[:, c * chunk:(c + 1) * chunk] = stage[c % 2].astype(BF16)


def _dense_body(*refs, ks, split, head_tiles, has_gain, has_res, slab_out, cn):
    n_in = len(ks)
    step = pl.program_id(0)
    refs = list(refs)

    def take(is_split):
        count = 2 if is_split else 1
        parts = tuple(refs[:count])
        del refs[:count]
        return parts

    def read(parts, cols=slice(None)):
        if len(parts) == 1:
            return parts[0][:, cols]
        return jnp.where(step < head_tiles, parts[0][:, cols], parts[1][:, cols])

    xs = [take(s) for s in split[:n_in]]
    g_ref = refs.pop(0) if has_gain else None
    w_hbm = refs.pop(0)
    res = take(split[n_in]) if has_res else None
    o_ref, w_ref, stage, sem = refs

    @pl.when(step == 0)
    def _():
        _load_bf16(w_hbm, w_ref, stage, sem, axis=1, chunk=cn)

    if has_gain:
        lhs = [_rms(read(xs[0]), g_ref[...]).astype(BF16)]
    else:
        lhs = [read(x).astype(BF16) for x in xs]
    n = w_ref.shape[1]
    for c in range(n // cn):
        cols = slice(c * cn, (c + 1) * cn)
        acc = None
        row0 = 0
        for a, k in zip(lhs, ks):
            d = jnp.dot(a, w_ref[row0:row0 + k, cols], preferred_element_type=F32)
            acc = d if acc is None else acc + d
            row0 += k
        if has_res:
            acc = read(res, cols) + acc
        if slab_out:
            for s in range(cn // LANES):
                o_ref[c * (cn // LANES) + s] = acc[:, s * LANES:(s + 1) * LANES].astype(o_ref.dtype)
        else:
            o_ref[:, cols] = acc.astype(o_ref.dtype)


def _dense(xs, w, *, gain=None, res=None, out_dtype, name, slab_out=False):
    def n_rows(a):
        return sum(p.shape[0] for p in a) if isinstance(a, tuple) else a.shape[0]

    def width(a):
        return a[0].shape[1] if isinstance(a, tuple) else a.shape[1]

    operands = list(xs) + ([res] if res is not None else [])
    pairs = [a for a in operands if isinstance(a, tuple)]
    t = n_rows(xs[0])
    k_all, n = w.shape
    ks = tuple(width(x) for x in xs)
    assert sum(ks) == k_all
    gcd_rows = t
    for a in pairs:
        gcd_rows = min(gcd_rows, a[1].shape[0])
    tm = _pick(gcd_rows, (512, 256, 128, 64, 32, 16, 8))
    assert t % tm == 0
    head_tiles = pairs[0][0].shape[0] // tm if pairs else 0
    for a in pairs:
        assert a[0].shape[0] == head_tiles * tm
    cn = _pick(n, (512, 384, 256, 128))
    in_specs, args = [], []

    def add_rows(a, k):
        if isinstance(a, tuple):
            in_specs.append(pl.BlockSpec((tm, k), lambda i: (jnp.minimum(i, head_tiles - 1), 0)))
            in_specs.append(pl.BlockSpec((tm, k), lambda i: (jnp.maximum(i - head_tiles, 0), 0)))
            args.extend(a)
        else:
            in_specs.append(pl.BlockSpec((tm, k), lambda i: (i, 0)))
            args.append(a)

    for x, k in zip(xs, ks):
        add_rows(x, k)
    if gain is not None:
        in_specs.append(pl.BlockSpec((1, ks[0]), lambda i: (0, 0)))
        args.append(gain.reshape(1, -1))
    in_specs.append(pl.BlockSpec(memory_space=pl.ANY))
    args.append(w)
    if res is not None:
        add_rows(res, n)
    split = tuple(isinstance(a, tuple) for a in operands)
    if slab_out:
        out_shape = jax.ShapeDtypeStruct((n // LANES, t, LANES), out_dtype)
        out_spec = pl.BlockSpec((n // LANES, tm, LANES), lambda i: (0, i, 0))
    else:
        out_shape = jax.ShapeDtypeStruct((t, n), out_dtype)
        out_spec = pl.BlockSpec((tm, n), lambda i: (i, 0))
    return pl.pallas_call(
        functools.partial(_dense_body, ks=ks, split=split, head_tiles=head_tiles, has_gain=gain is not None,
                          has_res=res is not None, slab_out=slab_out, cn=cn),
        out_shape=out_shape,
        grid=(t // tm,),
        in_specs=in_specs,
        out_specs=out_spec,
        scratch_shapes=[pltpu.VMEM((k_all, n), BF16), pltpu.VMEM((2, k_all, cn), F32),
                        pltpu.SemaphoreType.DMA((2,))],
        compiler_params=_params("arbitrary"),
        name=name,
    )(*args)


def _proj_t_body(w_ref, x_ref, g_ref, o_ref):
    xn = _rms(x_ref[...], g_ref[...]).astype(BF16)
    o_ref[...] = lax.dot_general(w_ref[...].astype(BF16), xn, NT_DIMS, preferred_element_type=F32)


def _proj_t(w_t, x, gain, name):
    c, k = w_t.shape
    rows = x.shape[0]
    tc = _pick(c, (512, 256, 128))
    return pl.pallas_call(
        _proj_t_body,
        out_shape=jax.ShapeDtypeStruct((c, rows), F32),
        grid=(c // tc,),
        in_specs=[pl.BlockSpec((tc, k), lambda i: (i, 0)),
                  pl.BlockSpec((rows, k), lambda i: (0, 0)),
                  pl.BlockSpec((1, k), lambda i: (0, 0))],
        out_specs=pl.BlockSpec((tc, rows), lambda i: (i, 0)),
        compiler_params=_params("parallel"),
        name=name,
    )(w_t, x, gain.reshape(1, -1))


def _ffn_body(x_ref, g_ref, wg_hbm, wu_hbm, wd_hbm, o_ref, wg_ref, wu_ref, wd_ref, stage_in, stage_out, sem, *,
              tf, ld):
    @pl.when(pl.program_id(0) == 0)
    def _():
        _load_bf16(wg_hbm, wg_ref, stage_in, sem, axis=1, chunk=ld)
        _load_bf16(wu_hbm, wu_ref, stage_in, sem, axis=1, chunk=ld)
        _load_bf16(wd_hbm, wd_ref, stage_out, sem, axis=0, chunk=ld)

    x = x_ref[...]
    xn = _rms(x, g_ref[...]).astype(BF16)
    acc = None
    for c in range(wg_ref.shape[1] // tf):
        cols = slice(c * tf, (c + 1) * tf)
        gate = jnp.dot(xn, wg_ref[:, cols], preferred_element_type=F32)
        up = jnp.dot(xn, wu_ref[:, cols], preferred_element_type=F32)
        act = (gate * jax.nn.sigmoid(gate) * up).astype(BF16)
        d = jnp.dot(act, wd_ref[cols, :], preferred_element_type=F32)
        acc = d if acc is None else acc + d
    o_ref[...] = x + acc


def _ffn(x, gain, w_gate, w_up, w_down, name):
    t, d = x.shape
    f = w_gate.shape[1]
    tm = _pick(t, (512, 256, 128, 64, 32, 16, 8))
    tf = f
    ld = _pick(f, (256, 128))
    anywhere = pl.BlockSpec(memory_space=pl.ANY)
    return pl.pallas_call(
        functools.partial(_ffn_body, tf=tf, ld=ld),
        out_shape=jax.ShapeDtypeStruct((t, d), F32),
        grid=(t // tm,),
        in_specs=[pl.BlockSpec((tm, d), lambda i: (i, 0)),
                  pl.BlockSpec((1, d), lambda i: (0, 0)),
                  anywhere, anywhere, anywhere],
        out_specs=pl.BlockSpec((tm, d), lambda i: (i, 0)),
        scratch_shapes=[pltpu.VMEM((d, f), BF16), pltpu.VMEM((d, f), BF16), pltpu.VMEM((f, d), BF16),
                        pltpu.VMEM((2, d, ld), F32), pltpu.VMEM((2, ld, d), F32),
                        pltpu.SemaphoreType.DMA((2,))],
        compiler_params=_params("arbitrary"),
        name=name,
    )(x, gain.reshape(1, -1), w_gate, w_up, w_down)


def _band_body(*refs, dil, mb, shared_kv, has_sink, want_lse):
    q_ref, kc_ref, kp_ref, vc_ref, vp_ref = refs[:5]
    pos = 5
    sink_ref = refs[pos] if has_sink else None
    pos += int(has_sink)
    o_ref = refs[pos]
    lse_ref = refs[pos + 1] if want_lse else None
    t = pl.program_id(1)
    scale = HEAD_DIM ** -0.5
    qi = lax.broadcasted_iota(I32, (2 * SPAN, 2 * SPAN), 0) & (SPAN - 1)
    kj = lax.broadcasted_iota(I32, (2 * SPAN, 2 * SPAN), 1)
    band = (kj >= qi) & (kj <= qi + SPAN)
    band_first = band & (kj >= jnp.where(t == 0, SPAN, 0))
    lane = lax.broadcasted_iota(I32, (1, LANES), 1)
    halves = [lane < HEAD_DIM, lane >= HEAD_DIM]

    def rows_of(ref, s, r, count):
        if dil == 1:
            return ref[s, 0:count, :]
        return ref[s, pl.ds(r, count, stride=dil), :]

    def pair(q_a, mask_a, q_b, mask_b, kw, vw, msk, heads):
        q2 = jnp.concatenate([jnp.where(mask_a, q_a, 0.0), jnp.where(mask_b, q_b, 0.0)], axis=0).astype(BF16)
        s = lax.dot_general(q2, kw, NT_DIMS, preferred_element_type=F32)
        s = jnp.where(msk, s, -jnp.inf)
        m = jnp.max(s, axis=-1, keepdims=True)
        if has_sink:
            sink = jnp.concatenate([jnp.full((SPAN, 1), sink_ref[h], F32) for h in heads], axis=0)
            m = jnp.maximum(m, sink)
        p = jnp.exp(s - m)
        den = jnp.sum(p, axis=-1, keepdims=True)
        if has_sink:
            den = den + jnp.exp(sink - m)
        o = jnp.dot(p.astype(BF16), vw, preferred_element_type=F32) / den
        return o, m + jnp.log(den)

    def store(s, r, b, o_tile, lse_tile):
        if dil == 1:
            o_ref[b * SPAN:(b + 1) * SPAN, s * LANES:(s + 1) * LANES] = o_tile.astype(o_ref.dtype)
            if want_lse:
                lse_ref[b * SPAN:(b + 1) * SPAN, s * LANES:(s + 1) * LANES] = lse_tile
        else:
            o_ref[s, pl.ds(r + dil * b * SPAN, SPAN, stride=dil), :] = o_tile
            if want_lse:
                lse_ref[s, pl.ds(r + dil * b * SPAN, SPAN, stride=dil), :] = lse_tile

    n_slabs = q_ref.shape[0]
    first_head = 2 * n_slabs * pl.program_id(2)
    top, bot = slice(0, SPAN), slice(SPAN, 2 * SPAN)
    for r in range(dil):
        qs = [rows_of(q_ref, s, r, mb * SPAN) * scale for s in range(n_slabs)]
        kf = [jnp.concatenate([rows_of(kp_ref, s, r, SPAN), rows_of(kc_ref, s, r, mb * SPAN)], axis=0)
              for s in range(kc_ref.shape[0])]
        vf = [jnp.concatenate([rows_of(vp_ref, s, r, SPAN), rows_of(vc_ref, s, r, mb * SPAN)], axis=0)
              for s in range(vc_ref.shape[0])]
        ks = [k.astype(BF16) for k in kf]
        vs = [v.astype(BF16) for v in vf]
        if shared_kv:
            k_sw = pltpu.roll(kf[0], HEAD_DIM, 1).astype(BF16)
            v_sw = pltpu.roll(vf[0], HEAD_DIM, 1).astype(BF16)
        for b in range(mb):
            blk = slice(b * SPAN, (b + 1) * SPAN)
            keys = slice(b * SPAN, (b + 2) * SPAN)
            msk = band_first if b == 0 else band
            if not shared_kv:
                for s in range(n_slabs):
                    o, lse = pair(qs[s][blk], halves[0], qs[s][blk], halves[1], ks[s][keys], vs[s][keys], msk,
                                  (first_head + 2 * s, first_head + 2 * s + 1))
                    store(s, r, b, jnp.where(halves[0], o[top], o[bot]), jnp.where(halves[0], lse[top], lse[bot]))
            else:
                for kvh in range(2):
                    sa, sb = 2 * kvh, 2 * kvh + 1
                    o_al, l_al = pair(qs[sa][blk], halves[kvh], qs[sb][blk], halves[kvh], ks[0][keys],
                                      vs[0][keys], msk, (2 * sa + kvh, 2 * sb + kvh))
                    o_sw, l_sw = pair(qs[sa][blk], halves[1 - kvh], qs[sb][blk], halves[1 - kvh], k_sw[keys],
                                      v_sw[keys], msk, (2 * sa + 1 - kvh, 2 * sb + 1 - kvh))
                    store(sa, r, b, jnp.where(halves[kvh], o_al[top], o_sw[top]),
                          jnp.where(halves[kvh], l_al[top], l_sw[top]))
                    store(sb, r, b, jnp.where(halves[kvh], o_al[bot], o_sw[bot]),
                          jnp.where(halves[kvh], l_al[bot], l_sw[bot]))


def _band_attn(slabs, *, n_seq, seq_len, dil, q_slab, k_slab, v_slab, n_q_slabs, shared_kv=False, sink=None,
               want_lse, name):
    mb = max(1, 512 // (dil * SPAN))
    tp = dil * SPAN * mb
    nt = seq_len // tp
    assert seq_len % tp == 0 and (not shared_kv or (n_q_slabs == 4 and dil == 1))
    prev_rows = dil * SPAN
    per_step = n_q_slabs if shared_kv else 1
    kv_step = 0 if shared_kv else 1

    def cur(base, step):
        return lambda n, t, s: (base + s * step, n * nt + t, 0)

    def prev(base, step):
        return lambda n, t, s: (base + s * step, jnp.maximum((n * nt + t) * mb - 1, 0), 0)

    in_specs = [pl.BlockSpec((per_step, tp, LANES), cur(q_slab // per_step, 1)),
                pl.BlockSpec((1, tp, LANES), cur(k_slab, kv_step)),
                pl.BlockSpec((1, prev_rows, LANES), prev(k_slab, kv_step)),
                pl.BlockSpec((1, tp, LANES), cur(v_slab, kv_step)),
                pl.BlockSpec((1, prev_rows, LANES), prev(v_slab, kv_step))]
    assert q_slab % per_step == 0
    args = [slabs] * 5
    if sink is not None:
        in_specs.append(pl.BlockSpec(memory_space=pltpu.SMEM))
        args.append(sink.reshape(-1).astype(F32))
    rows = n_seq * seq_len
    if dil == 1:
        spec = pl.BlockSpec((tp, per_step * LANES), lambda n, t, s: (n * nt + t, s))
        out_shape = [jax.ShapeDtypeStruct((rows, n_q_slabs * LANES), BF16)]
        lse_shape = jax.ShapeDtypeStruct((rows, n_q_slabs * LANES), F32)
    else:
        spec = pl.BlockSpec((1, tp, LANES), lambda n, t, s: (s, n * nt + t, 0))
        out_shape = [jax.ShapeDtypeStruct((n_q_slabs, rows, LANES), F32)]
        lse_shape = jax.ShapeDtypeStruct((n_q_slabs, rows, LANES), F32)
    out_specs = [spec]
    if want_lse:
        out_shape.append(lse_shape)
        out_specs.append(spec)
    return pl.pallas_call(
        functools.partial(_band_body, dil=dil, mb=mb, shared_kv=shared_kv, has_sink=sink is not None,
                          want_lse=want_lse),
        out_shape=out_shape,
        grid=(n_seq, nt, n_q_slabs // per_step),
        in_specs=in_specs,
        out_specs=out_specs,
        compiler_params=_params("parallel", "parallel", "parallel"),
        name=name,
    )(*args)


def _combine_body(o0, l0, o1, l1, o2, l2, out_ref):
    def tile(ref):
        return ref[...].reshape(ref.shape[-2:]).astype(F32)

    a0, a1, a2 = tile(l0), tile(l1), tile(l2)
    m = jnp.maximum(jnp.maximum(a0, a1), a2)
    e0, e1, e2 = jnp.exp(a0 - m), jnp.exp(a1 - m), jnp.exp(a2 - m)
    num = e0 * tile(o0) + e1 * tile(o1) + e2 * tile(o2)
    out_ref[...] = (num / (e0 + e1 + e2)).astype(out_ref.dtype)


def _combine(pairs, name):
    first = pairs[0][0]
    rows, c = first.shape if first.ndim == 2 else (first.shape[1], first.shape[0] * LANES)
    tm = _pick(rows, (1024, 512, 256, 128, 64, 32, 16, 8))
    flat = pl.BlockSpec((tm, LANES), lambda i, s: (i, s))
    slab = pl.BlockSpec((1, tm, LANES), lambda i, s: (s, i, 0))
    args = [a for pair in pairs for a in pair]
    return pl.pallas_call(
        _combine_body,
        out_shape=jax.ShapeDtypeStruct((rows, c), BF16),
        grid=(rows // tm, c // LANES),
        in_specs=[flat if a.ndim == 2 else slab for a in args],
        out_specs=flat,
        compiler_params=_params("parallel", "parallel"),
        name=name,
    )(*args)


def _state_body(x_ref, o_ref):
    o_ref[0, 0] = x_ref[0].T


def _state_slabs(slabs, *, first_slab, n_slabs, n_seq, seq_len, keep, name):
    assert seq_len % keep == 0
    per_seq = seq_len // keep
    return pl.pallas_call(
        _state_body,
        out_shape=jax.ShapeDtypeStruct((n_seq, n_slabs, LANES, keep), F32),
        grid=(n_seq, n_slabs),
        in_specs=[pl.BlockSpec((1, keep, LANES), lambda n, s: (first_slab + s, (n + 1) * per_seq - 1, 0))],
        out_specs=pl.BlockSpec((1, 1, LANES, keep), lambda n, s: (n, s, 0, 0)),
        compiler_params=_params("parallel", "parallel"),
        name=name,
    )(slabs)


def _conv_prompt_body(gb_ref, gc_ref, xa_ref, gcp_ref, xap_ref, w_ref, a_ref, st_ref):
    t = pl.program_id(1)
    w = w_ref[...]
    for s in range(gb_ref.shape[0]):
        lanes = slice(s * LANES, (s + 1) * LANES)
        u = gc_ref[s] * xa_ref[s]
        up = jnp.where(t == 0, 0.0, gcp_ref[s] * xap_ref[s])
        ext = jnp.concatenate([up, u], axis=0)
        y = (w[0:1, lanes] * pltpu.roll(ext, 2, 0)[SUBLANES:]
             + w[1:2, lanes] * pltpu.roll(ext, 1, 0)[SUBLANES:]) + w[2:3, lanes] * u
        a_ref[:, lanes] = (gb_ref[s] * y).astype(a_ref.dtype)
        st_ref[0, :, lanes] = u[u.shape[0] - SUBLANES:]


def _conv_prompt(slabs, conv_w, *, n_seq, seq_len, c):
    tq = _pick(seq_len, (512, 256, 128))
    nt = seq_len // tq
    rb = tq // SUBLANES
    ns = c // LANES

    def cur(part):
        return lambda n, t: (part, n * nt + t, 0)

    def prev(part):
        return lambda n, t: (part, jnp.maximum((n * nt + t) * rb - 1, 0), 0)

    return pl.pallas_call(
        _conv_prompt_body,
        out_shape=[jax.ShapeDtypeStruct((n_seq * seq_len, c), BF16),
                   jax.ShapeDtypeStruct((n_seq, SUBLANES, c), F32)],
        grid=(n_seq, nt),
        in_specs=[pl.BlockSpec((ns, tq, LANES), cur(0)), pl.BlockSpec((ns, tq, LANES), cur(1)),
                  pl.BlockSpec((ns, tq, LANES), cur(2)),
                  pl.BlockSpec((ns, SUBLANES, LANES), prev(1)), pl.BlockSpec((ns, SUBLANES, LANES), prev(2)),
                  pl.BlockSpec((CONV_W, c), lambda n, t: (0, 0))],
        out_specs=[pl.BlockSpec((tq, c), lambda n, t: (n * nt + t, 0)),
                   pl.BlockSpec((1, SUBLANES, c), lambda n, t: (n, 0, 0))],
        compiler_params=_params("parallel", "arbitrary"),
        name="conv_prompt",
    )(slabs, slabs, slabs, slabs, slabs, conv_w)


def _conv_step_body(p_ref, prev_ref, w_ref, a_ref, st_ref):
    s_len = p_ref.shape[1]
    w = w_ref[...]
    hist = [prev_ref[k] for k in range(CONV_W - 1)] + [p_ref[1, s] * p_ref[2, s] for s in range(s_len)]
    for s in range(s_len):
        y = (w[0:1] * hist[s] + w[1:2] * hist[s + 1]) + w[2:3] * hist[s + 2]
        a_ref[s] = p_ref[0, s] * y
    for k in range(CONV_W - 1):
        st_ref[k] = hist[s_len + k]


def _conv_step(p3, prev, conv_w):
    _, s_len, n, c = p3.shape
    return pl.pallas_call(
        _conv_step_body,
        out_shape=[jax.ShapeDtypeStruct((s_len, n, c), F32), jax.ShapeDtypeStruct((CONV_W - 1, n, c), F32)],
        name="conv_step",
        compiler_params=pltpu.CompilerParams(vmem_limit_bytes=VMEM_LIMIT_BYTES),
    )(p3, prev, conv_w)


def _step_body(*refs, n_blk, qr, tok_div, w, dil, cw, bsz, has_sink, want_lse):
    q_ref, kvt_ref, cache_ref = refs[:3]
    pos = 3
    sink_ref = refs[pos] if has_sink else None
    pos += int(has_sink)
    o_ref = refs[pos]
    pos += 1
    lse_ref = refs[pos] if want_lse else None
    pos += int(want_lse)
    cout_ref = refs[pos]
    r_dim = n_blk * HEAD_DIM
    nrb = n_blk * qr
    n_chunks = w // cw
    scale = HEAD_DIM ** -0.5
    step = pl.program_id(0)
    lane_r = lax.broadcasted_iota(I32, (1, r_dim), 1)
    blk_masks = [(lane_r >= j * HEAD_DIM) & (lane_r < (j + 1) * HEAD_DIM) for j in range(n_blk)]
    row = lax.broadcasted_iota(I32, (nrb, 1), 0)
    tok = (row % qr) // tok_div
    lane_c = lax.broadcasted_iota(I32, (1, cw), 1)
    lane_n = lax.broadcasted_iota(I32, (1, LANES), 1)

    def valid(pos_l):
        ok = (pos_l >= tok) & (pos_l <= w + tok)
        if dil > 1:
            ok = ok & (((pos_l - tok) & (dil - 1)) == 0)
        return ok

    for b in range(bsz):
        off = ((step * bsz + b) * 4) % LANES
        shift = (LANES - off) % LANES
        new_k = pltpu.roll(kvt_ref[0], shift, 1)
        new_v = pltpu.roll(kvt_ref[1], shift, 1)
        q = q_ref[b] * scale
        qbd = jnp.concatenate([jnp.where(mk, q, 0.0) for mk in blk_masks], axis=0).astype(BF16)
        scores = []
        for c in range(n_chunks):
            lo, hi = c * cw, (c + 1) * cw
            kc = cache_ref[b, 0, :, lo:hi]
            sc = jnp.dot(qbd, kc.astype(BF16), preferred_element_type=F32)
            scores.append(jnp.where(valid(lane_c + lo), sc, -jnp.inf))
            nxt = cache_ref[b, 0, :, hi:hi + LANES] if c + 1 < n_chunks else new_k
            ext = jnp.concatenate([kc, nxt], axis=1)
            cout_ref[b, 0, :, lo:hi] = pltpu.roll(ext, cw + LANES - 4, 1)[:, :cw]
        sc = jnp.dot(qbd, new_k.astype(BF16), preferred_element_type=F32)
        scores.append(jnp.where(valid(lane_n + w), sc, -jnp.inf))
        m = functools.reduce(jnp.maximum, [jnp.max(s, axis=-1, keepdims=True) for s in scores])
        if has_sink:
            m = jnp.maximum(m, sink_ref[...])
        probs = [jnp.exp(s - m) for s in scores]
        den = functools.reduce(lambda a, c: a + c, [jnp.sum(p, axis=-1, keepdims=True) for p in probs])
        if has_sink:
            den = den + jnp.exp(sink_ref[...] - m)
        pv = lax.dot_general(probs[-1].astype(BF16), new_v.astype(BF16), NT_DIMS, preferred_element_type=F32)
        for c in range(n_chunks):
            lo, hi = c * cw, (c + 1) * cw
            vc = cache_ref[b, 1, :, lo:hi]
            pv = pv + lax.dot_general(probs[c].astype(BF16), vc.astype(BF16), NT_DIMS,
                                      preferred_element_type=F32)
            nxt = cache_ref[b, 1, :, hi:hi + LANES] if c + 1 < n_chunks else new_v
            ext = jnp.concatenate([vc, nxt], axis=1)
            cout_ref[b, 1, :, lo:hi] = pltpu.roll(ext, cw + LANES - 4, 1)[:, :cw]
        o = jnp.zeros((qr, r_dim), F32)
        m_e = jnp.zeros((qr, r_dim), F32)
        den_e = jnp.zeros((qr, r_dim), F32)
        for j, mk in enumerate(blk_masks):
            rows = slice(j * qr, (j + 1) * qr)
            o = jnp.where(mk, pv[rows], o)
            m_e = jnp.where(mk, m[rows], m_e)
            den_e = jnp.where(mk, den[rows], den_e)
        o_ref[b] = o / den_e
        if want_lse:
            lse_ref[b] = m_e + jnp.log(den_e)


def _step_attn(q, kvt, cache, *, dil, tok_div, sink_col=None, want_lse, bsz, name):
    n, qr, r_dim = q.shape
    w = cache.shape[-1]
    n_blk = r_dim // HEAD_DIM
    cw = min(w, 512)
    in_specs = [pl.BlockSpec((bsz, qr, r_dim), lambda i: (i, 0, 0)),
                pl.BlockSpec((2, r_dim, LANES), lambda i: (0, 0, (i * bsz * 4) // LANES)),
                pl.BlockSpec((bsz, 2, r_dim, w), lambda i: (i, 0, 0, 0))]
    args = [q, kvt, cache]
    if sink_col is not None:
        in_specs.append(pl.BlockSpec((n_blk * qr, 1), lambda i: (0, 0)))
        args.append(sink_col)
    o_spec = pl.BlockSpec((bsz, qr, r_dim), lambda i: (i, 0, 0))
    out_shape = [jax.ShapeDtypeStruct((n, qr, r_dim), F32)]
    out_specs = [o_spec]
    if want_lse:
        out_shape.append(jax.ShapeDtypeStruct((n, qr, r_dim), F32))
        out_specs.append(o_spec)
    out_shape.append(jax.ShapeDtypeStruct(cache.shape, F32))
    out_specs.append(pl.BlockSpec((bsz, 2, r_dim, w), lambda i: (i, 0, 0, 0)))
    return pl.pallas_call(
        functools.partial(_step_body, n_blk=n_blk, qr=qr, tok_div=tok_div, w=w, dil=dil, cw=cw, bsz=bsz,
                          has_sink=sink_col is not None, want_lse=want_lse),
        out_shape=out_shape,
        grid=(n // bsz,),
        in_specs=in_specs,
        out_specs=out_specs,
        compiler_params=_params("parallel"),
        name=name,
    )(*args)


def _router_body(h_ref, g_ref, wr_ref, xn_ref, im_ref, gm_ref, cnt_ref, carry_ref, *, n_exp):
    i = pl.program_id(0)

    @pl.when(i == 0)
    def _():
        carry_ref[...] = jnp.zeros_like(carry_ref)

    xn = _rms(h_ref[...], g_ref[...])
    xn_ref[...] = xn
    tm = xn.shape[0]
    wr = wr_ref[...]
    xh = xn.astype(BF16)
    xl = (xn - xh.astype(F32)).astype(BF16)
    wh = wr.astype(BF16)
    wl = (wr - wh.astype(F32)).astype(BF16)
    lg = jnp.dot(xh, wh, preferred_element_type=F32) + (
        jnp.dot(xh, wl, preferred_element_type=F32) + jnp.dot(xl, wh, preferred_element_type=F32))
    lane = lax.broadcasted_iota(I32, (tm, LANES), 1)
    lane_f = lane.astype(F32)
    lg = jnp.where(lane < n_exp, lg, -jnp.inf)
    m1 = jnp.max(lg, axis=-1, keepdims=True)
    i1 = jnp.min(jnp.where(lg == m1, lane_f, float(LANES)), axis=-1, keepdims=True)
    lg2 = jnp.where(lane_f == i1, -jnp.inf, lg)
    m2 = jnp.max(lg2, axis=-1, keepdims=True)
    i2 = jnp.min(jnp.where(lg2 == m2, lane_f, float(LANES)), axis=-1, keepdims=True)
    e = jnp.exp(m2 - m1)
    g1 = 1.0 / (1.0 + e)
    g2 = e / (1.0 + e)
    sel1 = lane_f == i1
    sel2 = lane_f == i2
    onehot = jnp.where(sel1 | sel2, 1.0, 0.0)
    r_i = lax.broadcasted_iota(I32, (tm, tm), 0)
    c_i = lax.broadcasted_iota(I32, (tm, tm), 1)
    tri = jnp.where(c_i < r_i, 1.0, 0.0).astype(BF16)
    before = jnp.dot(tri, onehot.astype(BF16), preferred_element_type=F32) + carry_ref[0:1]
    r1 = jnp.sum(jnp.where(sel1, before, 0.0), axis=-1, keepdims=True)
    r2 = jnp.sum(jnp.where(sel2, before, 0.0), axis=-1, keepdims=True)
    total = carry_ref[0:1] + jnp.sum(onehot, axis=0, keepdims=True)
    carry_ref[...] = jnp.broadcast_to(total, carry_ref.shape)
    cnt_ref[...] = jnp.broadcast_to(total, cnt_ref.shape).astype(I32)
    meta = jnp.where(lane == 0, i1, jnp.where(lane == 1, i2, jnp.where(lane == 2, r1, jnp.where(lane == 3, r2, 0.0))))
    im_ref[...] = meta.astype(I32)
    gm_ref[...] = jnp.where(lane == 0, g1, jnp.where(lane == 1, g2, 0.0))


def _router(h, gain, w_router):
    t, d = h.shape
    n_exp = w_router.shape[1]
    tm = _pick(t, (512, 256, 128, 64, 32, 16, 8))
    wr = jnp.zeros((d, LANES), F32).at[:, :n_exp].set(w_router)
    return pl.pallas_call(
        functools.partial(_router_body, n_exp=n_exp),
        out_shape=[jax.ShapeDtypeStruct((t, d), F32), jax.ShapeDtypeStruct((t, LANES), I32),
                   jax.ShapeDtypeStruct((t, LANES), F32), jax.ShapeDtypeStruct((SUBLANES, LANES), I32)],
        grid=(t // tm,),
        in_specs=[pl.BlockSpec((tm, d), lambda i: (i, 0)),
                  pl.BlockSpec((1, d), lambda i: (0, 0)),
                  pl.BlockSpec((d, LANES), lambda i: (0, 0))],
        out_specs=[pl.BlockSpec((tm, d), lambda i: (i, 0)),
                   pl.BlockSpec((tm, LANES), lambda i: (i, 0)),
                   pl.BlockSpec((tm, LANES), lambda i: (i, 0)),
                   pl.BlockSpec((SUBLANES, LANES), lambda i: (0, 0))],
        scratch_shapes=[pltpu.VMEM((SUBLANES, LANES), F32)],
        compiler_params=_params("arbitrary"),
        name="moe_router",
    )(h, gain.reshape(1, -1), wr)


def _dispatch_body(s1_ref, s2_ref, x_ref, zero_ref, out_ref, sem, *, tm):
    del zero_ref

    def issue(r, carry):
        src = x_ref.at[pl.ds(r, 1)]
        pltpu.make_async_copy(src, out_ref.at[pl.ds(s1_ref[0, 0, r], 1)], sem.at[0]).start()
        pltpu.make_async_copy(src, out_ref.at[pl.ds(s2_ref[0, 0, r], 1)], sem.at[1]).start()
        return carry

    lax.fori_loop(0, tm, issue, 0, unroll=8)
    pltpu.make_async_copy(x_ref, out_ref.at[pl.ds(0, tm)], sem.at[0]).wait()
    pltpu.make_async_copy(x_ref, out_ref.at[pl.ds(0, tm)], sem.at[1]).wait()


def _dispatch(xn, slot1, slot2, n_slots):
    t, d = xn.shape
    tm = _pick(t, (512, 256, 128, 64, 32, 16, 8))
    nt = t // tm
    smem = pl.BlockSpec((1, 1, tm), lambda i: (i, 0, 0), memory_space=pltpu.SMEM)
    return pl.pallas_call(
        functools.partial(_dispatch_body, tm=tm),
        out_shape=jax.ShapeDtypeStruct((n_slots, d), F32),
        grid=(nt,),
        in_specs=[smem, smem, pl.BlockSpec((tm, d), lambda i: (i, 0)), pl.BlockSpec(memory_space=pl.ANY)],
        out_specs=pl.BlockSpec(memory_space=pl.ANY),
        scratch_shapes=[pltpu.SemaphoreType.DMA((2,))],
        input_output_aliases={3: 0},
        compiler_params=_params("arbitrary"),
        name="moe_dispatch",
    )(slot1.reshape(nt, 1, tm), slot2.reshape(nt, 1, tm), xn, jnp.zeros((n_slots, d), F32))


def _experts_body(te_ref, nu_ref, x_ref, wg_ref, wu_ref, wd_ref, o_ref, xb_ref, acc_ref, *, nj):
    del te_ref
    i = pl.program_id(0)
    j = pl.program_id(1)
    active = i < nu_ref[0]

    @pl.when(active & (j == 0))
    def _():
        xb_ref[...] = x_ref[...].astype(BF16)
        acc_ref[...] = jnp.zeros_like(acc_ref)

    @pl.when(active)
    def _():
        xb = xb_ref[...]
        gate = jnp.dot(xb, wg_ref[0].astype(BF16), preferred_element_type=F32)
        up = jnp.dot(xb, wu_ref[0].astype(BF16), preferred_element_type=F32)
        act = (gate * jax.nn.sigmoid(gate) * up).astype(BF16)
        acc_ref[...] += jnp.dot(act, wd_ref[0].astype(BF16), preferred_element_type=F32)

    @pl.when(active & (j == nj - 1))
    def _():
        o_ref[...] = acc_ref[...]

    @pl.when(jnp.logical_not(active) & (j == nj - 1))
    def _():
        o_ref[...] = jnp.zeros_like(o_ref)


def _experts(xs, tile_expert, n_used, w_gate, w_up, w_down, tm):
    n_slots, d = xs.shape
    f = w_gate.shape[2]
    tf = _pick(f, (512, 256, 128))
    nj = f // tf
    n_tiles = n_slots // tm

    def act_j(i, j, nu):
        return jnp.where(i < nu[0], j, nj - 1)

    def row(i, nu):
        return jnp.minimum(i, nu[0] - 1)

    grid_spec = pltpu.PrefetchScalarGridSpec(
        num_scalar_prefetch=2,
        grid=(n_tiles, nj),
        in_specs=[pl.BlockSpec((tm, d), lambda i, j, te, nu: (row(i, nu), 0)),
                  pl.BlockSpec((1, d, tf), lambda i, j, te, nu: (te[i], 0, act_j(i, j, nu))),
                  pl.BlockSpec((1, d, tf), lambda i, j, te, nu: (te[i], 0, act_j(i, j, nu))),
                  pl.BlockSpec((1, tf, d), lambda i, j, te, nu: (te[i], act_j(i, j, nu), 0))],
        out_specs=pl.BlockSpec((tm, d), lambda i, j, te, nu: (i, 0)),
        scratch_shapes=[pltpu.VMEM((tm, d), BF16), pltpu.VMEM((tm, d), F32)],
    )
    return pl.pallas_call(
        functools.partial(_experts_body, nj=nj),
        out_shape=jax.ShapeDtypeStruct((n_slots, d), F32),
        grid_spec=grid_spec,
        compiler_params=_params("arbitrary", "arbitrary"),
        name="moe_experts",
    )(tile_expert, n_used, xs, w_gate, w_up, w_down)


def _gather_norm_body(s1_ref, s2_ref, h_ref, gm_ref, g_ref, ys_ref, oa_ref, ob_ref, ya_ref, yb_ref, sem, *,
                      tm, head_tiles):
    def issue(r, carry):
        pltpu.make_async_copy(ys_ref.at[pl.ds(s1_ref[0, 0, r], 1)], ya_ref.at[pl.ds(r, 1)], sem.at[0]).start()
        pltpu.make_async_copy(ys_ref.at[pl.ds(s2_ref[0, 0, r], 1)], yb_ref.at[pl.ds(r, 1)], sem.at[1]).start()
        return carry

    lax.fori_loop(0, tm, issue, 0, unroll=8)
    pltpu.make_async_copy(ys_ref.at[pl.ds(0, tm)], ya_ref, sem.at[0]).wait()
    pltpu.make_async_copy(ys_ref.at[pl.ds(0, tm)], yb_ref, sem.at[1]).wait()
    gm = gm_ref[...]
    y = _rms(h_ref[...] + (gm[:, 0:1] * ya_ref[...] + gm[:, 1:2] * yb_ref[...]), g_ref[...])
    i = pl.program_id(0)

    @pl.when(i < head_tiles)
    def _():
        oa_ref[...] = y

    @pl.when(i >= head_tiles)
    def _():
        ob_ref[...] = y


def _gather_norm(h, gates, slot1, slot2, ys, gain, split):
    t, d = h.shape
    tm = _pick(split, (512, 256, 128, 64, 32, 16, 8))
    assert t % tm == 0
    nt = t // tm
    head_tiles = split // tm
    smem = pl.BlockSpec((1, 1, tm), lambda i: (i, 0, 0), memory_space=pltpu.SMEM)
    return pl.pallas_call(
        functools.partial(_gather_norm_body, tm=tm, head_tiles=head_tiles),
        out_shape=[jax.ShapeDtypeStruct((split, d), F32), jax.ShapeDtypeStruct((t - split, d), F32)],
        grid=(nt,),
        in_specs=[smem, smem, pl.BlockSpec((tm, d), lambda i: (i, 0)),
                  pl.BlockSpec((tm, LANES), lambda i: (i, 0)),
                  pl.BlockSpec((1, d), lambda i: (0, 0)),
                  pl.BlockSpec(memory_space=pl.ANY)],
        out_specs=[pl.BlockSpec((tm, d), lambda i: (jnp.minimum(i, head_tiles - 1), 0)),
                   pl.BlockSpec((tm, d), lambda i: (jnp.maximum(i - head_tiles, 0), 0))],
        scratch_shapes=[pltpu.VMEM((tm, d), F32), pltpu.VMEM((tm, d), F32), pltpu.SemaphoreType.DMA((2,))],
        compiler_params=_params("arbitrary"),
        name="moe_gather_norm",
    )(slot1.reshape(nt, 1, tm), slot2.reshape(nt, 1, tm), h, gates, gain.reshape(1, -1), ys)


def _moe(h, gain, w_router, w_gate, w_up, w_down, final_gain, split):
    t, _ = h.shape
    n_exp = w_router.shape[1]
    tm_e = 1024
    xn, imeta, gates, counts = _router(h, gain, w_router)
    idx1, idx2, rank1, rank2 = imeta[:, 0], imeta[:, 1], imeta[:, 2], imeta[:, 3]
    cnt = counts[0, :n_exp]
    padded = ((cnt + tm_e - 1) // tm_e) * tm_e
    ends = jnp.cumsum(padded)
    starts = ends - padded
    slot1 = starts[idx1] + rank1
    slot2 = starts[idx2] + rank2
    n_tiles = (TOP_K * t + n_exp * (tm_e - 1)) // tm_e
    n_used = (ends[-1] // tm_e).astype(I32)
    tile_start = jnp.arange(n_tiles, dtype=I32) * tm_e
    tile_expert = jnp.minimum(jnp.searchsorted(ends, tile_start, side="right"), n_exp - 1).astype(I32)
    last = tile_expert[jnp.maximum(n_used - 1, 0)]
    tile_expert = jnp.where(jnp.arange(n_tiles) < n_used, tile_expert, last)
    xs = _dispatch(xn, slot1, slot2, n_tiles * tm_e)
    ys = _experts(xs, tile_expert, n_used.reshape(1), w_gate, w_up, w_down, tm_e)
    return _gather_norm(h, gates, slot1, slot2, ys, final_gain, split)


def _cache_to_slabs(cache):
    n, w, two, heads, hd = cache.shape
    return jnp.transpose(cache, (0, 2, 3, 4, 1)).reshape(n, two, heads * hd, w)


def _slabs_to_cache(slabs, heads):
    n, two, _, w = slabs.shape
    return jnp.transpose(slabs.reshape(n, two, heads, HEAD_DIM, w), (0, 4, 1, 2, 3))


def kernel(x_prompt, x_sample, cache_conv, cache_swa_kv, cache_dil_kv0, cache_dil_kv1, cache_dil_kv2, norm_mix0, w_in0, conv_w, swa_sink, w_out0, norm_ffn0, w_gate0, w_up0, w_down0, norm_mix1, w_in1, w_out1, norm_ffn1, w_router, w_gate1, w_up1, w_down1, norm_final):
    n_p, seq, d = x_prompt.shape
    n_s, s_len, _ = x_sample.shape
    tp, ts = n_p * seq, n_s * s_len
    d_conv = conv_w.shape[2]
    kvh, grp = swa_sink.shape[1], swa_sink.shape[2]
    hq0 = kvh * grp
    h1 = cache_dil_kv0.shape[4]
    c_q0, c_kv0 = hq0 * HEAD_DIM, kvh * HEAD_DIM
    q0_col = 3 * d_conv
    k0_col = q0_col + c_q0
    c_g = 3 * h1 * HEAD_DIM
    dil_caches = (cache_dil_kv0, cache_dil_kv1, cache_dil_kv2)

    x_rows = (x_prompt.reshape(tp, d), x_sample.reshape(ts, d))

    assert c_kv0 == LANES and grp % 2 == 0 and d_conv % LANES == 0
    nsc = d_conv // LANES
    q0_slab = 3 * nsc
    k0_slab = q0_slab + c_q0 // LANES
    proj0 = _dense([x_rows], w_in0[0], gain=norm_mix0[0], out_dtype=F32, name="in_proj0", slab_out=True)
    a_p, conv_tail = _conv_prompt(proj0, conv_w[0], n_seq=n_p, seq_len=seq, c=d_conv)
    att_p = _band_attn(proj0, n_seq=n_p, seq_len=seq, dil=1, q_slab=q0_slab, k_slab=k0_slab, v_slab=k0_slab + 1,
                       n_q_slabs=c_q0 // LANES, shared_kv=True, sink=swa_sink[0], want_lse=False,
                       name="swa_prompt")[0]
    proj0_s = proj0[:, tp:]
    p3 = proj0_s[:q0_slab].reshape(3, nsc, n_s, s_len, LANES)
    p3 = jnp.transpose(p3, (0, 3, 2, 1, 4)).reshape(3, s_len, n_s, d_conv)
    a_s, conv_new = _conv_step(p3, jnp.transpose(cache_conv[0], (1, 0, 2)), conv_w[0])
    a_s = jnp.transpose(a_s, (1, 0, 2)).reshape(ts, d_conv)
    kvt0 = _proj_t(jnp.transpose(w_in0[0][:, k0_col:]), x_rows[1], norm_mix0[0], "kv_t0")
    q_s = proj0_s[q0_slab:k0_slab].reshape(kvh, grp // 2, n_s, s_len, 2, HEAD_DIM)
    q_s = jnp.transpose(q_s, (2, 3, 1, 4, 0, 5)).reshape(n_s, s_len * grp, c_kv0)
    sink_col = jnp.broadcast_to(swa_sink[0][:, None, :], (kvh, s_len, grp)).reshape(kvh * s_len * grp, 1)
    o_s, swa_new = _step_attn(q_s, kvt0.reshape(2, c_kv0, ts), _cache_to_slabs(cache_swa_kv[0]), dil=1,
                              tok_div=grp, sink_col=sink_col, want_lse=False, bsz=8, name="swa_step")
    att_s = jnp.transpose(o_s.reshape(n_s, s_len, grp, kvh, HEAD_DIM), (0, 1, 3, 2, 4)).reshape(ts, c_q0)
    h = _dense([(a_p, a_s.astype(BF16)), (att_p, att_s.astype(BF16))], w_out0[0], res=x_rows, out_dtype=F32,
               name="out_proj0")
    h = _ffn(h, norm_ffn0[0], w_gate0[0], w_up0[0], w_down0[0], "ffn0")

    proj1 = _dense([h], w_in1[0], gain=norm_mix1[0], out_dtype=F32, name="in_proj1", slab_out=True)
    proj1_s = proj1[:, tp:]
    c_h = h1 * HEAD_DIM
    nsh = c_h // LANES
    pairs_p, pairs_s, dil_p, dil_s = [], [], [], []
    for g, dil in enumerate(DILATIONS):
        window = dil_caches[g].shape[2]
        s0 = g * 3 * nsh
        pairs_p.append(tuple(_band_attn(proj1, n_seq=n_p, seq_len=seq, dil=dil, q_slab=s0, k_slab=s0 + nsh,
                                        v_slab=s0 + 2 * nsh, n_q_slabs=nsh, want_lse=True,
                                        name=f"dil{g}_prompt")))
        keep = min(window, seq)
        state = _state_slabs(proj1, first_slab=s0 + nsh, n_slabs=2 * nsh, n_seq=n_p, seq_len=seq, keep=keep,
                             name=f"dil{g}_state")
        dil_p.append(_slabs_to_cache(state.reshape(n_p, 2, c_h, keep), h1)[None])
        lo = g * c_g
        q_g = jnp.transpose(proj1_s[s0:s0 + nsh].reshape(nsh, n_s, s_len, LANES), (1, 2, 0, 3))
        q_g = q_g.reshape(n_s, s_len, c_h)
        q_g = jnp.concatenate([q_g, jnp.zeros((n_s, SUBLANES - s_len, c_h), F32)], axis=1)
        kvt = _proj_t(jnp.transpose(w_in1[0][:, lo + c_h:lo + c_g]), h[tp:], norm_mix1[0], f"kv_t1_{g}")
        o_sg, l_sg, cache_new = _step_attn(q_g, kvt.reshape(2, c_h, ts), _cache_to_slabs(dil_caches[g][0]),
                                           dil=dil, tok_div=1, want_lse=True,
                                           bsz=max(1, 2048 // window), name=f"dil{g}_step")
        pairs_s.append((o_sg.reshape(n_s * SUBLANES, c_h), l_sg.reshape(n_s * SUBLANES, c_h)))
        dil_s.append(_slabs_to_cache(cache_new, h1)[None])
    comb_p = _combine(pairs_p, "combine_prompt")
    comb_s = _combine(pairs_s, "combine_step").reshape(n_s, SUBLANES, c_h)[:, :s_len].reshape(ts, c_h)
    h = _dense([(comb_p, comb_s)], w_out1[0], res=h, out_dtype=F32, name="out_proj1")
    y_p, y_s = _moe(h, norm_ffn1[0], w_router[0], w_gate1[0], w_up1[0], w_down1[0], norm_final, tp)

    y_prompt = y_p.reshape(n_p, seq, d)
    y_sample = y_s.reshape(n_s, s_len, d)
    new_conv_prompt = conv_tail[:, SUBLANES - (CONV_W - 1):][None]
    new_conv_sample = jnp.transpose(conv_new, (1, 0, 2))[None]
    keep0 = min(SPAN, seq)
    swa_state = _state_slabs(proj0, first_slab=k0_slab, n_slabs=2, n_seq=n_p, seq_len=seq, keep=keep0,
                             name="swa_state")
    new_swa_kv_prompt = _slabs_to_cache(swa_state, kvh)[None]
    new_swa_kv_sample = _slabs_to_cache(swa_new, kvh)[None]
    return (y_prompt, y_sample, new_conv_prompt, new_conv_sample, new_swa_kv_prompt, new_swa_kv_sample,
            dil_p[0], dil_s[0], dil_p[1], dil_s[1], dil_p[2], dil_s[2])
```

```python
import functools

import jax
import jax.numpy as jnp
from jax import lax
from jax.experimental import pallas as pl
from jax.experimental.pallas import tpu as pltpu

F32 = jnp.float32
BF16 = jnp.bfloat16
I32 = jnp.int32

EPS = 1e-5
HEAD_DIM = 64
SPAN = 128
CONV_W = 3
DILATIONS = (1, 4, 16)
TOP_K = 2
LANES = 128
SUBLANES = 8
VMEM_LIMIT_BYTES = 56 * 1024 * 1024
NT_DIMS = (((1,), (1,)), ((), ()))


def _params(*sem):
    return pltpu.CompilerParams(dimension_semantics=sem, vmem_limit_bytes=VMEM_LIMIT_BYTES)


def _pick(n, candidates):
    for c in candidates:
        if n % c == 0:
            return c
    raise ValueError(f"no tile for {n} in {candidates}")


def _rms(x, g):
    y = x * lax.rsqrt(jnp.mean(x * x, axis=-1, keepdims=True) + EPS)
    return y * g


def _load_bf16(w_hbm, w_vmem, stage, sem, *, axis, chunk):
    n_chunks = w_hbm.shape[axis] // chunk

    def window(ref, c):
        return ref.at[pl.ds(c * chunk, chunk), :] if axis == 0 else ref.at[:, pl.ds(c * chunk, chunk)]

    def copy(c):
        return pltpu.make_async_copy(window(w_hbm, c), stage.at[c % 2], sem.at[c % 2])

    copy(0).start()
    for c in range(n_chunks):
        if c + 1 < n_chunks:
            copy(c + 1).start()
        copy(c).wait()
        if axis == 0:
            w_vmem[c * chunk:(c + 1) * chunk, :] = stage[c % 2].astype(BF16)
        else:
            w_vmem[:, c * chunk:(c + 1) * chunk] = stage[c % 2].astype(BF16)


def _dense_body(*refs, ks, split, head_tiles, has_gain, has_res, slab_out, cn):
    n_in = len(ks)
    step = pl.program_id(0)
    refs = list(refs)

    def take(is_split):
        count = 2 if is_split else 1
        parts = tuple(refs[:count])
        del refs[:count]
        return parts

    def read(parts, cols=slice(None)):
        if len(parts) == 1:
            return parts[0][:, cols]
        return jnp.where(step < head_tiles, parts[0][:, cols], parts[1][:, cols])

    xs = [take(s) for s in split[:n_in]]
    g_ref = refs.pop(0) if has_gain else None
    w_hbm = refs.pop(0)
    res = take(split[n_in]) if has_res else None
    o_ref, w_ref, stage, sem = refs

    @pl.when(step == 0)
    def _():
        _load_bf16(w_hbm, w_ref, stage, sem, axis=1, chunk=cn)

    if has_gain:
        lhs = [_rms(read(xs[0]), g_ref[...]).astype(BF16)]
    else:
        lhs = [read(x).astype(BF16) for x in xs]
    n = w_ref.shape[1]
    for c in range(n // cn):
        cols = slice(c * cn, (c + 1) * cn)
        acc = None
        row0 = 0
        for a, k in zip(lhs, ks):
            d = jnp.dot(a, w_ref[row0:row0 + k, cols], preferred_element_type=F32)
            acc = d if acc is None else acc + d
            row0 += k
        if has_res:
            acc = read(res, cols) + acc
        if slab_out:
            for s in range(cn // LANES):
                o_ref[c * (cn // LANES) + s] = acc[:, s * LANES:(s + 1) * LANES].astype(o_ref.dtype)
        else:
            o_ref[:, cols] = acc.astype(o_ref.dtype)


def _dense(xs, w, *, gain=None, res=None, out_dtype, name, slab_out=False):
    def n_rows(a):
        return sum(p.shape[0] for p in a) if isinstance(a, tuple) else a.shape[0]

    def width(a):
        return a[0].shape[1] if isinstance(a, tuple) else a.shape[1]

    operands = list(xs) + ([res] if res is not None else [])
    pairs = [a for a in operands if isinstance(a, tuple)]
    t = n_rows(xs[0])
    k_all, n = w.shape
    ks = tuple(width(x) for x in xs)
    assert sum(ks) == k_all
    gcd_rows = t
    for a in pairs:
        gcd_rows = min(gcd_rows, a[1].shape[0])
    tm = _pick(gcd_rows, (512, 256, 128, 64, 32, 16, 8))
    assert t % tm == 0
    head_tiles = pairs[0][0].shape[0] // tm if pairs else 0
    for a in pairs:
        assert a[0].shape[0] == head_tiles * tm
    cn = _pick(n, (512, 384, 256, 128))
    in_specs, args = [], []

    def add_rows(a, k):
        if isinstance(a, tuple):
            in_specs.append(pl.BlockSpec((tm, k), lambda i: (jnp.minimum(i, head_tiles - 1), 0)))
            in_specs.append(pl.BlockSpec((tm, k), lambda i: (jnp.maximum(i - head_tiles, 0), 0)))
            args.extend(a)
        else:
            in_specs.append(pl.BlockSpec((tm, k), lambda i: (i, 0)))
            args.append(a)

    for x, k in zip(xs, ks):
        add_rows(x, k)
    if gain is not None:
        in_specs.append(pl.BlockSpec((1, ks[0]), lambda i: (0, 0)))
        args.append(gain.reshape(1, -1))
    in_specs.append(pl.BlockSpec(memory_space=pl.ANY))
    args.append(w)
    if res is not None:
        add_rows(res, n)
    split = tuple(isinstance(a, tuple) for a in operands)
    if slab_out:
        out_shape = jax.ShapeDtypeStruct((n // LANES, t, LANES), out_dtype)
        out_spec = pl.BlockSpec((n // LANES, tm, LANES), lambda i: (0, i, 0))
    else:
        out_shape = jax.ShapeDtypeStruct((t, n), out_dtype)
        out_spec = pl.BlockSpec((tm, n), lambda i: (i, 0))
    return pl.pallas_call(
        functools.partial(_dense_body, ks=ks, split=split, head_tiles=head_tiles, has_gain=gain is not None,
                          has_res=res is not None, slab_out=slab_out, cn=cn),
        out_shape=out_shape,
        grid=(t // tm,),
        in_specs=in_specs,
        out_specs=out_spec,
        scratch_shapes=[pltpu.VMEM((k_all, n), BF16), pltpu.VMEM((2, k_all, cn), F32),
                        pltpu.SemaphoreType.DMA((2,))],
        compiler_params=_params("arbitrary"),
        name=name,
    )(*args)


def _proj_t_body(w_ref, x_ref, g_ref, o_ref):
    xn = _rms(x_ref[...], g_ref[...]).astype(BF16)
    o_ref[...] = lax.dot_general(w_ref[...].astype(BF16), xn, NT_DIMS, preferred_element_type=F32)


def _proj_t(w_t, x, gain, name):
    c, k = w_t.shape
    rows = x.shape[0]
    tc = _pick(c, (512, 256, 128))
    return pl.pallas_call(
        _proj_t_body,
        out_shape=jax.ShapeDtypeStruct((c, rows), F32),
        grid=(c // tc,),
        in_specs=[pl.BlockSpec((tc, k), lambda i: (i, 0)),
                  pl.BlockSpec((rows, k), lambda i: (0, 0)),
                  pl.BlockSpec((1, k), lambda i: (0, 0))],
        out_specs=pl.BlockSpec((tc, rows), lambda i: (i, 0)),
        compiler_params=_params("parallel"),
        name=name,
    )(w_t, x, gain.reshape(1, -1))


def _ffn_body(x_ref, g_ref, wg_hbm, wu_hbm, wd_hbm, o_ref, wg_ref, wu_ref, wd_ref, stage_in, stage_out, sem, *,
              tf, ld):
    @pl.when(pl.program_id(0) == 0)
    def _():
        _load_bf16(wg_hbm, wg_ref, stage_in, sem, axis=1, chunk=ld)
        _load_bf16(wu_hbm, wu_ref, stage_in, sem, axis=1, chunk=ld)
        _load_bf16(wd_hbm, wd_ref, stage_out, sem, axis=0, chunk=ld)

    x = x_ref[...]
    xn = _rms(x, g_ref[...]).astype(BF16)
    acc = None
    for c in range(wg_ref.shape[1] // tf):
        cols = slice(c * tf, (c + 1) * tf)
        gate = jnp.dot(xn, wg_ref[:, cols], preferred_element_type=F32)
        up = jnp.dot(xn, wu_ref[:, cols], preferred_element_type=F32)
        act = (gate * jax.nn.sigmoid(gate) * up).astype(BF16)
        d = jnp.dot(act, wd_ref[cols, :], preferred_element_type=F32)
        acc = d if acc is None else acc + d
    o_ref[...] = x + acc


def _ffn(x, gain, w_gate, w_up, w_down, name):
    t, d = x.shape
    f = w_gate.shape[1]
    tm = _pick(t, (512, 256, 128, 64, 32, 16, 8))
    tf = f
    ld = _pick(f, (256, 128))
    anywhere = pl.BlockSpec(memory_space=pl.ANY)
    return pl.pallas_call(
        functools.partial(_ffn_body, tf=tf, ld=ld),
        out_shape=jax.ShapeDtypeStruct((t, d), F32),
        grid=(t // tm,),
        in_specs=[pl.BlockSpec((tm, d), lambda i: (i, 0)),
                  pl.BlockSpec((1, d), lambda i: (0, 0)),
                  anywhere, anywhere, anywhere],
        out_specs=pl.BlockSpec((tm, d), lambda i: (i, 0)),
        scratch_shapes=[pltpu.VMEM((d, f), BF16), pltpu.VMEM((d, f), BF16), pltpu.VMEM((f, d), BF16),
                        pltpu.VMEM((2, d, ld), F32), pltpu.VMEM((2, ld, d), F32),
                        pltpu.SemaphoreType.DMA((2,))],
        compiler_params=_params("arbitrary"),
        name=name,
    )(x, gain.reshape(1, -1), w_gate, w_up, w_down)


def _band_body(*refs, dil, mb, shared_kv, has_sink, want_lse):
    q_ref, kc_ref, kp_ref, vc_ref, vp_ref = refs[:5]
    pos = 5
    sink_ref = refs[pos] if has_sink else None
    pos += int(has_sink)
    o_ref = refs[pos]
    lse_ref = refs[pos + 1] if want_lse else None
    t = pl.program_id(1)
    scale = HEAD_DIM ** -0.5
    qi = lax.broadcasted_iota(I32, (2 * SPAN, 2 * SPAN), 0) & (SPAN - 1)
    kj = lax.broadcasted_iota(I32, (2 * SPAN, 2 * SPAN), 1)
    band = (kj >= qi) & (kj <= qi + SPAN)
    band_first = band & (kj >= jnp.where(t == 0, SPAN, 0))
    lane = lax.broadcasted_iota(I32, (1, LANES), 1)
    halves = [lane < HEAD_DIM, lane >= HEAD_DIM]

    def rows_of(ref, s, r, count):
        if dil == 1:
            return ref[s, 0:count, :]
        return ref[s, pl.ds(r, count, stride=dil), :]

    def pair(q_a, mask_a, q_b, mask_b, kw, vw, msk, heads):
        q2 = jnp.concatenate([jnp.where(mask_a, q_a, 0.0), jnp.where(mask_b, q_b, 0.0)], axis=0).astype(BF16)
        s = lax.dot_general(q2, kw, NT_DIMS, preferred_element_type=F32)
        s = jnp.where(msk, s, -jnp.inf)
        m = jnp.max(s, axis=-1, keepdims=True)
        if has_sink:
            sink = jnp.concatenate([jnp.full((SPAN, 1), sink_ref[h], F32) for h in heads], axis=0)
            m = jnp.maximum(m, sink)
        p = jnp.exp(s - m)
        den = jnp.sum(p, axis=-1, keepdims=True)
        if has_sink:
            den = den + jnp.exp(sink - m)
        o = jnp.dot(p.astype(BF16), vw, preferred_element_type=F32) / den
        return o, m + jnp.log(den)

    def store(s, r, b, o_tile, lse_tile):
        if dil == 1:
            o_ref[b * SPAN:(b + 1) * SPAN, s * LANES:(s + 1) * LANES] = o_tile.astype(o_ref.dtype)
            if want_lse:
                lse_ref[b * SPAN:(b + 1) * SPAN, s * LANES:(s + 1) * LANES] = lse_tile
        else:
            o_ref[s, pl.ds(r + dil * b * SPAN, SPAN, stride=dil), :] = o_tile
            if want_lse:
                lse_ref[s, pl.ds(r + dil * b * SPAN, SPAN, stride=dil), :] = lse_tile

    n_slabs = q_ref.shape[0]
    first_head = 2 * n_slabs * pl.program_id(2)
    top, bot = slice(0, SPAN), slice(SPAN, 2 * SPAN)
    for r in range(dil):
        qs = [rows_of(q_ref, s, r, mb * SPAN) * scale for s in range(n_slabs)]
        kf = [jnp.concatenate([rows_of(kp_ref, s, r, SPAN), rows_of(kc_ref, s, r, mb * SPAN)], axis=0)
              for s in range(kc_ref.shape[0])]
        vf = [jnp.concatenate([rows_of(vp_ref, s, r, SPAN), rows_of(vc_ref, s, r, mb * SPAN)], axis=0)
              for s in range(vc_ref.shape[0])]
        ks = [k.astype(BF16) for k in kf]
        vs = [v.astype(BF16) for v in vf]
        if shared_kv:
            k_sw = pltpu.roll(kf[0], HEAD_DIM, 1).astype(BF16)
            v_sw = pltpu.roll(vf[0], HEAD_DIM, 1).astype(BF16)
        for b in range(mb):
            blk = slice(b * SPAN, (b + 1) * SPAN)
            keys = slice(b * SPAN, (b + 2) * SPAN)
            msk = band_first if b == 0 else band
            if not shared_kv:
                for s in range(n_slabs):
                    o, lse = pair(qs[s][blk], halves[0], qs[s][blk], halves[1], ks[s][keys], vs[s][keys], msk,
                                  (first_head + 2 * s, first_head + 2 * s + 1))
                    store(s, r, b, jnp.where(halves[0], o[top], o[bot]), jnp.where(halves[0], lse[top], lse[bot]))
            else:
                for kvh in range(2):
                    sa, sb = 2 * kvh, 2 * kvh + 1
                    o_al, l_al = pair(qs[sa][blk], halves[kvh], qs[sb][blk], halves[kvh], ks[0][keys],
                                      vs[0][keys], msk, (2 * sa + kvh, 2 * sb + kvh))
                    o_sw, l_sw = pair(qs[sa][blk], halves[1 - kvh], qs[sb][blk], halves[1 - kvh], k_sw[keys],
                                      v_sw[keys], msk, (2 * sa + 1 - kvh, 2 * sb + 1 - kvh))
                    store(sa, r, b, jnp.where(halves[kvh], o_al[top], o_sw[top]),
                          jnp.where(halves[kvh], l_al[top], l_sw[top]))
                    store(sb, r, b, jnp.where(halves[kvh], o_al[bot], o_sw[bot]),
                          jnp.where(halves[kvh], l_al[bot], l_sw[bot]))


def _band_attn(slabs, *, n_seq, seq_len, dil, q_slab, k_slab, v_slab, n_q_slabs, shared_kv=False, sink=None,
               want_lse, name):
    mb = max(1, 512 // (dil * SPAN))
    tp = dil * SPAN * mb
    nt = seq_len // tp
    assert seq_len % tp == 0 and (not shared_kv or (n_q_slabs == 4 and dil == 1))
    prev_rows = dil * SPAN
    per_step = n_q_slabs if (shared_kv or tp <= 512) else 1
    kv_block = 1 if shared_kv else per_step
    kv_step = 0 if shared_kv else 1

    def cur(base, step):
        return lambda n, t, s: (base + s * step, n * nt + t, 0)

    def prev(base, step):
        return lambda n, t, s: (base + s * step, jnp.maximum((n * nt + t) * mb - 1, 0), 0)

    in_specs = [pl.BlockSpec((per_step, tp, LANES), cur(q_slab // per_step, 1)),
                pl.BlockSpec((kv_block, tp, LANES), cur(k_slab // kv_block, kv_step)),
                pl.BlockSpec((kv_block, prev_rows, LANES), prev(k_slab // kv_block, kv_step)),
                pl.BlockSpec((kv_block, tp, LANES), cur(v_slab // kv_block, kv_step)),
                pl.BlockSpec((kv_block, prev_rows, LANES), prev(v_slab // kv_block, kv_step))]
    assert q_slab % per_step == 0 and k_slab % kv_block == 0 and v_slab % kv_block == 0
    args = [slabs] * 5
    if sink is not None:
        in_specs.append(pl.BlockSpec(memory_space=pltpu.SMEM))
        args.append(sink.reshape(-1).astype(F32))
    rows = n_seq * seq_len
    if dil == 1:
        spec = pl.BlockSpec((tp, per_step * LANES), lambda n, t, s: (n * nt + t, s))
        out_shape = [jax.ShapeDtypeStruct((rows, n_q_slabs * LANES), BF16)]
        lse_shape = jax.ShapeDtypeStruct((rows, n_q_slabs * LANES), F32)
    else:
        spec = pl.BlockSpec((per_step, tp, LANES), lambda n, t, s: (s, n * nt + t, 0))
        out_shape = [jax.ShapeDtypeStruct((n_q_slabs, rows, LANES), F32)]
        lse_shape = jax.ShapeDtypeStruct((n_q_slabs, rows, LANES), F32)
    out_specs = [spec]
    if want_lse:
        out_shape.append(lse_shape)
        out_specs.append(spec)
    return pl.pallas_call(
        functools.partial(_band_body, dil=dil, mb=mb, shared_kv=shared_kv, has_sink=sink is not None,
                          want_lse=want_lse),
        out_shape=out_shape,
        grid=(n_seq, nt, n_q_slabs // per_step),
        in_specs=in_specs,
        out_specs=out_specs,
        compiler_params=_params("parallel", "parallel", "parallel"),
        name=name,
    )(*args)


def _combine_body(o0, l0, o1, l1, o2, l2, out_ref):
    def tile(ref):
        return ref[...].reshape(ref.shape[-2:]).astype(F32)

    a0, a1, a2 = tile(l0), tile(l1), tile(l2)
    m = jnp.maximum(jnp.maximum(a0, a1), a2)
    e0, e1, e2 = jnp.exp(a0 - m), jnp.exp(a1 - m), jnp.exp(a2 - m)
    num = e0 * tile(o0) + e1 * tile(o1) + e2 * tile(o2)
    out_ref[...] = (num / (e0 + e1 + e2)).astype(out_ref.dtype)


def _combine(pairs, name):
    first = pairs[0][0]
    rows, c = first.shape if first.ndim == 2 else (first.shape[1], first.shape[0] * LANES)
    tm = _pick(rows, (1024, 512, 256, 128, 64, 32, 16, 8))
    flat = pl.BlockSpec((tm, LANES), lambda i, s: (i, s))
    slab = pl.BlockSpec((1, tm, LANES), lambda i, s: (s, i, 0))
    args = [a for pair in pairs for a in pair]
    return pl.pallas_call(
        _combine_body,
        out_shape=jax.ShapeDtypeStruct((rows, c), BF16),
        grid=(rows // tm, c // LANES),
        in_specs=[flat if a.ndim == 2 else slab for a in args],
        out_specs=flat,
        compiler_params=_params("parallel", "parallel"),
        name=name,
    )(*args)


def _state_body(x_ref, o_ref):
    o_ref[0, 0] = x_ref[0].T


def _state_slabs(slabs, *, first_slab, n_slabs, n_seq, seq_len, keep, name):
    assert seq_len % keep == 0
    per_seq = seq_len // keep
    return pl.pallas_call(
        _state_body,
        out_shape=jax.ShapeDtypeStruct((n_seq, n_slabs, LANES, keep), F32),
        grid=(n_seq, n_slabs),
        in_specs=[pl.BlockSpec((1, keep, LANES), lambda n, s: (first_slab + s, (n + 1) * per_seq - 1, 0))],
        out_specs=pl.BlockSpec((1, 1, LANES, keep), lambda n, s: (n, s, 0, 0)),
        compiler_params=_params("parallel", "parallel"),
        name=name,
    )(slabs)


def _conv_prompt_body(gb_ref, gc_ref, xa_ref, gcp_ref, xap_ref, w_ref, a_ref, st_ref):
    t = pl.program_id(1)
    w = w_ref[...]
    for s in range(gb_ref.shape[0]):
        lanes = slice(s * LANES, (s + 1) * LANES)
        u = gc_ref[s] * xa_ref[s]
        up = jnp.where(t == 0, 0.0, gcp_ref[s] * xap_ref[s])
        ext = jnp.concatenate([up, u], axis=0)
        y = (w[0:1, lanes] * pltpu.roll(ext, 2, 0)[SUBLANES:]
             + w[1:2, lanes] * pltpu.roll(ext, 1, 0)[SUBLANES:]) + w[2:3, lanes] * u
        a_ref[:, lanes] = (gb_ref[s] * y).astype(a_ref.dtype)
        st_ref[0, :, lanes] = u[u.shape[0] - SUBLANES:]


def _conv_prompt(slabs, conv_w, *, n_seq, seq_len, c):
    tq = _pick(seq_len, (512, 256, 128))
    nt = seq_len // tq
    rb = tq // SUBLANES
    ns = c // LANES

    def cur(part):
        return lambda n, t: (part, n * nt + t, 0)

    def prev(part):
        return lambda n, t: (part, jnp.maximum((n * nt + t) * rb - 1, 0), 0)

    return pl.pallas_call(
        _conv_prompt_body,
        out_shape=[jax.ShapeDtypeStruct((n_seq * seq_len, c), BF16),
                   jax.ShapeDtypeStruct((n_seq, SUBLANES, c), F32)],
        grid=(n_seq, nt),
        in_specs=[pl.BlockSpec((ns, tq, LANES), cur(0)), pl.BlockSpec((ns, tq, LANES), cur(1)),
                  pl.BlockSpec((ns, tq, LANES), cur(2)),
                  pl.BlockSpec((ns, SUBLANES, LANES), prev(1)), pl.BlockSpec((ns, SUBLANES, LANES), prev(2)),
                  pl.BlockSpec((CONV_W, c), lambda n, t: (0, 0))],
        out_specs=[pl.BlockSpec((tq, c), lambda n, t: (n * nt + t, 0)),
                   pl.BlockSpec((1, SUBLANES, c), lambda n, t: (n, 0, 0))],
        compiler_params=_params("parallel", "arbitrary"),
        name="conv_prompt",
    )(slabs, slabs, slabs, slabs, slabs, conv_w)


def _conv_step_body(p_ref, prev_ref, w_ref, a_ref, st_ref):
    s_len = p_ref.shape[1]
    w = w_ref[...]
    hist = [prev_ref[k] for k in range(CONV_W - 1)] + [p_ref[1, s] * p_ref[2, s] for s in range(s_len)]
    for s in range(s_len):
        y = (w[0:1] * hist[s] + w[1:2] * hist[s + 1]) + w[2:3] * hist[s + 2]
        a_ref[s] = p_ref[0, s] * y
    for k in range(CONV_W - 1):
        st_ref[k] = hist[s_len + k]


def _conv_step(p3, prev, conv_w):
    _, s_len, n, c = p3.shape
    return pl.pallas_call(
        _conv_step_body,
        out_shape=[jax.ShapeDtypeStruct((s_len, n, c), F32), jax.ShapeDtypeStruct((CONV_W - 1, n, c), F32)],
        name="conv_step",
        compiler_params=pltpu.CompilerParams(vmem_limit_bytes=VMEM_LIMIT_BYTES),
    )(p3, prev, conv_w)


def _step_body(*refs, n_blk, qr, tok_div, w, dil, cw, bsz, has_sink, want_lse):
    q_ref, kvt_ref, cache_ref = refs[:3]
    pos = 3
    sink_ref = refs[pos] if has_sink else None
    pos += int(has_sink)
    o_ref = refs[pos]
    pos += 1
    lse_ref = refs[pos] if want_lse else None
    pos += int(want_lse)
    cout_ref = refs[pos]
    r_dim = n_blk * HEAD_DIM
    nrb = n_blk * qr
    n_chunks = w // cw
    scale = HEAD_DIM ** -0.5
    step = pl.program_id(0)
    lane_r = lax.broadcasted_iota(I32, (1, r_dim), 1)
    blk_masks = [(lane_r >= j * HEAD_DIM) & (lane_r < (j + 1) * HEAD_DIM) for j in range(n_blk)]
    row = lax.broadcasted_iota(I32, (nrb, 1), 0)
    tok = (row % qr) // tok_div
    lane_c = lax.broadcasted_iota(I32, (1, cw), 1)
    lane_n = lax.broadcasted_iota(I32, (1, LANES), 1)

    def valid(pos_l):
        ok = (pos_l >= tok) & (pos_l <= w + tok)
        if dil > 1:
            ok = ok & (((pos_l - tok) & (dil - 1)) == 0)
        return ok

    for b in range(bsz):
        off = ((step * bsz + b) * 4) % LANES
        shift = (LANES - off) % LANES
        new_k = pltpu.roll(kvt_ref[0], shift, 1)
        new_v = pltpu.roll(kvt_ref[1], shift, 1)
        q = q_ref[b] * scale
        qbd = jnp.concatenate([jnp.where(mk, q, 0.0) for mk in blk_masks], axis=0).astype(BF16)
        scores = []
        for c in range(n_chunks):
            lo, hi = c * cw, (c + 1) * cw
            kc = cache_ref[b, 0, :, lo:hi]
            sc = jnp.dot(qbd, kc.astype(BF16), preferred_element_type=F32)
            scores.append(jnp.where(valid(lane_c + lo), sc, -jnp.inf))
            nxt = cache_ref[b, 0, :, hi:hi + LANES] if c + 1 < n_chunks else new_k
            ext = jnp.concatenate([kc, nxt], axis=1)
            cout_ref[b, 0, :, lo:hi] = pltpu.roll(ext, cw + LANES - 4, 1)[:, :cw]
        sc = jnp.dot(qbd, new_k.astype(BF16), preferred_element_type=F32)
        scores.append(jnp.where(valid(lane_n + w), sc, -jnp.inf))
        m = functools.reduce(jnp.maximum, [jnp.max(s, axis=-1, keepdims=True) for s in scores])
        if has_sink:
            m = jnp.maximum(m, sink_ref[...])
        probs = [jnp.exp(s - m) for s in scores]
        den = functools.reduce(lambda a, c: a + c, [jnp.sum(p, axis=-1, keepdims=True) for p in probs])
        if has_sink:
            den = den + jnp.exp(sink_ref[...] - m)
        pv = lax.dot_general(probs[-1].astype(BF16), new_v.astype(BF16), NT_DIMS, preferred_element_type=F32)
        for c in range(n_chunks):
            lo, hi = c * cw, (c + 1) * cw
            vc = cache_ref[b, 1, :, lo:hi]
            pv = pv + lax.dot_general(probs[c].astype(BF16), vc.astype(BF16), NT_DIMS,
                                      preferred_element_type=F32)
            nxt = cache_ref[b, 1, :, hi:hi + LANES] if c + 1 < n_chunks else new_v
            ext = jnp.concatenate([vc, nxt], axis=1)
            cout_ref[b, 1, :, lo:hi] = pltpu.roll(ext, cw + LANES - 4, 1)[:, :cw]
        o = jnp.zeros((qr, r_dim), F32)
        m_e = jnp.zeros((qr, r_dim), F32)
        den_e = jnp.zeros((qr, r_dim), F32)
        for j, mk in enumerate(blk_masks):
            rows = slice(j * qr, (j + 1) * qr)
            o = jnp.where(mk, pv[rows], o)
            m_e = jnp.where(mk, m[rows], m_e)
            den_e = jnp.where(mk, den[rows], den_e)
        o_ref[b] = o / den_e
        if want_lse:
            lse_ref[b] = m_e + jnp.log(den_e)


def _step_attn(q, kvt, cache, *, dil, tok_div, sink_col=None, want_lse, bsz, name):
    n, qr, r_dim = q.shape
    w = cache.shape[-1]
    n_blk = r_dim // HEAD_DIM
    cw = min(w, 512)
    in_specs = [pl.BlockSpec((bsz, qr, r_dim), lambda i: (i, 0, 0)),
                pl.BlockSpec((2, r_dim, LANES), lambda i: (0, 0, (i * bsz * 4) // LANES)),
                pl.BlockSpec((bsz, 2, r_dim, w), lambda i: (i, 0, 0, 0))]
    args = [q, kvt, cache]
    if sink_col is not None:
        in_specs.append(pl.BlockSpec((n_blk * qr, 1), lambda i: (0, 0)))
        args.append(sink_col)
    o_spec = pl.BlockSpec((bsz, qr, r_dim), lambda i: (i, 0, 0))
    out_shape = [jax.ShapeDtypeStruct((n, qr, r_dim), F32)]
    out_specs = [o_spec]
    if want_lse:
        out_shape.append(jax.ShapeDtypeStruct((n, qr, r_dim), F32))
        out_specs.append(o_spec)
    out_shape.append(jax.ShapeDtypeStruct(cache.shape, F32))
    out_specs.append(pl.BlockSpec((bsz, 2, r_dim, w), lambda i: (i, 0, 0, 0)))
    return pl.pallas_call(
        functools.partial(_step_body, n_blk=n_blk, qr=qr, tok_div=tok_div, w=w, dil=dil, cw=cw, bsz=bsz,
                          has_sink=sink_col is not None, want_lse=want_lse),
        out_shape=out_shape,
        grid=(n // bsz,),
        in_specs=in_specs,
        out_specs=out_specs,
        compiler_params=_params("parallel"),
        name=name,
    )(*args)


def _router_body(h_ref, g_ref, wr_ref, xn_ref, im_ref, gm_ref, cnt_ref, carry_ref, *, n_exp):
    i = pl.program_id(0)

    @pl.when(i == 0)
    def _():
        carry_ref[...] = jnp.zeros_like(carry_ref)

    xn = _rms(h_ref[...], g_ref[...])
    xn_ref[...] = xn
    tm = xn.shape[0]
    wr = wr_ref[...]
    xh = xn.astype(BF16)
    xl = (xn - xh.astype(F32)).astype(BF16)
    wh = wr.astype(BF16)
    wl = (wr - wh.astype(F32)).astype(BF16)
    lg = jnp.dot(xh, wh, preferred_element_type=F32) + (
        jnp.dot(xh, wl, preferred_element_type=F32) + jnp.dot(xl, wh, preferred_element_type=F32))
    lane = lax.broadcasted_iota(I32, (tm, LANES), 1)
    lane_f = lane.astype(F32)
    lg = jnp.where(lane < n_exp, lg, -jnp.inf)
    m1 = jnp.max(lg, axis=-1, keepdims=True)
    i1 = jnp.min(jnp.where(lg == m1, lane_f, float(LANES)), axis=-1, keepdims=True)
    lg2 = jnp.where(lane_f == i1, -jnp.inf, lg)
    m2 = jnp.max(lg2, axis=-1, keepdims=True)
    i2 = jnp.min(jnp.where(lg2 == m2, lane_f, float(LANES)), axis=-1, keepdims=True)
    e = jnp.exp(m2 - m1)
    g1 = 1.0 / (1.0 + e)
    g2 = e / (1.0 + e)
    sel1 = lane_f == i1
    sel2 = lane_f == i2
    onehot = jnp.where(sel1 | sel2, 1.0, 0.0)
    r_i = lax.broadcasted_iota(I32, (tm, tm), 0)
    c_i = lax.broadcasted_iota(I32, (tm, tm), 1)
    tri = jnp.where(c_i < r_i, 1.0, 0.0).astype(BF16)
    before = jnp.dot(tri, onehot.astype(BF16), preferred_element_type=F32) + carry_ref[0:1]
    r1 = jnp.sum(jnp.where(sel1, before, 0.0), axis=-1, keepdims=True)
    r2 = jnp.sum(jnp.where(sel2, before, 0.0), axis=-1, keepdims=True)
    total = carry_ref[0:1] + jnp.sum(onehot, axis=0, keepdims=True)
    carry_ref[...] = jnp.broadcast_to(total, carry_ref.shape)
    cnt_ref[...] = jnp.broadcast_to(total, cnt_ref.shape).astype(I32)
    meta = jnp.where(lane == 0, i1, jnp.where(lane == 1, i2, jnp.where(lane == 2, r1, jnp.where(lane == 3, r2, 0.0))))
    im_ref[...] = meta.astype(I32)
    gm_ref[...] = jnp.where(lane == 0, g1, jnp.where(lane == 1, g2, 0.0))


def _router(h, gain, w_router):
    t, d = h.shape
    n_exp = w_router.shape[1]
    tm = _pick(t, (512, 256, 128, 64, 32, 16, 8))
    wr = jnp.zeros((d, LANES), F32).at[:, :n_exp].set(w_router)
    return pl.pallas_call(
        functools.partial(_router_body, n_exp=n_exp),
        out_shape=[jax.ShapeDtypeStruct((t, d), F32), jax.ShapeDtypeStruct((t, LANES), I32),
                   jax.ShapeDtypeStruct((t, LANES), F32), jax.ShapeDtypeStruct((SUBLANES, LANES), I32)],
        grid=(t // tm,),
        in_specs=[pl.BlockSpec((tm, d), lambda i: (i, 0)),
                  pl.BlockSpec((1, d), lambda i: (0, 0)),
                  pl.BlockSpec((d, LANES), lambda i: (0, 0))],
        out_specs=[pl.BlockSpec((tm, d), lambda i: (i, 0)),
                   pl.BlockSpec((tm, LANES), lambda i: (i, 0)),
                   pl.BlockSpec((tm, LANES), lambda i: (i, 0)),
                   pl.BlockSpec((SUBLANES, LANES), lambda i: (0, 0))],
        scratch_shapes=[pltpu.VMEM((SUBLANES, LANES), F32)],
        compiler_params=_params("arbitrary"),
        name="moe_router",
    )(h, gain.reshape(1, -1), wr)


def _dispatch_body(s1_ref, s2_ref, x_ref, zero_ref, out_ref, sem, *, tm):
    del zero_ref

    def issue(r, carry):
        src = x_ref.at[pl.ds(r, 1)]
        pltpu.make_async_copy(src, out_ref.at[pl.ds(s1_ref[0, 0, r], 1)], sem.at[0]).start(0)
        pltpu.make_async_copy(src, out_ref.at[pl.ds(s2_ref[0, 0, r], 1)], sem.at[1]).start(1)
        return carry

    lax.fori_loop(0, tm, issue, 0, unroll=8)
    pltpu.make_async_copy(x_ref, out_ref.at[pl.ds(0, tm)], sem.at[0]).wait()
    pltpu.make_async_copy(x_ref, out_ref.at[pl.ds(0, tm)], sem.at[1]).wait()


def _dispatch(xn, slot1, slot2, n_slots):
    t, d = xn.shape
    tm = _pick(t, (512, 256, 128, 64, 32, 16, 8))
    nt = t // tm
    smem = pl.BlockSpec((1, 1, tm), lambda i: (i, 0, 0), memory_space=pltpu.SMEM)
    return pl.pallas_call(
        functools.partial(_dispatch_body, tm=tm),
        out_shape=jax.ShapeDtypeStruct((n_slots, d), F32),
        grid=(nt,),
        in_specs=[smem, smem, pl.BlockSpec((tm, d), lambda i: (i, 0)), pl.BlockSpec(memory_space=pl.ANY)],
        out_specs=pl.BlockSpec(memory_space=pl.ANY),
        scratch_shapes=[pltpu.SemaphoreType.DMA((2,))],
        input_output_aliases={3: 0},
        compiler_params=_params("arbitrary"),
        name="moe_dispatch",
    )(slot1.reshape(nt, 1, tm), slot2.reshape(nt, 1, tm), xn, jnp.zeros((n_slots, d), F32))


def _experts_body(te_ref, nu_ref, x_ref, wg_hbm, wu_hbm, wd_hbm, o_ref, wg_ref, wu_ref, wd_ref, stage_in, stage_out,
                  sem, *, tf, ld):
    i = pl.program_id(0)
    active = i < nu_ref[0]
    expert = te_ref[i]
    changed = (i == 0) | (expert != te_ref[jnp.maximum(i - 1, 0)])

    @pl.when(active & changed)
    def _():
        _load_bf16(wg_hbm.at[expert], wg_ref, stage_in, sem, axis=1, chunk=ld)
        _load_bf16(wu_hbm.at[expert], wu_ref, stage_in, sem, axis=1, chunk=ld)
        _load_bf16(wd_hbm.at[expert], wd_ref, stage_out, sem, axis=0, chunk=ld)

    @pl.when(active)
    def _():
        xb = x_ref[...].astype(BF16)
        acc = None
        for c in range(wg_ref.shape[1] // tf):
            cols = slice(c * tf, (c + 1) * tf)
            gate = jnp.dot(xb, wg_ref[:, cols], preferred_element_type=F32)
            up = jnp.dot(xb, wu_ref[:, cols], preferred_element_type=F32)
            act = (gate * jax.nn.sigmoid(gate) * up).astype(BF16)
            d = jnp.dot(act, wd_ref[cols, :], preferred_element_type=F32)
            acc = d if acc is None else acc + d
        o_ref[...] = acc

    @pl.when(jnp.logical_not(active))
    def _():
        o_ref[...] = jnp.zeros_like(o_ref)


def _experts(xs, tile_expert, n_used, w_gate, w_up, w_down, tm):
    n_slots, d = xs.shape
    f = w_gate.shape[2]
    tf = _pick(f, (1792, 1024, 512, 256, 128))
    ld = _pick(f, (256, 128))
    n_tiles = n_slots // tm
    anywhere = pl.BlockSpec(memory_space=pl.ANY)
    grid_spec = pltpu.PrefetchScalarGridSpec(
        num_scalar_prefetch=2,
        grid=(n_tiles,),
        in_specs=[pl.BlockSpec((tm, d), lambda i, te, nu: (jnp.minimum(i, nu[0] - 1), 0)),
                  anywhere, anywhere, anywhere],
        out_specs=pl.BlockSpec((tm, d), lambda i, te, nu: (i, 0)),
        scratch_shapes=[pltpu.VMEM((d, f), BF16), pltpu.VMEM((d, f), BF16), pltpu.VMEM((f, d), BF16),
                        pltpu.VMEM((2, d, ld), F32), pltpu.VMEM((2, ld, d), F32),
                        pltpu.SemaphoreType.DMA((2,))],
    )
    return pl.pallas_call(
        functools.partial(_experts_body, tf=tf, ld=ld),
        out_shape=jax.ShapeDtypeStruct((n_slots, d), F32),
        grid_spec=grid_spec,
        compiler_params=_params("arbitrary"),
        name="moe_experts",
    )(tile_expert, n_used, xs, w_gate, w_up, w_down)


def _gather_norm_body(s1_ref, s2_ref, h_ref, gm_ref, g_ref, ys_ref, oa_ref, ob_ref, ya_ref, yb_ref, sem, *,
                      tm, head_tiles):
    def issue(r, carry):
        pltpu.make_async_copy(ys_ref.at[pl.ds(s1_ref[0, 0, r], 1)], ya_ref.at[pl.ds(r, 1)], sem.at[0]).start(0)
        pltpu.make_async_copy(ys_ref.at[pl.ds(s2_ref[0, 0, r], 1)], yb_ref.at[pl.ds(r, 1)], sem.at[1]).start(1)
        return carry

    lax.fori_loop(0, tm, issue, 0, unroll=8)
    pltpu.make_async_copy(ys_ref.at[pl.ds(0, tm)], ya_ref, sem.at[0]).wait()
    pltpu.make_async_copy(ys_ref.at[pl.ds(0, tm)], yb_ref, sem.at[1]).wait()
    gm = gm_ref[...]
    y = _rms(h_ref[...] + (gm[:, 0:1] * ya_ref[...] + gm[:, 1:2] * yb_ref[...]), g_ref[...])
    i = pl.program_id(0)

    @pl.when(i < head_tiles)
    def _():
        oa_ref[...] = y

    @pl.when(i >= head_tiles)
    def _():
        ob_ref[...] = y


def _gather_norm(h, gates, slot1, slot2, ys, gain, split):
    t, d = h.shape
    tm = _pick(split, (512, 256, 128, 64, 32, 16, 8))
    assert t % tm == 0
    nt = t // tm
    head_tiles = split // tm
    smem = pl.BlockSpec((1, 1, tm), lambda i: (i, 0, 0), memory_space=pltpu.SMEM)
    return pl.pallas_call(
        functools.partial(_gather_norm_body, tm=tm, head_tiles=head_tiles),
        out_shape=[jax.ShapeDtypeStruct((split, d), F32), jax.ShapeDtypeStruct((t - split, d), F32)],
        grid=(nt,),
        in_specs=[smem, smem, pl.BlockSpec((tm, d), lambda i: (i, 0)),
                  pl.BlockSpec((tm, LANES), lambda i: (i, 0)),
                  pl.BlockSpec((1, d), lambda i: (0, 0)),
                  pl.BlockSpec(memory_space=pl.ANY)],
        out_specs=[pl.BlockSpec((tm, d), lambda i: (jnp.minimum(i, head_tiles - 1), 0)),
                   pl.BlockSpec((tm, d), lambda i: (jnp.maximum(i - head_tiles, 0), 0))],
        scratch_shapes=[pltpu.VMEM((tm, d), F32), pltpu.VMEM((tm, d), F32), pltpu.SemaphoreType.DMA((2,))],
        compiler_params=_params("arbitrary"),
        name="moe_gather_norm",
    )(slot1.reshape(nt, 1, tm), slot2.reshape(nt, 1, tm), h, gates, gain.reshape(1, -1), ys)


def _moe(h, gain, w_router, w_gate, w_up, w_down, final_gain, split):
    t, _ = h.shape
    n_exp = w_router.shape[1]
    tm_e = 512
    xn, imeta, gates, counts = _router(h, gain, w_router)
    idx1, idx2, rank1, rank2 = imeta[:, 0], imeta[:, 1], imeta[:, 2], imeta[:, 3]
    cnt = counts[0, :n_exp]
    padded = ((cnt + tm_e - 1) // tm_e) * tm_e
    ends = jnp.cumsum(padded)
    starts = ends - padded
    slot1 = starts[idx1] + rank1
    slot2 = starts[idx2] + rank2
    n_tiles = (TOP_K * t + n_exp * (tm_e - 1)) // tm_e
    n_used = (ends[-1] // tm_e).astype(I32)
    tile_start = jnp.arange(n_tiles, dtype=I32) * tm_e
    tile_expert = jnp.sum((tile_start[:, None] >= ends[None, :]).astype(I32), axis=1)
    tile_expert = jnp.minimum(tile_expert, n_exp - 1)
    xs = _dispatch(xn, slot1, slot2, n_tiles * tm_e)
    ys = _experts(xs, tile_expert, n_used.reshape(1), w_gate, w_up, w_down, tm_e)
    return _gather_norm(h, gates, slot1, slot2, ys, final_gain, split)


def _cache_to_slabs(cache):
    n, w, two, heads, hd = cache.shape
    return jnp.transpose(cache, (0, 2, 3, 4, 1)).reshape(n, two, heads * hd, w)


def _slabs_to_cache(slabs, heads):
    n, two, _, w = slabs.shape
    return jnp.transpose(slabs.reshape(n, two, heads, HEAD_DIM, w), (0, 4, 1, 2, 3))


def kernel(x_prompt, x_sample, cache_conv, cache_swa_kv, cache_dil_kv0, cache_dil_kv1, cache_dil_kv2, norm_mix0, w_in0, conv_w, swa_sink, w_out0, norm_ffn0, w_gate0, w_up0, w_down0, norm_mix1, w_in1, w_out1, norm_ffn1, w_router, w_gate1, w_up1, w_down1, norm_final):
    n_p, seq, d = x_prompt.shape
    n_s, s_len, _ = x_sample.shape
    tp, ts = n_p * seq, n_s * s_len
    d_conv = conv_w.shape[2]
    kvh, grp = swa_sink.shape[1], swa_sink.shape[2]
    hq0 = kvh * grp
    h1 = cache_dil_kv0.shape[4]
    c_q0, c_kv0 = hq0 * HEAD_DIM, kvh * HEAD_DIM
    q0_col = 3 * d_conv
    k0_col = q0_col + c_q0
    c_g = 3 * h1 * HEAD_DIM
    dil_caches = (cache_dil_kv0, cache_dil_kv1, cache_dil_kv2)

    x_rows = (x_prompt.reshape(tp, d), x_sample.reshape(ts, d))

    assert c_kv0 == LANES and grp % 2 == 0 and d_conv % LANES == 0
    nsc = d_conv // LANES
    q0_slab = 3 * nsc
    k0_slab = q0_slab + c_q0 // LANES
    proj0 = _dense([x_rows], w_in0[0], gain=norm_mix0[0], out_dtype=F32, name="in_proj0", slab_out=True)
    a_p, conv_tail = _conv_prompt(proj0, conv_w[0], n_seq=n_p, seq_len=seq, c=d_conv)
    att_p = _band_attn(proj0, n_seq=n_p, seq_len=seq, dil=1, q_slab=q0_slab, k_slab=k0_slab, v_slab=k0_slab + 1,
                       n_q_slabs=c_q0 // LANES, shared_kv=True, sink=swa_sink[0], want_lse=False,
                       name="swa_prompt")[0]
    proj0_s = proj0[:, tp:]
    p3 = proj0_s[:q0_slab].reshape(3, nsc, n_s, s_len, LANES)
    p3 = jnp.transpose(p3, (0, 3, 2, 1, 4)).reshape(3, s_len, n_s, d_conv)
    a_s, conv_new = _conv_step(p3, jnp.transpose(cache_conv[0], (1, 0, 2)), conv_w[0])
    a_s = jnp.transpose(a_s, (1, 0, 2)).reshape(ts, d_conv)
    kvt0 = _proj_t(jnp.transpose(w_in0[0][:, k0_col:]), x_rows[1], norm_mix0[0], "kv_t0")
    q_s = proj0_s[q0_slab:k0_slab].reshape(kvh, grp // 2, n_s, s_len, 2, HEAD_DIM)
    q_s = jnp.transpose(q_s, (2, 3, 1, 4, 0, 5)).reshape(n_s, s_len * grp, c_kv0)
    sink_col = jnp.broadcast_to(swa_sink[0][:, None, :], (kvh, s_len, grp)).reshape(kvh * s_len * grp, 1)
    o_s, swa_new = _step_attn(q_s, kvt0.reshape(2, c_kv0, ts), _cache_to_slabs(cache_swa_kv[0]), dil=1,
                              tok_div=grp, sink_col=sink_col, want_lse=False, bsz=8, name="swa_step")
    att_s = jnp.transpose(o_s.reshape(n_s, s_len, grp, kvh, HEAD_DIM), (0, 1, 3, 2, 4)).reshape(ts, c_q0)
    h = _dense([(a_p, a_s.astype(BF16)), (att_p, att_s.astype(BF16))], w_out0[0], res=x_rows, out_dtype=F32,
               name="out_proj0")
    h = _ffn(h, norm_ffn0[0], w_gate0[0], w_up0[0], w_down0[0], "ffn0")

    proj1 = _dense([h], w_in1[0], gain=norm_mix1[0], out_dtype=F32, name="in_proj1", slab_out=True)
    proj1_s = proj1[:, tp:]
    c_h = h1 * HEAD_DIM
    nsh = c_h // LANES
    pairs_p, pairs_s, dil_p, dil_s = [], [], [], []
    for g, dil in enumerate(DILATIONS):
        window = dil_caches[g].shape[2]
        s0 = g * 3 * nsh
        pairs_p.append(tuple(_band_attn(proj1, n_seq=n_p, seq_len=seq, dil=dil, q_slab=s0, k_slab=s0 + nsh,
                                        v_slab=s0 + 2 * nsh, n_q_slabs=nsh, want_lse=True,
                                        name=f"dil{g}_prompt")))
        keep = min(window, seq)
        state = _state_slabs(proj1, first_slab=s0 + nsh, n_slabs=2 * nsh, n_seq=n_p, seq_len=seq, keep=keep,
                             name=f"dil{g}_state")
        dil_p.append(_slabs_to_cache(state.reshape(n_p, 2, c_h, keep), h1)[None])
        lo = g * c_g
        q_g = jnp.transpose(proj1_s[s0:s0 + nsh].reshape(nsh, n_s, s_len, LANES), (1, 2, 0, 3))
        q_g = q_g.reshape(n_s, s_len, c_h)
        q_g = jnp.concatenate([q_g, jnp.zeros((n_s, SUBLANES - s_len, c_h), F32)], axis=1)
        kvt = _proj_t(jnp.transpose(w_in1[0][:, lo + c_h:lo + c_g]), h[tp:], norm_mix1[0], f"kv_t1_{g}")
        o_sg, l_sg, cache_new = _step_attn(q_g, kvt.reshape(2, c_h, ts), _cache_to_slabs(dil_caches[g][0]),
                                           dil=dil, tok_div=1, want_lse=True,
                                           bsz=max(1, 2048 // window), name=f"dil{g}_step")
        pairs_s.append((o_sg.reshape(n_s * SUBLANES, c_h), l_sg.reshape(n_s * SUBLANES, c_h)))
        dil_s.append(_slabs_to_cache(cache_new, h1)[None])
    comb_p = _combine(pairs_p, "combine_prompt")
    comb_s = _combine(pairs_s, "combine_step").reshape(n_s, SUBLANES, c_h)[:, :s_len].reshape(ts, c_h)
    h = _dense([(comb_p, comb_s)], w_out1[0], res=h, out_dtype=F32, name="out_proj1")
    y_p, y_s = _moe(h, norm_ffn1[0], w_router[0], w_gate1[0], w_up1[0], w_down1[0], norm_final, tp)

    y_prompt = y_p.reshape(n_p, seq, d)
    y_sample = y_s.reshape(n_s, s_len, d)
    new_conv_prompt = conv_tail[:, SUBLANES - (CONV_W - 1):][None]
    new_conv_sample = jnp.transpose(conv_new, (1, 0, 2))[None]
    keep0 = min(SPAN, seq)
    swa_state = _state_slabs(proj0, first_slab=k0_slab, n_slabs=2, n_seq=n_p, seq_len=seq, keep=keep0,
                             name="swa_state")
    new_swa_kv_prompt = _slabs_to_cache(swa_state, kvh)[None]
    new_swa_kv_sample = _slabs_to_cache(swa_new, kvh)[None]
    return (y_prompt, y_sample, new_conv_prompt, new_conv_sample, new_swa_kv_prompt, new_swa_kv_sample,
            dil_p[0], dil_s[0], dil_p[1], dil_s[1], dil_p[2], dil_s[2])
```

```python
import functools

import jax
import jax.numpy as jnp
from jax import lax
from jax.experimental import pallas as pl
from jax.experimental.pallas import tpu as pltpu

F32 = jnp.float32
BF16 = jnp.bfloat16
I32 = jnp.int32

EPS = 1e-5
HEAD_DIM = 64
SPAN = 128
CONV_W = 3
DILATIONS = (1, 4, 16)
TOP_K = 2
LANES = 128
SUBLANES = 8
VMEM_LIMIT_BYTES = 56 * 1024 * 1024
NT_DIMS = (((1,), (1,)), ((), ()))


def _params(*sem):
    return pltpu.CompilerParams(dimension_semantics=sem, vmem_limit_bytes=VMEM_LIMIT_BYTES)


def _pick(n, candidates):
    for c in candidates:
        if n % c == 0:
            return c
    raise ValueError(f"no tile for {n} in {candidates}")


def _rms(x, g):
    y = x * lax.rsqrt(jnp.mean(x * x, axis=-1, keepdims=True) + EPS)
    return y * g


STAGE_SLOTS = 4


def _load_bf16(w_hbm, w_vmem, stage, sem, *, axis, chunk):
    n_chunks = w_hbm.shape[axis] // chunk
    n_slots = stage.shape[0]

    def window(ref, c):
        return ref.at[pl.ds(c * chunk, chunk), :] if axis == 0 else ref.at[:, pl.ds(c * chunk, chunk)]

    def copy(c):
        return pltpu.make_async_copy(window(w_hbm, c), stage.at[c % n_slots], sem.at[c % n_slots])

    for c in range(min(n_slots - 1, n_chunks)):
        copy(c).start()
    for c in range(n_chunks):
        if c + n_slots - 1 < n_chunks:
            copy(c + n_slots - 1).start()
        copy(c).wait()
        if axis == 0:
            w_vmem[c * chunk:(c + 1) * chunk, :] = stage[c % n_slots].astype(BF16)
        else:
            w_vmem[:, c * chunk:(c + 1) * chunk] = stage[c % n_slots].astype(BF16)


def _dense_body(*refs, ks, split, head_tiles, has_gain, has_res, slab_out, cn):
    n_in = len(ks)
    step = pl.program_id(0)
    refs = list(refs)

    def take(is_split):
        count = 2 if is_split else 1
        parts = tuple(refs[:count])
        del refs[:count]
        return parts

    def read(parts, cols=slice(None)):
        if len(parts) == 1:
            return parts[0][:, cols]
        return jnp.where(step < head_tiles, parts[0][:, cols], parts[1][:, cols])

    xs = [take(s) for s in split[:n_in]]
    g_ref = refs.pop(0) if has_gain else None
    w_hbm = refs.pop(0)
    res = take(split[n_in]) if has_res else None
    o_ref, w_ref, stage, sem = refs

    @pl.when(step == 0)
    def _():
        _load_bf16(w_hbm, w_ref, stage, sem, axis=1, chunk=cn)

    if has_gain:
        lhs = [_rms(read(xs[0]), g_ref[...]).astype(BF16)]
    else:
        lhs = [read(x).astype(BF16) for x in xs]
    n = w_ref.shape[1]
    for c in range(n // cn):
        cols = slice(c * cn, (c + 1) * cn)
        acc = None
        row0 = 0
        for a, k in zip(lhs, ks):
            d = jnp.dot(a, w_ref[row0:row0 + k, cols], preferred_element_type=F32)
            acc = d if acc is None else acc + d
            row0 += k
        if has_res:
            acc = read(res, cols) + acc
        if slab_out:
            for s in range(cn // LANES):
                o_ref[c * (cn // LANES) + s] = acc[:, s * LANES:(s + 1) * LANES].astype(o_ref.dtype)
        else:
            o_ref[:, cols] = acc.astype(o_ref.dtype)


def _dense(xs, w, *, gain=None, res=None, out_dtype, name, slab_out=False):
    def n_rows(a):
        return sum(p.shape[0] for p in a) if isinstance(a, tuple) else a.shape[0]

    def width(a):
        return a[0].shape[1] if isinstance(a, tuple) else a.shape[1]

    operands = list(xs) + ([res] if res is not None else [])
    pairs = [a for a in operands if isinstance(a, tuple)]
    t = n_rows(xs[0])
    k_all, n = w.shape
    ks = tuple(width(x) for x in xs)
    assert sum(ks) == k_all
    gcd_rows = t
    for a in pairs:
        gcd_rows = min(gcd_rows, a[1].shape[0])
    tm = _pick(gcd_rows, (512, 256, 128, 64, 32, 16, 8))
    assert t % tm == 0
    head_tiles = pairs[0][0].shape[0] // tm if pairs else 0
    for a in pairs:
        assert a[0].shape[0] == head_tiles * tm
    cn = _pick(n, (512, 384, 256, 128))
    in_specs, args = [], []

    def add_rows(a, k):
        if isinstance(a, tuple):
            in_specs.append(pl.BlockSpec((tm, k), lambda i: (jnp.minimum(i, head_tiles - 1), 0)))
            in_specs.append(pl.BlockSpec((tm, k), lambda i: (jnp.maximum(i - head_tiles, 0), 0)))
            args.extend(a)
        else:
            in_specs.append(pl.BlockSpec((tm, k), lambda i: (i, 0)))
            args.append(a)

    for x, k in zip(xs, ks):
        add_rows(x, k)
    if gain is not None:
        in_specs.append(pl.BlockSpec((1, ks[0]), lambda i: (0, 0)))
        args.append(gain.reshape(1, -1))
    in_specs.append(pl.BlockSpec(memory_space=pl.ANY))
    args.append(w)
    if res is not None:
        add_rows(res, n)
    split = tuple(isinstance(a, tuple) for a in operands)
    if slab_out:
        out_shape = jax.ShapeDtypeStruct((n // LANES, t, LANES), out_dtype)
        out_spec = pl.BlockSpec((n // LANES, tm, LANES), lambda i: (0, i, 0))
    else:
        out_shape = jax.ShapeDtypeStruct((t, n), out_dtype)
        out_spec = pl.BlockSpec((tm, n), lambda i: (i, 0))
    return pl.pallas_call(
        functools.partial(_dense_body, ks=ks, split=split, head_tiles=head_tiles, has_gain=gain is not None,
                          has_res=res is not None, slab_out=slab_out, cn=cn),
        out_shape=out_shape,
        grid=(t // tm,),
        in_specs=in_specs,
        out_specs=out_spec,
        scratch_shapes=[pltpu.VMEM((k_all, n), BF16), pltpu.VMEM((STAGE_SLOTS, k_all, cn), F32),
                        pltpu.SemaphoreType.DMA((STAGE_SLOTS,))],
        compiler_params=_params("arbitrary"),
        name=name,
    )(*args)


def _proj_t_body(w_ref, x_ref, g_ref, o_ref):
    xn = _rms(x_ref[...], g_ref[...]).astype(BF16)
    o_ref[...] = lax.dot_general(w_ref[...].astype(BF16), xn, NT_DIMS, preferred_element_type=F32)


def _proj_t(w_t, x, gain, name):
    c, k = w_t.shape
    rows = x.shape[0]
    tc = _pick(c, (512, 256, 128))
    return pl.pallas_call(
        _proj_t_body,
        out_shape=jax.ShapeDtypeStruct((c, rows), F32),
        grid=(c // tc,),
        in_specs=[pl.BlockSpec((tc, k), lambda i: (i, 0)),
                  pl.BlockSpec((rows, k), lambda i: (0, 0)),
                  pl.BlockSpec((1, k), lambda i: (0, 0))],
        out_specs=pl.BlockSpec((tc, rows), lambda i: (i, 0)),
        compiler_params=_params("parallel"),
        name=name,
    )(w_t, x, gain.reshape(1, -1))


def _ffn_body(x_ref, g_ref, wg_hbm, wu_hbm, wd_hbm, o_ref, wg_ref, wu_ref, wd_ref, stage_in, stage_out, sem, *,
              tf, ld):
    @pl.when(pl.program_id(0) == 0)
    def _():
        _load_bf16(wg_hbm, wg_ref, stage_in, sem, axis=1, chunk=ld)
        _load_bf16(wu_hbm, wu_ref, stage_in, sem, axis=1, chunk=ld)
        _load_bf16(wd_hbm, wd_ref, stage_out, sem, axis=0, chunk=ld)

    x = x_ref[...]
    xn = _rms(x, g_ref[...]).astype(BF16)
    acc = None
    for c in range(wg_ref.shape[1] // tf):
        cols = slice(c * tf, (c + 1) * tf)
        gate = jnp.dot(xn, wg_ref[:, cols], preferred_element_type=F32)
        up = jnp.dot(xn, wu_ref[:, cols], preferred_element_type=F32)
        act = (gate * jax.nn.sigmoid(gate) * up).astype(BF16)
        d = jnp.dot(act, wd_ref[cols, :], preferred_element_type=F32)
        acc = d if acc is None else acc + d
    o_ref[...] = x + acc


def _ffn(x, gain, w_gate, w_up, w_down, name):
    t, d = x.shape
    f = w_gate.shape[1]
    tm = _pick(t, (512, 256, 128, 64, 32, 16, 8))
    tf = f
    ld = _pick(f, (256, 128))
    anywhere = pl.BlockSpec(memory_space=pl.ANY)
    return pl.pallas_call(
        functools.partial(_ffn_body, tf=tf, ld=ld),
        out_shape=jax.ShapeDtypeStruct((t, d), F32),
        grid=(t // tm,),
        in_specs=[pl.BlockSpec((tm, d), lambda i: (i, 0)),
                  pl.BlockSpec((1, d), lambda i: (0, 0)),
                  anywhere, anywhere, anywhere],
        out_specs=pl.BlockSpec((tm, d), lambda i: (i, 0)),
        scratch_shapes=[pltpu.VMEM((d, f), BF16), pltpu.VMEM((d, f), BF16), pltpu.VMEM((f, d), BF16),
                        pltpu.VMEM((STAGE_SLOTS, d, ld), F32), pltpu.VMEM((STAGE_SLOTS, ld, d), F32),
                        pltpu.SemaphoreType.DMA((STAGE_SLOTS,))],
        compiler_params=_params("arbitrary"),
        name=name,
    )(x, gain.reshape(1, -1), w_gate, w_up, w_down)


def _band_body(*refs, dil, mb, shared_kv, has_sink, want_lse):
    q_ref, kc_ref, kp_ref, vc_ref, vp_ref = refs[:5]
    pos = 5
    sink_ref = refs[pos] if has_sink else None
    pos += int(has_sink)
    o_ref = refs[pos]
    lse_ref = refs[pos + 1] if want_lse else None
    t = pl.program_id(1)
    scale = HEAD_DIM ** -0.5
    qi = lax.broadcasted_iota(I32, (2 * SPAN, 2 * SPAN), 0) & (SPAN - 1)
    kj = lax.broadcasted_iota(I32, (2 * SPAN, 2 * SPAN), 1)
    band = (kj >= qi) & (kj <= qi + SPAN)
    band_first = band & (kj >= jnp.where(t == 0, SPAN, 0))
    lane = lax.broadcasted_iota(I32, (1, LANES), 1)
    halves = [lane < HEAD_DIM, lane >= HEAD_DIM]

    def rows_of(ref, s, r, count):
        if dil == 1:
            return ref[s, 0:count, :]
        return ref[s, pl.ds(r, count, stride=dil), :]

    def pair(q_a, mask_a, q_b, mask_b, kw, vw, msk, heads):
        q2 = jnp.concatenate([jnp.where(mask_a, q_a, 0.0), jnp.where(mask_b, q_b, 0.0)], axis=0).astype(BF16)
        s = lax.dot_general(q2, kw, NT_DIMS, preferred_element_type=F32)
        s = jnp.where(msk, s, -jnp.inf)
        m = jnp.max(s, axis=-1, keepdims=True)
        if has_sink:
            sink = jnp.concatenate([jnp.full((SPAN, 1), sink_ref[h], F32) for h in heads], axis=0)
            m = jnp.maximum(m, sink)
        p = jnp.exp(s - m)
        den = jnp.sum(p, axis=-1, keepdims=True)
        if has_sink:
            den = den + jnp.exp(sink - m)
        o = jnp.dot(p.astype(BF16), vw, preferred_element_type=F32) / den
        return o, m + jnp.log(den)

    def store(s, r, b, o_tile, lse_tile):
        if dil == 1:
            o_ref[b * SPAN:(b + 1) * SPAN, s * LANES:(s + 1) * LANES] = o_tile.astype(o_ref.dtype)
            if want_lse:
                lse_ref[b * SPAN:(b + 1) * SPAN, s * LANES:(s + 1) * LANES] = lse_tile
        else:
            o_ref[s, pl.ds(r + dil * b * SPAN, SPAN, stride=dil), :] = o_tile
            if want_lse:
                lse_ref[s, pl.ds(r + dil * b * SPAN, SPAN, stride=dil), :] = lse_tile

    n_slabs = q_ref.shape[0]
    first_head = 2 * n_slabs * pl.program_id(2)
    top, bot = slice(0, SPAN), slice(SPAN, 2 * SPAN)
    for r in range(dil):
        qs = [rows_of(q_ref, s, r, mb * SPAN) * scale for s in range(n_slabs)]
        kf = [jnp.concatenate([rows_of(kp_ref, s, r, SPAN), rows_of(kc_ref, s, r, mb * SPAN)], axis=0)
              for s in range(kc_ref.shape[0])]
        vf = [jnp.concatenate([rows_of(vp_ref, s, r, SPAN), rows_of(vc_ref, s, r, mb * SPAN)], axis=0)
              for s in range(vc_ref.shape[0])]
        ks = [k.astype(BF16) for k in kf]
        vs = [v.astype(BF16) for v in vf]
        if shared_kv:
            k_sw = pltpu.roll(kf[0], HEAD_DIM, 1).astype(BF16)
            v_sw = pltpu.roll(vf[0], HEAD_DIM, 1).astype(BF16)
        for b in range(mb):
            blk = slice(b * SPAN, (b + 1) * SPAN)
            keys = slice(b * SPAN, (b + 2) * SPAN)
            msk = band_first if b == 0 else band
            if not shared_kv:
                for s in range(n_slabs):
                    o, lse = pair(qs[s][blk], halves[0], qs[s][blk], halves[1], ks[s][keys], vs[s][keys], msk,
                                  (first_head + 2 * s, first_head + 2 * s + 1))
                    store(s, r, b, jnp.where(halves[0], o[top], o[bot]), jnp.where(halves[0], lse[top], lse[bot]))
            else:
                for kvh in range(2):
                    sa, sb = 2 * kvh, 2 * kvh + 1
                    o_al, l_al = pair(qs[sa][blk], halves[kvh], qs[sb][blk], halves[kvh], ks[0][keys],
                                      vs[0][keys], msk, (2 * sa + kvh, 2 * sb + kvh))
                    o_sw, l_sw = pair(qs[sa][blk], halves[1 - kvh], qs[sb][blk], halves[1 - kvh], k_sw[keys],
                                      v_sw[keys], msk, (2 * sa + 1 - kvh, 2 * sb + 1 - kvh))
                    store(sa, r, b, jnp.where(halves[kvh], o_al[top], o_sw[top]),
                          jnp.where(halves[kvh], l_al[top], l_sw[top]))
                    store(sb, r, b, jnp.where(halves[kvh], o_al[bot], o_sw[bot]),
                          jnp.where(halves[kvh], l_al[bot], l_sw[bot]))


def _band_attn(slabs, *, n_seq, seq_len, dil, q_slab, k_slab, v_slab, n_q_slabs, shared_kv=False, sink=None,
               want_lse, name):
    mb = max(1, 512 // (dil * SPAN))
    tp = dil * SPAN * mb
    nt = seq_len // tp
    assert seq_len % tp == 0 and (not shared_kv or (n_q_slabs == 4 and dil == 1))
    prev_rows = dil * SPAN
    per_step = n_q_slabs if (shared_kv or tp <= 512) else 1
    kv_block = 1 if shared_kv else per_step
    kv_step = 0 if shared_kv else 1

    def cur(base, step):
        return lambda n, t, s: (base + s * step, n * nt + t, 0)

    def prev(base, step):
        return lambda n, t, s: (base + s * step, jnp.maximum((n * nt + t) * mb - 1, 0), 0)

    in_specs = [pl.BlockSpec((per_step, tp, LANES), cur(q_slab // per_step, 1)),
                pl.BlockSpec((kv_block, tp, LANES), cur(k_slab // kv_block, kv_step)),
                pl.BlockSpec((kv_block, prev_rows, LANES), prev(k_slab // kv_block, kv_step)),
                pl.BlockSpec((kv_block, tp, LANES), cur(v_slab // kv_block, kv_step)),
                pl.BlockSpec((kv_block, prev_rows, LANES), prev(v_slab // kv_block, kv_step))]
    assert q_slab % per_step == 0 and k_slab % kv_block == 0 and v_slab % kv_block == 0
    args = [slabs] * 5
    if sink is not None:
        in_specs.append(pl.BlockSpec(memory_space=pltpu.SMEM))
        args.append(sink.reshape(-1).astype(F32))
    rows = n_seq * seq_len
    if dil == 1:
        spec = pl.BlockSpec((tp, per_step * LANES), lambda n, t, s: (n * nt + t, s))
        out_shape = [jax.ShapeDtypeStruct((rows, n_q_slabs * LANES), BF16)]
        lse_shape = jax.ShapeDtypeStruct((rows, n_q_slabs * LANES), F32)
    else:
        spec = pl.BlockSpec((per_step, tp, LANES), lambda n, t, s: (s, n * nt + t, 0))
        out_shape = [jax.ShapeDtypeStruct((n_q_slabs, rows, LANES), F32)]
        lse_shape = jax.ShapeDtypeStruct((n_q_slabs, rows, LANES), F32)
    out_specs = [spec]
    if want_lse:
        out_shape.append(lse_shape)
        out_specs.append(spec)
    return pl.pallas_call(
        functools.partial(_band_body, dil=dil, mb=mb, shared_kv=shared_kv, has_sink=sink is not None,
                          want_lse=want_lse),
        out_shape=out_shape,
        grid=(n_seq, nt, n_q_slabs // per_step),
        in_specs=in_specs,
        out_specs=out_specs,
        compiler_params=_params("parallel", "parallel", "parallel"),
        name=name,
    )(*args)


def _combine_body(o0, l0, o1, l1, o2, l2, out_ref):
    def tile(ref):
        return ref[...].reshape(ref.shape[-2:]).astype(F32)

    a0, a1, a2 = tile(l0), tile(l1), tile(l2)
    m = jnp.maximum(jnp.maximum(a0, a1), a2)
    e0, e1, e2 = jnp.exp(a0 - m), jnp.exp(a1 - m), jnp.exp(a2 - m)
    num = e0 * tile(o0) + e1 * tile(o1) + e2 * tile(o2)
    out_ref[...] = (num / (e0 + e1 + e2)).astype(out_ref.dtype)


def _combine(pairs, name):
    first = pairs[0][0]
    rows, c = first.shape if first.ndim == 2 else (first.shape[1], first.shape[0] * LANES)
    tm = _pick(rows, (1024, 512, 256, 128, 64, 32, 16, 8))
    flat = pl.BlockSpec((tm, LANES), lambda i, s: (i, s))
    slab = pl.BlockSpec((1, tm, LANES), lambda i, s: (s, i, 0))
    args = [a for pair in pairs for a in pair]
    return pl.pallas_call(
        _combine_body,
        out_shape=jax.ShapeDtypeStruct((rows, c), BF16),
        grid=(rows // tm, c // LANES),
        in_specs=[flat if a.ndim == 2 else slab for a in args],
        out_specs=flat,
        compiler_params=_params("parallel", "parallel"),
        name=name,
    )(*args)


def _state_body(x_ref, o_ref):
    o_ref[0, 0] = x_ref[0].T


def _state_slabs(slabs, *, first_slab, n_slabs, n_seq, seq_len, keep, name):
    assert seq_len % keep == 0
    per_seq = seq_len // keep
    return pl.pallas_call(
        _state_body,
        out_shape=jax.ShapeDtypeStruct((n_seq, n_slabs, LANES, keep), F32),
        grid=(n_seq, n_slabs),
        in_specs=[pl.BlockSpec((1, keep, LANES), lambda n, s: (first_slab + s, (n + 1) * per_seq - 1, 0))],
        out_specs=pl.BlockSpec((1, 1, LANES, keep), lambda n, s: (n, s, 0, 0)),
        compiler_params=_params("parallel", "parallel"),
        name=name,
    )(slabs)


def _conv_prompt_body(gb_ref, gc_ref, xa_ref, gcp_ref, xap_ref, w_ref, a_ref, st_ref):
    t = pl.program_id(1)
    w = w_ref[...]
    for s in range(gb_ref.shape[0]):
        lanes = slice(s * LANES, (s + 1) * LANES)
        u = gc_ref[s] * xa_ref[s]
        up = jnp.where(t == 0, 0.0, gcp_ref[s] * xap_ref[s])
        ext = jnp.concatenate([up, u], axis=0)
        y = (w[0:1, lanes] * pltpu.roll(ext, 2, 0)[SUBLANES:]
             + w[1:2, lanes] * pltpu.roll(ext, 1, 0)[SUBLANES:]) + w[2:3, lanes] * u
        a_ref[:, lanes] = (gb_ref[s] * y).astype(a_ref.dtype)
        st_ref[0, :, lanes] = u[u.shape[0] - SUBLANES:]


def _conv_prompt(slabs, conv_w, *, n_seq, seq_len, c):
    tq = _pick(seq_len, (512, 256, 128))
    nt = seq_len // tq
    rb = tq // SUBLANES
    ns = c // LANES

    def cur(part):
        return lambda n, t: (part, n * nt + t, 0)

    def prev(part):
        return lambda n, t: (part, jnp.maximum((n * nt + t) * rb - 1, 0), 0)

    return pl.pallas_call(
        _conv_prompt_body,
        out_shape=[jax.ShapeDtypeStruct((n_seq * seq_len, c), BF16),
                   jax.ShapeDtypeStruct((n_seq, SUBLANES, c), F32)],
        grid=(n_seq, nt),
        in_specs=[pl.BlockSpec((ns, tq, LANES), cur(0)), pl.BlockSpec((ns, tq, LANES), cur(1)),
                  pl.BlockSpec((ns, tq, LANES), cur(2)),
                  pl.BlockSpec((ns, SUBLANES, LANES), prev(1)), pl.BlockSpec((ns, SUBLANES, LANES), prev(2)),
                  pl.BlockSpec((CONV_W, c), lambda n, t: (0, 0))],
        out_specs=[pl.BlockSpec((tq, c), lambda n, t: (n * nt + t, 0)),
                   pl.BlockSpec((1, SUBLANES, c), lambda n, t: (n, 0, 0))],
        compiler_params=_params("parallel", "arbitrary"),
        name="conv_prompt",
    )(slabs, slabs, slabs, slabs, slabs, conv_w)


def _conv_step_body(p_ref, prev_ref, w_ref, a_ref, st_ref):
    s_len = p_ref.shape[1]
    w = w_ref[...]
    hist = [prev_ref[k] for k in range(CONV_W - 1)] + [p_ref[1, s] * p_ref[2, s] for s in range(s_len)]
    for s in range(s_len):
        y = (w[0:1] * hist[s] + w[1:2] * hist[s + 1]) + w[2:3] * hist[s + 2]
        a_ref[s] = p_ref[0, s] * y
    for k in range(CONV_W - 1):
        st_ref[k] = hist[s_len + k]


def _conv_step(p3, prev, conv_w):
    _, s_len, n, c = p3.shape
    return pl.pallas_call(
        _conv_step_body,
        out_shape=[jax.ShapeDtypeStruct((s_len, n, c), F32), jax.ShapeDtypeStruct((CONV_W - 1, n, c), F32)],
        name="conv_step",
        compiler_params=pltpu.CompilerParams(vmem_limit_bytes=VMEM_LIMIT_BYTES),
    )(p3, prev, conv_w)


def _step_body(*refs, n_blk, qr, tok_div, w, dil, cw, bsz, has_sink, want_lse):
    q_ref, kvt_ref, cache_ref = refs[:3]
    pos = 3
    sink_ref = refs[pos] if has_sink else None
    pos += int(has_sink)
    o_ref = refs[pos]
    pos += 1
    lse_ref = refs[pos] if want_lse else None
    pos += int(want_lse)
    cout_ref = refs[pos]
    r_dim = n_blk * HEAD_DIM
    nrb = n_blk * qr
    n_chunks = w // cw
    scale = HEAD_DIM ** -0.5
    step = pl.program_id(0)
    lane_r = lax.broadcasted_iota(I32, (1, r_dim), 1)
    blk_masks = [(lane_r >= j * HEAD_DIM) & (lane_r < (j + 1) * HEAD_DIM) for j in range(n_blk)]
    row = lax.broadcasted_iota(I32, (nrb, 1), 0)
    tok = (row % qr) // tok_div
    lane_c = lax.broadcasted_iota(I32, (1, cw), 1)
    lane_n = lax.broadcasted_iota(I32, (1, LANES), 1)

    def valid(pos_l):
        ok = (pos_l >= tok) & (pos_l <= w + tok)
        if dil > 1:
            ok = ok & (((pos_l - tok) & (dil - 1)) == 0)
        return ok

    for b in range(bsz):
        off = ((step * bsz + b) * 4) % LANES
        shift = (LANES - off) % LANES
        new_k = pltpu.roll(kvt_ref[0], shift, 1)
        new_v = pltpu.roll(kvt_ref[1], shift, 1)
        q = q_ref[b] * scale
        qbd = jnp.concatenate([jnp.where(mk, q, 0.0) for mk in blk_masks], axis=0).astype(BF16)
        scores = []
        for c in range(n_chunks):
            lo, hi = c * cw, (c + 1) * cw
            kc = cache_ref[b, 0, :, lo:hi]
            sc = jnp.dot(qbd, kc.astype(BF16), preferred_element_type=F32)
            scores.append(jnp.where(valid(lane_c + lo), sc, -jnp.inf))
            nxt = cache_ref[b, 0, :, hi:hi + LANES] if c + 1 < n_chunks else new_k
            ext = jnp.concatenate([kc, nxt], axis=1)
            cout_ref[b, 0, :, lo:hi] = pltpu.roll(ext, cw + LANES - 4, 1)[:, :cw]
        sc = jnp.dot(qbd, new_k.astype(BF16), preferred_element_type=F32)
        scores.append(jnp.where(valid(lane_n + w), sc, -jnp.inf))
        m = functools.reduce(jnp.maximum, [jnp.max(s, axis=-1, keepdims=True) for s in scores])
        if has_sink:
            m = jnp.maximum(m, sink_ref[...])
        probs = [jnp.exp(s - m) for s in scores]
        den = functools.reduce(lambda a, c: a + c, [jnp.sum(p, axis=-1, keepdims=True) for p in probs])
        if has_sink:
            den = den + jnp.exp(sink_ref[...] - m)
        pv = lax.dot_general(probs[-1].astype(BF16), new_v.astype(BF16), NT_DIMS, preferred_element_type=F32)
        for c in range(n_chunks):
            lo, hi = c * cw, (c + 1) * cw
            vc = cache_ref[b, 1, :, lo:hi]
            pv = pv + lax.dot_general(probs[c].astype(BF16), vc.astype(BF16), NT_DIMS,
                                      preferred_element_type=F32)
            nxt = cache_ref[b, 1, :, hi:hi + LANES] if c + 1 < n_chunks else new_v
            ext = jnp.concatenate([vc, nxt], axis=1)
            cout_ref[b, 1, :, lo:hi] = pltpu.roll(ext, cw + LANES - 4, 1)[:, :cw]
        o = jnp.zeros((qr, r_dim), F32)
        m_e = jnp.zeros((qr, r_dim), F32)
        den_e = jnp.zeros((qr, r_dim), F32)
        for j, mk in enumerate(blk_masks):
            rows = slice(j * qr, (j + 1) * qr)
            o = jnp.where(mk, pv[rows], o)
            m_e = jnp.where(mk, m[rows], m_e)
            den_e = jnp.where(mk, den[rows], den_e)
        o_ref[b] = o / den_e
        if want_lse:
            lse_ref[b] = m_e + jnp.log(den_e)


def _step_attn(q, kvt, cache, *, dil, tok_div, sink_col=None, want_lse, bsz, name):
    n, qr, r_dim = q.shape
    w = cache.shape[-1]
    n_blk = r_dim // HEAD_DIM
    cw = min(w, 512)
    in_specs = [pl.BlockSpec((bsz, qr, r_dim), lambda i: (i, 0, 0)),
                pl.BlockSpec((2, r_dim, LANES), lambda i: (0, 0, (i * bsz * 4) // LANES)),
                pl.BlockSpec((bsz, 2, r_dim, w), lambda i: (i, 0, 0, 0))]
    args = [q, kvt, cache]
    if sink_col is not None:
        in_specs.append(pl.BlockSpec((n_blk * qr, 1), lambda i: (0, 0)))
        args.append(sink_col)
    o_spec = pl.BlockSpec((bsz, qr, r_dim), lambda i: (i, 0, 0))
    out_shape = [jax.ShapeDtypeStruct((n, qr, r_dim), F32)]
    out_specs = [o_spec]
    if want_lse:
        out_shape.append(jax.ShapeDtypeStruct((n, qr, r_dim), F32))
        out_specs.append(o_spec)
    out_shape.append(jax.ShapeDtypeStruct(cache.shape, F32))
    out_specs.append(pl.BlockSpec((bsz, 2, r_dim, w), lambda i: (i, 0, 0, 0)))
    return pl.pallas_call(
        functools.partial(_step_body, n_blk=n_blk, qr=qr, tok_div=tok_div, w=w, dil=dil, cw=cw, bsz=bsz,
                          has_sink=sink_col is not None, want_lse=want_lse),
        out_shape=out_shape,
        grid=(n // bsz,),
        in_specs=in_specs,
        out_specs=out_specs,
        compiler_params=_params("parallel"),
        name=name,
    )(*args)


def _rows_to_tiles(ref, x):
    rows = x.shape[0]
    for s in range(SUBLANES):
        ref[pl.ds(s, rows, stride=SUBLANES), :] = x[:, s * LANES:(s + 1) * LANES]


def _tiles_to_rows(ref, rows):
    return jnp.concatenate([ref[pl.ds(s, rows, stride=SUBLANES), :] for s in range(SUBLANES)], axis=1)


def _router_body(h_ref, g_ref, wr_ref, xn_ref, im_ref, gm_ref, cnt_ref, carry_ref, *, n_exp):
    i = pl.program_id(0)

    @pl.when(i == 0)
    def _():
        carry_ref[...] = jnp.zeros_like(carry_ref)

    xn = _rms(h_ref[...], g_ref[...])
    _rows_to_tiles(xn_ref, xn)
    tm = xn.shape[0]
    wr = wr_ref[...]
    xh = xn.astype(BF16)
    xl = (xn - xh.astype(F32)).astype(BF16)
    wh = wr.astype(BF16)
    wl = (wr - wh.astype(F32)).astype(BF16)
    lg = jnp.dot(xh, wh, preferred_element_type=F32) + (
        jnp.dot(xh, wl, preferred_element_type=F32) + jnp.dot(xl, wh, preferred_element_type=F32))
    lane = lax.broadcasted_iota(I32, (tm, LANES), 1)
    lane_f = lane.astype(F32)
    lg = jnp.where(lane < n_exp, lg, -jnp.inf)
    m1 = jnp.max(lg, axis=-1, keepdims=True)
    i1 = jnp.min(jnp.where(lg == m1, lane_f, float(LANES)), axis=-1, keepdims=True)
    lg2 = jnp.where(lane_f == i1, -jnp.inf, lg)
    m2 = jnp.max(lg2, axis=-1, keepdims=True)
    i2 = jnp.min(jnp.where(lg2 == m2, lane_f, float(LANES)), axis=-1, keepdims=True)
    e = jnp.exp(m2 - m1)
    g1 = 1.0 / (1.0 + e)
    g2 = e / (1.0 + e)
    sel1 = lane_f == i1
    sel2 = lane_f == i2
    onehot = jnp.where(sel1 | sel2, 1.0, 0.0)
    r_i = lax.broadcasted_iota(I32, (tm, tm), 0)
    c_i = lax.broadcasted_iota(I32, (tm, tm), 1)
    tri = jnp.where(c_i < r_i, 1.0, 0.0).astype(BF16)
    before = jnp.dot(tri, onehot.astype(BF16), preferred_element_type=F32) + carry_ref[0:1]
    r1 = jnp.sum(jnp.where(sel1, before, 0.0), axis=-1, keepdims=True)
    r2 = jnp.sum(jnp.where(sel2, before, 0.0), axis=-1, keepdims=True)
    total = carry_ref[0:1] + jnp.sum(onehot, axis=0, keepdims=True)
    carry_ref[...] = jnp.broadcast_to(total, carry_ref.shape)
    cnt_ref[...] = jnp.broadcast_to(total, cnt_ref.shape).astype(I32)
    meta = jnp.where(lane == 0, i1, jnp.where(lane == 1, i2, jnp.where(lane == 2, r1, jnp.where(lane == 3, r2, 0.0))))
    im_ref[...] = meta.astype(I32)
    gm_ref[...] = jnp.where(lane == 0, g1, jnp.where(lane == 1, g2, 0.0))


def _router(h, gain, w_router):
    t, d = h.shape
    n_exp = w_router.shape[1]
    tm = _pick(t, (512, 256, 128, 64, 32, 16, 8))
    wr = jnp.zeros((d, LANES), F32).at[:, :n_exp].set(w_router)
    return pl.pallas_call(
        functools.partial(_router_body, n_exp=n_exp),
        out_shape=[jax.ShapeDtypeStruct((t * SUBLANES, LANES), F32), jax.ShapeDtypeStruct((t, LANES), I32),
                   jax.ShapeDtypeStruct((t, LANES), F32), jax.ShapeDtypeStruct((SUBLANES, LANES), I32)],
        grid=(t // tm,),
        in_specs=[pl.BlockSpec((tm, d), lambda i: (i, 0)),
                  pl.BlockSpec((1, d), lambda i: (0, 0)),
                  pl.BlockSpec((d, LANES), lambda i: (0, 0))],
        out_specs=[pl.BlockSpec((tm * SUBLANES, LANES), lambda i: (i, 0)),
                   pl.BlockSpec((tm, LANES), lambda i: (i, 0)),
                   pl.BlockSpec((tm, LANES), lambda i: (i, 0)),
                   pl.BlockSpec((SUBLANES, LANES), lambda i: (0, 0))],
        scratch_shapes=[pltpu.VMEM((SUBLANES, LANES), F32)],
        compiler_params=_params("arbitrary"),
        name="moe_router",
    )(h, gain.reshape(1, -1), wr)


def _dispatch_body(s1_ref, s2_ref, x_ref, zero_ref, out_ref, sem, *, tm):
    del zero_ref

    def issue(r, carry):
        src = x_ref.at[r]
        pltpu.make_async_copy(src, out_ref.at[s1_ref[0, 0, r]], sem.at[0]).start(0)
        pltpu.make_async_copy(src, out_ref.at[s2_ref[0, 0, r]], sem.at[1]).start(1)
        return carry

    lax.fori_loop(0, tm, issue, 0, unroll=8)
    pltpu.make_async_copy(x_ref, out_ref.at[pl.ds(0, tm)], sem.at[0]).wait()
    pltpu.make_async_copy(x_ref, out_ref.at[pl.ds(0, tm)], sem.at[1]).wait()


def _dispatch(xn, slot1, slot2, n_slots):
    t = xn.shape[0]
    tm = _pick(t, (512, 256, 128, 64, 32, 16, 8))
    nt = t // tm
    tile = xn.shape[1:]
    smem = pl.BlockSpec((1, 1, tm), lambda i: (i, 0, 0), memory_space=pltpu.SMEM)
    return pl.pallas_call(
        functools.partial(_dispatch_body, tm=tm),
        out_shape=jax.ShapeDtypeStruct((n_slots,) + tile, F32),
        grid=(nt,),
        in_specs=[smem, smem, pl.BlockSpec((tm,) + tile, lambda i: (i, 0, 0)), pl.BlockSpec(memory_space=pl.ANY)],
        out_specs=pl.BlockSpec(memory_space=pl.ANY),
        scratch_shapes=[pltpu.SemaphoreType.DMA((2,))],
        input_output_aliases={3: 0},
        compiler_params=_params("arbitrary"),
        name="moe_dispatch",
    )(slot1.reshape(nt, 1, tm), slot2.reshape(nt, 1, tm), xn, jnp.zeros((n_slots,) + tile, F32))


def _experts_body(te_ref, nu_ref, x_ref, wg_hbm, wu_hbm, wd_hbm, o_ref, wg_ref, wu_ref, wd_ref, stage_in, stage_out,
                  sem, *, tm, tf, ld):
    i = pl.program_id(0)
    active = i < nu_ref[0]
    expert = te_ref[i]
    changed = (i == 0) | (expert != te_ref[jnp.maximum(i - 1, 0)])

    @pl.when(active & changed)
    def _():
        _load_bf16(wg_hbm.at[expert], wg_ref, stage_in, sem, axis=1, chunk=ld)
        _load_bf16(wu_hbm.at[expert], wu_ref, stage_in, sem, axis=1, chunk=ld)
        _load_bf16(wd_hbm.at[expert], wd_ref, stage_out, sem, axis=0, chunk=ld)

    @pl.when(active)
    def _():
        xb = _tiles_to_rows(x_ref, tm).astype(BF16)
        acc = None
        for c in range(wg_ref.shape[1] // tf):
            cols = slice(c * tf, (c + 1) * tf)
            gate = jnp.dot(xb, wg_ref[:, cols], preferred_element_type=F32)
            up = jnp.dot(xb, wu_ref[:, cols], preferred_element_type=F32)
            act = (gate * jax.nn.sigmoid(gate) * up).astype(BF16)
            d = jnp.dot(act, wd_ref[cols, :], preferred_element_type=F32)
            acc = d if acc is None else acc + d
        _rows_to_tiles(o_ref, acc)

    @pl.when(jnp.logical_not(active))
    def _():
        o_ref[...] = jnp.zeros_like(o_ref)


def _experts(xs, tile_expert, n_used, w_gate, w_up, w_down, tm):
    n_slots = xs.shape[0] // SUBLANES
    d = SUBLANES * LANES
    f = w_gate.shape[2]
    tf = _pick(f, (1792, 1024, 512, 256, 128))
    ld = _pick(f, (256, 128))
    n_tiles = n_slots // tm
    anywhere = pl.BlockSpec(memory_space=pl.ANY)
    grid_spec = pltpu.PrefetchScalarGridSpec(
        num_scalar_prefetch=2,
        grid=(n_tiles,),
        in_specs=[pl.BlockSpec((tm * SUBLANES, LANES), lambda i, te, nu: (jnp.minimum(i, nu[0] - 1), 0)),
                  anywhere, anywhere, anywhere],
        out_specs=pl.BlockSpec((tm * SUBLANES, LANES), lambda i, te, nu: (i, 0)),
        scratch_shapes=[pltpu.VMEM((d, f), BF16), pltpu.VMEM((d, f), BF16), pltpu.VMEM((f, d), BF16),
                        pltpu.VMEM((STAGE_SLOTS, d, ld), F32), pltpu.VMEM((STAGE_SLOTS, ld, d), F32),
                        pltpu.SemaphoreType.DMA((STAGE_SLOTS,))],
    )
    return pl.pallas_call(
        functools.partial(_experts_body, tm=tm, tf=tf, ld=ld),
        out_shape=jax.ShapeDtypeStruct((n_slots * SUBLANES, LANES), F32),
        grid_spec=grid_spec,
        compiler_params=_params("arbitrary"),
        name="moe_experts",
    )(tile_expert, n_used, xs, w_gate, w_up, w_down)


def _gather_norm_body(s1_ref, s2_ref, h_ref, gm_ref, g_ref, ys_ref, oa_ref, ob_ref, ya_ref, yb_ref, sem, *,
                      tm, head_tiles):
    def issue(r, carry):
        dst = pl.ds(pl.multiple_of(r * SUBLANES, SUBLANES), SUBLANES)
        pltpu.make_async_copy(ys_ref.at[s1_ref[0, 0, r]], ya_ref.at[dst], sem.at[0]).start(0)
        pltpu.make_async_copy(ys_ref.at[s2_ref[0, 0, r]], yb_ref.at[dst], sem.at[1]).start(1)
        return carry

    lax.fori_loop(0, tm, issue, 0, unroll=8)
    pltpu.make_async_copy(ya_ref, ya_ref, sem.at[0]).wait()
    pltpu.make_async_copy(yb_ref, yb_ref, sem.at[1]).wait()
    gm = gm_ref[...]
    y = _rms(h_ref[...] + (gm[:, 0:1] * _tiles_to_rows(ya_ref, tm) + gm[:, 1:2] * _tiles_to_rows(yb_ref, tm)),
             g_ref[...])
    i = pl.program_id(0)

    @pl.when(i < head_tiles)
    def _():
        oa_ref[...] = y

    @pl.when(i >= head_tiles)
    def _():
        ob_ref[...] = y


def _gather_norm(h, gates, slot1, slot2, ys, gain, split):
    t, d = h.shape
    tm = _pick(split, (512, 256, 128, 64, 32, 16, 8))
    assert t % tm == 0
    nt = t // tm
    head_tiles = split // tm
    smem = pl.BlockSpec((1, 1, tm), lambda i: (i, 0, 0), memory_space=pltpu.SMEM)
    return pl.pallas_call(
        functools.partial(_gather_norm_body, tm=tm, head_tiles=head_tiles),
        out_shape=[jax.ShapeDtypeStruct((split, d), F32), jax.ShapeDtypeStruct((t - split, d), F32)],
        grid=(nt,),
        in_specs=[smem, smem, pl.BlockSpec((tm, d), lambda i: (i, 0)),
                  pl.BlockSpec((tm, LANES), lambda i: (i, 0)),
                  pl.BlockSpec((1, d), lambda i: (0, 0)),
                  pl.BlockSpec(memory_space=pl.ANY)],
        out_specs=[pl.BlockSpec((tm, d), lambda i: (jnp.minimum(i, head_tiles - 1), 0)),
                   pl.BlockSpec((tm, d), lambda i: (jnp.maximum(i - head_tiles, 0), 0))],
        scratch_shapes=[pltpu.VMEM((tm * SUBLANES, LANES), F32), pltpu.VMEM((tm * SUBLANES, LANES), F32),
                        pltpu.SemaphoreType.DMA((2,))],
        compiler_params=_params("arbitrary"),
        name="moe_gather_norm",
    )(slot1.reshape(nt, 1, tm), slot2.reshape(nt, 1, tm), h, gates, gain.reshape(1, -1), ys)


def _moe(h, gain, w_router, w_gate, w_up, w_down, final_gain, split):
    t, _ = h.shape
    n_exp = w_router.shape[1]
    tm_e = 512
    xn, imeta, gates, counts = _router(h, gain, w_router)
    idx1, idx2, rank1, rank2 = imeta[:, 0], imeta[:, 1], imeta[:, 2], imeta[:, 3]
    cnt = counts[0, :n_exp]
    padded = ((cnt + tm_e - 1) // tm_e) * tm_e
    ends = jnp.cumsum(padded)
    starts = ends - padded
    slot1 = starts[idx1] + rank1
    slot2 = starts[idx2] + rank2
    n_tiles = (TOP_K * t + n_exp * (tm_e - 1)) // tm_e
    n_used = (ends[-1] // tm_e).astype(I32)
    tile_start = jnp.arange(n_tiles, dtype=I32) * tm_e
    tile_expert = jnp.sum((tile_start[:, None] >= ends[None, :]).astype(I32), axis=1)
    tile_expert = jnp.minimum(tile_expert, n_exp - 1)
    assert h.shape[1] == SUBLANES * LANES
    n_slots = n_tiles * tm_e
    xs = _dispatch(xn.reshape(t, SUBLANES, LANES), slot1, slot2, n_slots)
    ys = _experts(xs.reshape(n_slots * SUBLANES, LANES), tile_expert, n_used.reshape(1), w_gate, w_up, w_down, tm_e)
    return _gather_norm(h, gates, slot1, slot2, ys.reshape(n_slots, SUBLANES, LANES), final_gain, split)


def _cache_to_slabs(cache):
    n, w, two, heads, hd = cache.shape
    return jnp.transpose(cache, (0, 2, 3, 4, 1)).reshape(n, two, heads * hd, w)


def _slabs_to_cache(slabs, heads):
    n, two, _, w = slabs.shape
    return jnp.transpose(slabs.reshape(n, two, heads, HEAD_DIM, w), (0, 4, 1, 2, 3))


def kernel(x_prompt, x_sample, cache_conv, cache_swa_kv, cache_dil_kv0, cache_dil_kv1, cache_dil_kv2, norm_mix0, w_in0, conv_w, swa_sink, w_out0, norm_ffn0, w_gate0, w_up0, w_down0, norm_mix1, w_in1, w_out1, norm_ffn1, w_router, w_gate1, w_up1, w_down1, norm_final):
    n_p, seq, d = x_prompt.shape
    n_s, s_len, _ = x_sample.shape
    tp, ts = n_p * seq, n_s * s_len
    d_conv = conv_w.shape[2]
    kvh, grp = swa_sink.shape[1], swa_sink.shape[2]
    hq0 = kvh * grp
    h1 = cache_dil_kv0.shape[4]
    c_q0, c_kv0 = hq0 * HEAD_DIM, kvh * HEAD_DIM
    q0_col = 3 * d_conv
    k0_col = q0_col + c_q0
    c_g = 3 * h1 * HEAD_DIM
    dil_caches = (cache_dil_kv0, cache_dil_kv1, cache_dil_kv2)

    x_rows = (x_prompt.reshape(tp, d), x_sample.reshape(ts, d))

    assert c_kv0 == LANES and grp % 2 == 0 and d_conv % LANES == 0
    nsc = d_conv // LANES
    q0_slab = 3 * nsc
    k0_slab = q0_slab + c_q0 // LANES
    proj0 = _dense([x_rows], w_in0[0], gain=norm_mix0[0], out_dtype=F32, name="in_proj0", slab_out=True)
    a_p, conv_tail = _conv_prompt(proj0, conv_w[0], n_seq=n_p, seq_len=seq, c=d_conv)
    att_p = _band_attn(proj0, n_seq=n_p, seq_len=seq, dil=1, q_slab=q0_slab, k_slab=k0_slab, v_slab=k0_slab + 1,
                       n_q_slabs=c_q0 // LANES, shared_kv=True, sink=swa_sink[0], want_lse=False,
                       name="swa_prompt")[0]
    proj0_s = proj0[:, tp:]
    p3 = proj0_s[:q0_slab].reshape(3, nsc, n_s, s_len, LANES)
    p3 = jnp.transpose(p3, (0, 3, 2, 1, 4)).reshape(3, s_len, n_s, d_conv)
    a_s, conv_new = _conv_step(p3, jnp.transpose(cache_conv[0], (1, 0, 2)), conv_w[0])
    a_s = jnp.transpose(a_s, (1, 0, 2)).reshape(ts, d_conv)
    kvt0 = _proj_t(jnp.transpose(w_in0[0][:, k0_col:]), x_rows[1], norm_mix0[0], "kv_t0")
    q_s = proj0_s[q0_slab:k0_slab].reshape(kvh, grp // 2, n_s, s_len, 2, HEAD_DIM)
    q_s = jnp.transpose(q_s, (2, 3, 1, 4, 0, 5)).reshape(n_s, s_len * grp, c_kv0)
    sink_col = jnp.broadcast_to(swa_sink[0][:, None, :], (kvh, s_len, grp)).reshape(kvh * s_len * grp, 1)
    o_s, swa_new = _step_attn(q_s, kvt0.reshape(2, c_kv0, ts), _cache_to_slabs(cache_swa_kv[0]), dil=1,
                              tok_div=grp, sink_col=sink_col, want_lse=False, bsz=8, name="swa_step")
    att_s = jnp.transpose(o_s.reshape(n_s, s_len, grp, kvh, HEAD_DIM), (0, 1, 3, 2, 4)).reshape(ts, c_q0)
    h = _dense([(a_p, a_s.astype(BF16)), (att_p, att_s.astype(BF16))], w_out0[0], res=x_rows, out_dtype=F32,
               name="out_proj0")
    h = _ffn(h, norm_ffn0[0], w_gate0[0], w_up0[0], w_down0[0], "ffn0")

    proj1 = _dense([h], w_in1[0], gain=norm_mix1[0], out_dtype=F32, name="in_proj1", slab_out=True)
    proj1_s = proj1[:, tp:]
    c_h = h1 * HEAD_DIM
    nsh = c_h // LANES
    pairs_p, pairs_s, dil_p, dil_s = [], [], [], []
    for g, dil in enumerate(DILATIONS):
        window = dil_caches[g].shape[2]
        s0 = g * 3 * nsh
        pairs_p.append(tuple(_band_attn(proj1, n_seq=n_p, seq_len=seq, dil=dil, q_slab=s0, k_slab=s0 + nsh,
                                        v_slab=s0 + 2 * nsh, n_q_slabs=nsh, want_lse=True,
                                        name=f"dil{g}_prompt")))
        keep = min(window, seq)
        state = _state_slabs(proj1, first_slab=s0 + nsh, n_slabs=2 * nsh, n_seq=n_p, seq_len=seq, keep=keep,
                             name=f"dil{g}_state")
        dil_p.append(_slabs_to_cache(state.reshape(n_p, 2, c_h, keep), h1)[None])
        lo = g * c_g
        q_g = jnp.transpose(proj1_s[s0:s0 + nsh].reshape(nsh, n_s, s_len, LANES), (1, 2, 0, 3))
        q_g = q_g.reshape(n_s, s_len, c_h)
        q_g = jnp.concatenate([q_g, jnp.zeros((n_s, SUBLANES - s_len, c_h), F32)], axis=1)
        kvt = _proj_t(jnp.transpose(w_in1[0][:, lo + c_h:lo + c_g]), h[tp:], norm_mix1[0], f"kv_t1_{g}")
        o_sg, l_sg, cache_new = _step_attn(q_g, kvt.reshape(2, c_h, ts), _cache_to_slabs(dil_caches[g][0]),
                                           dil=dil, tok_div=1, want_lse=True,
                                           bsz=max(1, 2048 // window), name=f"dil{g}_step")
        pairs_s.append((o_sg.reshape(n_s * SUBLANES, c_h), l_sg.reshape(n_s * SUBLANES, c_h)))
        dil_s.append(_slabs_to_cache(cache_new, h1)[None])
    comb_p = _combine(pairs_p, "combine_prompt")
    comb_s = _combine(pairs_s, "combine_step").reshape(n_s, SUBLANES, c_h)[:, :s_len].reshape(ts, c_h)
    h = _dense([(comb_p, comb_s)], w_out1[0], res=h, out_dtype=F32, name="out_proj1")
    y_p, y_s = _moe(h, norm_ffn1[0], w_router[0], w_gate1[0], w_up1[0], w_down1[0], norm_final, tp)

    y_prompt = y_p.reshape(n_p, seq, d)
    y_sample = y_s.reshape(n_s, s_len, d)
    new_conv_prompt = conv_tail[:, SUBLANES - (CONV_W - 1):][None]
    new_conv_sample = jnp.transpose(conv_new, (1, 0, 2))[None]
    keep0 = min(SPAN, seq)
    swa_state = _state_slabs(proj0, first_slab=k0_slab, n_slabs=2, n_seq=n_p, seq_len=seq, keep=keep0,
                             name="swa_state")
    new_swa_kv_prompt = _slabs_to_cache(swa_state, kvh)[None]
    new_swa_kv_sample = _slabs_to_cache(swa_new, kvh)[None]
    return (y_prompt, y_sample, new_conv_prompt, new_conv_sample, new_swa_kv_prompt, new_swa_kv_sample,
            dil_p[0], dil_s[0], dil_p[1], dil_s[1], dil_p[2], dil_s[2])
```

```python
import functools

import jax
import jax.numpy as jnp
from jax import lax
from jax.experimental import pallas as pl
from jax.experimental.pallas import tpu as pltpu

F32 = jnp.float32
BF16 = jnp.bfloat16
I32 = jnp.int32

EPS = 1e-5
HEAD_DIM = 64
SPAN = 128
CONV_W = 3
DILATIONS = (1, 4, 16)
TOP_K = 2
LANES = 128
SUBLANES = 8
VMEM_LIMIT_BYTES = 56 * 1024 * 1024
NT_DIMS = (((1,), (1,)), ((), ()))


def _params(*sem):
    return pltpu.CompilerParams(dimension_semantics=sem, vmem_limit_bytes=VMEM_LIMIT_BYTES)


def _pick(n, candidates):
    for c in candidates:
        if n % c == 0:
            return c
    raise ValueError(f"no tile for {n} in {candidates}")


def _rms(x, g):
    y = x * lax.rsqrt(jnp.mean(x * x, axis=-1, keepdims=True) + EPS)
    return y * g


STAGE_SLOTS = 3
STAGE_BYTES = 2 * 1024 * 1024


def _stage_rows(k, n):
    rows = k
    while rows * n * 4 > STAGE_BYTES and rows % 32 == 0:
        rows //= 2
    return rows


def _load_bf16(w_hbm, w_vmem, stage, sem):
    n_slots, chunk = stage.shape[0], stage.shape[1]
    n_chunks = w_hbm.shape[0] // chunk

    def copy(c):
        return pltpu.make_async_copy(w_hbm.at[pl.ds(c * chunk, chunk), :], stage.at[c % n_slots],
                                     sem.at[c % n_slots])

    for c in range(min(n_slots - 1, n_chunks)):
        copy(c).start()
    for c in range(n_chunks):
        if c + n_slots - 1 < n_chunks:
            copy(c + n_slots - 1).start()
        copy(c).wait()
        w_vmem[c * chunk:(c + 1) * chunk, :] = stage[c % n_slots].astype(BF16)


def _dense_body(*refs, ks, split, head_tiles, has_gain, has_res, slab_out, cn):
    n_in = len(ks)
    step = pl.program_id(0)
    refs = list(refs)

    def take(is_split):
        count = 2 if is_split else 1
        parts = tuple(refs[:count])
        del refs[:count]
        return parts

    def read(parts, cols=slice(None)):
        if len(parts) == 1:
            return parts[0][:, cols]
        return jnp.where(step < head_tiles, parts[0][:, cols], parts[1][:, cols])

    xs = [take(s) for s in split[:n_in]]
    g_ref = refs.pop(0) if has_gain else None
    w_hbm = refs.pop(0)
    res = take(split[n_in]) if has_res else None
    o_ref, w_ref, stage, sem = refs

    @pl.when(step == 0)
    def _():
        _load_bf16(w_hbm, w_ref, stage, sem)

    if has_gain:
        lhs = [_rms(read(xs[0]), g_ref[...]).astype(BF16)]
    else:
        lhs = [read(x).astype(BF16) for x in xs]
    n = w_ref.shape[1]
    for c in range(n // cn):
        cols = slice(c * cn, (c + 1) * cn)
        acc = None
        row0 = 0
        for a, k in zip(lhs, ks):
            d = jnp.dot(a, w_ref[row0:row0 + k, cols], preferred_element_type=F32)
            acc = d if acc is None else acc + d
            row0 += k
        if has_res:
            acc = read(res, cols) + acc
        if slab_out:
            for s in range(cn // LANES):
                o_ref[c * (cn // LANES) + s] = acc[:, s * LANES:(s + 1) * LANES].astype(o_ref.dtype)
        else:
            o_ref[:, cols] = acc.astype(o_ref.dtype)


def _dense(xs, w, *, gain=None, res=None, out_dtype, name, slab_out=False):
    def n_rows(a):
        return sum(p.shape[0] for p in a) if isinstance(a, tuple) else a.shape[0]

    def width(a):
        return a[0].shape[1] if isinstance(a, tuple) else a.shape[1]

    operands = list(xs) + ([res] if res is not None else [])
    pairs = [a for a in operands if isinstance(a, tuple)]
    t = n_rows(xs[0])
    k_all, n = w.shape
    ks = tuple(width(x) for x in xs)
    assert sum(ks) == k_all
    gcd_rows = t
    for a in pairs:
        gcd_rows = min(gcd_rows, a[1].shape[0])
    tm = _pick(gcd_rows, (512, 256, 128, 64, 32, 16, 8))
    assert t % tm == 0
    head_tiles = pairs[0][0].shape[0] // tm if pairs else 0
    for a in pairs:
        assert a[0].shape[0] == head_tiles * tm
    cn = _pick(n, (512, 384, 256, 128))
    in_specs, args = [], []

    def add_rows(a, k):
        if isinstance(a, tuple):
            in_specs.append(pl.BlockSpec((tm, k), lambda i: (jnp.minimum(i, head_tiles - 1), 0)))
            in_specs.append(pl.BlockSpec((tm, k), lambda i: (jnp.maximum(i - head_tiles, 0), 0)))
            args.extend(a)
        else:
            in_specs.append(pl.BlockSpec((tm, k), lambda i: (i, 0)))
            args.append(a)

    for x, k in zip(xs, ks):
        add_rows(x, k)
    if gain is not None:
        in_specs.append(pl.BlockSpec((1, ks[0]), lambda i: (0, 0)))
        args.append(gain.reshape(1, -1))
    in_specs.append(pl.BlockSpec(memory_space=pl.ANY))
    args.append(w)
    if res is not None:
        add_rows(res, n)
    split = tuple(isinstance(a, tuple) for a in operands)
    if slab_out:
        out_shape = jax.ShapeDtypeStruct((n // LANES, t, LANES), out_dtype)
        out_spec = pl.BlockSpec((n // LANES, tm, LANES), lambda i: (0, i, 0))
    else:
        out_shape = jax.ShapeDtypeStruct((t, n), out_dtype)
        out_spec = pl.BlockSpec((tm, n), lambda i: (i, 0))
    return pl.pallas_call(
        functools.partial(_dense_body, ks=ks, split=split, head_tiles=head_tiles, has_gain=gain is not None,
                          has_res=res is not None, slab_out=slab_out, cn=cn),
        out_shape=out_shape,
        grid=(t // tm,),
        in_specs=in_specs,
        out_specs=out_spec,
        scratch_shapes=[pltpu.VMEM((k_all, n), BF16), pltpu.VMEM((STAGE_SLOTS, _stage_rows(k_all, n), n), F32),
                        pltpu.SemaphoreType.DMA((STAGE_SLOTS,))],
        compiler_params=_params("arbitrary"),
        name=name,
    )(*args)


def _proj_t_body(w_ref, x_ref, g_ref, o_ref):
    xn = _rms(x_ref[...], g_ref[...]).astype(BF16)
    o_ref[...] = lax.dot_general(w_ref[...].astype(BF16), xn, NT_DIMS, preferred_element_type=F32)


def _proj_t(w_t, x, gain, name):
    c, k = w_t.shape
    rows = x.shape[0]
    tc = _pick(c, (512, 256, 128))
    return pl.pallas_call(
        _proj_t_body,
        out_shape=jax.ShapeDtypeStruct((c, rows), F32),
        grid=(c // tc,),
        in_specs=[pl.BlockSpec((tc, k), lambda i: (i, 0)),
                  pl.BlockSpec((rows, k), lambda i: (0, 0)),
                  pl.BlockSpec((1, k), lambda i: (0, 0))],
        out_specs=pl.BlockSpec((tc, rows), lambda i: (i, 0)),
        compiler_params=_params("parallel"),
        name=name,
    )(w_t, x, gain.reshape(1, -1))


def _ffn_body(x_ref, g_ref, wg_hbm, wu_hbm, wd_hbm, o_ref, wg_ref, wu_ref, wd_ref, stage_in, stage_out, sem, *,
              tf):
    @pl.when(pl.program_id(0) == 0)
    def _():
        _load_bf16(wg_hbm, wg_ref, stage_in, sem)
        _load_bf16(wu_hbm, wu_ref, stage_in, sem)
        _load_bf16(wd_hbm, wd_ref, stage_out, sem)

    x = x_ref[...]
    xn = _rms(x, g_ref[...]).astype(BF16)
    acc = None
    for c in range(wg_ref.shape[1] // tf):
        cols = slice(c * tf, (c + 1) * tf)
        gate = jnp.dot(xn, wg_ref[:, cols], preferred_element_type=F32)
        up = jnp.dot(xn, wu_ref[:, cols], preferred_element_type=F32)
        act = (gate * jax.nn.sigmoid(gate) * up).astype(BF16)
        d = jnp.dot(act, wd_ref[cols, :], preferred_element_type=F32)
        acc = d if acc is None else acc + d
    o_ref[...] = x + acc


def _ffn(x, gain, w_gate, w_up, w_down, name):
    t, d = x.shape
    f = w_gate.shape[1]
    tm = _pick(t, (512, 256, 128, 64, 32, 16, 8))
    tf = f
    anywhere = pl.BlockSpec(memory_space=pl.ANY)
    return pl.pallas_call(
        functools.partial(_ffn_body, tf=tf),
        out_shape=jax.ShapeDtypeStruct((t, d), F32),
        grid=(t // tm,),
        in_specs=[pl.BlockSpec((tm, d), lambda i: (i, 0)),
                  pl.BlockSpec((1, d), lambda i: (0, 0)),
                  anywhere, anywhere, anywhere],
        out_specs=pl.BlockSpec((tm, d), lambda i: (i, 0)),
        scratch_shapes=[pltpu.VMEM((d, f), BF16), pltpu.VMEM((d, f), BF16), pltpu.VMEM((f, d), BF16),
                        pltpu.VMEM((STAGE_SLOTS, _stage_rows(d, f), f), F32),
                        pltpu.VMEM((STAGE_SLOTS, _stage_rows(f, d), d), F32),
                        pltpu.SemaphoreType.DMA((STAGE_SLOTS,))],
        compiler_params=_params("arbitrary"),
        name=name,
    )(x, gain.reshape(1, -1), w_gate, w_up, w_down)


def _band_body(*refs, dil, mb, shared_kv, has_sink, want_lse):
    q_ref, kc_ref, kp_ref, vc_ref, vp_ref = refs[:5]
    pos = 5
    sink_ref = refs[pos] if has_sink else None
    pos += int(has_sink)
    o_ref = refs[pos]
    lse_ref = refs[pos + 1] if want_lse else None
    t = pl.program_id(1)
    scale = HEAD_DIM ** -0.5
    qi = lax.broadcasted_iota(I32, (2 * SPAN, 2 * SPAN), 0) & (SPAN - 1)
    kj = lax.broadcasted_iota(I32, (2 * SPAN, 2 * SPAN), 1)
    band = (kj >= qi) & (kj <= qi + SPAN)
    band_first = band & (kj >= jnp.where(t == 0, SPAN, 0))
    lane = lax.broadcasted_iota(I32, (1, LANES), 1)
    halves = [lane < HEAD_DIM, lane >= HEAD_DIM]

    def rows_of(ref, s, r, count):
        if dil == 1:
            return ref[s, 0:count, :]
        return ref[s, pl.ds(r, count, stride=dil), :]

    def pair(q_a, mask_a, q_b, mask_b, kw, vw, msk, heads):
        q2 = jnp.concatenate([jnp.where(mask_a, q_a, 0.0), jnp.where(mask_b, q_b, 0.0)], axis=0).astype(BF16)
        s = lax.dot_general(q2, kw, NT_DIMS, preferred_element_type=F32)
        s = jnp.where(msk, s, -jnp.inf)
        m = jnp.max(s, axis=-1, keepdims=True)
        if has_sink:
            sink = jnp.concatenate([jnp.full((SPAN, 1), sink_ref[h], F32) for h in heads], axis=0)
            m = jnp.maximum(m, sink)
        p = jnp.exp(s - m)
        den = jnp.sum(p, axis=-1, keepdims=True)
        if has_sink:
            den = den + jnp.exp(sink - m)
        o = jnp.dot(p.astype(BF16), vw, preferred_element_type=F32) / den
        return o, m + jnp.log(den)

    def store(s, r, b, o_tile, lse_tile):
        if dil == 1:
            o_ref[b * SPAN:(b + 1) * SPAN, s * LANES:(s + 1) * LANES] = o_tile.astype(o_ref.dtype)
            if want_lse:
                lse_ref[b * SPAN:(b + 1) * SPAN, s * LANES:(s + 1) * LANES] = lse_tile
        else:
            o_ref[s, pl.ds(r + dil * b * SPAN, SPAN, stride=dil), :] = o_tile
            if want_lse:
                lse_ref[s, pl.ds(r + dil * b * SPAN, SPAN, stride=dil), :] = lse_tile

    n_slabs = q_ref.shape[0]
    first_head = 2 * n_slabs * pl.program_id(2)
    top, bot = slice(0, SPAN), slice(SPAN, 2 * SPAN)
    for r in range(dil):
        qs = [rows_of(q_ref, s, r, mb * SPAN) * scale for s in range(n_slabs)]
        kf = [jnp.concatenate([rows_of(kp_ref, s, r, SPAN), rows_of(kc_ref, s, r, mb * SPAN)], axis=0)
              for s in range(kc_ref.shape[0])]
        vf = [jnp.concatenate([rows_of(vp_ref, s, r, SPAN), rows_of(vc_ref, s, r, mb * SPAN)], axis=0)
              for s in range(vc_ref.shape[0])]
        ks = [k.astype(BF16) for k in kf]
        vs = [v.astype(BF16) for v in vf]
        if shared_kv:
            k_sw = pltpu.roll(kf[0], HEAD_DIM, 1).astype(BF16)
            v_sw = pltpu.roll(vf[0], HEAD_DIM, 1).astype(BF16)
        for b in range(mb):
            blk = slice(b * SPAN, (b + 1) * SPAN)
            keys = slice(b * SPAN, (b + 2) * SPAN)
            msk = band_first if b == 0 else band
            if not shared_kv:
                for s in range(n_slabs):
                    o, lse = pair(qs[s][blk], halves[0], qs[s][blk], halves[1], ks[s][keys], vs[s][keys], msk,
                                  (first_head + 2 * s, first_head + 2 * s + 1))
                    store(s, r, b, jnp.where(halves[0], o[top], o[bot]), jnp.where(halves[0], lse[top], lse[bot]))
            else:
                for kvh in range(2):
                    sa, sb = 2 * kvh, 2 * kvh + 1
                    o_al, l_al = pair(qs[sa][blk], halves[kvh], qs[sb][blk], halves[kvh], ks[0][keys],
                                      vs[0][keys], msk, (2 * sa + kvh, 2 * sb + kvh))
                    o_sw, l_sw = pair(qs[sa][blk], halves[1 - kvh], qs[sb][blk], halves[1 - kvh], k_sw[keys],
                                      v_sw[keys], msk, (2 * sa + 1 - kvh, 2 * sb + 1 - kvh))
                    store(sa, r, b, jnp.where(halves[kvh], o_al[top], o_sw[top]),
                          jnp.where(halves[kvh], l_al[top], l_sw[top]))
                    store(sb, r, b, jnp.where(halves[kvh], o_al[bot], o_sw[bot]),
                          jnp.where(halves[kvh], l_al[bot], l_sw[bot]))


def _band_attn(slabs, *, n_seq, seq_len, dil, q_slab, k_slab, v_slab, n_q_slabs, shared_kv=False, sink=None,
               want_lse, name):
    mb = max(1, 512 // (dil * SPAN))
    tp = dil * SPAN * mb
    nt = seq_len // tp
    assert seq_len % tp == 0 and (not shared_kv or (n_q_slabs == 4 and dil == 1))
    prev_rows = dil * SPAN
    per_step = n_q_slabs if (shared_kv or tp <= 512) else 1
    kv_block = 1 if shared_kv else per_step
    kv_step = 0 if shared_kv else 1

    def cur(base, step):
        return lambda n, t, s: (base + s * step, n * nt + t, 0)

    def prev(base, step):
        return lambda n, t, s: (base + s * step, jnp.maximum((n * nt + t) * mb - 1, 0), 0)

    in_specs = [pl.BlockSpec((per_step, tp, LANES), cur(q_slab // per_step, 1)),
                pl.BlockSpec((kv_block, tp, LANES), cur(k_slab // kv_block, kv_step)),
                pl.BlockSpec((kv_block, prev_rows, LANES), prev(k_slab // kv_block, kv_step)),
                pl.BlockSpec((kv_block, tp, LANES), cur(v_slab // kv_block, kv_step)),
                pl.BlockSpec((kv_block, prev_rows, LANES), prev(v_slab // kv_block, kv_step))]
    assert q_slab % per_step == 0 and k_slab % kv_block == 0 and v_slab % kv_block == 0
    args = [slabs] * 5
    if sink is not None:
        in_specs.append(pl.BlockSpec(memory_space=pltpu.SMEM))
        args.append(sink.reshape(-1).astype(F32))
    rows = n_seq * seq_len
    if dil == 1:
        spec = pl.BlockSpec((tp, per_step * LANES), lambda n, t, s: (n * nt + t, s))
        out_shape = [jax.ShapeDtypeStruct((rows, n_q_slabs * LANES), BF16)]
        lse_shape = jax.ShapeDtypeStruct((rows, n_q_slabs * LANES), F32)
    else:
        spec = pl.BlockSpec((per_step, tp, LANES), lambda n, t, s: (s, n * nt + t, 0))
        out_shape = [jax.ShapeDtypeStruct((n_q_slabs, rows, LANES), F32)]
        lse_shape = jax.ShapeDtypeStruct((n_q_slabs, rows, LANES), F32)
    out_specs = [spec]
    if want_lse:
        out_shape.append(lse_shape)
        out_specs.append(spec)
    return pl.pallas_call(
        functools.partial(_band_body, dil=dil, mb=mb, shared_kv=shared_kv, has_sink=sink is not None,
                          want_lse=want_lse),
        out_shape=out_shape,
        grid=(n_seq, nt, n_q_slabs // per_step),
        in_specs=in_specs,
        out_specs=out_specs,
        compiler_params=_params("parallel", "parallel", "parallel"),
        name=name,
    )(*args)


def _combine_body(o0, l0, o1, l1, o2, l2, out_ref):
    def tile(ref):
        return ref[...].reshape(ref.shape[-2:]).astype(F32)

    a0, a1, a2 = tile(l0), tile(l1), tile(l2)
    m = jnp.maximum(jnp.maximum(a0, a1), a2)
    e0, e1, e2 = jnp.exp(a0 - m), jnp.exp(a1 - m), jnp.exp(a2 - m)
    num = e0 * tile(o0) + e1 * tile(o1) + e2 * tile(o2)
    out_ref[...] = (num / (e0 + e1 + e2)).astype(out_ref.dtype)


def _combine(pairs, name):
    first = pairs[0][0]
    rows, c = first.shape if first.ndim == 2 else (first.shape[1], first.shape[0] * LANES)
    tm = _pick(rows, (1024, 512, 256, 128, 64, 32, 16, 8))
    flat = pl.BlockSpec((tm, LANES), lambda i, s: (i, s))
    slab = pl.BlockSpec((1, tm, LANES), lambda i, s: (s, i, 0))
    args = [a for pair in pairs for a in pair]
    return pl.pallas_call(
        _combine_body,
        out_shape=jax.ShapeDtypeStruct((rows, c), BF16),
        grid=(rows // tm, c // LANES),
        in_specs=[flat if a.ndim == 2 else slab for a in args],
        out_specs=flat,
        compiler_params=_params("parallel", "parallel"),
        name=name,
    )(*args)


def _state_body(x_ref, o_ref):
    o_ref[0, 0] = x_ref[0].T


def _state_slabs(slabs, *, first_slab, n_slabs, n_seq, seq_len, keep, name):
    assert seq_len % keep == 0
    per_seq = seq_len // keep
    return pl.pallas_call(
        _state_body,
        out_shape=jax.ShapeDtypeStruct((n_seq, n_slabs, LANES, keep), F32),
        grid=(n_seq, n_slabs),
        in_specs=[pl.BlockSpec((1, keep, LANES), lambda n, s: (first_slab + s, (n + 1) * per_seq - 1, 0))],
        out_specs=pl.BlockSpec((1, 1, LANES, keep), lambda n, s: (n, s, 0, 0)),
        compiler_params=_params("parallel", "parallel"),
        name=name,
    )(slabs)


def _conv_prompt_body(gb_ref, gc_ref, xa_ref, gcp_ref, xap_ref, w_ref, a_ref, st_ref):
    t = pl.program_id(1)
    w = w_ref[...]
    for s in range(gb_ref.shape[0]):
        lanes = slice(s * LANES, (s + 1) * LANES)
        u = gc_ref[s] * xa_ref[s]
        up = jnp.where(t == 0, 0.0, gcp_ref[s] * xap_ref[s])
        ext = jnp.concatenate([up, u], axis=0)
        y = (w[0:1, lanes] * pltpu.roll(ext, 2, 0)[SUBLANES:]
             + w[1:2, lanes] * pltpu.roll(ext, 1, 0)[SUBLANES:]) + w[2:3, lanes] * u
        a_ref[:, lanes] = (gb_ref[s] * y).astype(a_ref.dtype)
        st_ref[0, :, lanes] = u[u.shape[0] - SUBLANES:]


def _conv_prompt(slabs, conv_w, *, n_seq, seq_len, c):
    tq = _pick(seq_len, (512, 256, 128))
    nt = seq_len // tq
    rb = tq // SUBLANES
    ns = c // LANES

    def cur(part):
        return lambda n, t: (part, n * nt + t, 0)

    def prev(part):
        return lambda n, t: (part, jnp.maximum((n * nt + t) * rb - 1, 0), 0)

    return pl.pallas_call(
        _conv_prompt_body,
        out_shape=[jax.ShapeDtypeStruct((n_seq * seq_len, c), BF16),
                   jax.ShapeDtypeStruct((n_seq, SUBLANES, c), F32)],
        grid=(n_seq, nt),
        in_specs=[pl.BlockSpec((ns, tq, LANES), cur(0)), pl.BlockSpec((ns, tq, LANES), cur(1)),
                  pl.BlockSpec((ns, tq, LANES), cur(2)),
                  pl.BlockSpec((ns, SUBLANES, LANES), prev(1)), pl.BlockSpec((ns, SUBLANES, LANES), prev(2)),
                  pl.BlockSpec((CONV_W, c), lambda n, t: (0, 0))],
        out_specs=[pl.BlockSpec((tq, c), lambda n, t: (n * nt + t, 0)),
                   pl.BlockSpec((1, SUBLANES, c), lambda n, t: (n, 0, 0))],
        compiler_params=_params("parallel", "arbitrary"),
        name="conv_prompt",
    )(slabs, slabs, slabs, slabs, slabs, conv_w)


def _conv_step_body(p_ref, prev_ref, w_ref, a_ref, st_ref):
    s_len = p_ref.shape[1]
    w = w_ref[...]
    hist = [prev_ref[k] for k in range(CONV_W - 1)] + [p_ref[1, s] * p_ref[2, s] for s in range(s_len)]
    for s in range(s_len):
        y = (w[0:1] * hist[s] + w[1:2] * hist[s + 1]) + w[2:3] * hist[s + 2]
        a_ref[s] = p_ref[0, s] * y
    for k in range(CONV_W - 1):
        st_ref[k] = hist[s_len + k]


def _conv_step(p3, prev, conv_w):
    _, s_len, n, c = p3.shape
    return pl.pallas_call(
        _conv_step_body,
        out_shape=[jax.ShapeDtypeStruct((s_len, n, c), F32), jax.ShapeDtypeStruct((CONV_W - 1, n, c), F32)],
        name="conv_step",
        compiler_params=pltpu.CompilerParams(vmem_limit_bytes=VMEM_LIMIT_BYTES),
    )(p3, prev, conv_w)


def _step_body(*refs, n_blk, qr, tok_div, w, dil, cw, bsz, has_sink, want_lse):
    q_ref, kvt_ref, cache_ref = refs[:3]
    pos = 3
    sink_ref = refs[pos] if has_sink else None
    pos += int(has_sink)
    o_ref = refs[pos]
    pos += 1
    lse_ref = refs[pos] if want_lse else None
    pos += int(want_lse)
    cout_ref = refs[pos]
    r_dim = n_blk * HEAD_DIM
    nrb = n_blk * qr
    n_chunks = w // cw
    scale = HEAD_DIM ** -0.5
    step = pl.program_id(0)
    lane_r = lax.broadcasted_iota(I32, (1, r_dim), 1)
    blk_masks = [(lane_r >= j * HEAD_DIM) & (lane_r < (j + 1) * HEAD_DIM) for j in range(n_blk)]
    row = lax.broadcasted_iota(I32, (nrb, 1), 0)
    tok = (row % qr) // tok_div
    lane_c = lax.broadcasted_iota(I32, (1, cw), 1)
    lane_n = lax.broadcasted_iota(I32, (1, LANES), 1)

    def valid(pos_l):
        ok = (pos_l >= tok) & (pos_l <= w + tok)
        if dil > 1:
            ok = ok & (((pos_l - tok) & (dil - 1)) == 0)
        return ok

    for b in range(bsz):
        off = ((step * bsz + b) * 4) % LANES
        shift = (LANES - off) % LANES
        new_k = pltpu.roll(kvt_ref[0], shift, 1)
        new_v = pltpu.roll(kvt_ref[1], shift, 1)
        q = q_ref[b] * scale
        qbd = jnp.concatenate([jnp.where(mk, q, 0.0) for mk in blk_masks], axis=0).astype(BF16)
        scores = []
        for c in range(n_chunks):
            lo, hi = c * cw, (c + 1) * cw
            kc = cache_ref[b, 0, :, lo:hi]
            sc = jnp.dot(qbd, kc.astype(BF16), preferred_element_type=F32)
            scores.append(jnp.where(valid(lane_c + lo), sc, -jnp.inf))
            nxt = cache_ref[b, 0, :, hi:hi + LANES] if c + 1 < n_chunks else new_k
            ext = jnp.concatenate([kc, nxt], axis=1)
            cout_ref[b, 0, :, lo:hi] = pltpu.roll(ext, cw + LANES - 4, 1)[:, :cw]
        sc = jnp.dot(qbd, new_k.astype(BF16), preferred_element_type=F32)
        scores.append(jnp.where(valid(lane_n + w), sc, -jnp.inf))
        m = functools.reduce(jnp.maximum, [jnp.max(s, axis=-1, keepdims=True) for s in scores])
        if has_sink:
            m = jnp.maximum(m, sink_ref[...])
        probs = [jnp.exp(s - m) for s in scores]
        den = functools.reduce(lambda a, c: a + c, [jnp.sum(p, axis=-1, keepdims=True) for p in probs])
        if has_sink:
            den = den + jnp.exp(sink_ref[...] - m)
        pv = lax.dot_general(probs[-1].astype(BF16), new_v.astype(BF16), NT_DIMS, preferred_element_type=F32)
        for c in range(n_chunks):
            lo, hi = c * cw, (c + 1) * cw
            vc = cache_ref[b, 1, :, lo:hi]
            pv = pv + lax.dot_general(probs[c].astype(BF16), vc.astype(BF16), NT_DIMS,
                                      preferred_element_type=F32)
            nxt = cache_ref[b, 1, :, hi:hi + LANES] if c + 1 < n_chunks else new_v
            ext = jnp.concatenate([vc, nxt], axis=1)
            cout_ref[b, 1, :, lo:hi] = pltpu.roll(ext, cw + LANES - 4, 1)[:, :cw]
        o = jnp.zeros((qr, r_dim), F32)
        m_e = jnp.zeros((qr, r_dim), F32)
        den_e = jnp.zeros((qr, r_dim), F32)
        for j, mk in enumerate(blk_masks):
            rows = slice(j * qr, (j + 1) * qr)
            o = jnp.where(mk, pv[rows], o)
            m_e = jnp.where(mk, m[rows], m_e)
            den_e = jnp.where(mk, den[rows], den_e)
        o_ref[b] = o / den_e
        if want_lse:
            lse_ref[b] = m_e + jnp.log(den_e)


def _step_attn(q, kvt, cache, *, dil, tok_div, sink_col=None, want_lse, bsz, name):
    n, qr, r_dim = q.shape
    w = cache.shape[-1]
    n_blk = r_dim // HEAD_DIM
    cw = min(w, 512)
    in_specs = [pl.BlockSpec((bsz, qr, r_dim), lambda i: (i, 0, 0)),
                pl.BlockSpec((2, r_dim, LANES), lambda i: (0, 0, (i * bsz * 4) // LANES)),
                pl.BlockSpec((bsz, 2, r_dim, w), lambda i: (i, 0, 0, 0))]
    args = [q, kvt, cache]
    if sink_col is not None:
        in_specs.append(pl.BlockSpec((n_blk * qr, 1), lambda i: (0, 0)))
        args.append(sink_col)
    o_spec = pl.BlockSpec((bsz, qr, r_dim), lambda i: (i, 0, 0))
    out_shape = [jax.ShapeDtypeStruct((n, qr, r_dim), F32)]
    out_specs = [o_spec]
    if want_lse:
        out_shape.append(jax.ShapeDtypeStruct((n, qr, r_dim), F32))
        out_specs.append(o_spec)
    out_shape.append(jax.ShapeDtypeStruct(cache.shape, F32))
    out_specs.append(pl.BlockSpec((bsz, 2, r_dim, w), lambda i: (i, 0, 0, 0)))
    return pl.pallas_call(
        functools.partial(_step_body, n_blk=n_blk, qr=qr, tok_div=tok_div, w=w, dil=dil, cw=cw, bsz=bsz,
                          has_sink=sink_col is not None, want_lse=want_lse),
        out_shape=out_shape,
        grid=(n // bsz,),
        in_specs=in_specs,
        out_specs=out_specs,
        compiler_params=_params("parallel"),
        name=name,
    )(*args)


def _rows_to_tiles(ref, x):
    rows = x.shape[0]
    for s in range(SUBLANES):
        ref[pl.ds(s, rows, stride=SUBLANES), :] = x[:, s * LANES:(s + 1) * LANES]


def _tiles_to_rows(ref, rows):
    return jnp.concatenate([ref[pl.ds(s, rows, stride=SUBLANES), :] for s in range(SUBLANES)], axis=1)


def _router_body(h_ref, g_ref, wr_ref, xn_ref, im_ref, gm_ref, cnt_ref, carry_ref, *, n_exp):
    i = pl.program_id(0)

    @pl.when(i == 0)
    def _():
        carry_ref[...] = jnp.zeros_like(carry_ref)

    xn = _rms(h_ref[...], g_ref[...])
    _rows_to_tiles(xn_ref, xn)
    tm = xn.shape[0]
    wr = wr_ref[...]
    xh = xn.astype(BF16)
    xl = (xn - xh.astype(F32)).astype(BF16)
    wh = wr.astype(BF16)
    wl = (wr - wh.astype(F32)).astype(BF16)
    lg = jnp.dot(xh, wh, preferred_element_type=F32) + (
        jnp.dot(xh, wl, preferred_element_type=F32) + jnp.dot(xl, wh, preferred_element_type=F32))
    lane = lax.broadcasted_iota(I32, (tm, LANES), 1)
    lane_f = lane.astype(F32)
    lg = jnp.where(lane < n_exp, lg, -jnp.inf)
    m1 = jnp.max(lg, axis=-1, keepdims=True)
    i1 = jnp.min(jnp.where(lg == m1, lane_f, float(LANES)), axis=-1, keepdims=True)
    lg2 = jnp.where(lane_f == i1, -jnp.inf, lg)
    m2 = jnp.max(lg2, axis=-1, keepdims=True)
    i2 = jnp.min(jnp.where(lg2 == m2, lane_f, float(LANES)), axis=-1, keepdims=True)
    e = jnp.exp(m2 - m1)
    g1 = 1.0 / (1.0 + e)
    g2 = e / (1.0 + e)
    sel1 = lane_f == i1
    sel2 = lane_f == i2
    onehot = jnp.where(sel1 | sel2, 1.0, 0.0)
    r_i = lax.broadcasted_iota(I32, (tm, tm), 0)
    c_i = lax.broadcasted_iota(I32, (tm, tm), 1)
    tri = jnp.where(c_i < r_i, 1.0, 0.0).astype(BF16)
    before = jnp.dot(tri, onehot.astype(BF16), preferred_element_type=F32) + carry_ref[0:1]
    r1 = jnp.sum(jnp.where(sel1, before, 0.0), axis=-1, keepdims=True)
    r2 = jnp.sum(jnp.where(sel2, before, 0.0), axis=-1, keepdims=True)
    total = carry_ref[0:1] + jnp.sum(onehot, axis=0, keepdims=True)
    carry_ref[...] = jnp.broadcast_to(total, carry_ref.shape)
    cnt_ref[...] = jnp.broadcast_to(total, cnt_ref.shape).astype(I32)
    meta = jnp.where(lane == 0, i1, jnp.where(lane == 1, i2, jnp.where(lane == 2, r1, jnp.where(lane == 3, r2, 0.0))))
    im_ref[...] = meta.astype(I32)
    gm_ref[...] = jnp.where(lane == 0, g1, jnp.where(lane == 1, g2, 0.0))


def _router(h, gain, w_router):
    t, d = h.shape
    n_exp = w_router.shape[1]
    tm = _pick(t, (512, 256, 128, 64, 32, 16, 8))
    wr = jnp.zeros((d, LANES), F32).at[:, :n_exp].set(w_router)
    return pl.pallas_call(
        functools.partial(_router_body, n_exp=n_exp),
        out_shape=[jax.ShapeDtypeStruct((t * SUBLANES, LANES), F32), jax.ShapeDtypeStruct((t, LANES), I32),
                   jax.ShapeDtypeStruct((t, LANES), F32), jax.ShapeDtypeStruct((SUBLANES, LANES), I32)],
        grid=(t // tm,),
        in_specs=[pl.BlockSpec((tm, d), lambda i: (i, 0)),
                  pl.BlockSpec((1, d), lambda i: (0, 0)),
                  pl.BlockSpec((d, LANES), lambda i: (0, 0))],
        out_specs=[pl.BlockSpec((tm * SUBLANES, LANES), lambda i: (i, 0)),
                   pl.BlockSpec((tm, LANES), lambda i: (i, 0)),
                   pl.BlockSpec((tm, LANES), lambda i: (i, 0)),
                   pl.BlockSpec((SUBLANES, LANES), lambda i: (0, 0))],
        scratch_shapes=[pltpu.VMEM((SUBLANES, LANES), F32)],
        compiler_params=_params("arbitrary"),
        name="moe_router",
    )(h, gain.reshape(1, -1), wr)


def _dispatch_body(s1_ref, s2_ref, x_ref, zero_ref, out_ref, sem, *, tm):
    del zero_ref

    def issue(r, carry):
        src = x_ref.at[r]
        pltpu.make_async_copy(src, out_ref.at[s1_ref[0, 0, r]], sem.at[0]).start(0)
        pltpu.make_async_copy(src, out_ref.at[s2_ref[0, 0, r]], sem.at[1]).start(1)
        return carry

    lax.fori_loop(0, tm, issue, 0, unroll=8)
    pltpu.make_async_copy(x_ref, out_ref.at[pl.ds(0, tm)], sem.at[0]).wait()
    pltpu.make_async_copy(x_ref, out_ref.at[pl.ds(0, tm)], sem.at[1]).wait()


def _dispatch(xn, slot1, slot2, n_slots):
    t = xn.shape[0]
    tm = _pick(t, (512, 256, 128, 64, 32, 16, 8))
    nt = t // tm
    tile = xn.shape[1:]
    smem = pl.BlockSpec((1, 1, tm), lambda i: (i, 0, 0), memory_space=pltpu.SMEM)
    return pl.pallas_call(
        functools.partial(_dispatch_body, tm=tm),
        out_shape=jax.ShapeDtypeStruct((n_slots,) + tile, F32),
        grid=(nt,),
        in_specs=[smem, smem, pl.BlockSpec((tm,) + tile, lambda i: (i, 0, 0)), pl.BlockSpec(memory_space=pl.ANY)],
        out_specs=pl.BlockSpec(memory_space=pl.ANY),
        scratch_shapes=[pltpu.SemaphoreType.DMA((2,))],
        input_output_aliases={3: 0},
        compiler_params=_params("arbitrary"),
        name="moe_dispatch",
    )(slot1.reshape(nt, 1, tm), slot2.reshape(nt, 1, tm), xn, jnp.zeros((n_slots,) + tile, F32))


def _experts_body(te_ref, nu_ref, x_ref, wg_hbm, wu_hbm, wd_hbm, o_ref, wg_ref, wu_ref, wd_ref, stage_in, stage_out,
                  sem, *, tm, tf):
    i = pl.program_id(0)
    active = i < nu_ref[0]
    expert = te_ref[i]
    changed = (i == 0) | (expert != te_ref[jnp.maximum(i - 1, 0)])

    @pl.when(active & changed)
    def _():
        _load_bf16(wg_hbm.at[expert], wg_ref, stage_in, sem)
        _load_bf16(wu_hbm.at[expert], wu_ref, stage_in, sem)
        _load_bf16(wd_hbm.at[expert], wd_ref, stage_out, sem)

    @pl.when(active)
    def _():
        xb = _tiles_to_rows(x_ref, tm).astype(BF16)
        acc = None
        for c in range(wg_ref.shape[1] // tf):
            cols = slice(c * tf, (c + 1) * tf)
            gate = jnp.dot(xb, wg_ref[:, cols], preferred_element_type=F32)
            up = jnp.dot(xb, wu_ref[:, cols], preferred_element_type=F32)
            act = (gate * jax.nn.sigmoid(gate) * up).astype(BF16)
            d = jnp.dot(act, wd_ref[cols, :], preferred_element_type=F32)
            acc = d if acc is None else acc + d
        _rows_to_tiles(o_ref, acc)

    @pl.when(jnp.logical_not(active))
    def _():
        o_ref[...] = jnp.zeros_like(o_ref)


def _experts(xs, tile_expert, n_used, w_gate, w_up, w_down, tm):
    n_slots = xs.shape[0] // SUBLANES
    d = SUBLANES * LANES
    f = w_gate.shape[2]
    tf = _pick(f, (1792, 1024, 512, 256, 128))
    n_tiles = n_slots // tm
    anywhere = pl.BlockSpec(memory_space=pl.ANY)
    grid_spec = pltpu.PrefetchScalarGridSpec(
        num_scalar_prefetch=2,
        grid=(n_tiles,),
        in_specs=[pl.BlockSpec((tm * SUBLANES, LANES), lambda i, te, nu: (jnp.minimum(i, nu[0] - 1), 0)),
                  anywhere, anywhere, anywhere],
        out_specs=pl.BlockSpec((tm * SUBLANES, LANES), lambda i, te, nu: (i, 0)),
        scratch_shapes=[pltpu.VMEM((d, f), BF16), pltpu.VMEM((d, f), BF16), pltpu.VMEM((f, d), BF16),
                        pltpu.VMEM((STAGE_SLOTS, _stage_rows(d, f), f), F32),
                        pltpu.VMEM((STAGE_SLOTS, _stage_rows(f, d), d), F32),
                        pltpu.SemaphoreType.DMA((STAGE_SLOTS,))],
    )
    return pl.pallas_call(
        functools.partial(_experts_body, tm=tm, tf=tf),
        out_shape=jax.ShapeDtypeStruct((n_slots * SUBLANES, LANES), F32),
        grid_spec=grid_spec,
        compiler_params=_params("arbitrary"),
        name="moe_experts",
    )(tile_expert, n_used, xs, w_gate, w_up, w_down)


def _gather_norm_body(s1_ref, s2_ref, n1_ref, n2_ref, h_ref, gm_ref, g_ref, ys_ref, oa_ref, ob_ref, ya_ref, yb_ref,
                      sem, *, tm, head_tiles, n_steps):
    i = pl.program_id(0)
    cur = i % 2

    def gather(a_ref, b_ref, buf):
        def issue(r, carry):
            dst = pl.ds(pl.multiple_of(r * SUBLANES, SUBLANES), SUBLANES)
            pltpu.make_async_copy(ys_ref.at[a_ref[0, 0, r]], ya_ref.at[buf, dst], sem.at[buf, 0]).start(0)
            pltpu.make_async_copy(ys_ref.at[b_ref[0, 0, r]], yb_ref.at[buf, dst], sem.at[buf, 1]).start(1)
            return carry

        lax.fori_loop(0, tm, issue, 0, unroll=8)

    @pl.when(i == 0)
    def _():
        gather(s1_ref, s2_ref, 0)

    @pl.when(i + 1 < n_steps)
    def _():
        gather(n1_ref, n2_ref, 1 - cur)

    pltpu.make_async_copy(ya_ref.at[cur], ya_ref.at[cur], sem.at[cur, 0]).wait()
    pltpu.make_async_copy(yb_ref.at[cur], yb_ref.at[cur], sem.at[cur, 1]).wait()
    gm = gm_ref[...]
    y = _rms(h_ref[...] + (gm[:, 0:1] * _tiles_to_rows(ya_ref.at[cur], tm)
                           + gm[:, 1:2] * _tiles_to_rows(yb_ref.at[cur], tm)), g_ref[...])

    @pl.when(i < head_tiles)
    def _():
        oa_ref[...] = y

    @pl.when(i >= head_tiles)
    def _():
        ob_ref[...] = y


def _gather_norm(h, gates, slot1, slot2, ys, gain, split):
    t, d = h.shape
    tm = _pick(split, (512, 256, 128, 64, 32, 16, 8))
    assert t % tm == 0
    nt = t // tm
    head_tiles = split // tm
    smem = pl.BlockSpec((1, 1, tm), lambda i: (i, 0, 0), memory_space=pltpu.SMEM)
    smem_next = pl.BlockSpec((1, 1, tm), lambda i: (jnp.minimum(i + 1, nt - 1), 0, 0), memory_space=pltpu.SMEM)
    s1, s2 = slot1.reshape(nt, 1, tm), slot2.reshape(nt, 1, tm)
    return pl.pallas_call(
        functools.partial(_gather_norm_body, tm=tm, head_tiles=head_tiles, n_steps=nt),
        out_shape=[jax.ShapeDtypeStruct((split, d), F32), jax.ShapeDtypeStruct((t - split, d), F32)],
        grid=(nt,),
        in_specs=[smem, smem, smem_next, smem_next, pl.BlockSpec((tm, d), lambda i: (i, 0)),
                  pl.BlockSpec((tm, LANES), lambda i: (i, 0)),
                  pl.BlockSpec((1, d), lambda i: (0, 0)),
                  pl.BlockSpec(memory_space=pl.ANY)],
        out_specs=[pl.BlockSpec((tm, d), lambda i: (jnp.minimum(i, head_tiles - 1), 0)),
                   pl.BlockSpec((tm, d), lambda i: (jnp.maximum(i - head_tiles, 0), 0))],
        scratch_shapes=[pltpu.VMEM((2, tm * SUBLANES, LANES), F32), pltpu.VMEM((2, tm * SUBLANES, LANES), F32),
                        pltpu.SemaphoreType.DMA((2, 2))],
        compiler_params=_params("arbitrary"),
        name="moe_gather_norm",
    )(s1, s2, s1, s2, h, gates, gain.reshape(1, -1), ys)


def _moe(h, gain, w_router, w_gate, w_up, w_down, final_gain, split):
    t, _ = h.shape
    n_exp = w_router.shape[1]
    tm_e = 512
    xn, imeta, gates, counts = _router(h, gain, w_router)
    idx1, idx2, rank1, rank2 = imeta[:, 0], imeta[:, 1], imeta[:, 2], imeta[:, 3]
    cnt = counts[0, :n_exp]
    padded = ((cnt + tm_e - 1) // tm_e) * tm_e
    ends = jnp.cumsum(padded)
    starts = ends - padded
    slot1 = starts[idx1] + rank1
    slot2 = starts[idx2] + rank2
    n_tiles = (TOP_K * t + n_exp * (tm_e - 1)) // tm_e
    n_used = (ends[-1] // tm_e).astype(I32)
    tile_start = jnp.arange(n_tiles, dtype=I32) * tm_e
    tile_expert = jnp.sum((tile_start[:, None] >= ends[None, :]).astype(I32), axis=1)
    tile_expert = jnp.minimum(tile_expert, n_exp - 1)
    assert h.shape[1] == SUBLANES * LANES
    n_slots = n_tiles * tm_e
    xs = _dispatch(xn.reshape(t, SUBLANES, LANES), slot1, slot2, n_slots)
    ys = _experts(xs.reshape(n_slots * SUBLANES, LANES), tile_expert, n_used.reshape(1), w_gate, w_up, w_down, tm_e)
    return _gather_norm(h, gates, slot1, slot2, ys.reshape(n_slots, SUBLANES, LANES), final_gain, split)


def _cache_to_slabs(cache):
    n, w, two, heads, hd = cache.shape
    return jnp.transpose(cache, (0, 2, 3, 4, 1)).reshape(n, two, heads * hd, w)


def _slabs_to_cache(slabs, heads):
    n, two, _, w = slabs.shape
    return jnp.transpose(slabs.reshape(n, two, heads, HEAD_DIM, w), (0, 4, 1, 2, 3))


def kernel(x_prompt, x_sample, cache_conv, cache_swa_kv, cache_dil_kv0, cache_dil_kv1, cache_dil_kv2, norm_mix0, w_in0, conv_w, swa_sink, w_out0, norm_ffn0, w_gate0, w_up0, w_down0, norm_mix1, w_in1, w_out1, norm_ffn1, w_router, w_gate1, w_up1, w_down1, norm_final):
    n_p, seq, d = x_prompt.shape
    n_s, s_len, _ = x_sample.shape
    tp, ts = n_p * seq, n_s * s_len
    d_conv = conv_w.shape[2]
    kvh, grp = swa_sink.shape[1], swa_sink.shape[2]
    hq0 = kvh * grp
    h1 = cache_dil_kv0.shape[4]
    c_q0, c_kv0 = hq0 * HEAD_DIM, kvh * HEAD_DIM
    q0_col = 3 * d_conv
    k0_col = q0_col + c_q0
    c_g = 3 * h1 * HEAD_DIM
    dil_caches = (cache_dil_kv0, cache_dil_kv1, cache_dil_kv2)

    x_rows = (x_prompt.reshape(tp, d), x_sample.reshape(ts, d))

    assert c_kv0 == LANES and grp % 2 == 0 and d_conv % LANES == 0
    nsc = d_conv // LANES
    q0_slab = 3 * nsc
    k0_slab = q0_slab + c_q0 // LANES
    proj0 = _dense([x_rows], w_in0[0], gain=norm_mix0[0], out_dtype=F32, name="in_proj0", slab_out=True)
    a_p, conv_tail = _conv_prompt(proj0, conv_w[0], n_seq=n_p, seq_len=seq, c=d_conv)
    att_p = _band_attn(proj0, n_seq=n_p, seq_len=seq, dil=1, q_slab=q0_slab, k_slab=k0_slab, v_slab=k0_slab + 1,
                       n_q_slabs=c_q0 // LANES, shared_kv=True, sink=swa_sink[0], want_lse=False,
                       name="swa_prompt")[0]
    proj0_s = proj0[:, tp:]
    p3 = proj0_s[:q0_slab].reshape(3, nsc, n_s, s_len, LANES)
    p3 = jnp.transpose(p3, (0, 3, 2, 1, 4)).reshape(3, s_len, n_s, d_conv)
    a_s, conv_new = _conv_step(p3, jnp.transpose(cache_conv[0], (1, 0, 2)), conv_w[0])
    a_s = jnp.transpose(a_s, (1, 0, 2)).reshape(ts, d_conv)
    kvt0 = _proj_t(jnp.transpose(w_in0[0][:, k0_col:]), x_rows[1], norm_mix0[0], "kv_t0")
    q_s = proj0_s[q0_slab:k0_slab].reshape(kvh, grp // 2, n_s, s_len, 2, HEAD_DIM)
    q_s = jnp.transpose(q_s, (2, 3, 1, 4, 0, 5)).reshape(n_s, s_len * grp, c_kv0)
    sink_col = jnp.broadcast_to(swa_sink[0][:, None, :], (kvh, s_len, grp)).reshape(kvh * s_len * grp, 1)
    o_s, swa_new = _step_attn(q_s, kvt0.reshape(2, c_kv0, ts), _cache_to_slabs(cache_swa_kv[0]), dil=1,
                              tok_div=grp, sink_col=sink_col, want_lse=False, bsz=8, name="swa_step")
    att_s = jnp.transpose(o_s.reshape(n_s, s_len, grp, kvh, HEAD_DIM), (0, 1, 3, 2, 4)).reshape(ts, c_q0)
    h = _dense([(a_p, a_s.astype(BF16)), (att_p, att_s.astype(BF16))], w_out0[0], res=x_rows, out_dtype=F32,
               name="out_proj0")
    h = _ffn(h, norm_ffn0[0], w_gate0[0], w_up0[0], w_down0[0], "ffn0")

    proj1 = _dense([h], w_in1[0], gain=norm_mix1[0], out_dtype=F32, name="in_proj1", slab_out=True)
    proj1_s = proj1[:, tp:]
    c_h = h1 * HEAD_DIM
    nsh = c_h // LANES
    pairs_p, pairs_s, dil_p, dil_s = [], [], [], []
    for g, dil in enumerate(DILATIONS):
        window = dil_caches[g].shape[2]
        s0 = g * 3 * nsh
        pairs_p.append(tuple(_band_attn(proj1, n_seq=n_p, seq_len=seq, dil=dil, q_slab=s0, k_slab=s0 + nsh,
                                        v_slab=s0 + 2 * nsh, n_q_slabs=nsh, want_lse=True,
                                        name=f"dil{g}_prompt")))
        keep = min(window, seq)
        state = _state_slabs(proj1, first_slab=s0 + nsh, n_slabs=2 * nsh, n_seq=n_p, seq_len=seq, keep=keep,
                             name=f"dil{g}_state")
        dil_p.append(_slabs_to_cache(state.reshape(n_p, 2, c_h, keep), h1)[None])
        lo = g * c_g
        q_g = jnp.transpose(proj1_s[s0:s0 + nsh].reshape(nsh, n_s, s_len, LANES), (1, 2, 0, 3))
        q_g = q_g.reshape(n_s, s_len, c_h)
        q_g = jnp.concatenate([q_g, jnp.zeros((n_s, SUBLANES - s_len, c_h), F32)], axis=1)
        kvt = _proj_t(jnp.transpose(w_in1[0][:, lo + c_h:lo + c_g]), h[tp:], norm_mix1[0], f"kv_t1_{g}")
        o_sg, l_sg, cache_new = _step_attn(q_g, kvt.reshape(2, c_h, ts), _cache_to_slabs(dil_caches[g][0]),
                                           dil=dil, tok_div=1, want_lse=True,
                                           bsz=max(1, 2048 // window), name=f"dil{g}_step")
        pairs_s.append((o_sg.reshape(n_s * SUBLANES, c_h), l_sg.reshape(n_s * SUBLANES, c_h)))
        dil_s.append(_slabs_to_cache(cache_new, h1)[None])
    comb_p = _combine(pairs_p, "combine_prompt")
    comb_s = _combine(pairs_s, "combine_step").reshape(n_s, SUBLANES, c_h)[:, :s_len].reshape(ts, c_h)
    h = _dense([(comb_p, comb_s)], w_out1[0], res=h, out_dtype=F32, name="out_proj1")
    y_p, y_s = _moe(h, norm_ffn1[0], w_router[0], w_gate1[0], w_up1[0], w_down1[0], norm_final, tp)

    y_prompt = y_p.reshape(n_p, seq, d)
    y_sample = y_s.reshape(n_s, s_len, d)
    new_conv_prompt = conv_tail[:, SUBLANES - (CONV_W - 1):][None]
    new_conv_sample = jnp.transpose(conv_new, (1, 0, 2))[None]
    keep0 = min(SPAN, seq)
    swa_state = _state_slabs(proj0, first_slab=k0_slab, n_slabs=2, n_seq=n_p, seq_len=seq, keep=keep0,
                             name="swa_state")
    new_swa_kv_prompt = _slabs_to_cache(swa_state, kvh)[None]
    new_swa_kv_sample = _slabs_to_cache(swa_new, kvh)[None]
    return (y_prompt, y_sample, new_conv_prompt, new_conv_sample, new_swa_kv_prompt, new_swa_kv_sample,
            dil_p[0], dil_s[0], dil_p[1], dil_s[1], dil_p[2], dil_s[2])
```

```python
import functools

import jax
import jax.numpy as jnp
from jax import lax
from jax.experimental import pallas as pl
from jax.experimental.pallas import tpu as pltpu

F32 = jnp.float32
BF16 = jnp.bfloat16
I32 = jnp.int32

EPS = 1e-5
HEAD_DIM = 64
SPAN = 128
CONV_W = 3
DILATIONS = (1, 4, 16)
TOP_K = 2
LANES = 128
SUBLANES = 8
VMEM_LIMIT_BYTES = 56 * 1024 * 1024
NT_DIMS = (((1,), (1,)), ((), ()))


def _params(*sem):
    return pltpu.CompilerParams(dimension_semantics=sem, vmem_limit_bytes=VMEM_LIMIT_BYTES)


def _pick(n, candidates):
    for c in candidates:
        if n % c == 0:
            return c
    raise ValueError(f"no tile for {n} in {candidates}")


def _rms(x, g):
    y = x * lax.rsqrt(jnp.mean(x * x, axis=-1, keepdims=True) + EPS)
    return y * g


STAGE_SLOTS = 3
STAGE_BYTES = 2 * 1024 * 1024


def _stage_rows(k, n):
    rows = k
    while rows * n * 4 > STAGE_BYTES and rows % 32 == 0:
        rows //= 2
    return rows


def _load_bf16(w_hbm, w_vmem, stage, sem):
    n_slots, chunk = stage.shape[0], stage.shape[1]
    n_chunks = w_hbm.shape[0] // chunk

    def copy(c):
        return pltpu.make_async_copy(w_hbm.at[pl.ds(c * chunk, chunk), :], stage.at[c % n_slots],
                                     sem.at[c % n_slots])

    for c in range(min(n_slots - 1, n_chunks)):
        copy(c).start(c % 2)
    for c in range(n_chunks):
        if c + n_slots - 1 < n_chunks:
            copy(c + n_slots - 1).start((c + n_slots - 1) % 2)
        copy(c).wait()
        w_vmem[c * chunk:(c + 1) * chunk, :] = stage[c % n_slots].astype(BF16)


def _dense_body(*refs, ks, split, head_tiles, has_gain, has_res, slab_out, cn):
    n_in = len(ks)
    step = pl.program_id(0)
    refs = list(refs)

    def take(is_split):
        count = 2 if is_split else 1
        parts = tuple(refs[:count])
        del refs[:count]
        return parts

    def read(parts, cols=slice(None)):
        if len(parts) == 1:
            return parts[0][:, cols]
        return jnp.where(step < head_tiles, parts[0][:, cols], parts[1][:, cols])

    xs = [take(s) for s in split[:n_in]]
    g_ref = refs.pop(0) if has_gain else None
    w_hbm = refs.pop(0)
    res = take(split[n_in]) if has_res else None
    o_ref, w_ref, stage, sem = refs

    @pl.when(step == 0)
    def _():
        _load_bf16(w_hbm, w_ref, stage, sem)

    if has_gain:
        lhs = [_rms(read(xs[0]), g_ref[...]).astype(BF16)]
    else:
        lhs = [read(x).astype(BF16) for x in xs]
    n = w_ref.shape[1]
    for c in range(n // cn):
        cols = slice(c * cn, (c + 1) * cn)
        acc = None
        row0 = 0
        for a, k in zip(lhs, ks):
            d = jnp.dot(a, w_ref[row0:row0 + k, cols], preferred_element_type=F32)
            acc = d if acc is None else acc + d
            row0 += k
        if has_res:
            acc = read(res, cols) + acc
        if slab_out:
            for s in range(cn // LANES):
                o_ref[c * (cn // LANES) + s] = acc[:, s * LANES:(s + 1) * LANES].astype(o_ref.dtype)
        else:
            o_ref[:, cols] = acc.astype(o_ref.dtype)


def _dense(xs, w, *, gain=None, res=None, out_dtype, name, slab_out=False):
    def n_rows(a):
        return sum(p.shape[0] for p in a) if isinstance(a, tuple) else a.shape[0]

    def width(a):
        return a[0].shape[1] if isinstance(a, tuple) else a.shape[1]

    operands = list(xs) + ([res] if res is not None else [])
    pairs = [a for a in operands if isinstance(a, tuple)]
    t = n_rows(xs[0])
    k_all, n = w.shape
    ks = tuple(width(x) for x in xs)
    assert sum(ks) == k_all
    gcd_rows = t
    for a in pairs:
        gcd_rows = min(gcd_rows, a[1].shape[0])
    tm = _pick(gcd_rows, (512, 256, 128, 64, 32, 16, 8))
    assert t % tm == 0
    head_tiles = pairs[0][0].shape[0] // tm if pairs else 0
    for a in pairs:
        assert a[0].shape[0] == head_tiles * tm
    cn = _pick(n, (512, 384, 256, 128))
    in_specs, args = [], []

    def add_rows(a, k):
        if isinstance(a, tuple):
            in_specs.append(pl.BlockSpec((tm, k), lambda i: (jnp.minimum(i, head_tiles - 1), 0)))
            in_specs.append(pl.BlockSpec((tm, k), lambda i: (jnp.maximum(i - head_tiles, 0), 0)))
            args.extend(a)
        else:
            in_specs.append(pl.BlockSpec((tm, k), lambda i: (i, 0)))
            args.append(a)

    for x, k in zip(xs, ks):
        add_rows(x, k)
    if gain is not None:
        in_specs.append(pl.BlockSpec((1, ks[0]), lambda i: (0, 0)))
        args.append(gain.reshape(1, -1))
    in_specs.append(pl.BlockSpec(memory_space=pl.ANY))
    args.append(w)
    if res is not None:
        add_rows(res, n)
    split = tuple(isinstance(a, tuple) for a in operands)
    if slab_out:
        out_shape = jax.ShapeDtypeStruct((n // LANES, t, LANES), out_dtype)
        out_spec = pl.BlockSpec((n // LANES, tm, LANES), lambda i: (0, i, 0))
    else:
        out_shape = jax.ShapeDtypeStruct((t, n), out_dtype)
        out_spec = pl.BlockSpec((tm, n), lambda i: (i, 0))
    return pl.pallas_call(
        functools.partial(_dense_body, ks=ks, split=split, head_tiles=head_tiles, has_gain=gain is not None,
                          has_res=res is not None, slab_out=slab_out, cn=cn),
        out_shape=out_shape,
        grid=(t // tm,),
        in_specs=in_specs,
        out_specs=out_spec,
        scratch_shapes=[pltpu.VMEM((k_all, n), BF16), pltpu.VMEM((STAGE_SLOTS, _stage_rows(k_all, n), n), F32),
                        pltpu.SemaphoreType.DMA((STAGE_SLOTS,))],
        compiler_params=_params("arbitrary"),
        name=name,
    )(*args)


def _proj_t_body(w_ref, x_ref, g_ref, o_ref):
    xn = _rms(x_ref[...], g_ref[...]).astype(BF16)
    o_ref[...] = lax.dot_general(w_ref[...].astype(BF16), xn, NT_DIMS, preferred_element_type=F32)


def _proj_t(w_t, x, gain, name):
    c, k = w_t.shape
    rows = x.shape[0]
    tc = _pick(c, (512, 256, 128))
    return pl.pallas_call(
        _proj_t_body,
        out_shape=jax.ShapeDtypeStruct((c, rows), F32),
        grid=(c // tc,),
        in_specs=[pl.BlockSpec((tc, k), lambda i: (i, 0)),
                  pl.BlockSpec((rows, k), lambda i: (0, 0)),
                  pl.BlockSpec((1, k), lambda i: (0, 0))],
        out_specs=pl.BlockSpec((tc, rows), lambda i: (i, 0)),
        compiler_params=_params("parallel"),
        name=name,
    )(w_t, x, gain.reshape(1, -1))


def _ffn_body(x_ref, g_ref, wg_hbm, wu_hbm, wd_hbm, o_ref, wg_ref, wu_ref, wd_ref, stage_in, stage_out, sem, *,
              tf):
    @pl.when(pl.program_id(0) == 0)
    def _():
        _load_bf16(wg_hbm, wg_ref, stage_in, sem)
        _load_bf16(wu_hbm, wu_ref, stage_in, sem)
        _load_bf16(wd_hbm, wd_ref, stage_out, sem)

    x = x_ref[...]
    xn = _rms(x, g_ref[...]).astype(BF16)
    acc = None
    for c in range(wg_ref.shape[1] // tf):
        cols = slice(c * tf, (c + 1) * tf)
        gate = jnp.dot(xn, wg_ref[:, cols], preferred_element_type=F32)
        up = jnp.dot(xn, wu_ref[:, cols], preferred_element_type=F32)
        act = (gate * jax.nn.sigmoid(gate) * up).astype(BF16)
        d = jnp.dot(act, wd_ref[cols, :], preferred_element_type=F32)
        acc = d if acc is None else acc + d
    o_ref[...] = x + acc


def _ffn(x, gain, w_gate, w_up, w_down, name):
    t, d = x.shape
    f = w_gate.shape[1]
    tm = _pick(t, (512, 256, 128, 64, 32, 16, 8))
    tf = f
    anywhere = pl.BlockSpec(memory_space=pl.ANY)
    return pl.pallas_call(
        functools.partial(_ffn_body, tf=tf),
        out_shape=jax.ShapeDtypeStruct((t, d), F32),
        grid=(t // tm,),
        in_specs=[pl.BlockSpec((tm, d), lambda i: (i, 0)),
                  pl.BlockSpec((1, d), lambda i: (0, 0)),
                  anywhere, anywhere, anywhere],
        out_specs=pl.BlockSpec((tm, d), lambda i: (i, 0)),
        scratch_shapes=[pltpu.VMEM((d, f), BF16), pltpu.VMEM((d, f), BF16), pltpu.VMEM((f, d), BF16),
                        pltpu.VMEM((STAGE_SLOTS, _stage_rows(d, f), f), F32),
                        pltpu.VMEM((STAGE_SLOTS, _stage_rows(f, d), d), F32),
                        pltpu.SemaphoreType.DMA((STAGE_SLOTS,))],
        compiler_params=_params("arbitrary"),
        name=name,
    )(x, gain.reshape(1, -1), w_gate, w_up, w_down)


def _band_body(*refs, dil, mb, shared_kv, has_sink, want_lse):
    q_ref, kc_ref, kp_ref, vc_ref, vp_ref = refs[:5]
    pos = 5
    sink_ref = refs[pos] if has_sink else None
    pos += int(has_sink)
    o_ref = refs[pos]
    lse_ref = refs[pos + 1] if want_lse else None
    t = pl.program_id(1)
    scale = HEAD_DIM ** -0.5
    qi = lax.broadcasted_iota(I32, (2 * SPAN, 2 * SPAN), 0) & (SPAN - 1)
    kj = lax.broadcasted_iota(I32, (2 * SPAN, 2 * SPAN), 1)
    band = (kj >= qi) & (kj <= qi + SPAN)
    band_first = band & (kj >= jnp.where(t == 0, SPAN, 0))
    lane = lax.broadcasted_iota(I32, (1, LANES), 1)
    halves = [lane < HEAD_DIM, lane >= HEAD_DIM]

    def rows_of(ref, s, r, count):
        if dil == 1:
            return ref[s, 0:count, :]
        return ref[s, pl.ds(r, count, stride=dil), :]

    def pair(q_a, mask_a, q_b, mask_b, kw, vw, msk, heads):
        q2 = jnp.concatenate([jnp.where(mask_a, q_a, 0.0), jnp.where(mask_b, q_b, 0.0)], axis=0).astype(BF16)
        s = lax.dot_general(q2, kw, NT_DIMS, preferred_element_type=F32)
        s = jnp.where(msk, s, -jnp.inf)
        m = jnp.max(s, axis=-1, keepdims=True)
        if has_sink:
            sink = jnp.concatenate([jnp.full((SPAN, 1), sink_ref[h], F32) for h in heads], axis=0)
            m = jnp.maximum(m, sink)
        p = jnp.exp(s - m)
        den = jnp.sum(p, axis=-1, keepdims=True)
        if has_sink:
            den = den + jnp.exp(sink - m)
        o = jnp.dot(p.astype(BF16), vw, preferred_element_type=F32) / den
        return o, m + jnp.log(den)

    def store(s, r, b, o_tile, lse_tile):
        if dil == 1:
            o_ref[b * SPAN:(b + 1) * SPAN, s * LANES:(s + 1) * LANES] = o_tile.astype(o_ref.dtype)
            if want_lse:
                lse_ref[b * SPAN:(b + 1) * SPAN, s * LANES:(s + 1) * LANES] = lse_tile
        else:
            o_ref[s, pl.ds(r + dil * b * SPAN, SPAN, stride=dil), :] = o_tile
            if want_lse:
                lse_ref[s, pl.ds(r + dil * b * SPAN, SPAN, stride=dil), :] = lse_tile

    n_slabs = q_ref.shape[0]
    first_head = 2 * n_slabs * pl.program_id(2)
    top, bot = slice(0, SPAN), slice(SPAN, 2 * SPAN)
    for r in range(dil):
        qs = [rows_of(q_ref, s, r, mb * SPAN) * scale for s in range(n_slabs)]
        kf = [jnp.concatenate([rows_of(kp_ref, s, r, SPAN), rows_of(kc_ref, s, r, mb * SPAN)], axis=0)
              for s in range(kc_ref.shape[0])]
        vf = [jnp.concatenate([rows_of(vp_ref, s, r, SPAN), rows_of(vc_ref, s, r, mb * SPAN)], axis=0)
              for s in range(vc_ref.shape[0])]
        ks = [k.astype(BF16) for k in kf]
        vs = [v.astype(BF16) for v in vf]
        if shared_kv:
            k_sw = pltpu.roll(kf[0], HEAD_DIM, 1).astype(BF16)
            v_sw = pltpu.roll(vf[0], HEAD_DIM, 1).astype(BF16)
        for b in range(mb):
            blk = slice(b * SPAN, (b + 1) * SPAN)
            keys = slice(b * SPAN, (b + 2) * SPAN)
            msk = band_first if b == 0 else band
            if not shared_kv:
                for s in range(n_slabs):
                    o, lse = pair(qs[s][blk], halves[0], qs[s][blk], halves[1], ks[s][keys], vs[s][keys], msk,
                                  (first_head + 2 * s, first_head + 2 * s + 1))
                    store(s, r, b, jnp.where(halves[0], o[top], o[bot]), jnp.where(halves[0], lse[top], lse[bot]))
            else:
                for kvh in range(2):
                    sa, sb = 2 * kvh, 2 * kvh + 1
                    o_al, l_al = pair(qs[sa][blk], halves[kvh], qs[sb][blk], halves[kvh], ks[0][keys],
                                      vs[0][keys], msk, (2 * sa + kvh, 2 * sb + kvh))
                    o_sw, l_sw = pair(qs[sa][blk], halves[1 - kvh], qs[sb][blk], halves[1 - kvh], k_sw[keys],
                                      v_sw[keys], msk, (2 * sa + 1 - kvh, 2 * sb + 1 - kvh))
                    store(sa, r, b, jnp.where(halves[kvh], o_al[top], o_sw[top]),
                          jnp.where(halves[kvh], l_al[top], l_sw[top]))
                    store(sb, r, b, jnp.where(halves[kvh], o_al[bot], o_sw[bot]),
                          jnp.where(halves[kvh], l_al[bot], l_sw[bot]))


def _band_attn(slabs, *, n_seq, seq_len, dil, q_slab, k_slab, v_slab, n_q_slabs, shared_kv=False, sink=None,
               want_lse, name):
    mb = max(1, 512 // (dil * SPAN))
    tp = dil * SPAN * mb
    nt = seq_len // tp
    assert seq_len % tp == 0 and (not shared_kv or (n_q_slabs == 4 and dil == 1))
    prev_rows = dil * SPAN
    per_step = n_q_slabs if (shared_kv or tp <= 512) else 1
    kv_block = 1 if shared_kv else per_step
    kv_step = 0 if shared_kv else 1

    def cur(base, step):
        return lambda n, t, s: (base + s * step, n * nt + t, 0)

    def prev(base, step):
        return lambda n, t, s: (base + s * step, jnp.maximum((n * nt + t) * mb - 1, 0), 0)

    in_specs = [pl.BlockSpec((per_step, tp, LANES), cur(q_slab // per_step, 1)),
                pl.BlockSpec((kv_block, tp, LANES), cur(k_slab // kv_block, kv_step)),
                pl.BlockSpec((kv_block, prev_rows, LANES), prev(k_slab // kv_block, kv_step)),
                pl.BlockSpec((kv_block, tp, LANES), cur(v_slab // kv_block, kv_step)),
                pl.BlockSpec((kv_block, prev_rows, LANES), prev(v_slab // kv_block, kv_step))]
    assert q_slab % per_step == 0 and k_slab % kv_block == 0 and v_slab % kv_block == 0
    args = [slabs] * 5
    if sink is not None:
        in_specs.append(pl.BlockSpec(memory_space=pltpu.SMEM))
        args.append(sink.reshape(-1).astype(F32))
    rows = n_seq * seq_len
    if dil == 1:
        spec = pl.BlockSpec((tp, per_step * LANES), lambda n, t, s: (n * nt + t, s))
        out_shape = [jax.ShapeDtypeStruct((rows, n_q_slabs * LANES), BF16)]
        lse_shape = jax.ShapeDtypeStruct((rows, n_q_slabs * LANES), F32)
    else:
        spec = pl.BlockSpec((per_step, tp, LANES), lambda n, t, s: (s, n * nt + t, 0))
        out_shape = [jax.ShapeDtypeStruct((n_q_slabs, rows, LANES), F32)]
        lse_shape = jax.ShapeDtypeStruct((n_q_slabs, rows, LANES), F32)
    out_specs = [spec]
    if want_lse:
        out_shape.append(lse_shape)
        out_specs.append(spec)
    return pl.pallas_call(
        functools.partial(_band_body, dil=dil, mb=mb, shared_kv=shared_kv, has_sink=sink is not None,
                          want_lse=want_lse),
        out_shape=out_shape,
        grid=(n_seq, nt, n_q_slabs // per_step),
        in_specs=in_specs,
        out_specs=out_specs,
        compiler_params=_params("parallel", "parallel", "parallel"),
        name=name,
    )(*args)


def _combine_body(o0, l0, o1, l1, o2, l2, out_ref):
    def tile(ref):
        return ref[...].reshape(ref.shape[-2:]).astype(F32)

    a0, a1, a2 = tile(l0), tile(l1), tile(l2)
    m = jnp.maximum(jnp.maximum(a0, a1), a2)
    e0, e1, e2 = jnp.exp(a0 - m), jnp.exp(a1 - m), jnp.exp(a2 - m)
    num = e0 * tile(o0) + e1 * tile(o1) + e2 * tile(o2)
    out_ref[...] = (num / (e0 + e1 + e2)).astype(out_ref.dtype)


def _combine(pairs, name):
    first = pairs[0][0]
    rows, c = first.shape if first.ndim == 2 else (first.shape[1], first.shape[0] * LANES)
    tm = _pick(rows, (1024, 512, 256, 128, 64, 32, 16, 8))
    flat = pl.BlockSpec((tm, LANES), lambda i, s: (i, s))
    slab = pl.BlockSpec((1, tm, LANES), lambda i, s: (s, i, 0))
    args = [a for pair in pairs for a in pair]
    return pl.pallas_call(
        _combine_body,
        out_shape=jax.ShapeDtypeStruct((rows, c), BF16),
        grid=(rows // tm, c // LANES),
        in_specs=[flat if a.ndim == 2 else slab for a in args],
        out_specs=flat,
        compiler_params=_params("parallel", "parallel"),
        name=name,
    )(*args)


def _combine_proj_body(o0, l0, o1, l1, o2, l2, tail_ref, res_ref, w_hbm, out_ref, w_ref, stage, sem, *, head_tiles):
    step = pl.program_id(0)

    @pl.when(step == 0)
    def _():
        _load_bf16(w_hbm, w_ref, stage, sem)

    parts = []
    for s in range(o1.shape[0]):
        lanes = slice(s * LANES, (s + 1) * LANES)
        a0, a1, a2 = l0[:, lanes], l1[s], l2[s]
        m = jnp.maximum(jnp.maximum(a0, a1), a2)
        e0, e1, e2 = jnp.exp(a0 - m), jnp.exp(a1 - m), jnp.exp(a2 - m)
        num = e0 * o0[:, lanes].astype(F32) + e1 * o1[s] + e2 * o2[s]
        parts.append(num / (e0 + e1 + e2))
    comb = jnp.concatenate(parts, axis=1).astype(BF16)
    comb = jnp.where(step < head_tiles, comb, tail_ref[...])
    out_ref[...] = res_ref[...] + jnp.dot(comb, w_ref[...], preferred_element_type=F32)


def _combine_proj(pairs, tail, w, res, name):
    (o0, l0), (o1, l1), (o2, l2) = pairs
    head, c = o0.shape
    t, n = res.shape
    tm = _pick(tail.shape[0], (512, 256, 128, 64, 32, 16, 8))
    assert head % tm == 0 and t == head + tail.shape[0] and o1.ndim == 3 and o2.ndim == 3
    head_tiles = head // tm

    def first(i):
        return jnp.minimum(i, head_tiles - 1)

    flat = pl.BlockSpec((tm, c), lambda i: (first(i), 0))
    slab = pl.BlockSpec((c // LANES, tm, LANES), lambda i: (0, first(i), 0))
    return pl.pallas_call(
        functools.partial(_combine_proj_body, head_tiles=head_tiles),
        out_shape=jax.ShapeDtypeStruct((t, n), F32),
        grid=(t // tm,),
        in_specs=[flat, flat, slab, slab, slab, slab,
                  pl.BlockSpec((tm, c), lambda i: (jnp.maximum(i - head_tiles, 0), 0)),
                  pl.BlockSpec((tm, n), lambda i: (i, 0)),
                  pl.BlockSpec(memory_space=pl.ANY)],
        out_specs=pl.BlockSpec((tm, n), lambda i: (i, 0)),
        scratch_shapes=[pltpu.VMEM((c, n), BF16), pltpu.VMEM((STAGE_SLOTS, _stage_rows(c, n), n), F32),
                        pltpu.SemaphoreType.DMA((STAGE_SLOTS,))],
        compiler_params=_params("arbitrary"),
        name=name,
    )(o0, l0, o1, l1, o2, l2, tail, res, w)


def _state_body(x_ref, o_ref):
    o_ref[0, 0] = x_ref[0].T


def _state_slabs(slabs, *, first_slab, n_slabs, n_seq, seq_len, keep, name):
    assert seq_len % keep == 0
    per_seq = seq_len // keep
    return pl.pallas_call(
        _state_body,
        out_shape=jax.ShapeDtypeStruct((n_seq, n_slabs, LANES, keep), F32),
        grid=(n_seq, n_slabs),
        in_specs=[pl.BlockSpec((1, keep, LANES), lambda n, s: (first_slab + s, (n + 1) * per_seq - 1, 0))],
        out_specs=pl.BlockSpec((1, 1, LANES, keep), lambda n, s: (n, s, 0, 0)),
        compiler_params=_params("parallel", "parallel"),
        name=name,
    )(slabs)


def _conv_prompt_body(gb_ref, gc_ref, xa_ref, gcp_ref, xap_ref, w_ref, a_ref, st_ref):
    t = pl.program_id(1)
    w = w_ref[...]
    for s in range(gb_ref.shape[0]):
        lanes = slice(s * LANES, (s + 1) * LANES)
        u = gc_ref[s] * xa_ref[s]
        up = jnp.where(t == 0, 0.0, gcp_ref[s] * xap_ref[s])
        ext = jnp.concatenate([up, u], axis=0)
        y = (w[0:1, lanes] * pltpu.roll(ext, 2, 0)[SUBLANES:]
             + w[1:2, lanes] * pltpu.roll(ext, 1, 0)[SUBLANES:]) + w[2:3, lanes] * u
        a_ref[:, lanes] = (gb_ref[s] * y).astype(a_ref.dtype)
        st_ref[0, :, lanes] = u[u.shape[0] - SUBLANES:]


def _conv_prompt(slabs, conv_w, *, n_seq, seq_len, c):
    tq = _pick(seq_len, (512, 256, 128))
    nt = seq_len // tq
    rb = tq // SUBLANES
    ns = c // LANES

    def cur(part):
        return lambda n, t: (part, n * nt + t, 0)

    def prev(part):
        return lambda n, t: (part, jnp.maximum((n * nt + t) * rb - 1, 0), 0)

    return pl.pallas_call(
        _conv_prompt_body,
        out_shape=[jax.ShapeDtypeStruct((n_seq * seq_len, c), BF16),
                   jax.ShapeDtypeStruct((n_seq, SUBLANES, c), F32)],
        grid=(n_seq, nt),
        in_specs=[pl.BlockSpec((ns, tq, LANES), cur(0)), pl.BlockSpec((ns, tq, LANES), cur(1)),
                  pl.BlockSpec((ns, tq, LANES), cur(2)),
                  pl.BlockSpec((ns, SUBLANES, LANES), prev(1)), pl.BlockSpec((ns, SUBLANES, LANES), prev(2)),
                  pl.BlockSpec((CONV_W, c), lambda n, t: (0, 0))],
        out_specs=[pl.BlockSpec((tq, c), lambda n, t: (n * nt + t, 0)),
                   pl.BlockSpec((1, SUBLANES, c), lambda n, t: (n, 0, 0))],
        compiler_params=_params("parallel", "arbitrary"),
        name="conv_prompt",
    )(slabs, slabs, slabs, slabs, slabs, conv_w)


def _conv_step_body(p_ref, prev_ref, w_ref, a_ref, st_ref):
    s_len = p_ref.shape[1]
    w = w_ref[...]
    hist = [prev_ref[k] for k in range(CONV_W - 1)] + [p_ref[1, s] * p_ref[2, s] for s in range(s_len)]
    for s in range(s_len):
        y = (w[0:1] * hist[s] + w[1:2] * hist[s + 1]) + w[2:3] * hist[s + 2]
        a_ref[s] = p_ref[0, s] * y
    for k in range(CONV_W - 1):
        st_ref[k] = hist[s_len + k]


def _conv_step(p3, prev, conv_w):
    _, s_len, n, c = p3.shape
    return pl.pallas_call(
        _conv_step_body,
        out_shape=[jax.ShapeDtypeStruct((s_len, n, c), F32), jax.ShapeDtypeStruct((CONV_W - 1, n, c), F32)],
        name="conv_step",
        compiler_params=pltpu.CompilerParams(vmem_limit_bytes=VMEM_LIMIT_BYTES),
    )(p3, prev, conv_w)


def _step_body(*refs, n_blk, qr, tok_div, w, dil, cw, bsz, has_sink, want_lse):
    q_ref, kvt_ref, cache_ref = refs[:3]
    pos = 3
    sink_ref = refs[pos] if has_sink else None
    pos += int(has_sink)
    o_ref = refs[pos]
    pos += 1
    lse_ref = refs[pos] if want_lse else None
    pos += int(want_lse)
    cout_ref = refs[pos]
    r_dim = n_blk * HEAD_DIM
    nrb = n_blk * qr
    n_chunks = w // cw
    scale = HEAD_DIM ** -0.5
    step = pl.program_id(0)
    lane_r = lax.broadcasted_iota(I32, (1, r_dim), 1)
    blk_masks = [(lane_r >= j * HEAD_DIM) & (lane_r < (j + 1) * HEAD_DIM) for j in range(n_blk)]
    row = lax.broadcasted_iota(I32, (nrb, 1), 0)
    tok = (row % qr) // tok_div
    lane_c = lax.broadcasted_iota(I32, (1, cw), 1)
    lane_n = lax.broadcasted_iota(I32, (1, LANES), 1)

    def valid(pos_l):
        ok = (pos_l >= tok) & (pos_l <= w + tok)
        if dil > 1:
            ok = ok & (((pos_l - tok) & (dil - 1)) == 0)
        return ok

    for b in range(bsz):
        off = ((step * bsz + b) * 4) % LANES
        shift = (LANES - off) % LANES
        new_k = pltpu.roll(kvt_ref[0], shift, 1)
        new_v = pltpu.roll(kvt_ref[1], shift, 1)
        q = q_ref[b] * scale
        qbd = jnp.concatenate([jnp.where(mk, q, 0.0) for mk in blk_masks], axis=0).astype(BF16)
        scores = []
        for c in range(n_chunks):
            lo, hi = c * cw, (c + 1) * cw
            kc = cache_ref[b, 0, :, lo:hi]
            sc = jnp.dot(qbd, kc.astype(BF16), preferred_element_type=F32)
            scores.append(jnp.where(valid(lane_c + lo), sc, -jnp.inf))
            nxt = cache_ref[b, 0, :, hi:hi + LANES] if c + 1 < n_chunks else new_k
            ext = jnp.concatenate([kc, nxt], axis=1)
            cout_ref[b, 0, :, lo:hi] = pltpu.roll(ext, cw + LANES - 4, 1)[:, :cw]
        sc = jnp.dot(qbd, new_k.astype(BF16), preferred_element_type=F32)
        scores.append(jnp.where(valid(lane_n + w), sc, -jnp.inf))
        m = functools.reduce(jnp.maximum, [jnp.max(s, axis=-1, keepdims=True) for s in scores])
        if has_sink:
            m = jnp.maximum(m, sink_ref[...])
        probs = [jnp.exp(s - m) for s in scores]
        den = functools.reduce(lambda a, c: a + c, [jnp.sum(p, axis=-1, keepdims=True) for p in probs])
        if has_sink:
            den = den + jnp.exp(sink_ref[...] - m)
        pv = lax.dot_general(probs[-1].astype(BF16), new_v.astype(BF16), NT_DIMS, preferred_element_type=F32)
        for c in range(n_chunks):
            lo, hi = c * cw, (c + 1) * cw
            vc = cache_ref[b, 1, :, lo:hi]
            pv = pv + lax.dot_general(probs[c].astype(BF16), vc.astype(BF16), NT_DIMS,
                                      preferred_element_type=F32)
            nxt = cache_ref[b, 1, :, hi:hi + LANES] if c + 1 < n_chunks else new_v
            ext = jnp.concatenate([vc, nxt], axis=1)
            cout_ref[b, 1, :, lo:hi] = pltpu.roll(ext, cw + LANES - 4, 1)[:, :cw]
        o = jnp.zeros((qr, r_dim), F32)
        m_e = jnp.zeros((qr, r_dim), F32)
        den_e = jnp.zeros((qr, r_dim), F32)
        for j, mk in enumerate(blk_masks):
            rows = slice(j * qr, (j + 1) * qr)
            o = jnp.where(mk, pv[rows], o)
            m_e = jnp.where(mk, m[rows], m_e)
            den_e = jnp.where(mk, den[rows], den_e)
        o_ref[b] = o / den_e
        if want_lse:
            lse_ref[b] = m_e + jnp.log(den_e)


def _step_attn(q, kvt, cache, *, dil, tok_div, sink_col=None, want_lse, bsz, name):
    n, qr, r_dim = q.shape
    w = cache.shape[-1]
    n_blk = r_dim // HEAD_DIM
    cw = min(w, 512)
    in_specs = [pl.BlockSpec((bsz, qr, r_dim), lambda i: (i, 0, 0)),
                pl.BlockSpec((2, r_dim, LANES), lambda i: (0, 0, (i * bsz * 4) // LANES)),
                pl.BlockSpec((bsz, 2, r_dim, w), lambda i: (i, 0, 0, 0))]
    args = [q, kvt, cache]
    if sink_col is not None:
        in_specs.append(pl.BlockSpec((n_blk * qr, 1), lambda i: (0, 0)))
        args.append(sink_col)
    o_spec = pl.BlockSpec((bsz, qr, r_dim), lambda i: (i, 0, 0))
    out_shape = [jax.ShapeDtypeStruct((n, qr, r_dim), F32)]
    out_specs = [o_spec]
    if want_lse:
        out_shape.append(jax.ShapeDtypeStruct((n, qr, r_dim), F32))
        out_specs.append(o_spec)
    out_shape.append(jax.ShapeDtypeStruct(cache.shape, F32))
    out_specs.append(pl.BlockSpec((bsz, 2, r_dim, w), lambda i: (i, 0, 0, 0)))
    return pl.pallas_call(
        functools.partial(_step_body, n_blk=n_blk, qr=qr, tok_div=tok_div, w=w, dil=dil, cw=cw, bsz=bsz,
                          has_sink=sink_col is not None, want_lse=want_lse),
        out_shape=out_shape,
        grid=(n // bsz,),
        in_specs=in_specs,
        out_specs=out_specs,
        compiler_params=_params("parallel"),
        name=name,
    )(*args)


def _rows_to_tiles(ref, x):
    rows = x.shape[0]
    for s in range(SUBLANES):
        ref[pl.ds(s, rows, stride=SUBLANES), :] = x[:, s * LANES:(s + 1) * LANES]


def _tiles_to_rows(ref, rows):
    return jnp.concatenate([ref[pl.ds(s, rows, stride=SUBLANES), :] for s in range(SUBLANES)], axis=1)


def _router_body(h_ref, g_ref, wr_ref, xn_ref, im_ref, gm_ref, cnt_ref, carry_ref, *, n_exp):
    i = pl.program_id(0)

    @pl.when(i == 0)
    def _():
        carry_ref[...] = jnp.zeros_like(carry_ref)

    xn = _rms(h_ref[...], g_ref[...])
    _rows_to_tiles(xn_ref, xn)
    tm = xn.shape[0]
    wr = wr_ref[...]
    xh = xn.astype(BF16)
    xl = (xn - xh.astype(F32)).astype(BF16)
    wh = wr.astype(BF16)
    wl = (wr - wh.astype(F32)).astype(BF16)
    lg = jnp.dot(xh, wh, preferred_element_type=F32) + (
        jnp.dot(xh, wl, preferred_element_type=F32) + jnp.dot(xl, wh, preferred_element_type=F32))
    lane = lax.broadcasted_iota(I32, (tm, LANES), 1)
    lane_f = lane.astype(F32)
    lg = jnp.where(lane < n_exp, lg, -jnp.inf)
    m1 = jnp.max(lg, axis=-1, keepdims=True)
    i1 = jnp.min(jnp.where(lg == m1, lane_f, float(LANES)), axis=-1, keepdims=True)
    lg2 = jnp.where(lane_f == i1, -jnp.inf, lg)
    m2 = jnp.max(lg2, axis=-1, keepdims=True)
    i2 = jnp.min(jnp.where(lg2 == m2, lane_f, float(LANES)), axis=-1, keepdims=True)
    e = jnp.exp(m2 - m1)
    g1 = 1.0 / (1.0 + e)
    g2 = e / (1.0 + e)
    sel1 = lane_f == i1
    sel2 = lane_f == i2
    onehot = jnp.where(sel1 | sel2, 1.0, 0.0)
    r_i = lax.broadcasted_iota(I32, (tm, tm), 0)
    c_i = lax.broadcasted_iota(I32, (tm, tm), 1)
    tri = jnp.where(c_i < r_i, 1.0, 0.0).astype(BF16)
    before = jnp.dot(tri, onehot.astype(BF16), preferred_element_type=F32) + carry_ref[0:1]
    r1 = jnp.sum(jnp.where(sel1, before, 0.0), axis=-1, keepdims=True)
    r2 = jnp.sum(jnp.where(sel2, before, 0.0), axis=-1, keepdims=True)
    total = carry_ref[0:1] + jnp.sum(onehot, axis=0, keepdims=True)
    carry_ref[...] = jnp.broadcast_to(total, carry_ref.shape)
    cnt_ref[...] = jnp.broadcast_to(total, cnt_ref.shape).astype(I32)
    meta = jnp.where(lane == 0, i1, jnp.where(lane == 1, i2, jnp.where(lane == 2, r1, jnp.where(lane == 3, r2, 0.0))))
    im_ref[...] = meta.astype(I32)
    gm_ref[...] = jnp.where(lane == 0, g1, jnp.where(lane == 1, g2, 0.0))


def _router(h, gain, w_router):
    t, d = h.shape
    n_exp = w_router.shape[1]
    tm = _pick(t, (512, 256, 128, 64, 32, 16, 8))
    wr = jnp.zeros((d, LANES), F32).at[:, :n_exp].set(w_router)
    return pl.pallas_call(
        functools.partial(_router_body, n_exp=n_exp),
        out_shape=[jax.ShapeDtypeStruct((t * SUBLANES, LANES), F32), jax.ShapeDtypeStruct((t, LANES), I32),
                   jax.ShapeDtypeStruct((t, LANES), F32), jax.ShapeDtypeStruct((SUBLANES, LANES), I32)],
        grid=(t // tm,),
        in_specs=[pl.BlockSpec((tm, d), lambda i: (i, 0)),
                  pl.BlockSpec((1, d), lambda i: (0, 0)),
                  pl.BlockSpec((d, LANES), lambda i: (0, 0))],
        out_specs=[pl.BlockSpec((tm * SUBLANES, LANES), lambda i: (i, 0)),
                   pl.BlockSpec((tm, LANES), lambda i: (i, 0)),
                   pl.BlockSpec((tm, LANES), lambda i: (i, 0)),
                   pl.BlockSpec((SUBLANES, LANES), lambda i: (0, 0))],
        scratch_shapes=[pltpu.VMEM((SUBLANES, LANES), F32)],
        compiler_params=_params("arbitrary"),
        name="moe_router",
    )(h, gain.reshape(1, -1), wr)


def _dispatch_body(s1_ref, s2_ref, x_ref, zero_ref, out_ref, sem, *, tm):
    del zero_ref

    def issue(r, carry):
        src = x_ref.at[r]
        pltpu.make_async_copy(src, out_ref.at[s1_ref[0, 0, r]], sem.at[0]).start(0)
        pltpu.make_async_copy(src, out_ref.at[s2_ref[0, 0, r]], sem.at[1]).start(1)
        return carry

    lax.fori_loop(0, tm, issue, 0, unroll=8)
    pltpu.make_async_copy(x_ref, out_ref.at[pl.ds(0, tm)], sem.at[0]).wait()
    pltpu.make_async_copy(x_ref, out_ref.at[pl.ds(0, tm)], sem.at[1]).wait()


def _dispatch(xn, slot1, slot2, n_slots):
    t = xn.shape[0]
    tm = _pick(t, (512, 256, 128, 64, 32, 16, 8))
    nt = t // tm
    tile = xn.shape[1:]
    smem = pl.BlockSpec((1, 1, tm), lambda i: (i, 0, 0), memory_space=pltpu.SMEM)
    return pl.pallas_call(
        functools.partial(_dispatch_body, tm=tm),
        out_shape=jax.ShapeDtypeStruct((n_slots,) + tile, F32),
        grid=(nt,),
        in_specs=[smem, smem, pl.BlockSpec((tm,) + tile, lambda i: (i, 0, 0)), pl.BlockSpec(memory_space=pl.ANY)],
        out_specs=pl.BlockSpec(memory_space=pl.ANY),
        scratch_shapes=[pltpu.SemaphoreType.DMA((2,))],
        input_output_aliases={3: 0},
        compiler_params=_params("arbitrary"),
        name="moe_dispatch",
    )(slot1.reshape(nt, 1, tm), slot2.reshape(nt, 1, tm), xn, jnp.zeros((n_slots,) + tile, F32))


def _experts_body(te_ref, nu_ref, x_ref, wg_hbm, wu_hbm, wd_hbm, o_ref, wg_ref, wu_ref, wd_ref, stage_in, stage_out,
                  sem, *, tm, tf):
    i = pl.program_id(0)
    active = i < nu_ref[0]
    expert = te_ref[i]
    changed = (i == 0) | (expert != te_ref[jnp.maximum(i - 1, 0)])

    @pl.when(active & changed)
    def _():
        _load_bf16(wg_hbm.at[expert], wg_ref, stage_in, sem)
        _load_bf16(wu_hbm.at[expert], wu_ref, stage_in, sem)
        _load_bf16(wd_hbm.at[expert], wd_ref, stage_out, sem)

    @pl.when(active)
    def _():
        xb = _tiles_to_rows(x_ref, tm).astype(BF16)
        acc = None
        for c in range(wg_ref.shape[1] // tf):
            cols = slice(c * tf, (c + 1) * tf)
            gate = jnp.dot(xb, wg_ref[:, cols], preferred_element_type=F32)
            up = jnp.dot(xb, wu_ref[:, cols], preferred_element_type=F32)
            act = (gate * jax.nn.sigmoid(gate) * up).astype(BF16)
            d = jnp.dot(act, wd_ref[cols, :], preferred_element_type=F32)
            acc = d if acc is None else acc + d
        _rows_to_tiles(o_ref, acc)

    @pl.when(jnp.logical_not(active))
    def _():
        o_ref[...] = jnp.zeros_like(o_ref)


def _experts(xs, tile_expert, n_used, w_gate, w_up, w_down, tm):
    n_slots = xs.shape[0] // SUBLANES
    d = SUBLANES * LANES
    f = w_gate.shape[2]
    tf = _pick(f, (1792, 1024, 512, 256, 128))
    n_tiles = n_slots // tm
    anywhere = pl.BlockSpec(memory_space=pl.ANY)
    grid_spec = pltpu.PrefetchScalarGridSpec(
        num_scalar_prefetch=2,
        grid=(n_tiles,),
        in_specs=[pl.BlockSpec((tm * SUBLANES, LANES), lambda i, te, nu: (jnp.minimum(i, nu[0] - 1), 0)),
                  anywhere, anywhere, anywhere],
        out_specs=pl.BlockSpec((tm * SUBLANES, LANES), lambda i, te, nu: (i, 0)),
        scratch_shapes=[pltpu.VMEM((d, f), BF16), pltpu.VMEM((d, f), BF16), pltpu.VMEM((f, d), BF16),
                        pltpu.VMEM((STAGE_SLOTS, _stage_rows(d, f), f), F32),
                        pltpu.VMEM((STAGE_SLOTS, _stage_rows(f, d), d), F32),
                        pltpu.SemaphoreType.DMA((STAGE_SLOTS,))],
    )
    return pl.pallas_call(
        functools.partial(_experts_body, tm=tm, tf=tf),
        out_shape=jax.ShapeDtypeStruct((n_slots * SUBLANES, LANES), F32),
        grid_spec=grid_spec,
        compiler_params=_params("arbitrary"),
        name="moe_experts",
    )(tile_expert, n_used, xs, w_gate, w_up, w_down)


def _gather_norm_body(s1_ref, s2_ref, n1_ref, n2_ref, h_ref, gm_ref, g_ref, ys_ref, oa_ref, ob_ref, ya_ref, yb_ref,
                      sem, *, tm, head_tiles, n_steps):
    i = pl.program_id(0)
    cur = i % 2

    def gather(a_ref, b_ref, buf):
        def issue(r, carry):
            dst = pl.ds(pl.multiple_of(r * SUBLANES, SUBLANES), SUBLANES)
            pltpu.make_async_copy(ys_ref.at[a_ref[0, 0, r]], ya_ref.at[buf, dst], sem.at[buf, 0]).start(0)
            pltpu.make_async_copy(ys_ref.at[b_ref[0, 0, r]], yb_ref.at[buf, dst], sem.at[buf, 1]).start(1)
            return carry

        lax.fori_loop(0, tm, issue, 0, unroll=8)

    @pl.when(i == 0)
    def _():
        gather(s1_ref, s2_ref, 0)

    @pl.when(i + 1 < n_steps)
    def _():
        gather(n1_ref, n2_ref, 1 - cur)

    pltpu.make_async_copy(ya_ref.at[cur], ya_ref.at[cur], sem.at[cur, 0]).wait()
    pltpu.make_async_copy(yb_ref.at[cur], yb_ref.at[cur], sem.at[cur, 1]).wait()
    gm = gm_ref[...]
    y = _rms(h_ref[...] + (gm[:, 0:1] * _tiles_to_rows(ya_ref.at[cur], tm)
                           + gm[:, 1:2] * _tiles_to_rows(yb_ref.at[cur], tm)), g_ref[...])

    @pl.when(i < head_tiles)
    def _():
        oa_ref[...] = y

    @pl.when(i >= head_tiles)
    def _():
        ob_ref[...] = y


def _gather_norm(h, gates, slot1, slot2, ys, gain, split):
    t, d = h.shape
    tm = _pick(split, (512, 256, 128, 64, 32, 16, 8))
    assert t % tm == 0
    nt = t // tm
    head_tiles = split // tm
    smem = pl.BlockSpec((1, 1, tm), lambda i: (i, 0, 0), memory_space=pltpu.SMEM)
    smem_next = pl.BlockSpec((1, 1, tm), lambda i: (jnp.minimum(i + 1, nt - 1), 0, 0), memory_space=pltpu.SMEM)
    s1, s2 = slot1.reshape(nt, 1, tm), slot2.reshape(nt, 1, tm)
    return pl.pallas_call(
        functools.partial(_gather_norm_body, tm=tm, head_tiles=head_tiles, n_steps=nt),
        out_shape=[jax.ShapeDtypeStruct((split, d), F32), jax.ShapeDtypeStruct((t - split, d), F32)],
        grid=(nt,),
        in_specs=[smem, smem, smem_next, smem_next, pl.BlockSpec((tm, d), lambda i: (i, 0)),
                  pl.BlockSpec((tm, LANES), lambda i: (i, 0)),
                  pl.BlockSpec((1, d), lambda i: (0, 0)),
                  pl.BlockSpec(memory_space=pl.ANY)],
        out_specs=[pl.BlockSpec((tm, d), lambda i: (jnp.minimum(i, head_tiles - 1), 0)),
                   pl.BlockSpec((tm, d), lambda i: (jnp.maximum(i - head_tiles, 0), 0))],
        scratch_shapes=[pltpu.VMEM((2, tm * SUBLANES, LANES), F32), pltpu.VMEM((2, tm * SUBLANES, LANES), F32),
                        pltpu.SemaphoreType.DMA((2, 2))],
        compiler_params=_params("arbitrary"),
        name="moe_gather_norm",
    )(s1, s2, s1, s2, h, gates, gain.reshape(1, -1), ys)


def _moe(h, gain, w_router, w_gate, w_up, w_down, final_gain, split):
    t, _ = h.shape
    n_exp = w_router.shape[1]
    tm_e = 512
    xn, imeta, gates, counts = _router(h, gain, w_router)
    idx1, idx2, rank1, rank2 = imeta[:, 0], imeta[:, 1], imeta[:, 2], imeta[:, 3]
    cnt = counts[0, :n_exp]
    padded = ((cnt + tm_e - 1) // tm_e) * tm_e
    ends = jnp.cumsum(padded)
    starts = ends - padded
    slot1 = starts[idx1] + rank1
    slot2 = starts[idx2] + rank2
    n_tiles = (TOP_K * t + n_exp * (tm_e - 1)) // tm_e
    n_used = (ends[-1] // tm_e).astype(I32)
    tile_start = jnp.arange(n_tiles, dtype=I32) * tm_e
    tile_expert = jnp.sum((tile_start[:, None] >= ends[None, :]).astype(I32), axis=1)
    tile_expert = jnp.minimum(tile_expert, n_exp - 1)
    assert h.shape[1] == SUBLANES * LANES
    n_slots = n_tiles * tm_e
    xs = _dispatch(xn.reshape(t, SUBLANES, LANES), slot1, slot2, n_slots)
    ys = _experts(xs.reshape(n_slots * SUBLANES, LANES), tile_expert, n_used.reshape(1), w_gate, w_up, w_down, tm_e)
    return _gather_norm(h, gates, slot1, slot2, ys.reshape(n_slots, SUBLANES, LANES), final_gain, split)


def _cache_to_slabs(cache):
    n, w, two, heads, hd = cache.shape
    return jnp.transpose(cache, (0, 2, 3, 4, 1)).reshape(n, two, heads * hd, w)


def _slabs_to_cache(slabs, heads):
    n, two, _, w = slabs.shape
    return jnp.transpose(slabs.reshape(n, two, heads, HEAD_DIM, w), (0, 4, 1, 2, 3))


def kernel(x_prompt, x_sample, cache_conv, cache_swa_kv, cache_dil_kv0, cache_dil_kv1, cache_dil_kv2, norm_mix0, w_in0, conv_w, swa_sink, w_out0, norm_ffn0, w_gate0, w_up0, w_down0, norm_mix1, w_in1, w_out1, norm_ffn1, w_router, w_gate1, w_up1, w_down1, norm_final):
    n_p, seq, d = x_prompt.shape
    n_s, s_len, _ = x_sample.shape
    tp, ts = n_p * seq, n_s * s_len
    d_conv = conv_w.shape[2]
    kvh, grp = swa_sink.shape[1], swa_sink.shape[2]
    hq0 = kvh * grp
    h1 = cache_dil_kv0.shape[4]
    c_q0, c_kv0 = hq0 * HEAD_DIM, kvh * HEAD_DIM
    q0_col = 3 * d_conv
    k0_col = q0_col + c_q0
    c_g = 3 * h1 * HEAD_DIM
    dil_caches = (cache_dil_kv0, cache_dil_kv1, cache_dil_kv2)

    x_rows = (x_prompt.reshape(tp, d), x_sample.reshape(ts, d))

    assert c_kv0 == LANES and grp % 2 == 0 and d_conv % LANES == 0
    nsc = d_conv // LANES
    q0_slab = 3 * nsc
    k0_slab = q0_slab + c_q0 // LANES
    proj0 = _dense([x_rows], w_in0[0], gain=norm_mix0[0], out_dtype=F32, name="in_proj0", slab_out=True)
    a_p, conv_tail = _conv_prompt(proj0, conv_w[0], n_seq=n_p, seq_len=seq, c=d_conv)
    att_p = _band_attn(proj0, n_seq=n_p, seq_len=seq, dil=1, q_slab=q0_slab, k_slab=k0_slab, v_slab=k0_slab + 1,
                       n_q_slabs=c_q0 // LANES, shared_kv=True, sink=swa_sink[0], want_lse=False,
                       name="swa_prompt")[0]
    proj0_s = proj0[:, tp:]
    p3 = proj0_s[:q0_slab].reshape(3, nsc, n_s, s_len, LANES)
    p3 = jnp.transpose(p3, (0, 3, 2, 1, 4)).reshape(3, s_len, n_s, d_conv)
    a_s, conv_new = _conv_step(p3, jnp.transpose(cache_conv[0], (1, 0, 2)), conv_w[0])
    a_s = jnp.transpose(a_s, (1, 0, 2)).reshape(ts, d_conv)
    kvt0 = _proj_t(jnp.transpose(w_in0[0][:, k0_col:]), x_rows[1], norm_mix0[0], "kv_t0")
    q_s = proj0_s[q0_slab:k0_slab].reshape(kvh, grp // 2, n_s, s_len, 2, HEAD_DIM)
    q_s = jnp.transpose(q_s, (2, 3, 1, 4, 0, 5)).reshape(n_s, s_len * grp, c_kv0)
    sink_col = jnp.broadcast_to(swa_sink[0][:, None, :], (kvh, s_len, grp)).reshape(kvh * s_len * grp, 1)
    o_s, swa_new = _step_attn(q_s, kvt0.reshape(2, c_kv0, ts), _cache_to_slabs(cache_swa_kv[0]), dil=1,
                              tok_div=grp, sink_col=sink_col, want_lse=False, bsz=8, name="swa_step")
    att_s = jnp.transpose(o_s.reshape(n_s, s_len, grp, kvh, HEAD_DIM), (0, 1, 3, 2, 4)).reshape(ts, c_q0)
    h = _dense([(a_p, a_s.astype(BF16)), (att_p, att_s.astype(BF16))], w_out0[0], res=x_rows, out_dtype=F32,
               name="out_proj0")
    h = _ffn(h, norm_ffn0[0], w_gate0[0], w_up0[0], w_down0[0], "ffn0")

    proj1 = _dense([h], w_in1[0], gain=norm_mix1[0], out_dtype=F32, name="in_proj1", slab_out=True)
    proj1_s = proj1[:, tp:]
    c_h = h1 * HEAD_DIM
    nsh = c_h // LANES
    pairs_p, pairs_s, dil_p, dil_s = [], [], [], []
    for g, dil in enumerate(DILATIONS):
        window = dil_caches[g].shape[2]
        s0 = g * 3 * nsh
        pairs_p.append(tuple(_band_attn(proj1, n_seq=n_p, seq_len=seq, dil=dil, q_slab=s0, k_slab=s0 + nsh,
                                        v_slab=s0 + 2 * nsh, n_q_slabs=nsh, want_lse=True,
                                        name=f"dil{g}_prompt")))
        keep = min(window, seq)
        state = _state_slabs(proj1, first_slab=s0 + nsh, n_slabs=2 * nsh, n_seq=n_p, seq_len=seq, keep=keep,
                             name=f"dil{g}_state")
        dil_p.append(_slabs_to_cache(state.reshape(n_p, 2, c_h, keep), h1)[None])
        lo = g * c_g
        q_g = jnp.transpose(proj1_s[s0:s0 + nsh].reshape(nsh, n_s, s_len, LANES), (1, 2, 0, 3))
        q_g = q_g.reshape(n_s, s_len, c_h)
        q_g = jnp.concatenate([q_g, jnp.zeros((n_s, SUBLANES - s_len, c_h), F32)], axis=1)
        kvt = _proj_t(jnp.transpose(w_in1[0][:, lo + c_h:lo + c_g]), h[tp:], norm_mix1[0], f"kv_t1_{g}")
        o_sg, l_sg, cache_new = _step_attn(q_g, kvt.reshape(2, c_h, ts), _cache_to_slabs(dil_caches[g][0]),
                                           dil=dil, tok_div=1, want_lse=True,
                                           bsz=max(1, 2048 // window), name=f"dil{g}_step")
        pairs_s.append((o_sg.reshape(n_s * SUBLANES, c_h), l_sg.reshape(n_s * SUBLANES, c_h)))
        dil_s.append(_slabs_to_cache(cache_new, h1)[None])
    comb_s = _combine(pairs_s, "combine_step").reshape(n_s, SUBLANES, c_h)[:, :s_len].reshape(ts, c_h)
    h = _combine_proj(pairs_p, comb_s, w_out1[0], h, "out_proj1")
    y_p, y_s = _moe(h, norm_ffn1[0], w_router[0], w_gate1[0], w_up1[0], w_down1[0], norm_final, tp)

    y_prompt = y_p.reshape(n_p, seq, d)
    y_sample = y_s.reshape(n_s, s_len, d)
    new_conv_prompt = conv_tail[:, SUBLANES - (CONV_W - 1):][None]
    new_conv_sample = jnp.transpose(conv_new, (1, 0, 2))[None]
    keep0 = min(SPAN, seq)
    swa_state = _state_slabs(proj0, first_slab=k0_slab, n_slabs=2, n_seq=n_p, seq_len=seq, keep=keep0,
                             name="swa_state")
    new_swa_kv_prompt = _slabs_to_cache(swa_state, kvh)[None]
    new_swa_kv_sample = _slabs_to_cache(swa_new, kvh)[None]
    return (y_prompt, y_sample, new_conv_prompt, new_conv_sample, new_swa_kv_prompt, new_swa_kv_sample,
            dil_p[0], dil_s[0], dil_p[1], dil_s[1], dil_p[2], dil_s[2])
```

```python
import functools

import jax
import jax.numpy as jnp
from jax import lax
from jax.experimental import pallas as pl
from jax.experimental.pallas import tpu as pltpu

F32 = jnp.float32
BF16 = jnp.bfloat16
I32 = jnp.int32

EPS = 1e-5
HEAD_DIM = 64
SPAN = 128
CONV_W = 3
DILATIONS = (1, 4, 16)
TOP_K = 2
LANES = 128
SUBLANES = 8
VMEM_LIMIT_BYTES = 56 * 1024 * 1024
NT_DIMS = (((1,), (1,)), ((), ()))


def _params(*sem):
    return pltpu.CompilerParams(dimension_semantics=sem, vmem_limit_bytes=VMEM_LIMIT_BYTES)


def _pick(n, candidates):
    for c in candidates:
        if n % c == 0:
            return c
    raise ValueError(f"no tile for {n} in {candidates}")


def _rms(x, g):
    y = x * lax.rsqrt(jnp.mean(x * x, axis=-1, keepdims=True) + EPS)
    return y * g


STAGE_SLOTS = 4
STAGE_BYTES = 2 * 1024 * 1024


def _stage_rows(k, n):
    rows = k
    while rows * n * 4 > STAGE_BYTES and rows % 32 == 0:
        rows //= 2
    return rows


def _load_bf16(w_hbm, w_vmem, stage, sem):
    n_slots, chunk = stage.shape[0], stage.shape[1]
    n_chunks = w_hbm.shape[0] // chunk

    def copy(c):
        return pltpu.make_async_copy(w_hbm.at[pl.ds(c * chunk, chunk), :], stage.at[c % n_slots],
                                     sem.at[c % n_slots])

    for c in range(min(n_slots - 1, n_chunks)):
        copy(c).start(c % 2)
    for c in range(n_chunks):
        if c + n_slots - 1 < n_chunks:
            copy(c + n_slots - 1).start((c + n_slots - 1) % 2)
        copy(c).wait()
        w_vmem[c * chunk:(c + 1) * chunk, :] = stage[c % n_slots].astype(BF16)


def _dense_body(*refs, ks, split, head_tiles, has_gain, has_res, slab_out, cn):
    n_in = len(ks)
    step = pl.program_id(0)
    refs = list(refs)

    def take(is_split):
        count = 2 if is_split else 1
        parts = tuple(refs[:count])
        del refs[:count]
        return parts

    def read(parts, cols=slice(None)):
        if len(parts) == 1:
            return parts[0][:, cols]
        return jnp.where(step < head_tiles, parts[0][:, cols], parts[1][:, cols])

    xs = [take(s) for s in split[:n_in]]
    g_ref = refs.pop(0) if has_gain else None
    w_hbm = refs.pop(0)
    res = take(split[n_in]) if has_res else None
    o_ref, w_ref, stage, sem = refs

    @pl.when(step == 0)
    def _():
        _load_bf16(w_hbm, w_ref, stage, sem)

    if has_gain:
        lhs = [_rms(read(xs[0]), g_ref[...]).astype(BF16)]
    else:
        lhs = [read(x).astype(BF16) for x in xs]
    n = w_ref.shape[1]
    for c in range(n // cn):
        cols = slice(c * cn, (c + 1) * cn)
        acc = None
        row0 = 0
        for a, k in zip(lhs, ks):
            d = jnp.dot(a, w_ref[row0:row0 + k, cols], preferred_element_type=F32)
            acc = d if acc is None else acc + d
            row0 += k
        if has_res:
            acc = read(res, cols) + acc
        if slab_out:
            for s in range(cn // LANES):
                o_ref[c * (cn // LANES) + s] = acc[:, s * LANES:(s + 1) * LANES].astype(o_ref.dtype)
        else:
            o_ref[:, cols] = acc.astype(o_ref.dtype)


def _dense(xs, w, *, gain=None, res=None, out_dtype, name, slab_out=False):
    def n_rows(a):
        return sum(p.shape[0] for p in a) if isinstance(a, tuple) else a.shape[0]

    def width(a):
        return a[0].shape[1] if isinstance(a, tuple) else a.shape[1]

    operands = list(xs) + ([res] if res is not None else [])
    pairs = [a for a in operands if isinstance(a, tuple)]
    t = n_rows(xs[0])
    k_all, n = w.shape
    ks = tuple(width(x) for x in xs)
    assert sum(ks) == k_all
    gcd_rows = t
    for a in pairs:
        gcd_rows = min(gcd_rows, a[1].shape[0])
    tm = _pick(gcd_rows, (512, 256, 128, 64, 32, 16, 8))
    assert t % tm == 0
    head_tiles = pairs[0][0].shape[0] // tm if pairs else 0
    for a in pairs:
        assert a[0].shape[0] == head_tiles * tm
    cn = _pick(n, (512, 384, 256, 128))
    in_specs, args = [], []

    def add_rows(a, k):
        if isinstance(a, tuple):
            in_specs.append(pl.BlockSpec((tm, k), lambda i: (jnp.minimum(i, head_tiles - 1), 0)))
            in_specs.append(pl.BlockSpec((tm, k), lambda i: (jnp.maximum(i - head_tiles, 0), 0)))
            args.extend(a)
        else:
            in_specs.append(pl.BlockSpec((tm, k), lambda i: (i, 0)))
            args.append(a)

    for x, k in zip(xs, ks):
        add_rows(x, k)
    if gain is not None:
        in_specs.append(pl.BlockSpec((1, ks[0]), lambda i: (0, 0)))
        args.append(gain.reshape(1, -1))
    in_specs.append(pl.BlockSpec(memory_space=pl.ANY))
    args.append(w)
    if res is not None:
        add_rows(res, n)
    split = tuple(isinstance(a, tuple) for a in operands)
    if slab_out:
        out_shape = jax.ShapeDtypeStruct((n // LANES, t, LANES), out_dtype)
        out_spec = pl.BlockSpec((n // LANES, tm, LANES), lambda i: (0, i, 0))
    else:
        out_shape = jax.ShapeDtypeStruct((t, n), out_dtype)
        out_spec = pl.BlockSpec((tm, n), lambda i: (i, 0))
    return pl.pallas_call(
        functools.partial(_dense_body, ks=ks, split=split, head_tiles=head_tiles, has_gain=gain is not None,
                          has_res=res is not None, slab_out=slab_out, cn=cn),
        out_shape=out_shape,
        grid=(t // tm,),
        in_specs=in_specs,
        out_specs=out_spec,
        scratch_shapes=[pltpu.VMEM((k_all, n), BF16), pltpu.VMEM((STAGE_SLOTS, _stage_rows(k_all, n), n), F32),
                        pltpu.SemaphoreType.DMA((STAGE_SLOTS,))],
        compiler_params=_params("arbitrary"),
        name=name,
    )(*args)


def _proj_t_body(w_ref, x_ref, g_ref, o_ref):
    xn = _rms(x_ref[...], g_ref[...]).astype(BF16)
    o_ref[...] = lax.dot_general(w_ref[...].astype(BF16), xn, NT_DIMS, preferred_element_type=F32)


def _proj_t(w_t, x, gain, name):
    c, k = w_t.shape
    rows = x.shape[0]
    tc = _pick(c, (512, 256, 128))
    return pl.pallas_call(
        _proj_t_body,
        out_shape=jax.ShapeDtypeStruct((c, rows), F32),
        grid=(c // tc,),
        in_specs=[pl.BlockSpec((tc, k), lambda i: (i, 0)),
                  pl.BlockSpec((rows, k), lambda i: (0, 0)),
                  pl.BlockSpec((1, k), lambda i: (0, 0))],
        out_specs=pl.BlockSpec((tc, rows), lambda i: (i, 0)),
        compiler_params=_params("parallel"),
        name=name,
    )(w_t, x, gain.reshape(1, -1))


def _ffn_body(x_ref, g_ref, wg_hbm, wu_hbm, wd_hbm, o_ref, wg_ref, wu_ref, wd_ref, stage_in, stage_out, sem, *,
              tf):
    @pl.when(pl.program_id(0) == 0)
    def _():
        _load_bf16(wg_hbm, wg_ref, stage_in, sem)
        _load_bf16(wu_hbm, wu_ref, stage_in, sem)
        _load_bf16(wd_hbm, wd_ref, stage_out, sem)

    x = x_ref[...]
    xn = _rms(x, g_ref[...]).astype(BF16)
    acc = None
    for c in range(wg_ref.shape[1] // tf):
        cols = slice(c * tf, (c + 1) * tf)
        gate = jnp.dot(xn, wg_ref[:, cols], preferred_element_type=F32)
        up = jnp.dot(xn, wu_ref[:, cols], preferred_element_type=F32)
        act = (gate * jax.nn.sigmoid(gate) * up).astype(BF16)
        d = jnp.dot(act, wd_ref[cols, :], preferred_element_type=F32)
        acc = d if acc is None else acc + d
    o_ref[...] = x + acc


def _ffn(x, gain, w_gate, w_up, w_down, name):
    t, d = x.shape
    f = w_gate.shape[1]
    tm = _pick(t, (512, 256, 128, 64, 32, 16, 8))
    tf = f
    anywhere = pl.BlockSpec(memory_space=pl.ANY)
    return pl.pallas_call(
        functools.partial(_ffn_body, tf=tf),
        out_shape=jax.ShapeDtypeStruct((t, d), F32),
        grid=(t // tm,),
        in_specs=[pl.BlockSpec((tm, d), lambda i: (i, 0)),
                  pl.BlockSpec((1, d), lambda i: (0, 0)),
                  anywhere, anywhere, anywhere],
        out_specs=pl.BlockSpec((tm, d), lambda i: (i, 0)),
        scratch_shapes=[pltpu.VMEM((d, f), BF16), pltpu.VMEM((d, f), BF16), pltpu.VMEM((f, d), BF16),
                        pltpu.VMEM((STAGE_SLOTS, _stage_rows(d, f), f), F32),
                        pltpu.VMEM((STAGE_SLOTS, _stage_rows(f, d), d), F32),
                        pltpu.SemaphoreType.DMA((STAGE_SLOTS,))],
        compiler_params=_params("arbitrary"),
        name=name,
    )(x, gain.reshape(1, -1), w_gate, w_up, w_down)


def _band_body(*refs, dil, mb, shared_kv, has_sink, want_lse):
    q_ref, kc_ref, kp_ref, vc_ref, vp_ref = refs[:5]
    pos = 5
    sink_ref = refs[pos] if has_sink else None
    pos += int(has_sink)
    o_ref = refs[pos]
    lse_ref = refs[pos + 1] if want_lse else None
    t = pl.program_id(1)
    scale = HEAD_DIM ** -0.5
    qi = lax.broadcasted_iota(I32, (2 * SPAN, 2 * SPAN), 0) & (SPAN - 1)
    kj = lax.broadcasted_iota(I32, (2 * SPAN, 2 * SPAN), 1)
    band = (kj >= qi) & (kj <= qi + SPAN)
    band_first = band & (kj >= jnp.where(t == 0, SPAN, 0))
    lane = lax.broadcasted_iota(I32, (1, LANES), 1)
    halves = [lane < HEAD_DIM, lane >= HEAD_DIM]

    def rows_of(ref, s, r, count):
        if dil == 1:
            return ref[s, 0:count, :]
        return ref[s, pl.ds(r, count, stride=dil), :]

    def pair(q_a, mask_a, q_b, mask_b, kw, vw, msk, heads):
        q2 = jnp.concatenate([jnp.where(mask_a, q_a, 0.0), jnp.where(mask_b, q_b, 0.0)], axis=0).astype(BF16)
        s = lax.dot_general(q2, kw, NT_DIMS, preferred_element_type=F32)
        s = jnp.where(msk, s, -jnp.inf)
        m = jnp.max(s, axis=-1, keepdims=True)
        if has_sink:
            sink = jnp.concatenate([jnp.full((SPAN, 1), sink_ref[h], F32) for h in heads], axis=0)
            m = jnp.maximum(m, sink)
        p = jnp.exp(s - m)
        den = jnp.sum(p, axis=-1, keepdims=True)
        if has_sink:
            den = den + jnp.exp(sink - m)
        o = jnp.dot(p.astype(BF16), vw, preferred_element_type=F32) / den
        return o, m + jnp.log(den)

    def store(s, r, b, o_tile, lse_tile):
        if dil == 1:
            o_ref[b * SPAN:(b + 1) * SPAN, s * LANES:(s + 1) * LANES] = o_tile.astype(o_ref.dtype)
            if want_lse:
                lse_ref[b * SPAN:(b + 1) * SPAN, s * LANES:(s + 1) * LANES] = lse_tile
        else:
            o_ref[s, pl.ds(r + dil * b * SPAN, SPAN, stride=dil), :] = o_tile
            if want_lse:
                lse_ref[s, pl.ds(r + dil * b * SPAN, SPAN, stride=dil), :] = lse_tile

    n_slabs = q_ref.shape[0]
    first_head = 2 * n_slabs * pl.program_id(2)
    top, bot = slice(0, SPAN), slice(SPAN, 2 * SPAN)
    for r in range(dil):
        qs = [rows_of(q_ref, s, r, mb * SPAN) * scale for s in range(n_slabs)]
        kf = [jnp.concatenate([rows_of(kp_ref, s, r, SPAN), rows_of(kc_ref, s, r, mb * SPAN)], axis=0)
              for s in range(kc_ref.shape[0])]
        vf = [jnp.concatenate([rows_of(vp_ref, s, r, SPAN), rows_of(vc_ref, s, r, mb * SPAN)], axis=0)
              for s in range(vc_ref.shape[0])]
        ks = [k.astype(BF16) for k in kf]
        vs = [v.astype(BF16) for v in vf]
        if shared_kv:
            k_sw = pltpu.roll(kf[0], HEAD_DIM, 1).astype(BF16)
            v_sw = pltpu.roll(vf[0], HEAD_DIM, 1).astype(BF16)
        for b in range(mb):
            blk = slice(b * SPAN, (b + 1) * SPAN)
            keys = slice(b * SPAN, (b + 2) * SPAN)
            msk = band_first if b == 0 else band
            if not shared_kv:
                for s in range(n_slabs):
                    o, lse = pair(qs[s][blk], halves[0], qs[s][blk], halves[1], ks[s][keys], vs[s][keys], msk,
                                  (first_head + 2 * s, first_head + 2 * s + 1))
                    store(s, r, b, jnp.where(halves[0], o[top], o[bot]), jnp.where(halves[0], lse[top], lse[bot]))
            else:
                for kvh in range(2):
                    sa, sb = 2 * kvh, 2 * kvh + 1
                    o_al, l_al = pair(qs[sa][blk], halves[kvh], qs[sb][blk], halves[kvh], ks[0][keys],
                                      vs[0][keys], msk, (2 * sa + kvh, 2 * sb + kvh))
                    o_sw, l_sw = pair(qs[sa][blk], halves[1 - kvh], qs[sb][blk], halves[1 - kvh], k_sw[keys],
                                      v_sw[keys], msk, (2 * sa + 1 - kvh, 2 * sb + 1 - kvh))
                    store(sa, r, b, jnp.where(halves[kvh], o_al[top], o_sw[top]),
                          jnp.where(halves[kvh], l_al[top], l_sw[top]))
                    store(sb, r, b, jnp.where(halves[kvh], o_al[bot], o_sw[bot]),
                          jnp.where(halves[kvh], l_al[bot], l_sw[bot]))


def _band_attn(slabs, *, n_seq, seq_len, dil, q_slab, k_slab, v_slab, n_q_slabs, shared_kv=False, sink=None,
               want_lse, name):
    mb = max(1, 512 // (dil * SPAN))
    tp = dil * SPAN * mb
    nt = seq_len // tp
    assert seq_len % tp == 0 and (not shared_kv or (n_q_slabs == 4 and dil == 1))
    prev_rows = dil * SPAN
    per_step = n_q_slabs if (shared_kv or tp <= 512) else 1
    kv_block = 1 if shared_kv else per_step
    kv_step = 0 if shared_kv else 1

    def cur(base, step):
        return lambda n, t, s: (base + s * step, n * nt + t, 0)

    def prev(base, step):
        return lambda n, t, s: (base + s * step, jnp.maximum((n * nt + t) * mb - 1, 0), 0)

    in_specs = [pl.BlockSpec((per_step, tp, LANES), cur(q_slab // per_step, 1)),
                pl.BlockSpec((kv_block, tp, LANES), cur(k_slab // kv_block, kv_step)),
                pl.BlockSpec((kv_block, prev_rows, LANES), prev(k_slab // kv_block, kv_step)),
                pl.BlockSpec((kv_block, tp, LANES), cur(v_slab // kv_block, kv_step)),
                pl.BlockSpec((kv_block, prev_rows, LANES), prev(v_slab // kv_block, kv_step))]
    assert q_slab % per_step == 0 and k_slab % kv_block == 0 and v_slab % kv_block == 0
    args = [slabs] * 5
    if sink is not None:
        in_specs.append(pl.BlockSpec(memory_space=pltpu.SMEM))
        args.append(sink.reshape(-1).astype(F32))
    rows = n_seq * seq_len
    if dil == 1:
        spec = pl.BlockSpec((tp, per_step * LANES), lambda n, t, s: (n * nt + t, s))
        out_shape = [jax.ShapeDtypeStruct((rows, n_q_slabs * LANES), BF16)]
        lse_shape = jax.ShapeDtypeStruct((rows, n_q_slabs * LANES), F32)
    else:
        spec = pl.BlockSpec((per_step, tp, LANES), lambda n, t, s: (s, n * nt + t, 0))
        out_shape = [jax.ShapeDtypeStruct((n_q_slabs, rows, LANES), F32)]
        lse_shape = jax.ShapeDtypeStruct((n_q_slabs, rows, LANES), F32)
    out_specs = [spec]
    if want_lse:
        out_shape.append(lse_shape)
        out_specs.append(spec)
    return pl.pallas_call(
        functools.partial(_band_body, dil=dil, mb=mb, shared_kv=shared_kv, has_sink=sink is not None,
                          want_lse=want_lse),
        out_shape=out_shape,
        grid=(n_seq, nt, n_q_slabs // per_step),
        in_specs=in_specs,
        out_specs=out_specs,
        compiler_params=_params("parallel", "parallel", "parallel"),
        name=name,
    )(*args)


def _combine_body(o0, l0, o1, l1, o2, l2, out_ref):
    def tile(ref):
        return ref[...].reshape(ref.shape[-2:]).astype(F32)

    a0, a1, a2 = tile(l0), tile(l1), tile(l2)
    m = jnp.maximum(jnp.maximum(a0, a1), a2)
    e0, e1, e2 = jnp.exp(a0 - m), jnp.exp(a1 - m), jnp.exp(a2 - m)
    num = e0 * tile(o0) + e1 * tile(o1) + e2 * tile(o2)
    out_ref[...] = (num / (e0 + e1 + e2)).astype(out_ref.dtype)


def _combine(pairs, name):
    first = pairs[0][0]
    rows, c = first.shape if first.ndim == 2 else (first.shape[1], first.shape[0] * LANES)
    tm = _pick(rows, (1024, 512, 256, 128, 64, 32, 16, 8))
    flat = pl.BlockSpec((tm, LANES), lambda i, s: (i, s))
    slab = pl.BlockSpec((1, tm, LANES), lambda i, s: (s, i, 0))
    args = [a for pair in pairs for a in pair]
    return pl.pallas_call(
        _combine_body,
        out_shape=jax.ShapeDtypeStruct((rows, c), BF16),
        grid=(rows // tm, c // LANES),
        in_specs=[flat if a.ndim == 2 else slab for a in args],
        out_specs=flat,
        compiler_params=_params("parallel", "parallel"),
        name=name,
    )(*args)


def _combine_proj_body(o0, l0, o1, l1, o2, l2, tail_ref, res_ref, w_hbm, out_ref, w_ref, stage, sem, *, head_tiles):
    step = pl.program_id(0)

    @pl.when(step == 0)
    def _():
        _load_bf16(w_hbm, w_ref, stage, sem)

    parts = []
    for s in range(o1.shape[0]):
        lanes = slice(s * LANES, (s + 1) * LANES)
        a0, a1, a2 = l0[:, lanes], l1[s], l2[s]
        m = jnp.maximum(jnp.maximum(a0, a1), a2)
        e0, e1, e2 = jnp.exp(a0 - m), jnp.exp(a1 - m), jnp.exp(a2 - m)
        num = e0 * o0[:, lanes].astype(F32) + e1 * o1[s] + e2 * o2[s]
        parts.append(num / (e0 + e1 + e2))
    comb = jnp.concatenate(parts, axis=1).astype(BF16)
    comb = jnp.where(step < head_tiles, comb, tail_ref[...])
    out_ref[...] = res_ref[...] + jnp.dot(comb, w_ref[...], preferred_element_type=F32)


def _combine_proj(pairs, tail, w, res, name):
    (o0, l0), (o1, l1), (o2, l2) = pairs
    head, c = o0.shape
    t, n = res.shape
    tm = _pick(tail.shape[0], (512, 256, 128, 64, 32, 16, 8))
    assert head % tm == 0 and t == head + tail.shape[0] and o1.ndim == 3 and o2.ndim == 3
    head_tiles = head // tm

    def first(i):
        return jnp.minimum(i, head_tiles - 1)

    flat = pl.BlockSpec((tm, c), lambda i: (first(i), 0))
    slab = pl.BlockSpec((c // LANES, tm, LANES), lambda i: (0, first(i), 0))
    return pl.pallas_call(
        functools.partial(_combine_proj_body, head_tiles=head_tiles),
        out_shape=jax.ShapeDtypeStruct((t, n), F32),
        grid=(t // tm,),
        in_specs=[flat, flat, slab, slab, slab, slab,
                  pl.BlockSpec((tm, c), lambda i: (jnp.maximum(i - head_tiles, 0), 0)),
                  pl.BlockSpec((tm, n), lambda i: (i, 0)),
                  pl.BlockSpec(memory_space=pl.ANY)],
        out_specs=pl.BlockSpec((tm, n), lambda i: (i, 0)),
        scratch_shapes=[pltpu.VMEM((c, n), BF16), pltpu.VMEM((STAGE_SLOTS, _stage_rows(c, n), n), F32),
                        pltpu.SemaphoreType.DMA((STAGE_SLOTS,))],
        compiler_params=_params("arbitrary"),
        name=name,
    )(o0, l0, o1, l1, o2, l2, tail, res, w)


def _state_body(x_ref, o_ref):
    o_ref[0, 0] = x_ref[0].T


def _state_slabs(slabs, *, first_slab, n_slabs, n_seq, seq_len, keep, name):
    assert seq_len % keep == 0
    per_seq = seq_len // keep
    return pl.pallas_call(
        _state_body,
        out_shape=jax.ShapeDtypeStruct((n_seq, n_slabs, LANES, keep), F32),
        grid=(n_seq, n_slabs),
        in_specs=[pl.BlockSpec((1, keep, LANES), lambda n, s: (first_slab + s, (n + 1) * per_seq - 1, 0))],
        out_specs=pl.BlockSpec((1, 1, LANES, keep), lambda n, s: (n, s, 0, 0)),
        compiler_params=_params("parallel", "parallel"),
        name=name,
    )(slabs)


def _conv_prompt_body(gb_ref, gc_ref, xa_ref, gcp_ref, xap_ref, w_ref, a_ref, st_ref):
    t = pl.program_id(1)
    w = w_ref[...]
    for s in range(gb_ref.shape[0]):
        lanes = slice(s * LANES, (s + 1) * LANES)
        u = gc_ref[s] * xa_ref[s]
        up = jnp.where(t == 0, 0.0, gcp_ref[s] * xap_ref[s])
        ext = jnp.concatenate([up, u], axis=0)
        y = (w[0:1, lanes] * pltpu.roll(ext, 2, 0)[SUBLANES:]
             + w[1:2, lanes] * pltpu.roll(ext, 1, 0)[SUBLANES:]) + w[2:3, lanes] * u
        a_ref[:, lanes] = (gb_ref[s] * y).astype(a_ref.dtype)
        st_ref[0, :, lanes] = u[u.shape[0] - SUBLANES:]


def _conv_prompt(slabs, conv_w, *, n_seq, seq_len, c):
    tq = _pick(seq_len, (512, 256, 128))
    nt = seq_len // tq
    rb = tq // SUBLANES
    ns = c // LANES

    def cur(part):
        return lambda n, t: (part, n * nt + t, 0)

    def prev(part):
        return lambda n, t: (part, jnp.maximum((n * nt + t) * rb - 1, 0), 0)

    return pl.pallas_call(
        _conv_prompt_body,
        out_shape=[jax.ShapeDtypeStruct((n_seq * seq_len, c), BF16),
                   jax.ShapeDtypeStruct((n_seq, SUBLANES, c), F32)],
        grid=(n_seq, nt),
        in_specs=[pl.BlockSpec((ns, tq, LANES), cur(0)), pl.BlockSpec((ns, tq, LANES), cur(1)),
                  pl.BlockSpec((ns, tq, LANES), cur(2)),
                  pl.BlockSpec((ns, SUBLANES, LANES), prev(1)), pl.BlockSpec((ns, SUBLANES, LANES), prev(2)),
                  pl.BlockSpec((CONV_W, c), lambda n, t: (0, 0))],
        out_specs=[pl.BlockSpec((tq, c), lambda n, t: (n * nt + t, 0)),
                   pl.BlockSpec((1, SUBLANES, c), lambda n, t: (n, 0, 0))],
        compiler_params=_params("parallel", "arbitrary"),
        name="conv_prompt",
    )(slabs, slabs, slabs, slabs, slabs, conv_w)


def _conv_step_body(p_ref, prev_ref, w_ref, a_ref, st_ref):
    s_len = p_ref.shape[1]
    w = w_ref[...]
    hist = [prev_ref[k] for k in range(CONV_W - 1)] + [p_ref[1, s] * p_ref[2, s] for s in range(s_len)]
    for s in range(s_len):
        y = (w[0:1] * hist[s] + w[1:2] * hist[s + 1]) + w[2:3] * hist[s + 2]
        a_ref[s] = p_ref[0, s] * y
    for k in range(CONV_W - 1):
        st_ref[k] = hist[s_len + k]


def _conv_step(p3, prev, conv_w):
    _, s_len, n, c = p3.shape
    return pl.pallas_call(
        _conv_step_body,
        out_shape=[jax.ShapeDtypeStruct((s_len, n, c), F32), jax.ShapeDtypeStruct((CONV_W - 1, n, c), F32)],
        name="conv_step",
        compiler_params=pltpu.CompilerParams(vmem_limit_bytes=VMEM_LIMIT_BYTES),
    )(p3, prev, conv_w)


def _step_body(*refs, n_blk, qr, tok_div, w, dil, cw, bsz, has_sink, want_lse):
    q_ref, kvt_ref, cache_ref = refs[:3]
    pos = 3
    sink_ref = refs[pos] if has_sink else None
    pos += int(has_sink)
    o_ref = refs[pos]
    pos += 1
    lse_ref = refs[pos] if want_lse else None
    pos += int(want_lse)
    cout_ref = refs[pos]
    r_dim = n_blk * HEAD_DIM
    nrb = n_blk * qr
    n_chunks = w // cw
    scale = HEAD_DIM ** -0.5
    step = pl.program_id(0)
    lane_r = lax.broadcasted_iota(I32, (1, r_dim), 1)
    blk_masks = [(lane_r >= j * HEAD_DIM) & (lane_r < (j + 1) * HEAD_DIM) for j in range(n_blk)]
    row = lax.broadcasted_iota(I32, (nrb, 1), 0)
    tok = (row % qr) // tok_div
    lane_c = lax.broadcasted_iota(I32, (1, cw), 1)
    lane_n = lax.broadcasted_iota(I32, (1, LANES), 1)

    def valid(pos_l):
        ok = (pos_l >= tok) & (pos_l <= w + tok)
        if dil > 1:
            ok = ok & (((pos_l - tok) & (dil - 1)) == 0)
        return ok

    def shift_window(b, kv, new_tile):
        n_cols = w // LANES
        prev = pltpu.roll(cache_ref[b, kv, :, 0:LANES], LANES - 4, 1)
        for j in range(n_cols):
            src = cache_ref[b, kv, :, (j + 1) * LANES:(j + 2) * LANES] if j + 1 < n_cols else new_tile
            nxt = pltpu.roll(src, LANES - 4, 1)
            cout_ref[b, kv, :, j * LANES:(j + 1) * LANES] = jnp.where(lane_n < LANES - 4, prev, nxt)
            prev = nxt

    for b in range(bsz):
        off = ((step * bsz + b) * 4) % LANES
        shift = (LANES - off) % LANES
        new_k = pltpu.roll(kvt_ref[0], shift, 1)
        new_v = pltpu.roll(kvt_ref[1], shift, 1)
        shift_window(b, 0, new_k)
        shift_window(b, 1, new_v)
        q = q_ref[b] * scale
        qbd = jnp.concatenate([jnp.where(mk, q, 0.0) for mk in blk_masks], axis=0).astype(BF16)
        scores = []
        for c in range(n_chunks):
            lo, hi = c * cw, (c + 1) * cw
            kc = cache_ref[b, 0, :, lo:hi]
            sc = jnp.dot(qbd, kc.astype(BF16), preferred_element_type=F32)
            scores.append(jnp.where(valid(lane_c + lo), sc, -jnp.inf))
        sc = jnp.dot(qbd, new_k.astype(BF16), preferred_element_type=F32)
        scores.append(jnp.where(valid(lane_n + w), sc, -jnp.inf))
        m = functools.reduce(jnp.maximum, [jnp.max(s, axis=-1, keepdims=True) for s in scores])
        if has_sink:
            m = jnp.maximum(m, sink_ref[...])
        probs = [jnp.exp(s - m) for s in scores]
        den = functools.reduce(lambda a, c: a + c, [jnp.sum(p, axis=-1, keepdims=True) for p in probs])
        if has_sink:
            den = den + jnp.exp(sink_ref[...] - m)
        pv = lax.dot_general(probs[-1].astype(BF16), new_v.astype(BF16), NT_DIMS, preferred_element_type=F32)
        for c in range(n_chunks):
            lo, hi = c * cw, (c + 1) * cw
            vc = cache_ref[b, 1, :, lo:hi]
            pv = pv + lax.dot_general(probs[c].astype(BF16), vc.astype(BF16), NT_DIMS,
                                      preferred_element_type=F32)
        o = jnp.zeros((qr, r_dim), F32)
        m_e = jnp.zeros((qr, r_dim), F32)
        den_e = jnp.zeros((qr, r_dim), F32)
        for j, mk in enumerate(blk_masks):
            rows = slice(j * qr, (j + 1) * qr)
            o = jnp.where(mk, pv[rows], o)
            m_e = jnp.where(mk, m[rows], m_e)
            den_e = jnp.where(mk, den[rows], den_e)
        o_ref[b] = o / den_e
        if want_lse:
            lse_ref[b] = m_e + jnp.log(den_e)


def _step_attn(q, kvt, cache, *, dil, tok_div, sink_col=None, want_lse, bsz, name):
    n, qr, r_dim = q.shape
    w = cache.shape[-1]
    n_blk = r_dim // HEAD_DIM
    cw = min(w, 512)
    in_specs = [pl.BlockSpec((bsz, qr, r_dim), lambda i: (i, 0, 0)),
                pl.BlockSpec((2, r_dim, LANES), lambda i: (0, 0, (i * bsz * 4) // LANES)),
                pl.BlockSpec((bsz, 2, r_dim, w), lambda i: (i, 0, 0, 0))]
    args = [q, kvt, cache]
    if sink_col is not None:
        in_specs.append(pl.BlockSpec((n_blk * qr, 1), lambda i: (0, 0)))
        args.append(sink_col)
    o_spec = pl.BlockSpec((bsz, qr, r_dim), lambda i: (i, 0, 0))
    out_shape = [jax.ShapeDtypeStruct((n, qr, r_dim), F32)]
    out_specs = [o_spec]
    if want_lse:
        out_shape.append(jax.ShapeDtypeStruct((n, qr, r_dim), F32))
        out_specs.append(o_spec)
    out_shape.append(jax.ShapeDtypeStruct(cache.shape, F32))
    out_specs.append(pl.BlockSpec((bsz, 2, r_dim, w), lambda i: (i, 0, 0, 0)))
    return pl.pallas_call(
        functools.partial(_step_body, n_blk=n_blk, qr=qr, tok_div=tok_div, w=w, dil=dil, cw=cw, bsz=bsz,
                          has_sink=sink_col is not None, want_lse=want_lse),
        out_shape=out_shape,
        grid=(n // bsz,),
        in_specs=in_specs,
        out_specs=out_specs,
        compiler_params=_params("parallel"),
        name=name,
    )(*args)


def _rows_to_tiles(ref, x):
    rows = x.shape[0]
    for s in range(SUBLANES):
        ref[pl.ds(s, rows, stride=SUBLANES), :] = x[:, s * LANES:(s + 1) * LANES]


def _tiles_to_rows(ref, rows):
    return jnp.concatenate([ref[pl.ds(s, rows, stride=SUBLANES), :] for s in range(SUBLANES)], axis=1)


def _router_body(h_ref, g_ref, wr_ref, xn_ref, im_ref, gm_ref, cnt_ref, carry_ref, *, n_exp):
    i = pl.program_id(0)

    @pl.when(i == 0)
    def _():
        carry_ref[...] = jnp.zeros_like(carry_ref)

    xn = _rms(h_ref[...], g_ref[...])
    _rows_to_tiles(xn_ref, xn)
    tm = xn.shape[0]
    wr = wr_ref[...]
    xh = xn.astype(BF16)
    xl = (xn - xh.astype(F32)).astype(BF16)
    wh = wr.astype(BF16)
    wl = (wr - wh.astype(F32)).astype(BF16)
    lg = jnp.dot(xh, wh, preferred_element_type=F32) + (
        jnp.dot(xh, wl, preferred_element_type=F32) + jnp.dot(xl, wh, preferred_element_type=F32))
    lane = lax.broadcasted_iota(I32, (tm, LANES), 1)
    lane_f = lane.astype(F32)
    lg = jnp.where(lane < n_exp, lg, -jnp.inf)
    m1 = jnp.max(lg, axis=-1, keepdims=True)
    i1 = jnp.min(jnp.where(lg == m1, lane_f, float(LANES)), axis=-1, keepdims=True)
    lg2 = jnp.where(lane_f == i1, -jnp.inf, lg)
    m2 = jnp.max(lg2, axis=-1, keepdims=True)
    i2 = jnp.min(jnp.where(lg2 == m2, lane_f, float(LANES)), axis=-1, keepdims=True)
    e = jnp.exp(m2 - m1)
    g1 = 1.0 / (1.0 + e)
    g2 = e / (1.0 + e)
    sel1 = lane_f == i1
    sel2 = lane_f == i2
    onehot = jnp.where(sel1 | sel2, 1.0, 0.0)
    r_i = lax.broadcasted_iota(I32, (tm, tm), 0)
    c_i = lax.broadcasted_iota(I32, (tm, tm), 1)
    tri = jnp.where(c_i < r_i, 1.0, 0.0).astype(BF16)
    before = jnp.dot(tri, onehot.astype(BF16), preferred_element_type=F32) + carry_ref[0:1]
    r1 = jnp.sum(jnp.where(sel1, before, 0.0), axis=-1, keepdims=True)
    r2 = jnp.sum(jnp.where(sel2, before, 0.0), axis=-1, keepdims=True)
    total = carry_ref[0:1] + jnp.sum(onehot, axis=0, keepdims=True)
    carry_ref[...] = jnp.broadcast_to(total, carry_ref.shape)
    cnt_ref[...] = jnp.broadcast_to(total, cnt_ref.shape).astype(I32)
    meta = jnp.where(lane == 0, i1, jnp.where(lane == 1, i2, jnp.where(lane == 2, r1, jnp.where(lane == 3, r2, 0.0))))
    im_ref[...] = meta.astype(I32)
    gm_ref[...] = jnp.where(lane == 0, g1, jnp.where(lane == 1, g2, 0.0))


def _router(h, gain, w_router):
    t, d = h.shape
    n_exp = w_router.shape[1]
    tm = _pick(t, (512, 256, 128, 64, 32, 16, 8))
    wr = jnp.zeros((d, LANES), F32).at[:, :n_exp].set(w_router)
    return pl.pallas_call(
        functools.partial(_router_body, n_exp=n_exp),
        out_shape=[jax.ShapeDtypeStruct((t * SUBLANES, LANES), F32), jax.ShapeDtypeStruct((t, LANES), I32),
                   jax.ShapeDtypeStruct((t, LANES), F32), jax.ShapeDtypeStruct((SUBLANES, LANES), I32)],
        grid=(t // tm,),
        in_specs=[pl.BlockSpec((tm, d), lambda i: (i, 0)),
                  pl.BlockSpec((1, d), lambda i: (0, 0)),
                  pl.BlockSpec((d, LANES), lambda i: (0, 0))],
        out_specs=[pl.BlockSpec((tm * SUBLANES, LANES), lambda i: (i, 0)),
                   pl.BlockSpec((tm, LANES), lambda i: (i, 0)),
                   pl.BlockSpec((tm, LANES), lambda i: (i, 0)),
                   pl.BlockSpec((SUBLANES, LANES), lambda i: (0, 0))],
        scratch_shapes=[pltpu.VMEM((SUBLANES, LANES), F32)],
        compiler_params=_params("arbitrary"),
        name="moe_router",
    )(h, gain.reshape(1, -1), wr)


def _dispatch_body(s1_ref, s2_ref, x_ref, zero_ref, out_ref, sem, *, tm):
    del zero_ref

    def issue(r, carry):
        src = x_ref.at[r]
        pltpu.make_async_copy(src, out_ref.at[s1_ref[0, 0, r]], sem.at[0]).start(0)
        pltpu.make_async_copy(src, out_ref.at[s2_ref[0, 0, r]], sem.at[1]).start(1)
        return carry

    lax.fori_loop(0, tm, issue, 0, unroll=8)
    pltpu.make_async_copy(x_ref, out_ref.at[pl.ds(0, tm)], sem.at[0]).wait()
    pltpu.make_async_copy(x_ref, out_ref.at[pl.ds(0, tm)], sem.at[1]).wait()


def _dispatch(xn, slot1, slot2, n_slots):
    t = xn.shape[0]
    tm = _pick(t, (512, 256, 128, 64, 32, 16, 8))
    nt = t // tm
    tile = xn.shape[1:]
    smem = pl.BlockSpec((1, 1, tm), lambda i: (i, 0, 0), memory_space=pltpu.SMEM)
    return pl.pallas_call(
        functools.partial(_dispatch_body, tm=tm),
        out_shape=jax.ShapeDtypeStruct((n_slots,) + tile, F32),
        grid=(nt,),
        in_specs=[smem, smem, pl.BlockSpec((tm,) + tile, lambda i: (i, 0, 0)), pl.BlockSpec(memory_space=pl.ANY)],
        out_specs=pl.BlockSpec(memory_space=pl.ANY),
        scratch_shapes=[pltpu.SemaphoreType.DMA((2,))],
        input_output_aliases={3: 0},
        compiler_params=_params("arbitrary"),
        name="moe_dispatch",
    )(slot1.reshape(nt, 1, tm), slot2.reshape(nt, 1, tm), xn, jnp.zeros((n_slots,) + tile, F32))


def _experts_body(te_ref, nu_ref, x_ref, wg_hbm, wu_hbm, wd_hbm, o_ref, wg_ref, wu_ref, wd_ref, stage_in, stage_out,
                  sem, *, tm, tf):
    i = pl.program_id(0)
    active = i < nu_ref[0]
    expert = te_ref[i]
    changed = (i == 0) | (expert != te_ref[jnp.maximum(i - 1, 0)])

    @pl.when(active & changed)
    def _():
        _load_bf16(wg_hbm.at[expert], wg_ref, stage_in, sem)
        _load_bf16(wu_hbm.at[expert], wu_ref, stage_in, sem)
        _load_bf16(wd_hbm.at[expert], wd_ref, stage_out, sem)

    @pl.when(active)
    def _():
        xb = _tiles_to_rows(x_ref, tm).astype(BF16)
        acc = None
        for c in range(wg_ref.shape[1] // tf):
            cols = slice(c * tf, (c + 1) * tf)
            gate = jnp.dot(xb, wg_ref[:, cols], preferred_element_type=F32)
            up = jnp.dot(xb, wu_ref[:, cols], preferred_element_type=F32)
            act = (gate * jax.nn.sigmoid(gate) * up).astype(BF16)
            d = jnp.dot(act, wd_ref[cols, :], preferred_element_type=F32)
            acc = d if acc is None else acc + d
        _rows_to_tiles(o_ref, acc)

    @pl.when(jnp.logical_not(active))
    def _():
        o_ref[...] = jnp.zeros_like(o_ref)


def _experts(xs, tile_expert, n_used, w_gate, w_up, w_down, tm):
    n_slots = xs.shape[0] // SUBLANES
    d = SUBLANES * LANES
    f = w_gate.shape[2]
    tf = _pick(f, (1792, 1024, 512, 256, 128))
    n_tiles = n_slots // tm
    anywhere = pl.BlockSpec(memory_space=pl.ANY)
    grid_spec = pltpu.PrefetchScalarGridSpec(
        num_scalar_prefetch=2,
        grid=(n_tiles,),
        in_specs=[pl.BlockSpec((tm * SUBLANES, LANES), lambda i, te, nu: (jnp.minimum(i, nu[0] - 1), 0)),
                  anywhere, anywhere, anywhere],
        out_specs=pl.BlockSpec((tm * SUBLANES, LANES), lambda i, te, nu: (i, 0)),
        scratch_shapes=[pltpu.VMEM((d, f), BF16), pltpu.VMEM((d, f), BF16), pltpu.VMEM((f, d), BF16),
                        pltpu.VMEM((STAGE_SLOTS, _stage_rows(d, f), f), F32),
                        pltpu.VMEM((STAGE_SLOTS, _stage_rows(f, d), d), F32),
                        pltpu.SemaphoreType.DMA((STAGE_SLOTS,))],
    )
    return pl.pallas_call(
        functools.partial(_experts_body, tm=tm, tf=tf),
        out_shape=jax.ShapeDtypeStruct((n_slots * SUBLANES, LANES), F32),
        grid_spec=grid_spec,
        compiler_params=_params("arbitrary"),
        name="moe_experts",
    )(tile_expert, n_used, xs, w_gate, w_up, w_down)


def _gather_norm_body(s1_ref, s2_ref, n1_ref, n2_ref, h_ref, gm_ref, g_ref, ys_ref, oa_ref, ob_ref, ya_ref, yb_ref,
                      sem, *, tm, head_tiles, n_steps):
    i = pl.program_id(0)
    cur = i % 2

    def gather(a_ref, b_ref, buf):
        def issue(r, carry):
            dst = pl.ds(pl.multiple_of(r * SUBLANES, SUBLANES), SUBLANES)
            pltpu.make_async_copy(ys_ref.at[a_ref[0, 0, r]], ya_ref.at[buf, dst], sem.at[buf, 0]).start(0)
            pltpu.make_async_copy(ys_ref.at[b_ref[0, 0, r]], yb_ref.at[buf, dst], sem.at[buf, 1]).start(1)
            return carry

        lax.fori_loop(0, tm, issue, 0, unroll=8)

    @pl.when(i == 0)
    def _():
        gather(s1_ref, s2_ref, 0)

    @pl.when(i + 1 < n_steps)
    def _():
        gather(n1_ref, n2_ref, 1 - cur)

    pltpu.make_async_copy(ya_ref.at[cur], ya_ref.at[cur], sem.at[cur, 0]).wait()
    pltpu.make_async_copy(yb_ref.at[cur], yb_ref.at[cur], sem.at[cur, 1]).wait()
    gm = gm_ref[...]
    y = _rms(h_ref[...] + (gm[:, 0:1] * _tiles_to_rows(ya_ref.at[cur], tm)
                           + gm[:, 1:2] * _tiles_to_rows(yb_ref.at[cur], tm)), g_ref[...])

    @pl.when(i < head_tiles)
    def _():
        oa_ref[...] = y

    @pl.when(i >= head_tiles)
    def _():
        ob_ref[...] = y


def _gather_norm(h, gates, slot1, slot2, ys, gain, split):
    t, d = h.shape
    tm = _pick(split, (512, 256, 128, 64, 32, 16, 8))
    assert t % tm == 0
    nt = t // tm
    head_tiles = split // tm
    smem = pl.BlockSpec((1, 1, tm), lambda i: (i, 0, 0), memory_space=pltpu.SMEM)
    smem_next = pl.BlockSpec((1, 1, tm), lambda i: (jnp.minimum(i + 1, nt - 1), 0, 0), memory_space=pltpu.SMEM)
    s1, s2 = slot1.reshape(nt, 1, tm), slot2.reshape(nt, 1, tm)
    return pl.pallas_call(
        functools.partial(_gather_norm_body, tm=tm, head_tiles=head_tiles, n_steps=nt),
        out_shape=[jax.ShapeDtypeStruct((split, d), F32), jax.ShapeDtypeStruct((t - split, d), F32)],
        grid=(nt,),
        in_specs=[smem, smem, smem_next, smem_next, pl.BlockSpec((tm, d), lambda i: (i, 0)),
                  pl.BlockSpec((tm, LANES), lambda i: (i, 0)),
                  pl.BlockSpec((1, d), lambda i: (0, 0)),
                  pl.BlockSpec(memory_space=pl.ANY)],
        out_specs=[pl.BlockSpec((tm, d), lambda i: (jnp.minimum(i, head_tiles - 1), 0)),
                   pl.BlockSpec((tm, d), lambda i: (jnp.maximum(i - head_tiles, 0), 0))],
        scratch_shapes=[pltpu.VMEM((2, tm * SUBLANES, LANES), F32), pltpu.VMEM((2, tm * SUBLANES, LANES), F32),
                        pltpu.SemaphoreType.DMA((2, 2))],
        compiler_params=_params("arbitrary"),
        name="moe_gather_norm",
    )(s1, s2, s1, s2, h, gates, gain.reshape(1, -1), ys)


def _moe(h, gain, w_router, w_gate, w_up, w_down, final_gain, split):
    t, _ = h.shape
    n_exp = w_router.shape[1]
    tm_e = 512
    xn, imeta, gates, counts = _router(h, gain, w_router)
    idx1, idx2, rank1, rank2 = imeta[:, 0], imeta[:, 1], imeta[:, 2], imeta[:, 3]
    cnt = counts[0, :n_exp]
    padded = ((cnt + tm_e - 1) // tm_e) * tm_e
    ends = jnp.cumsum(padded)
    starts = ends - padded
    slot1 = starts[idx1] + rank1
    slot2 = starts[idx2] + rank2
    n_tiles = (TOP_K * t + n_exp * (tm_e - 1)) // tm_e
    n_used = (ends[-1] // tm_e).astype(I32)
    tile_start = jnp.arange(n_tiles, dtype=I32) * tm_e
    tile_expert = jnp.sum((tile_start[:, None] >= ends[None, :]).astype(I32), axis=1)
    tile_expert = jnp.minimum(tile_expert, n_exp - 1)
    assert h.shape[1] == SUBLANES * LANES
    n_slots = n_tiles * tm_e
    xs = _dispatch(xn.reshape(t, SUBLANES, LANES), slot1, slot2, n_slots)
    ys = _experts(xs.reshape(n_slots * SUBLANES, LANES), tile_expert, n_used.reshape(1), w_gate, w_up, w_down, tm_e)
    return _gather_norm(h, gates, slot1, slot2, ys.reshape(n_slots, SUBLANES, LANES), final_gain, split)


def _cache_to_slabs(cache):
    n, w, two, heads, hd = cache.shape
    return jnp.transpose(cache, (0, 2, 3, 4, 1)).reshape(n, two, heads * hd, w)


def _slabs_to_cache(slabs, heads):
    n, two, _, w = slabs.shape
    return jnp.transpose(slabs.reshape(n, two, heads, HEAD_DIM, w), (0, 4, 1, 2, 3))


def kernel(x_prompt, x_sample, cache_conv, cache_swa_kv, cache_dil_kv0, cache_dil_kv1, cache_dil_kv2, norm_mix0, w_in0, conv_w, swa_sink, w_out0, norm_ffn0, w_gate0, w_up0, w_down0, norm_mix1, w_in1, w_out1, norm_ffn1, w_router, w_gate1, w_up1, w_down1, norm_final):
    n_p, seq, d = x_prompt.shape
    n_s, s_len, _ = x_sample.shape
    tp, ts = n_p * seq, n_s * s_len
    d_conv = conv_w.shape[2]
    kvh, grp = swa_sink.shape[1], swa_sink.shape[2]
    hq0 = kvh * grp
    h1 = cache_dil_kv0.shape[4]
    c_q0, c_kv0 = hq0 * HEAD_DIM, kvh * HEAD_DIM
    q0_col = 3 * d_conv
    k0_col = q0_col + c_q0
    c_g = 3 * h1 * HEAD_DIM
    dil_caches = (cache_dil_kv0, cache_dil_kv1, cache_dil_kv2)

    x_rows = (x_prompt.reshape(tp, d), x_sample.reshape(ts, d))

    assert c_kv0 == LANES and grp % 2 == 0 and d_conv % LANES == 0
    nsc = d_conv // LANES
    q0_slab = 3 * nsc
    k0_slab = q0_slab + c_q0 // LANES
    proj0 = _dense([x_rows], w_in0[0], gain=norm_mix0[0], out_dtype=F32, name="in_proj0", slab_out=True)
    a_p, conv_tail = _conv_prompt(proj0, conv_w[0], n_seq=n_p, seq_len=seq, c=d_conv)
    att_p = _band_attn(proj0, n_seq=n_p, seq_len=seq, dil=1, q_slab=q0_slab, k_slab=k0_slab, v_slab=k0_slab + 1,
                       n_q_slabs=c_q0 // LANES, shared_kv=True, sink=swa_sink[0], want_lse=False,
                       name="swa_prompt")[0]
    proj0_s = proj0[:, tp:]
    p3 = proj0_s[:q0_slab].reshape(3, nsc, n_s, s_len, LANES)
    p3 = jnp.transpose(p3, (0, 3, 2, 1, 4)).reshape(3, s_len, n_s, d_conv)
    a_s, conv_new = _conv_step(p3, jnp.transpose(cache_conv[0], (1, 0, 2)), conv_w[0])
    a_s = jnp.transpose(a_s, (1, 0, 2)).reshape(ts, d_conv)
    kvt0 = _proj_t(jnp.transpose(w_in0[0][:, k0_col:]), x_rows[1], norm_mix0[0], "kv_t0")
    q_s = proj0_s[q0_slab:k0_slab].reshape(kvh, grp // 2, n_s, s_len, 2, HEAD_DIM)
    q_s = jnp.transpose(q_s, (2, 3, 1, 4, 0, 5)).reshape(n_s, s_len * grp, c_kv0)
    sink_col = jnp.broadcast_to(swa_sink[0][:, None, :], (kvh, s_len, grp)).reshape(kvh * s_len * grp, 1)
    o_s, swa_new = _step_attn(q_s, kvt0.reshape(2, c_kv0, ts), _cache_to_slabs(cache_swa_kv[0]), dil=1,
                              tok_div=grp, sink_col=sink_col, want_lse=False, bsz=8, name="swa_step")
    att_s = jnp.transpose(o_s.reshape(n_s, s_len, grp, kvh, HEAD_DIM), (0, 1, 3, 2, 4)).reshape(ts, c_q0)
    h = _dense([(a_p, a_s.astype(BF16)), (att_p, att_s.astype(BF16))], w_out0[0], res=x_rows, out_dtype=F32,
               name="out_proj0")
    h = _ffn(h, norm_ffn0[0], w_gate0[0], w_up0[0], w_down0[0], "ffn0")

    proj1 = _dense([h], w_in1[0], gain=norm_mix1[0], out_dtype=F32, name="in_proj1", slab_out=True)
    proj1_s = proj1[:, tp:]
    c_h = h1 * HEAD_DIM
    nsh = c_h // LANES
    pairs_p, pairs_s, dil_p, dil_s = [], [], [], []
    for g, dil in enumerate(DILATIONS):
        window = dil_caches[g].shape[2]
        s0 = g * 3 * nsh
        pairs_p.append(tuple(_band_attn(proj1, n_seq=n_p, seq_len=seq, dil=dil, q_slab=s0, k_slab=s0 + nsh,
                                        v_slab=s0 + 2 * nsh, n_q_slabs=nsh, want_lse=True,
                                        name=f"dil{g}_prompt")))
        keep = min(window, seq)
        state = _state_slabs(proj1, first_slab=s0 + nsh, n_slabs=2 * nsh, n_seq=n_p, seq_len=seq, keep=keep,
                             name=f"dil{g}_state")
        dil_p.append(_slabs_to_cache(state.reshape(n_p, 2, c_h, keep), h1)[None])
        lo = g * c_g
        q_g = jnp.transpose(proj1_s[s0:s0 + nsh].reshape(nsh, n_s, s_len, LANES), (1, 2, 0, 3))
        q_g = q_g.reshape(n_s, s_len, c_h)
        q_g = jnp.concatenate([q_g, jnp.zeros((n_s, SUBLANES - s_len, c_h), F32)], axis=1)
        kvt = _proj_t(jnp.transpose(w_in1[0][:, lo + c_h:lo + c_g]), h[tp:], norm_mix1[0], f"kv_t1_{g}")
        o_sg, l_sg, cache_new = _step_attn(q_g, kvt.reshape(2, c_h, ts), _cache_to_slabs(dil_caches[g][0]),
                                           dil=dil, tok_div=1, want_lse=True,
                                           bsz=max(1, 2048 // window), name=f"dil{g}_step")
        pairs_s.append((o_sg.reshape(n_s * SUBLANES, c_h), l_sg.reshape(n_s * SUBLANES, c_h)))
        dil_s.append(_slabs_to_cache(cache_new, h1)[None])
    comb_s = _combine(pairs_s, "combine_step").reshape(n_s, SUBLANES, c_h)[:, :s_len].reshape(ts, c_h)
    h = _combine_proj(pairs_p, comb_s, w_out1[0], h, "out_proj1")
    y_p, y_s = _moe(h, norm_ffn1[0], w_router[0], w_gate1[0], w_up1[0], w_down1[0], norm_final, tp)

    y_prompt = y_p.reshape(n_p, seq, d)
    y_sample = y_s.reshape(n_s, s_len, d)
    new_conv_prompt = conv_tail[:, SUBLANES - (CONV_W - 1):][None]
    new_conv_sample = jnp.transpose(conv_new, (1, 0, 2))[None]
    keep0 = min(SPAN, seq)
    swa_state = _state_slabs(proj0, first_slab=k0_slab, n_slabs=2, n_seq=n_p, seq_len=seq, keep=keep0,
                             name="swa_state")
    new_swa_kv_prompt = _slabs_to_cache(swa_state, kvh)[None]
    new_swa_kv_sample = _slabs_to_cache(swa_new, kvh)[None]
    return (y_prompt, y_sample, new_conv_prompt, new_conv_sample, new_swa_kv_prompt, new_swa_kv_sample,
            dil_p[0], dil_s[0], dil_p[1], dil_s[1], dil_p[2], dil_s[2])
```

```python
import functools

import jax
import jax.numpy as jnp
from jax import lax
from jax.experimental import pallas as pl
from jax.experimental.pallas import tpu as pltpu

F32 = jnp.float32
BF16 = jnp.bfloat16
I32 = jnp.int32

EPS = 1e-5
HEAD_DIM = 64
SPAN = 128
CONV_W = 3
DILATIONS = (1, 4, 16)
TOP_K = 2
LANES = 128
SUBLANES = 8
VMEM_LIMIT_BYTES = 56 * 1024 * 1024
NT_DIMS = (((1,), (1,)), ((), ()))


def _params(*sem):
    return pltpu.CompilerParams(dimension_semantics=sem, vmem_limit_bytes=VMEM_LIMIT_BYTES)


def _pick(n, candidates):
    for c in candidates:
        if n % c == 0:
            return c
    raise ValueError(f"no tile for {n} in {candidates}")


def _rms(x, g):
    y = x * lax.rsqrt(jnp.mean(x * x, axis=-1, keepdims=True) + EPS)
    return y * g


STAGE_SLOTS = 4
STAGE_BYTES = 2 * 1024 * 1024


def _stage_rows(k, n):
    rows = k
    while rows * n * 4 > STAGE_BYTES and rows % 32 == 0:
        rows //= 2
    return rows


def _load_bf16(w_hbm, w_vmem, stage, sem):
    n_slots, chunk = stage.shape[0], stage.shape[1]
    n_chunks = w_hbm.shape[0] // chunk

    def copy(c):
        return pltpu.make_async_copy(w_hbm.at[pl.ds(c * chunk, chunk), :], stage.at[c % n_slots],
                                     sem.at[c % n_slots])

    for c in range(min(n_slots - 1, n_chunks)):
        copy(c).start(c % 2)
    for c in range(n_chunks):
        if c + n_slots - 1 < n_chunks:
            copy(c + n_slots - 1).start((c + n_slots - 1) % 2)
        copy(c).wait()
        w_vmem[c * chunk:(c + 1) * chunk, :] = stage[c % n_slots].astype(BF16)


def _dense_body(*refs, ks, split, head_tiles, has_gain, has_res, slab_out, cn):
    n_in = len(ks)
    step = pl.program_id(0)
    refs = list(refs)

    def take(is_split):
        count = 2 if is_split else 1
        parts = tuple(refs[:count])
        del refs[:count]
        return parts

    def read(parts, cols=slice(None)):
        if len(parts) == 1:
            return parts[0][:, cols]
        return jnp.where(step < head_tiles, parts[0][:, cols], parts[1][:, cols])

    xs = [take(s) for s in split[:n_in]]
    g_ref = refs.pop(0) if has_gain else None
    w_hbm = refs.pop(0)
    res = take(split[n_in]) if has_res else None
    o_ref, w_ref, stage, sem = refs

    @pl.when(step == 0)
    def _():
        _load_bf16(w_hbm, w_ref, stage, sem)

    if has_gain:
        lhs = [_rms(read(xs[0]), g_ref[...]).astype(BF16)]
    else:
        lhs = [read(x).astype(BF16) for x in xs]
    n = w_ref.shape[1]
    for c in range(n // cn):
        cols = slice(c * cn, (c + 1) * cn)
        acc = None
        row0 = 0
        for a, k in zip(lhs, ks):
            d = jnp.dot(a, w_ref[row0:row0 + k, cols], preferred_element_type=F32)
            acc = d if acc is None else acc + d
            row0 += k
        if has_res:
            acc = read(res, cols) + acc
        if slab_out:
            for s in range(cn // LANES):
                o_ref[c * (cn // LANES) + s] = acc[:, s * LANES:(s + 1) * LANES].astype(o_ref.dtype)
        else:
            o_ref[:, cols] = acc.astype(o_ref.dtype)


def _dense(xs, w, *, gain=None, res=None, out_dtype, name, slab_out=False):
    def n_rows(a):
        return sum(p.shape[0] for p in a) if isinstance(a, tuple) else a.shape[0]

    def width(a):
        return a[0].shape[1] if isinstance(a, tuple) else a.shape[1]

    operands = list(xs) + ([res] if res is not None else [])
    pairs = [a for a in operands if isinstance(a, tuple)]
    t = n_rows(xs[0])
    k_all, n = w.shape
    ks = tuple(width(x) for x in xs)
    assert sum(ks) == k_all
    gcd_rows = t
    for a in pairs:
        gcd_rows = min(gcd_rows, a[1].shape[0])
    tm = _pick(gcd_rows, (512, 256, 128, 64, 32, 16, 8))
    assert t % tm == 0
    head_tiles = pairs[0][0].shape[0] // tm if pairs else 0
    for a in pairs:
        assert a[0].shape[0] == head_tiles * tm
    cn = _pick(n, (512, 384, 256, 128))
    in_specs, args = [], []

    def add_rows(a, k):
        if isinstance(a, tuple):
            in_specs.append(pl.BlockSpec((tm, k), lambda i: (jnp.minimum(i, head_tiles - 1), 0)))
            in_specs.append(pl.BlockSpec((tm, k), lambda i: (jnp.maximum(i - head_tiles, 0), 0)))
            args.extend(a)
        else:
            in_specs.append(pl.BlockSpec((tm, k), lambda i: (i, 0)))
            args.append(a)

    for x, k in zip(xs, ks):
        add_rows(x, k)
    if gain is not None:
        in_specs.append(pl.BlockSpec((1, ks[0]), lambda i: (0, 0)))
        args.append(gain.reshape(1, -1))
    in_specs.append(pl.BlockSpec(memory_space=pl.ANY))
    args.append(w)
    if res is not None:
        add_rows(res, n)
    split = tuple(isinstance(a, tuple) for a in operands)
    if slab_out:
        out_shape = jax.ShapeDtypeStruct((n // LANES, t, LANES), out_dtype)
        out_spec = pl.BlockSpec((n // LANES, tm, LANES), lambda i: (0, i, 0))
    else:
        out_shape = jax.ShapeDtypeStruct((t, n), out_dtype)
        out_spec = pl.BlockSpec((tm, n), lambda i: (i, 0))
    return pl.pallas_call(
        functools.partial(_dense_body, ks=ks, split=split, head_tiles=head_tiles, has_gain=gain is not None,
                          has_res=res is not None, slab_out=slab_out, cn=cn),
        out_shape=out_shape,
        grid=(t // tm,),
        in_specs=in_specs,
        out_specs=out_spec,
        scratch_shapes=[pltpu.VMEM((k_all, n), BF16), pltpu.VMEM((STAGE_SLOTS, _stage_rows(k_all, n), n), F32),
                        pltpu.SemaphoreType.DMA((STAGE_SLOTS,))],
        compiler_params=_params("arbitrary"),
        name=name,
    )(*args)


def _proj_t_body(w_ref, x_ref, g_ref, o_ref):
    xn = _rms(x_ref[...], g_ref[...]).astype(BF16)
    o_ref[...] = lax.dot_general(w_ref[...].astype(BF16), xn, NT_DIMS, preferred_element_type=F32)


def _proj_t(w_t, x, gain, name):
    c, k = w_t.shape
    rows = x.shape[0]
    tc = _pick(c, (512, 256, 128))
    return pl.pallas_call(
        _proj_t_body,
        out_shape=jax.ShapeDtypeStruct((c, rows), F32),
        grid=(c // tc,),
        in_specs=[pl.BlockSpec((tc, k), lambda i: (i, 0)),
                  pl.BlockSpec((rows, k), lambda i: (0, 0)),
                  pl.BlockSpec((1, k), lambda i: (0, 0))],
        out_specs=pl.BlockSpec((tc, rows), lambda i: (i, 0)),
        compiler_params=_params("parallel"),
        name=name,
    )(w_t, x, gain.reshape(1, -1))


def _ffn_body(x_ref, g_ref, wg_hbm, wu_hbm, wd_hbm, o_ref, wg_ref, wu_ref, wd_ref, stage_in, stage_out, sem, *,
              tf):
    @pl.when(pl.program_id(0) == 0)
    def _():
        _load_bf16(wg_hbm, wg_ref, stage_in, sem)
        _load_bf16(wu_hbm, wu_ref, stage_in, sem)
        _load_bf16(wd_hbm, wd_ref, stage_out, sem)

    x = x_ref[...]
    xn = _rms(x, g_ref[...]).astype(BF16)
    acc = None
    for c in range(wg_ref.shape[1] // tf):
        cols = slice(c * tf, (c + 1) * tf)
        gate = jnp.dot(xn, wg_ref[:, cols], preferred_element_type=F32)
        up = jnp.dot(xn, wu_ref[:, cols], preferred_element_type=F32)
        act = (gate * jax.nn.sigmoid(gate) * up).astype(BF16)
        d = jnp.dot(act, wd_ref[cols, :], preferred_element_type=F32)
        acc = d if acc is None else acc + d
    o_ref[...] = x + acc


def _ffn(x, gain, w_gate, w_up, w_down, name):
    t, d = x.shape
    f = w_gate.shape[1]
    tm = _pick(t, (512, 256, 128, 64, 32, 16, 8))
    tf = f
    anywhere = pl.BlockSpec(memory_space=pl.ANY)
    return pl.pallas_call(
        functools.partial(_ffn_body, tf=tf),
        out_shape=jax.ShapeDtypeStruct((t, d), F32),
        grid=(t // tm,),
        in_specs=[pl.BlockSpec((tm, d), lambda i: (i, 0)),
                  pl.BlockSpec((1, d), lambda i: (0, 0)),
                  anywhere, anywhere, anywhere],
        out_specs=pl.BlockSpec((tm, d), lambda i: (i, 0)),
        scratch_shapes=[pltpu.VMEM((d, f), BF16), pltpu.VMEM((d, f), BF16), pltpu.VMEM((f, d), BF16),
                        pltpu.VMEM((STAGE_SLOTS, _stage_rows(d, f), f), F32),
                        pltpu.VMEM((STAGE_SLOTS, _stage_rows(f, d), d), F32),
                        pltpu.SemaphoreType.DMA((STAGE_SLOTS,))],
        compiler_params=_params("arbitrary"),
        name=name,
    )(x, gain.reshape(1, -1), w_gate, w_up, w_down)


def _band_body(*refs, dil, mb, shared_kv, has_sink, want_lse):
    q_ref, kc_ref, kp_ref, vc_ref, vp_ref = refs[:5]
    pos = 5
    sink_ref = refs[pos] if has_sink else None
    pos += int(has_sink)
    o_ref = refs[pos]
    lse_ref = refs[pos + 1] if want_lse else None
    t = pl.program_id(1)
    scale = HEAD_DIM ** -0.5
    qi = lax.broadcasted_iota(I32, (2 * SPAN, 2 * SPAN), 0) & (SPAN - 1)
    kj = lax.broadcasted_iota(I32, (2 * SPAN, 2 * SPAN), 1)
    band = (kj >= qi) & (kj <= qi + SPAN)
    band_first = band & (kj >= jnp.where(t == 0, SPAN, 0))
    lane = lax.broadcasted_iota(I32, (1, LANES), 1)
    halves = [lane < HEAD_DIM, lane >= HEAD_DIM]

    def rows_of(ref, s, r, count):
        if dil == 1:
            return ref[s, 0:count, :]
        return ref[s, pl.ds(r, count, stride=dil), :]

    def pair(q_a, mask_a, q_b, mask_b, kw, vw, msk, heads):
        q2 = jnp.concatenate([jnp.where(mask_a, q_a, 0.0), jnp.where(mask_b, q_b, 0.0)], axis=0).astype(BF16)
        s = lax.dot_general(q2, kw, NT_DIMS, preferred_element_type=F32)
        s = jnp.where(msk, s, -jnp.inf)
        m = jnp.max(s, axis=-1, keepdims=True)
        if has_sink:
            sink = jnp.concatenate([jnp.full((SPAN, 1), sink_ref[h], F32) for h in heads], axis=0)
            m = jnp.maximum(m, sink)
        p = jnp.exp(s - m)
        den = jnp.sum(p, axis=-1, keepdims=True)
        if has_sink:
            den = den + jnp.exp(sink - m)
        o = jnp.dot(p.astype(BF16), vw, preferred_element_type=F32) / den
        return o, m + jnp.log(den)

    def store(s, r, b, o_tile, lse_tile):
        if dil == 1:
            o_ref[b * SPAN:(b + 1) * SPAN, s * LANES:(s + 1) * LANES] = o_tile.astype(o_ref.dtype)
            if want_lse:
                lse_ref[b * SPAN:(b + 1) * SPAN, s * LANES:(s + 1) * LANES] = lse_tile
        else:
            o_ref[s, pl.ds(r + dil * b * SPAN, SPAN, stride=dil), :] = o_tile
            if want_lse:
                lse_ref[s, pl.ds(r + dil * b * SPAN, SPAN, stride=dil), :] = lse_tile

    n_slabs = q_ref.shape[0]
    first_head = 2 * n_slabs * pl.program_id(2)
    top, bot = slice(0, SPAN), slice(SPAN, 2 * SPAN)
    for r in range(dil):
        qs = [rows_of(q_ref, s, r, mb * SPAN) * scale for s in range(n_slabs)]
        kf = [jnp.concatenate([rows_of(kp_ref, s, r, SPAN), rows_of(kc_ref, s, r, mb * SPAN)], axis=0)
              for s in range(kc_ref.shape[0])]
        vf = [jnp.concatenate([rows_of(vp_ref, s, r, SPAN), rows_of(vc_ref, s, r, mb * SPAN)], axis=0)
              for s in range(vc_ref.shape[0])]
        ks = [k.astype(BF16) for k in kf]
        vs = [v.astype(BF16) for v in vf]
        if shared_kv:
            k_sw = pltpu.roll(kf[0], HEAD_DIM, 1).astype(BF16)
            v_sw = pltpu.roll(vf[0], HEAD_DIM, 1).astype(BF16)
        for b in range(mb):
            blk = slice(b * SPAN, (b + 1) * SPAN)
            keys = slice(b * SPAN, (b + 2) * SPAN)
            msk = band_first if b == 0 else band
            if not shared_kv:
                for s in range(n_slabs):
                    o, lse = pair(qs[s][blk], halves[0], qs[s][blk], halves[1], ks[s][keys], vs[s][keys], msk,
                                  (first_head + 2 * s, first_head + 2 * s + 1))
                    store(s, r, b, jnp.where(halves[0], o[top], o[bot]), jnp.where(halves[0], lse[top], lse[bot]))
            else:
                for kvh in range(2):
                    sa, sb = 2 * kvh, 2 * kvh + 1
                    o_al, l_al = pair(qs[sa][blk], halves[kvh], qs[sb][blk], halves[kvh], ks[0][keys],
                                      vs[0][keys], msk, (2 * sa + kvh, 2 * sb + kvh))
                    o_sw, l_sw = pair(qs[sa][blk], halves[1 - kvh], qs[sb][blk], halves[1 - kvh], k_sw[keys],
                                      v_sw[keys], msk, (2 * sa + 1 - kvh, 2 * sb + 1 - kvh))
                    store(sa, r, b, jnp.where(halves[kvh], o_al[top], o_sw[top]),
                          jnp.where(halves[kvh], l_al[top], l_sw[top]))
                    store(sb, r, b, jnp.where(halves[kvh], o_al[bot], o_sw[bot]),
                          jnp.where(halves[kvh], l_al[bot], l_sw[bot]))


def _band_attn(slabs, *, n_seq, seq_len, dil, q_slab, k_slab, v_slab, n_q_slabs, shared_kv=False, sink=None,
               want_lse, name):
    mb = max(1, 512 // (dil * SPAN))
    tp = dil * SPAN * mb
    nt = seq_len // tp
    assert seq_len % tp == 0 and (not shared_kv or (n_q_slabs == 4 and dil == 1))
    prev_rows = dil * SPAN
    per_step = n_q_slabs if (shared_kv or tp <= 512) else 1
    kv_block = 1 if shared_kv else per_step
    kv_step = 0 if shared_kv else 1

    def cur(base, step):
        return lambda n, t, s: (base + s * step, n * nt + t, 0)

    def prev(base, step):
        return lambda n, t, s: (base + s * step, jnp.maximum((n * nt + t) * mb - 1, 0), 0)

    in_specs = [pl.BlockSpec((per_step, tp, LANES), cur(q_slab // per_step, 1)),
                pl.BlockSpec((kv_block, tp, LANES), cur(k_slab // kv_block, kv_step)),
                pl.BlockSpec((kv_block, prev_rows, LANES), prev(k_slab // kv_block, kv_step)),
                pl.BlockSpec((kv_block, tp, LANES), cur(v_slab // kv_block, kv_step)),
                pl.BlockSpec((kv_block, prev_rows, LANES), prev(v_slab // kv_block, kv_step))]
    assert q_slab % per_step == 0 and k_slab % kv_block == 0 and v_slab % kv_block == 0
    args = [slabs] * 5
    if sink is not None:
        in_specs.append(pl.BlockSpec(memory_space=pltpu.SMEM))
        args.append(sink.reshape(-1).astype(F32))
    rows = n_seq * seq_len
    if dil == 1:
        spec = pl.BlockSpec((tp, per_step * LANES), lambda n, t, s: (n * nt + t, s))
        out_shape = [jax.ShapeDtypeStruct((rows, n_q_slabs * LANES), BF16)]
        lse_shape = jax.ShapeDtypeStruct((rows, n_q_slabs * LANES), F32)
    else:
        spec = pl.BlockSpec((per_step, tp, LANES), lambda n, t, s: (s, n * nt + t, 0))
        out_shape = [jax.ShapeDtypeStruct((n_q_slabs, rows, LANES), F32)]
        lse_shape = jax.ShapeDtypeStruct((n_q_slabs, rows, LANES), F32)
    out_specs = [spec]
    if want_lse:
        out_shape.append(lse_shape)
        out_specs.append(spec)
    return pl.pallas_call(
        functools.partial(_band_body, dil=dil, mb=mb, shared_kv=shared_kv, has_sink=sink is not None,
                          want_lse=want_lse),
        out_shape=out_shape,
        grid=(n_seq, nt, n_q_slabs // per_step),
        in_specs=in_specs,
        out_specs=out_specs,
        compiler_params=_params("parallel", "parallel", "parallel"),
        name=name,
    )(*args)


def _combine_body(o0, l0, o1, l1, o2, l2, out_ref):
    def tile(ref):
        return ref[...].reshape(ref.shape[-2:]).astype(F32)

    a0, a1, a2 = tile(l0), tile(l1), tile(l2)
    m = jnp.maximum(jnp.maximum(a0, a1), a2)
    e0, e1, e2 = jnp.exp(a0 - m), jnp.exp(a1 - m), jnp.exp(a2 - m)
    num = e0 * tile(o0) + e1 * tile(o1) + e2 * tile(o2)
    out_ref[...] = (num / (e0 + e1 + e2)).astype(out_ref.dtype)


def _combine(pairs, name):
    first = pairs[0][0]
    rows, c = first.shape if first.ndim == 2 else (first.shape[1], first.shape[0] * LANES)
    tm = _pick(rows, (1024, 512, 256, 128, 64, 32, 16, 8))
    flat = pl.BlockSpec((tm, LANES), lambda i, s: (i, s))
    slab = pl.BlockSpec((1, tm, LANES), lambda i, s: (s, i, 0))
    args = [a for pair in pairs for a in pair]
    return pl.pallas_call(
        _combine_body,
        out_shape=jax.ShapeDtypeStruct((rows, c), BF16),
        grid=(rows // tm, c // LANES),
        in_specs=[flat if a.ndim == 2 else slab for a in args],
        out_specs=flat,
        compiler_params=_params("parallel", "parallel"),
        name=name,
    )(*args)


def _combine_proj_body(o0, l0, o1, l1, o2, l2, tail_ref, res_ref, w_hbm, out_ref, w_ref, stage, sem, *, head_tiles):
    step = pl.program_id(0)

    @pl.when(step == 0)
    def _():
        _load_bf16(w_hbm, w_ref, stage, sem)

    parts = []
    for s in range(o1.shape[0]):
        lanes = slice(s * LANES, (s + 1) * LANES)
        a0, a1, a2 = l0[:, lanes], l1[s], l2[s]
        m = jnp.maximum(jnp.maximum(a0, a1), a2)
        e0, e1, e2 = jnp.exp(a0 - m), jnp.exp(a1 - m), jnp.exp(a2 - m)
        num = e0 * o0[:, lanes].astype(F32) + e1 * o1[s] + e2 * o2[s]
        parts.append(num / (e0 + e1 + e2))
    comb = jnp.concatenate(parts, axis=1).astype(BF16)
    comb = jnp.where(step < head_tiles, comb, tail_ref[...])
    out_ref[...] = res_ref[...] + jnp.dot(comb, w_ref[...], preferred_element_type=F32)


def _combine_proj(pairs, tail, w, res, name):
    (o0, l0), (o1, l1), (o2, l2) = pairs
    head, c = o0.shape
    t, n = res.shape
    tm = _pick(tail.shape[0], (512, 256, 128, 64, 32, 16, 8))
    assert head % tm == 0 and t == head + tail.shape[0] and o1.ndim == 3 and o2.ndim == 3
    head_tiles = head // tm

    def first(i):
        return jnp.minimum(i, head_tiles - 1)

    flat = pl.BlockSpec((tm, c), lambda i: (first(i), 0))
    slab = pl.BlockSpec((c // LANES, tm, LANES), lambda i: (0, first(i), 0))
    return pl.pallas_call(
        functools.partial(_combine_proj_body, head_tiles=head_tiles),
        out_shape=jax.ShapeDtypeStruct((t, n), F32),
        grid=(t // tm,),
        in_specs=[flat, flat, slab, slab, slab, slab,
                  pl.BlockSpec((tm, c), lambda i: (jnp.maximum(i - head_tiles, 0), 0)),
                  pl.BlockSpec((tm, n), lambda i: (i, 0)),
                  pl.BlockSpec(memory_space=pl.ANY)],
        out_specs=pl.BlockSpec((tm, n), lambda i: (i, 0)),
        scratch_shapes=[pltpu.VMEM((c, n), BF16), pltpu.VMEM((STAGE_SLOTS, _stage_rows(c, n), n), F32),
                        pltpu.SemaphoreType.DMA((STAGE_SLOTS,))],
        compiler_params=_params("arbitrary"),
        name=name,
    )(o0, l0, o1, l1, o2, l2, tail, res, w)


def _state_body(x_ref, o_ref):
    o_ref[0, 0] = x_ref[0].T


def _state_slabs(slabs, *, first_slab, n_slabs, n_seq, seq_len, keep, name):
    assert seq_len % keep == 0
    per_seq = seq_len // keep
    return pl.pallas_call(
        _state_body,
        out_shape=jax.ShapeDtypeStruct((n_seq, n_slabs, LANES, keep), F32),
        grid=(n_seq, n_slabs),
        in_specs=[pl.BlockSpec((1, keep, LANES), lambda n, s: (first_slab + s, (n + 1) * per_seq - 1, 0))],
        out_specs=pl.BlockSpec((1, 1, LANES, keep), lambda n, s: (n, s, 0, 0)),
        compiler_params=_params("parallel", "parallel"),
        name=name,
    )(slabs)


def _conv_prompt_body(gb_ref, gc_ref, xa_ref, gcp_ref, xap_ref, w_ref, a_ref, st_ref):
    t = pl.program_id(1)
    w = w_ref[...]
    for s in range(gb_ref.shape[0]):
        lanes = slice(s * LANES, (s + 1) * LANES)
        u = gc_ref[s] * xa_ref[s]
        up = jnp.where(t == 0, 0.0, gcp_ref[s] * xap_ref[s])
        ext = jnp.concatenate([up, u], axis=0)
        y = (w[0:1, lanes] * pltpu.roll(ext, 2, 0)[SUBLANES:]
             + w[1:2, lanes] * pltpu.roll(ext, 1, 0)[SUBLANES:]) + w[2:3, lanes] * u
        a_ref[:, lanes] = (gb_ref[s] * y).astype(a_ref.dtype)
        st_ref[0, :, lanes] = u[u.shape[0] - SUBLANES:]


def _conv_prompt(slabs, conv_w, *, n_seq, seq_len, c):
    tq = _pick(seq_len, (512, 256, 128))
    nt = seq_len // tq
    rb = tq // SUBLANES
    ns = c // LANES

    def cur(part):
        return lambda n, t: (part, n * nt + t, 0)

    def prev(part):
        return lambda n, t: (part, jnp.maximum((n * nt + t) * rb - 1, 0), 0)

    return pl.pallas_call(
        _conv_prompt_body,
        out_shape=[jax.ShapeDtypeStruct((n_seq * seq_len, c), BF16),
                   jax.ShapeDtypeStruct((n_seq, SUBLANES, c), F32)],
        grid=(n_seq, nt),
        in_specs=[pl.BlockSpec((ns, tq, LANES), cur(0)), pl.BlockSpec((ns, tq, LANES), cur(1)),
                  pl.BlockSpec((ns, tq, LANES), cur(2)),
                  pl.BlockSpec((ns, SUBLANES, LANES), prev(1)), pl.BlockSpec((ns, SUBLANES, LANES), prev(2)),
                  pl.BlockSpec((CONV_W, c), lambda n, t: (0, 0))],
        out_specs=[pl.BlockSpec((tq, c), lambda n, t: (n * nt + t, 0)),
                   pl.BlockSpec((1, SUBLANES, c), lambda n, t: (n, 0, 0))],
        compiler_params=_params("parallel", "arbitrary"),
        name="conv_prompt",
    )(slabs, slabs, slabs, slabs, slabs, conv_w)


def _conv_step_body(p_ref, prev_ref, w_ref, a_ref, st_ref):
    s_len = p_ref.shape[1]
    w = w_ref[...]
    hist = [prev_ref[k] for k in range(CONV_W - 1)] + [p_ref[1, s] * p_ref[2, s] for s in range(s_len)]
    for s in range(s_len):
        y = (w[0:1] * hist[s] + w[1:2] * hist[s + 1]) + w[2:3] * hist[s + 2]
        a_ref[s] = p_ref[0, s] * y
    for k in range(CONV_W - 1):
        st_ref[k] = hist[s_len + k]


def _conv_step(p3, prev, conv_w):
    _, s_len, n, c = p3.shape
    return pl.pallas_call(
        _conv_step_body,
        out_shape=[jax.ShapeDtypeStruct((s_len, n, c), F32), jax.ShapeDtypeStruct((CONV_W - 1, n, c), F32)],
        name="conv_step",
        compiler_params=pltpu.CompilerParams(vmem_limit_bytes=VMEM_LIMIT_BYTES),
    )(p3, prev, conv_w)


def _step_body(*refs, n_blk, qr, tok_div, w, dil, cw, bsz, has_sink, want_lse):
    q_ref, kvt_ref, cache_ref = refs[:3]
    pos = 3
    sink_ref = refs[pos] if has_sink else None
    pos += int(has_sink)
    o_ref = refs[pos]
    pos += 1
    lse_ref = refs[pos] if want_lse else None
    pos += int(want_lse)
    cout_ref = refs[pos]
    r_dim = n_blk * HEAD_DIM
    nrb = n_blk * qr
    n_chunks = w // cw
    scale = HEAD_DIM ** -0.5
    step = pl.program_id(0)
    lane_r = lax.broadcasted_iota(I32, (1, r_dim), 1)
    blk_masks = [(lane_r >= j * HEAD_DIM) & (lane_r < (j + 1) * HEAD_DIM) for j in range(n_blk)]
    row = lax.broadcasted_iota(I32, (nrb, 1), 0)
    tok = (row % qr) // tok_div
    lane_c = lax.broadcasted_iota(I32, (1, cw), 1)
    lane_n = lax.broadcasted_iota(I32, (1, LANES), 1)

    def valid(pos_l):
        ok = (pos_l >= tok) & (pos_l <= w + tok)
        if dil > 1:
            ok = ok & (((pos_l - tok) & (dil - 1)) == 0)
        return ok

    new_lane0 = LANES - 4

    def shift_window(b, kv, new_tile):
        n_cols = w // LANES
        prev = pltpu.roll(cache_ref[b, kv, :, 0:LANES], new_lane0, 1)
        for j in range(n_cols):
            if j + 1 < n_cols:
                nxt = pltpu.roll(cache_ref[b, kv, :, (j + 1) * LANES:(j + 2) * LANES], new_lane0, 1)
            else:
                nxt = new_tile
            cout_ref[b, kv, :, j * LANES:(j + 1) * LANES] = jnp.where(lane_n < new_lane0, prev, nxt)
            prev = nxt

    for b in range(bsz):
        off = ((step * bsz + b) * 4) % LANES
        shift = (new_lane0 + LANES - off) % LANES
        new_k = pltpu.roll(kvt_ref[0], shift, 1)
        new_v = pltpu.roll(kvt_ref[1], shift, 1)
        shift_window(b, 0, new_k)
        shift_window(b, 1, new_v)
        q = q_ref[b] * scale
        qbd = jnp.concatenate([jnp.where(mk, q, 0.0) for mk in blk_masks], axis=0).astype(BF16)
        scores = []
        for c in range(n_chunks):
            lo, hi = c * cw, (c + 1) * cw
            kc = cache_ref[b, 0, :, lo:hi]
            sc = jnp.dot(qbd, kc.astype(BF16), preferred_element_type=F32)
            scores.append(jnp.where(valid(lane_c + lo), sc, -jnp.inf))
        sc = jnp.dot(qbd, new_k.astype(BF16), preferred_element_type=F32)
        scores.append(jnp.where(valid(lane_n + (w - new_lane0)) & (lane_n >= new_lane0), sc, -jnp.inf))
        m = functools.reduce(jnp.maximum, [jnp.max(s, axis=-1, keepdims=True) for s in scores])
        if has_sink:
            m = jnp.maximum(m, sink_ref[...])
        probs = [jnp.exp(s - m) for s in scores]
        den = functools.reduce(lambda a, c: a + c, [jnp.sum(p, axis=-1, keepdims=True) for p in probs])
        if has_sink:
            den = den + jnp.exp(sink_ref[...] - m)
        pv = lax.dot_general(probs[-1].astype(BF16), new_v.astype(BF16), NT_DIMS, preferred_element_type=F32)
        for c in range(n_chunks):
            lo, hi = c * cw, (c + 1) * cw
            vc = cache_ref[b, 1, :, lo:hi]
            pv = pv + lax.dot_general(probs[c].astype(BF16), vc.astype(BF16), NT_DIMS,
                                      preferred_element_type=F32)
        o = jnp.zeros((qr, r_dim), F32)
        m_e = jnp.zeros((qr, r_dim), F32)
        den_e = jnp.zeros((qr, r_dim), F32)
        for j, mk in enumerate(blk_masks):
            rows = slice(j * qr, (j + 1) * qr)
            o = jnp.where(mk, pv[rows], o)
            m_e = jnp.where(mk, m[rows], m_e)
            den_e = jnp.where(mk, den[rows], den_e)
        o_ref[b] = o / den_e
        if want_lse:
            lse_ref[b] = m_e + jnp.log(den_e)


def _step_attn(q, kvt, cache, *, dil, tok_div, sink_col=None, want_lse, bsz, name):
    n, qr, r_dim = q.shape
    w = cache.shape[-1]
    n_blk = r_dim // HEAD_DIM
    cw = min(w, 512)
    in_specs = [pl.BlockSpec((bsz, qr, r_dim), lambda i: (i, 0, 0)),
                pl.BlockSpec((2, r_dim, LANES), lambda i: (0, 0, (i * bsz * 4) // LANES)),
                pl.BlockSpec((bsz, 2, r_dim, w), lambda i: (i, 0, 0, 0))]
    args = [q, kvt, cache]
    if sink_col is not None:
        in_specs.append(pl.BlockSpec((n_blk * qr, 1), lambda i: (0, 0)))
        args.append(sink_col)
    o_spec = pl.BlockSpec((bsz, qr, r_dim), lambda i: (i, 0, 0))
    out_shape = [jax.ShapeDtypeStruct((n, qr, r_dim), F32)]
    out_specs = [o_spec]
    if want_lse:
        out_shape.append(jax.ShapeDtypeStruct((n, qr, r_dim), F32))
        out_specs.append(o_spec)
    out_shape.append(jax.ShapeDtypeStruct(cache.shape, F32))
    out_specs.append(pl.BlockSpec((bsz, 2, r_dim, w), lambda i: (i, 0, 0, 0)))
    return pl.pallas_call(
        functools.partial(_step_body, n_blk=n_blk, qr=qr, tok_div=tok_div, w=w, dil=dil, cw=cw, bsz=bsz,
                          has_sink=sink_col is not None, want_lse=want_lse),
        out_shape=out_shape,
        grid=(n // bsz,),
        in_specs=in_specs,
        out_specs=out_specs,
        compiler_params=_params("parallel"),
        name=name,
    )(*args)


def _rows_to_tiles(ref, x):
    rows = x.shape[0]
    for s in range(SUBLANES):
        ref[pl.ds(s, rows, stride=SUBLANES), :] = x[:, s * LANES:(s + 1) * LANES]


def _tiles_to_rows(ref, rows):
    return jnp.concatenate([ref[pl.ds(s, rows, stride=SUBLANES), :] for s in range(SUBLANES)], axis=1)


def _router_body(h_ref, g_ref, wr_ref, xn_ref, im_ref, gm_ref, cnt_ref, carry_ref, *, n_exp):
    i = pl.program_id(0)

    @pl.when(i == 0)
    def _():
        carry_ref[...] = jnp.zeros_like(carry_ref)

    xn = _rms(h_ref[...], g_ref[...])
    _rows_to_tiles(xn_ref, xn)
    tm = xn.shape[0]
    wr = wr_ref[...]
    xh = xn.astype(BF16)
    xl = (xn - xh.astype(F32)).astype(BF16)
    wh = wr.astype(BF16)
    wl = (wr - wh.astype(F32)).astype(BF16)
    lg = jnp.dot(xh, wh, preferred_element_type=F32) + (
        jnp.dot(xh, wl, preferred_element_type=F32) + jnp.dot(xl, wh, preferred_element_type=F32))
    lane = lax.broadcasted_iota(I32, (tm, LANES), 1)
    lane_f = lane.astype(F32)
    lg = jnp.where(lane < n_exp, lg, -jnp.inf)
    m1 = jnp.max(lg, axis=-1, keepdims=True)
    i1 = jnp.min(jnp.where(lg == m1, lane_f, float(LANES)), axis=-1, keepdims=True)
    lg2 = jnp.where(lane_f == i1, -jnp.inf, lg)
    m2 = jnp.max(lg2, axis=-1, keepdims=True)
    i2 = jnp.min(jnp.where(lg2 == m2, lane_f, float(LANES)), axis=-1, keepdims=True)
    e = jnp.exp(m2 - m1)
    g1 = 1.0 / (1.0 + e)
    g2 = e / (1.0 + e)
    sel1 = lane_f == i1
    sel2 = lane_f == i2
    onehot = jnp.where(sel1 | sel2, 1.0, 0.0)
    r_i = lax.broadcasted_iota(I32, (tm, tm), 0)
    c_i = lax.broadcasted_iota(I32, (tm, tm), 1)
    tri = jnp.where(c_i < r_i, 1.0, 0.0).astype(BF16)
    before = jnp.dot(tri, onehot.astype(BF16), preferred_element_type=F32) + carry_ref[0:1]
    r1 = jnp.sum(jnp.where(sel1, before, 0.0), axis=-1, keepdims=True)
    r2 = jnp.sum(jnp.where(sel2, before, 0.0), axis=-1, keepdims=True)
    total = carry_ref[0:1] + jnp.sum(onehot, axis=0, keepdims=True)
    carry_ref[...] = jnp.broadcast_to(total, carry_ref.shape)
    cnt_ref[...] = jnp.broadcast_to(total, cnt_ref.shape).astype(I32)
    meta = jnp.where(lane == 0, i1, jnp.where(lane == 1, i2, jnp.where(lane == 2, r1, jnp.where(lane == 3, r2, 0.0))))
    im_ref[...] = meta.astype(I32)
    gm_ref[...] = jnp.where(lane == 0, g1, jnp.where(lane == 1, g2, 0.0))


def _router(h, gain, w_router):
    t, d = h.shape
    n_exp = w_router.shape[1]
    tm = _pick(t, (512, 256, 128, 64, 32, 16, 8))
    wr = jnp.zeros((d, LANES), F32).at[:, :n_exp].set(w_router)
    return pl.pallas_call(
        functools.partial(_router_body, n_exp=n_exp),
        out_shape=[jax.ShapeDtypeStruct((t * SUBLANES, LANES), F32), jax.ShapeDtypeStruct((t, LANES), I32),
                   jax.ShapeDtypeStruct((t, LANES), F32), jax.ShapeDtypeStruct((SUBLANES, LANES), I32)],
        grid=(t // tm,),
        in_specs=[pl.BlockSpec((tm, d), lambda i: (i, 0)),
                  pl.BlockSpec((1, d), lambda i: (0, 0)),
                  pl.BlockSpec((d, LANES), lambda i: (0, 0))],
        out_specs=[pl.BlockSpec((tm * SUBLANES, LANES), lambda i: (i, 0)),
                   pl.BlockSpec((tm, LANES), lambda i: (i, 0)),
                   pl.BlockSpec((tm, LANES), lambda i: (i, 0)),
                   pl.BlockSpec((SUBLANES, LANES), lambda i: (0, 0))],
        scratch_shapes=[pltpu.VMEM((SUBLANES, LANES), F32)],
        compiler_params=_params("arbitrary"),
        name="moe_router",
    )(h, gain.reshape(1, -1), wr)


def _dispatch_body(s1_ref, s2_ref, x_ref, zero_ref, out_ref, sem, *, tm):
    del zero_ref

    def issue(r, carry):
        src = x_ref.at[r]
        pltpu.make_async_copy(src, out_ref.at[s1_ref[0, 0, r]], sem.at[0]).start(0)
        pltpu.make_async_copy(src, out_ref.at[s2_ref[0, 0, r]], sem.at[1]).start(1)
        return carry

    lax.fori_loop(0, tm, issue, 0, unroll=8)
    pltpu.make_async_copy(x_ref, out_ref.at[pl.ds(0, tm)], sem.at[0]).wait()
    pltpu.make_async_copy(x_ref, out_ref.at[pl.ds(0, tm)], sem.at[1]).wait()


def _dispatch(xn, slot1, slot2, n_slots):
    t = xn.shape[0]
    tm = _pick(t, (512, 256, 128, 64, 32, 16, 8))
    nt = t // tm
    tile = xn.shape[1:]
    smem = pl.BlockSpec((1, 1, tm), lambda i: (i, 0, 0), memory_space=pltpu.SMEM)
    return pl.pallas_call(
        functools.partial(_dispatch_body, tm=tm),
        out_shape=jax.ShapeDtypeStruct((n_slots,) + tile, F32),
        grid=(nt,),
        in_specs=[smem, smem, pl.BlockSpec((tm,) + tile, lambda i: (i, 0, 0)), pl.BlockSpec(memory_space=pl.ANY)],
        out_specs=pl.BlockSpec(memory_space=pl.ANY),
        scratch_shapes=[pltpu.SemaphoreType.DMA((2,))],
        input_output_aliases={3: 0},
        compiler_params=_params("arbitrary"),
        name="moe_dispatch",
    )(slot1.reshape(nt, 1, tm), slot2.reshape(nt, 1, tm), xn, jnp.zeros((n_slots,) + tile, F32))


def _experts_body(te_ref, nu_ref, x_ref, wg_hbm, wu_hbm, wd_hbm, o_ref, wg_ref, wu_ref, wd_ref, stage_in, stage_out,
                  sem, *, tm, tf):
    i = pl.program_id(0)
    active = i < nu_ref[0]
    expert = te_ref[i]
    changed = (i == 0) | (expert != te_ref[jnp.maximum(i - 1, 0)])

    @pl.when(active & changed)
    def _():
        _load_bf16(wg_hbm.at[expert], wg_ref, stage_in, sem)
        _load_bf16(wu_hbm.at[expert], wu_ref, stage_in, sem)
        _load_bf16(wd_hbm.at[expert], wd_ref, stage_out, sem)

    @pl.when(active)
    def _():
        xb = _tiles_to_rows(x_ref, tm).astype(BF16)
        acc = None
        for c in range(wg_ref.shape[1] // tf):
            cols = slice(c * tf, (c + 1) * tf)
            gate = jnp.dot(xb, wg_ref[:, cols], preferred_element_type=F32)
            up = jnp.dot(xb, wu_ref[:, cols], preferred_element_type=F32)
            act = (gate * jax.nn.sigmoid(gate) * up).astype(BF16)
            d = jnp.dot(act, wd_ref[cols, :], preferred_element_type=F32)
            acc = d if acc is None else acc + d
        _rows_to_tiles(o_ref, acc)

    @pl.when(jnp.logical_not(active))
    def _():
        o_ref[...] = jnp.zeros_like(o_ref)


def _experts(xs, tile_expert, n_used, w_gate, w_up, w_down, tm):
    n_slots = xs.shape[0] // SUBLANES
    d = SUBLANES * LANES
    f = w_gate.shape[2]
    tf = _pick(f, (1792, 1024, 512, 256, 128))
    n_tiles = n_slots // tm
    anywhere = pl.BlockSpec(memory_space=pl.ANY)
    grid_spec = pltpu.PrefetchScalarGridSpec(
        num_scalar_prefetch=2,
        grid=(n_tiles,),
        in_specs=[pl.BlockSpec((tm * SUBLANES, LANES), lambda i, te, nu: (jnp.minimum(i, nu[0] - 1), 0)),
                  anywhere, anywhere, anywhere],
        out_specs=pl.BlockSpec((tm * SUBLANES, LANES), lambda i, te, nu: (i, 0)),
        scratch_shapes=[pltpu.VMEM((d, f), BF16), pltpu.VMEM((d, f), BF16), pltpu.VMEM((f, d), BF16),
                        pltpu.VMEM((STAGE_SLOTS, _stage_rows(d, f), f), F32),
                        pltpu.VMEM((STAGE_SLOTS, _stage_rows(f, d), d), F32),
                        pltpu.SemaphoreType.DMA((STAGE_SLOTS,))],
    )
    return pl.pallas_call(
        functools.partial(_experts_body, tm=tm, tf=tf),
        out_shape=jax.ShapeDtypeStruct((n_slots * SUBLANES, LANES), F32),
        grid_spec=grid_spec,
        compiler_params=_params("arbitrary"),
        name="moe_experts",
    )(tile_expert, n_used, xs, w_gate, w_up, w_down)


def _gather_norm_body(s1_ref, s2_ref, n1_ref, n2_ref, h_ref, gm_ref, g_ref, ys_ref, oa_ref, ob_ref, ya_ref, yb_ref,
                      sem, *, tm, head_tiles, n_steps):
    i = pl.program_id(0)
    cur = i % 2

    def gather(a_ref, b_ref, buf):
        def issue(r, carry):
            dst = pl.ds(pl.multiple_of(r * SUBLANES, SUBLANES), SUBLANES)
            pltpu.make_async_copy(ys_ref.at[a_ref[0, 0, r]], ya_ref.at[buf, dst], sem.at[buf, 0]).start(0)
            pltpu.make_async_copy(ys_ref.at[b_ref[0, 0, r]], yb_ref.at[buf, dst], sem.at[buf, 1]).start(1)
            return carry

        lax.fori_loop(0, tm, issue, 0, unroll=8)

    @pl.when(i == 0)
    def _():
        gather(s1_ref, s2_ref, 0)

    @pl.when(i + 1 < n_steps)
    def _():
        gather(n1_ref, n2_ref, 1 - cur)

    pltpu.make_async_copy(ya_ref.at[cur], ya_ref.at[cur], sem.at[cur, 0]).wait()
    pltpu.make_async_copy(yb_ref.at[cur], yb_ref.at[cur], sem.at[cur, 1]).wait()
    gm = gm_ref[...]
    y = _rms(h_ref[...] + (gm[:, 0:1] * _tiles_to_rows(ya_ref.at[cur], tm)
                           + gm[:, 1:2] * _tiles_to_rows(yb_ref.at[cur], tm)), g_ref[...])

    @pl.when(i < head_tiles)
    def _():
        oa_ref[...] = y

    @pl.when(i >= head_tiles)
    def _():
        ob_ref[...] = y


def _gather_norm(h, gates, slot1, slot2, ys, gain, split):
    t, d = h.shape
    tm = _pick(split, (512, 256, 128, 64, 32, 16, 8))
    assert t % tm == 0
    nt = t // tm
    head_tiles = split // tm
    smem = pl.BlockSpec((1, 1, tm), lambda i: (i, 0, 0), memory_space=pltpu.SMEM)
    smem_next = pl.BlockSpec((1, 1, tm), lambda i: (jnp.minimum(i + 1, nt - 1), 0, 0), memory_space=pltpu.SMEM)
    s1, s2 = slot1.reshape(nt, 1, tm), slot2.reshape(nt, 1, tm)
    return pl.pallas_call(
        functools.partial(_gather_norm_body, tm=tm, head_tiles=head_tiles, n_steps=nt),
        out_shape=[jax.ShapeDtypeStruct((split, d), F32), jax.ShapeDtypeStruct((t - split, d), F32)],
        grid=(nt,),
        in_specs=[smem, smem, smem_next, smem_next, pl.BlockSpec((tm, d), lambda i: (i, 0)),
                  pl.BlockSpec((tm, LANES), lambda i: (i, 0)),
                  pl.BlockSpec((1, d), lambda i: (0, 0)),
                  pl.BlockSpec(memory_space=pl.ANY)],
        out_specs=[pl.BlockSpec((tm, d), lambda i: (jnp.minimum(i, head_tiles - 1), 0)),
                   pl.BlockSpec((tm, d), lambda i: (jnp.maximum(i - head_tiles, 0), 0))],
        scratch_shapes=[pltpu.VMEM((2, tm * SUBLANES, LANES), F32), pltpu.VMEM((2, tm * SUBLANES, LANES), F32),
                        pltpu.SemaphoreType.DMA((2, 2))],
        compiler_params=_params("arbitrary"),
        name="moe_gather_norm",
    )(s1, s2, s1, s2, h, gates, gain.reshape(1, -1), ys)


def _moe(h, gain, w_router, w_gate, w_up, w_down, final_gain, split):
    t, _ = h.shape
    n_exp = w_router.shape[1]
    tm_e = 512
    xn, imeta, gates, counts = _router(h, gain, w_router)
    idx1, idx2, rank1, rank2 = imeta[:, 0], imeta[:, 1], imeta[:, 2], imeta[:, 3]
    cnt = counts[0, :n_exp]
    padded = ((cnt + tm_e - 1) // tm_e) * tm_e
    ends = jnp.cumsum(padded)
    starts = ends - padded
    slot1 = starts[idx1] + rank1
    slot2 = starts[idx2] + rank2
    n_tiles = (TOP_K * t + n_exp * (tm_e - 1)) // tm_e
    n_used = (ends[-1] // tm_e).astype(I32)
    tile_start = jnp.arange(n_tiles, dtype=I32) * tm_e
    tile_expert = jnp.sum((tile_start[:, None] >= ends[None, :]).astype(I32), axis=1)
    tile_expert = jnp.minimum(tile_expert, n_exp - 1)
    assert h.shape[1] == SUBLANES * LANES
    n_slots = n_tiles * tm_e
    xs = _dispatch(xn.reshape(t, SUBLANES, LANES), slot1, slot2, n_slots)
    ys = _experts(xs.reshape(n_slots * SUBLANES, LANES), tile_expert, n_used.reshape(1), w_gate, w_up, w_down, tm_e)
    return _gather_norm(h, gates, slot1, slot2, ys.reshape(n_slots, SUBLANES, LANES), final_gain, split)


def _cache_to_slabs(cache):
    n, w, two, heads, hd = cache.shape
    return jnp.transpose(cache, (0, 2, 3, 4, 1)).reshape(n, two, heads * hd, w)


def _slabs_to_cache(slabs, heads):
    n, two, _, w = slabs.shape
    return jnp.transpose(slabs.reshape(n, two, heads, HEAD_DIM, w), (0, 4, 1, 2, 3))


def kernel(x_prompt, x_sample, cache_conv, cache_swa_kv, cache_dil_kv0, cache_dil_kv1, cache_dil_kv2, norm_mix0, w_in0, conv_w, swa_sink, w_out0, norm_ffn0, w_gate0, w_up0, w_down0, norm_mix1, w_in1, w_out1, norm_ffn1, w_router, w_gate1, w_up1, w_down1, norm_final):
    n_p, seq, d = x_prompt.shape
    n_s, s_len, _ = x_sample.shape
    tp, ts = n_p * seq, n_s * s_len
    d_conv = conv_w.shape[2]
    kvh, grp = swa_sink.shape[1], swa_sink.shape[2]
    hq0 = kvh * grp
    h1 = cache_dil_kv0.shape[4]
    c_q0, c_kv0 = hq0 * HEAD_DIM, kvh * HEAD_DIM
    q0_col = 3 * d_conv
    k0_col = q0_col + c_q0
    c_g = 3 * h1 * HEAD_DIM
    dil_caches = (cache_dil_kv0, cache_dil_kv1, cache_dil_kv2)

    x_rows = (x_prompt.reshape(tp, d), x_sample.reshape(ts, d))

    assert c_kv0 == LANES and grp % 2 == 0 and d_conv % LANES == 0
    nsc = d_conv // LANES
    q0_slab = 3 * nsc
    k0_slab = q0_slab + c_q0 // LANES
    proj0 = _dense([x_rows], w_in0[0], gain=norm_mix0[0], out_dtype=F32, name="in_proj0", slab_out=True)
    a_p, conv_tail = _conv_prompt(proj0, conv_w[0], n_seq=n_p, seq_len=seq, c=d_conv)
    att_p = _band_attn(proj0, n_seq=n_p, seq_len=seq, dil=1, q_slab=q0_slab, k_slab=k0_slab, v_slab=k0_slab + 1,
                       n_q_slabs=c_q0 // LANES, shared_kv=True, sink=swa_sink[0], want_lse=False,
                       name="swa_prompt")[0]
    proj0_s = proj0[:, tp:]
    p3 = proj0_s[:q0_slab].reshape(3, nsc, n_s, s_len, LANES)
    p3 = jnp.transpose(p3, (0, 3, 2, 1, 4)).reshape(3, s_len, n_s, d_conv)
    a_s, conv_new = _conv_step(p3, jnp.transpose(cache_conv[0], (1, 0, 2)), conv_w[0])
    a_s = jnp.transpose(a_s, (1, 0, 2)).reshape(ts, d_conv)
    kvt0 = _proj_t(jnp.transpose(w_in0[0][:, k0_col:]), x_rows[1], norm_mix0[0], "kv_t0")
    q_s = proj0_s[q0_slab:k0_slab].reshape(kvh, grp // 2, n_s, s_len, 2, HEAD_DIM)
    q_s = jnp.transpose(q_s, (2, 3, 1, 4, 0, 5)).reshape(n_s, s_len * grp, c_kv0)
    sink_col = jnp.broadcast_to(swa_sink[0][:, None, :], (kvh, s_len, grp)).reshape(kvh * s_len * grp, 1)
    o_s, swa_new = _step_attn(q_s, kvt0.reshape(2, c_kv0, ts), _cache_to_slabs(cache_swa_kv[0]), dil=1,
                              tok_div=grp, sink_col=sink_col, want_lse=False, bsz=8, name="swa_step")
    att_s = jnp.transpose(o_s.reshape(n_s, s_len, grp, kvh, HEAD_DIM), (0, 1, 3, 2, 4)).reshape(ts, c_q0)
    h = _dense([(a_p, a_s.astype(BF16)), (att_p, att_s.astype(BF16))], w_out0[0], res=x_rows, out_dtype=F32,
               name="out_proj0")
    h = _ffn(h, norm_ffn0[0], w_gate0[0], w_up0[0], w_down0[0], "ffn0")

    proj1 = _dense([h], w_in1[0], gain=norm_mix1[0], out_dtype=F32, name="in_proj1", slab_out=True)
    proj1_s = proj1[:, tp:]
    c_h = h1 * HEAD_DIM
    nsh = c_h // LANES
    pairs_p, pairs_s, dil_p, dil_s = [], [], [], []
    for g, dil in enumerate(DILATIONS):
        window = dil_caches[g].shape[2]
        s0 = g * 3 * nsh
        pairs_p.append(tuple(_band_attn(proj1, n_seq=n_p, seq_len=seq, dil=dil, q_slab=s0, k_slab=s0 + nsh,
                                        v_slab=s0 + 2 * nsh, n_q_slabs=nsh, want_lse=True,
                                        name=f"dil{g}_prompt")))
        keep = min(window, seq)
        state = _state_slabs(proj1, first_slab=s0 + nsh, n_slabs=2 * nsh, n_seq=n_p, seq_len=seq, keep=keep,
                             name=f"dil{g}_state")
        dil_p.append(_slabs_to_cache(state.reshape(n_p, 2, c_h, keep), h1)[None])
        lo = g * c_g
        q_g = jnp.transpose(proj1_s[s0:s0 + nsh].reshape(nsh, n_s, s_len, LANES), (1, 2, 0, 3))
        q_g = q_g.reshape(n_s, s_len, c_h)
        q_g = jnp.concatenate([q_g, jnp.zeros((n_s, SUBLANES - s_len, c_h), F32)], axis=1)
        kvt = _proj_t(jnp.transpose(w_in1[0][:, lo + c_h:lo + c_g]), h[tp:], norm_mix1[0], f"kv_t1_{g}")
        o_sg, l_sg, cache_new = _step_attn(q_g, kvt.reshape(2, c_h, ts), _cache_to_slabs(dil_caches[g][0]),
                                           dil=dil, tok_div=1, want_lse=True,
                                           bsz=max(1, 2048 // window), name=f"dil{g}_step")
        pairs_s.append((o_sg.reshape(n_s * SUBLANES, c_h), l_sg.reshape(n_s * SUBLANES, c_h)))
        dil_s.append(_slabs_to_cache(cache_new, h1)[None])
    comb_s = _combine(pairs_s, "combine_step").reshape(n_s, SUBLANES, c_h)[:, :s_len].reshape(ts, c_h)
    h = _combine_proj(pairs_p, comb_s, w_out1[0], h, "out_proj1")
    y_p, y_s = _moe(h, norm_ffn1[0], w_router[0], w_gate1[0], w_up1[0], w_down1[0], norm_final, tp)

    y_prompt = y_p.reshape(n_p, seq, d)
    y_sample = y_s.reshape(n_s, s_len, d)
    new_conv_prompt = conv_tail[:, SUBLANES - (CONV_W - 1):][None]
    new_conv_sample = jnp.transpose(conv_new, (1, 0, 2))[None]
    keep0 = min(SPAN, seq)
    swa_state = _state_slabs(proj0, first_slab=k0_slab, n_slabs=2, n_seq=n_p, seq_len=seq, keep=keep0,
                             name="swa_state")
    new_swa_kv_prompt = _slabs_to_cache(swa_state, kvh)[None]
    new_swa_kv_sample = _slabs_to_cache(swa_new, kvh)[None]
    return (y_prompt, y_sample, new_conv_prompt, new_conv_sample, new_swa_kv_prompt, new_swa_kv_sample,
            dil_p[0], dil_s[0], dil_p[1], dil_s[1], dil_p[2], dil_s[2])
```

```python
import functools

import jax
import jax.numpy as jnp
from jax import lax
from jax.experimental import pallas as pl
from jax.experimental.pallas import tpu as pltpu

F32 = jnp.float32
BF16 = jnp.bfloat16
I32 = jnp.int32

EPS = 1e-5
HEAD_DIM = 64
SPAN = 128
CONV_W = 3
DILATIONS = (1, 4, 16)
TOP_K = 2
LANES = 128
SUBLANES = 8
VMEM_LIMIT_BYTES = 56 * 1024 * 1024
NT_DIMS = (((1,), (1,)), ((), ()))


def _params(*sem):
    return pltpu.CompilerParams(dimension_semantics=sem, vmem_limit_bytes=VMEM_LIMIT_BYTES)


def _pick(n, candidates):
    for c in candidates:
        if n % c == 0:
            return c
    raise ValueError(f"no tile for {n} in {candidates}")


def _rms(x, g):
    y = x * lax.rsqrt(jnp.mean(x * x, axis=-1, keepdims=True) + EPS)
    return y * g


STAGE_SLOTS = 4
STAGE_BYTES = 2 * 1024 * 1024


def _stage_rows(k, n):
    rows = k
    while rows * n * 4 > STAGE_BYTES and rows % 32 == 0:
        rows //= 2
    return rows


def _load_bf16(w_hbm, w_vmem, stage, sem):
    n_slots, chunk = stage.shape[0], stage.shape[1]
    n_chunks = w_hbm.shape[0] // chunk

    def copy(c):
        return pltpu.make_async_copy(w_hbm.at[pl.ds(c * chunk, chunk), :], stage.at[c % n_slots],
                                     sem.at[c % n_slots])

    for c in range(min(n_slots - 1, n_chunks)):
        copy(c).start(c % 2)
    for c in range(n_chunks):
        if c + n_slots - 1 < n_chunks:
            copy(c + n_slots - 1).start((c + n_slots - 1) % 2)
        copy(c).wait()
        w_vmem[c * chunk:(c + 1) * chunk, :] = stage[c % n_slots].astype(BF16)


def _dense_body(*refs, ks, split, head_tiles, has_gain, has_res, slab_out, cn):
    n_in = len(ks)
    step = pl.program_id(0)
    refs = list(refs)

    def take(is_split):
        count = 2 if is_split else 1
        parts = tuple(refs[:count])
        del refs[:count]
        return parts

    def read(parts, cols=slice(None)):
        if len(parts) == 1:
            return parts[0][:, cols]
        return jnp.where(step < head_tiles, parts[0][:, cols], parts[1][:, cols])

    xs = [take(s) for s in split[:n_in]]
    g_ref = refs.pop(0) if has_gain else None
    w_hbm = refs.pop(0)
    res = take(split[n_in]) if has_res else None
    o_ref, w_ref, stage, sem = refs

    @pl.when(step == 0)
    def _():
        _load_bf16(w_hbm, w_ref, stage, sem)

    if has_gain:
        lhs = [_rms(read(xs[0]), g_ref[...]).astype(BF16)]
    else:
        lhs = [read(x).astype(BF16) for x in xs]
    n = w_ref.shape[1]
    for c in range(n // cn):
        cols = slice(c * cn, (c + 1) * cn)
        acc = None
        row0 = 0
        for a, k in zip(lhs, ks):
            d = jnp.dot(a, w_ref[row0:row0 + k, cols], preferred_element_type=F32)
            acc = d if acc is None else acc + d
            row0 += k
        if has_res:
            acc = read(res, cols) + acc
        if slab_out:
            for s in range(cn // LANES):
                o_ref[c * (cn // LANES) + s] = acc[:, s * LANES:(s + 1) * LANES].astype(o_ref.dtype)
        else:
            o_ref[:, cols] = acc.astype(o_ref.dtype)


def _dense(xs, w, *, gain=None, res=None, out_dtype, name, slab_out=False):
    def n_rows(a):
        return sum(p.shape[0] for p in a) if isinstance(a, tuple) else a.shape[0]

    def width(a):
        return a[0].shape[1] if isinstance(a, tuple) else a.shape[1]

    operands = list(xs) + ([res] if res is not None else [])
    pairs = [a for a in operands if isinstance(a, tuple)]
    t = n_rows(xs[0])
    k_all, n = w.shape
    ks = tuple(width(x) for x in xs)
    assert sum(ks) == k_all
    gcd_rows = t
    for a in pairs:
        gcd_rows = min(gcd_rows, a[1].shape[0])
    tm = _pick(gcd_rows, (512, 256, 128, 64, 32, 16, 8))
    assert t % tm == 0
    head_tiles = pairs[0][0].shape[0] // tm if pairs else 0
    for a in pairs:
        assert a[0].shape[0] == head_tiles * tm
    cn = _pick(n, (512, 384, 256, 128))
    in_specs, args = [], []

    def add_rows(a, k):
        if isinstance(a, tuple):
            in_specs.append(pl.BlockSpec((tm, k), lambda i: (jnp.minimum(i, head_tiles - 1), 0)))
            in_specs.append(pl.BlockSpec((tm, k), lambda i: (jnp.maximum(i - head_tiles, 0), 0)))
            args.extend(a)
        else:
            in_specs.append(pl.BlockSpec((tm, k), lambda i: (i, 0)))
            args.append(a)

    for x, k in zip(xs, ks):
        add_rows(x, k)
    if gain is not None:
        in_specs.append(pl.BlockSpec((1, ks[0]), lambda i: (0, 0)))
        args.append(gain.reshape(1, -1))
    in_specs.append(pl.BlockSpec(memory_space=pl.ANY))
    args.append(w)
    if res is not None:
        add_rows(res, n)
    split = tuple(isinstance(a, tuple) for a in operands)
    if slab_out:
        out_shape = jax.ShapeDtypeStruct((n // LANES, t, LANES), out_dtype)
        out_spec = pl.BlockSpec((n // LANES, tm, LANES), lambda i: (0, i, 0))
    else:
        out_shape = jax.ShapeDtypeStruct((t, n), out_dtype)
        out_spec = pl.BlockSpec((tm, n), lambda i: (i, 0))
    return pl.pallas_call(
        functools.partial(_dense_body, ks=ks, split=split, head_tiles=head_tiles, has_gain=gain is not None,
                          has_res=res is not None, slab_out=slab_out, cn=cn),
        out_shape=out_shape,
        grid=(t // tm,),
        in_specs=in_specs,
        out_specs=out_spec,
        scratch_shapes=[pltpu.VMEM((k_all, n), BF16), pltpu.VMEM((STAGE_SLOTS, _stage_rows(k_all, n), n), F32),
                        pltpu.SemaphoreType.DMA((STAGE_SLOTS,))],
        compiler_params=_params("arbitrary"),
        name=name,
    )(*args)


def _proj_t_body(w_ref, x_ref, g_ref, o_ref):
    xn = _rms(x_ref[...], g_ref[...]).astype(BF16)
    o_ref[...] = lax.dot_general(w_ref[...].astype(BF16), xn, NT_DIMS, preferred_element_type=F32)


def _proj_t(w_t, x, gain, name):
    c, k = w_t.shape
    rows = x.shape[0]
    tc = _pick(c, (512, 256, 128))
    return pl.pallas_call(
        _proj_t_body,
        out_shape=jax.ShapeDtypeStruct((c, rows), F32),
        grid=(c // tc,),
        in_specs=[pl.BlockSpec((tc, k), lambda i: (i, 0)),
                  pl.BlockSpec((rows, k), lambda i: (0, 0)),
                  pl.BlockSpec((1, k), lambda i: (0, 0))],
        out_specs=pl.BlockSpec((tc, rows), lambda i: (i, 0)),
        compiler_params=_params("parallel"),
        name=name,
    )(w_t, x, gain.reshape(1, -1))


def _ffn_body(x_ref, g_ref, wg_hbm, wu_hbm, wd_hbm, o_ref, wg_ref, wu_ref, wd_ref, stage_in, stage_out, sem, *,
              tf):
    @pl.when(pl.program_id(0) == 0)
    def _():
        _load_bf16(wg_hbm, wg_ref, stage_in, sem)
        _load_bf16(wu_hbm, wu_ref, stage_in, sem)
        _load_bf16(wd_hbm, wd_ref, stage_out, sem)

    x = x_ref[...]
    xn = _rms(x, g_ref[...]).astype(BF16)
    acc = None
    for c in range(wg_ref.shape[1] // tf):
        cols = slice(c * tf, (c + 1) * tf)
        gate = jnp.dot(xn, wg_ref[:, cols], preferred_element_type=F32)
        up = jnp.dot(xn, wu_ref[:, cols], preferred_element_type=F32)
        act = (gate * jax.nn.sigmoid(gate) * up).astype(BF16)
        d = jnp.dot(act, wd_ref[cols, :], preferred_element_type=F32)
        acc = d if acc is None else acc + d
    o_ref[...] = x + acc


def _ffn(x, gain, w_gate, w_up, w_down, name):
    t, d = x.shape
    f = w_gate.shape[1]
    tm = _pick(t, (512, 256, 128, 64, 32, 16, 8))
    tf = f
    anywhere = pl.BlockSpec(memory_space=pl.ANY)
    return pl.pallas_call(
        functools.partial(_ffn_body, tf=tf),
        out_shape=jax.ShapeDtypeStruct((t, d), F32),
        grid=(t // tm,),
        in_specs=[pl.BlockSpec((tm, d), lambda i: (i, 0)),
                  pl.BlockSpec((1, d), lambda i: (0, 0)),
                  anywhere, anywhere, anywhere],
        out_specs=pl.BlockSpec((tm, d), lambda i: (i, 0)),
        scratch_shapes=[pltpu.VMEM((d, f), BF16), pltpu.VMEM((d, f), BF16), pltpu.VMEM((f, d), BF16),
                        pltpu.VMEM((STAGE_SLOTS, _stage_rows(d, f), f), F32),
                        pltpu.VMEM((STAGE_SLOTS, _stage_rows(f, d), d), F32),
                        pltpu.SemaphoreType.DMA((STAGE_SLOTS,))],
        compiler_params=_params("arbitrary"),
        name=name,
    )(x, gain.reshape(1, -1), w_gate, w_up, w_down)


def _band_body(*refs, dil, mb, shared_kv, has_sink, want_lse):
    q_ref, kc_ref, kp_ref, vc_ref, vp_ref = refs[:5]
    pos = 5
    sink_ref = refs[pos] if has_sink else None
    pos += int(has_sink)
    o_ref = refs[pos]
    lse_ref = refs[pos + 1] if want_lse else None
    t = pl.program_id(1)
    scale = HEAD_DIM ** -0.5
    qi = lax.broadcasted_iota(I32, (2 * SPAN, 2 * SPAN), 0) & (SPAN - 1)
    kj = lax.broadcasted_iota(I32, (2 * SPAN, 2 * SPAN), 1)
    band = (kj >= qi) & (kj <= qi + SPAN)
    band_first = band & (kj >= jnp.where(t == 0, SPAN, 0))
    lane = lax.broadcasted_iota(I32, (1, LANES), 1)
    halves = [lane < HEAD_DIM, lane >= HEAD_DIM]

    def rows_of(ref, s, r, count):
        if dil == 1:
            return ref[s, 0:count, :]
        return ref[s, pl.ds(r, count, stride=dil), :]

    def pair(q_a, mask_a, q_b, mask_b, kw, vw, msk, heads):
        q2 = jnp.concatenate([jnp.where(mask_a, q_a, 0.0), jnp.where(mask_b, q_b, 0.0)], axis=0).astype(BF16)
        s = lax.dot_general(q2, kw, NT_DIMS, preferred_element_type=F32)
        s = jnp.where(msk, s, -jnp.inf)
        m = jnp.max(s, axis=-1, keepdims=True)
        if has_sink:
            sink = jnp.concatenate([jnp.full((SPAN, 1), sink_ref[h], F32) for h in heads], axis=0)
            m = jnp.maximum(m, sink)
        p = jnp.exp(s - m)
        den = jnp.sum(p, axis=-1, keepdims=True)
        if has_sink:
            den = den + jnp.exp(sink - m)
        o = jnp.dot(p.astype(BF16), vw, preferred_element_type=F32) / den
        return o, m + jnp.log(den)

    def store(s, r, b, o_tile, lse_tile):
        if dil == 1:
            o_ref[b * SPAN:(b + 1) * SPAN, s * LANES:(s + 1) * LANES] = o_tile.astype(o_ref.dtype)
            if want_lse:
                lse_ref[b * SPAN:(b + 1) * SPAN, s * LANES:(s + 1) * LANES] = lse_tile
        else:
            o_ref[s, pl.ds(r + dil * b * SPAN, SPAN, stride=dil), :] = o_tile
            if want_lse:
                lse_ref[s, pl.ds(r + dil * b * SPAN, SPAN, stride=dil), :] = lse_tile

    n_slabs = q_ref.shape[0]
    first_head = 2 * n_slabs * pl.program_id(2)
    top, bot = slice(0, SPAN), slice(SPAN, 2 * SPAN)
    for r in range(dil):
        qs = [rows_of(q_ref, s, r, mb * SPAN) * scale for s in range(n_slabs)]
        kf = [jnp.concatenate([rows_of(kp_ref, s, r, SPAN), rows_of(kc_ref, s, r, mb * SPAN)], axis=0)
              for s in range(kc_ref.shape[0])]
        vf = [jnp.concatenate([rows_of(vp_ref, s, r, SPAN), rows_of(vc_ref, s, r, mb * SPAN)], axis=0)
              for s in range(vc_ref.shape[0])]
        ks = [k.astype(BF16) for k in kf]
        vs = [v.astype(BF16) for v in vf]
        if shared_kv:
            k_sw = pltpu.roll(kf[0], HEAD_DIM, 1).astype(BF16)
            v_sw = pltpu.roll(vf[0], HEAD_DIM, 1).astype(BF16)
        for b in range(mb):
            blk = slice(b * SPAN, (b + 1) * SPAN)
            keys = slice(b * SPAN, (b + 2) * SPAN)
            msk = band_first if b == 0 else band
            if not shared_kv:
                for s in range(n_slabs):
                    o, lse = pair(qs[s][blk], halves[0], qs[s][blk], halves[1], ks[s][keys], vs[s][keys], msk,
                                  (first_head + 2 * s, first_head + 2 * s + 1))
                    store(s, r, b, jnp.where(halves[0], o[top], o[bot]), jnp.where(halves[0], lse[top], lse[bot]))
            else:
                for kvh in range(2):
                    sa, sb = 2 * kvh, 2 * kvh + 1
                    o_al, l_al = pair(qs[sa][blk], halves[kvh], qs[sb][blk], halves[kvh], ks[0][keys],
                                      vs[0][keys], msk, (2 * sa + kvh, 2 * sb + kvh))
                    o_sw, l_sw = pair(qs[sa][blk], halves[1 - kvh], qs[sb][blk], halves[1 - kvh], k_sw[keys],
                                      v_sw[keys], msk, (2 * sa + 1 - kvh, 2 * sb + 1 - kvh))
                    store(sa, r, b, jnp.where(halves[kvh], o_al[top], o_sw[top]),
                          jnp.where(halves[kvh], l_al[top], l_sw[top]))
                    store(sb, r, b, jnp.where(halves[kvh], o_al[bot], o_sw[bot]),
                          jnp.where(halves[kvh], l_al[bot], l_sw[bot]))


def _band_attn(slabs, *, n_seq, seq_len, dil, q_slab, k_slab, v_slab, n_q_slabs, shared_kv=False, sink=None,
               want_lse, name):
    mb = max(1, 512 // (dil * SPAN))
    tp = dil * SPAN * mb
    nt = seq_len // tp
    assert seq_len % tp == 0 and (not shared_kv or (n_q_slabs == 4 and dil == 1))
    prev_rows = dil * SPAN
    per_step = n_q_slabs if (shared_kv or tp <= 512) else 1
    kv_block = 1 if shared_kv else per_step
    kv_step = 0 if shared_kv else 1

    def cur(base, step):
        return lambda n, t, s: (base + s * step, n * nt + t, 0)

    def prev(base, step):
        return lambda n, t, s: (base + s * step, jnp.maximum((n * nt + t) * mb - 1, 0), 0)

    in_specs = [pl.BlockSpec((per_step, tp, LANES), cur(q_slab // per_step, 1)),
                pl.BlockSpec((kv_block, tp, LANES), cur(k_slab // kv_block, kv_step)),
                pl.BlockSpec((kv_block, prev_rows, LANES), prev(k_slab // kv_block, kv_step)),
                pl.BlockSpec((kv_block, tp, LANES), cur(v_slab // kv_block, kv_step)),
                pl.BlockSpec((kv_block, prev_rows, LANES), prev(v_slab // kv_block, kv_step))]
    assert q_slab % per_step == 0 and k_slab % kv_block == 0 and v_slab % kv_block == 0
    args = [slabs] * 5
    if sink is not None:
        in_specs.append(pl.BlockSpec(memory_space=pltpu.SMEM))
        args.append(sink.reshape(-1).astype(F32))
    rows = n_seq * seq_len
    if dil == 1:
        spec = pl.BlockSpec((tp, per_step * LANES), lambda n, t, s: (n * nt + t, s))
        out_shape = [jax.ShapeDtypeStruct((rows, n_q_slabs * LANES), BF16)]
        lse_shape = jax.ShapeDtypeStruct((rows, n_q_slabs * LANES), F32)
    else:
        spec = pl.BlockSpec((per_step, tp, LANES), lambda n, t, s: (s, n * nt + t, 0))
        out_shape = [jax.ShapeDtypeStruct((n_q_slabs, rows, LANES), F32)]
        lse_shape = jax.ShapeDtypeStruct((n_q_slabs, rows, LANES), F32)
    out_specs = [spec]
    if want_lse:
        out_shape.append(lse_shape)
        out_specs.append(spec)
    return pl.pallas_call(
        functools.partial(_band_body, dil=dil, mb=mb, shared_kv=shared_kv, has_sink=sink is not None,
                          want_lse=want_lse),
        out_shape=out_shape,
        grid=(n_seq, nt, n_q_slabs // per_step),
        in_specs=in_specs,
        out_specs=out_specs,
        compiler_params=_params("parallel", "parallel", "parallel"),
        name=name,
    )(*args)


def _combine_body(o0, l0, o1, l1, o2, l2, out_ref):
    def tile(ref):
        return ref[...].reshape(ref.shape[-2:]).astype(F32)

    a0, a1, a2 = tile(l0), tile(l1), tile(l2)
    m = jnp.maximum(jnp.maximum(a0, a1), a2)
    e0, e1, e2 = jnp.exp(a0 - m), jnp.exp(a1 - m), jnp.exp(a2 - m)
    num = e0 * tile(o0) + e1 * tile(o1) + e2 * tile(o2)
    out_ref[...] = (num / (e0 + e1 + e2)).astype(out_ref.dtype)


def _combine(pairs, name):
    first = pairs[0][0]
    rows, c = first.shape if first.ndim == 2 else (first.shape[1], first.shape[0] * LANES)
    tm = _pick(rows, (1024, 512, 256, 128, 64, 32, 16, 8))
    flat = pl.BlockSpec((tm, LANES), lambda i, s: (i, s))
    slab = pl.BlockSpec((1, tm, LANES), lambda i, s: (s, i, 0))
    args = [a for pair in pairs for a in pair]
    return pl.pallas_call(
        _combine_body,
        out_shape=jax.ShapeDtypeStruct((rows, c), BF16),
        grid=(rows // tm, c // LANES),
        in_specs=[flat if a.ndim == 2 else slab for a in args],
        out_specs=flat,
        compiler_params=_params("parallel", "parallel"),
        name=name,
    )(*args)


def _combine_proj_body(o0, l0, o1, l1, o2, l2, tail_ref, res_ref, w_hbm, out_ref, w_ref, stage, sem, *, head_tiles):
    step = pl.program_id(0)

    @pl.when(step == 0)
    def _():
        _load_bf16(w_hbm, w_ref, stage, sem)

    parts = []
    for s in range(o1.shape[0]):
        lanes = slice(s * LANES, (s + 1) * LANES)
        a0, a1, a2 = l0[:, lanes], l1[s], l2[s]
        m = jnp.maximum(jnp.maximum(a0, a1), a2)
        e0, e1, e2 = jnp.exp(a0 - m), jnp.exp(a1 - m), jnp.exp(a2 - m)
        num = e0 * o0[:, lanes].astype(F32) + e1 * o1[s] + e2 * o2[s]
        parts.append(num / (e0 + e1 + e2))
    comb = jnp.concatenate(parts, axis=1).astype(BF16)
    comb = jnp.where(step < head_tiles, comb, tail_ref[...])
    out_ref[...] = res_ref[...] + jnp.dot(comb, w_ref[...], preferred_element_type=F32)


def _combine_proj(pairs, tail, w, res, name):
    (o0, l0), (o1, l1), (o2, l2) = pairs
    head, c = o0.shape
    t, n = res.shape
    tm = _pick(tail.shape[0], (512, 256, 128, 64, 32, 16, 8))
    assert head % tm == 0 and t == head + tail.shape[0] and o1.ndim == 3 and o2.ndim == 3
    head_tiles = head // tm

    def first(i):
        return jnp.minimum(i, head_tiles - 1)

    flat = pl.BlockSpec((tm, c), lambda i: (first(i), 0))
    slab = pl.BlockSpec((c // LANES, tm, LANES), lambda i: (0, first(i), 0))
    return pl.pallas_call(
        functools.partial(_combine_proj_body, head_tiles=head_tiles),
        out_shape=jax.ShapeDtypeStruct((t, n), F32),
        grid=(t // tm,),
        in_specs=[flat, flat, slab, slab, slab, slab,
                  pl.BlockSpec((tm, c), lambda i: (jnp.maximum(i - head_tiles, 0), 0)),
                  pl.BlockSpec((tm, n), lambda i: (i, 0)),
                  pl.BlockSpec(memory_space=pl.ANY)],
        out_specs=pl.BlockSpec((tm, n), lambda i: (i, 0)),
        scratch_shapes=[pltpu.VMEM((c, n), BF16), pltpu.VMEM((STAGE_SLOTS, _stage_rows(c, n), n), F32),
                        pltpu.SemaphoreType.DMA((STAGE_SLOTS,))],
        compiler_params=_params("arbitrary"),
        name=name,
    )(o0, l0, o1, l1, o2, l2, tail, res, w)


def _state_body(x_ref, o_ref):
    o_ref[0, 0] = x_ref[0].T


def _state_slabs(slabs, *, first_slab, n_slabs, n_seq, seq_len, keep, name):
    assert seq_len % keep == 0
    per_seq = seq_len // keep
    return pl.pallas_call(
        _state_body,
        out_shape=jax.ShapeDtypeStruct((n_seq, n_slabs, LANES, keep), F32),
        grid=(n_seq, n_slabs),
        in_specs=[pl.BlockSpec((1, keep, LANES), lambda n, s: (first_slab + s, (n + 1) * per_seq - 1, 0))],
        out_specs=pl.BlockSpec((1, 1, LANES, keep), lambda n, s: (n, s, 0, 0)),
        compiler_params=_params("parallel", "parallel"),
        name=name,
    )(slabs)


def _conv_prompt_body(gb_ref, gc_ref, xa_ref, gcp_ref, xap_ref, w_ref, a_ref, st_ref):
    t = pl.program_id(1)
    w = w_ref[...]
    for s in range(gb_ref.shape[0]):
        lanes = slice(s * LANES, (s + 1) * LANES)
        u = gc_ref[s] * xa_ref[s]
        up = jnp.where(t == 0, 0.0, gcp_ref[s] * xap_ref[s])
        ext = jnp.concatenate([up, u], axis=0)
        y = (w[0:1, lanes] * pltpu.roll(ext, 2, 0)[SUBLANES:]
             + w[1:2, lanes] * pltpu.roll(ext, 1, 0)[SUBLANES:]) + w[2:3, lanes] * u
        a_ref[:, lanes] = (gb_ref[s] * y).astype(a_ref.dtype)
        st_ref[0, :, lanes] = u[u.shape[0] - SUBLANES:]


def _conv_prompt(slabs, conv_w, *, n_seq, seq_len, c):
    tq = _pick(seq_len, (512, 256, 128))
    nt = seq_len // tq
    rb = tq // SUBLANES
    ns = c // LANES

    def cur(part):
        return lambda n, t: (part, n * nt + t, 0)

    def prev(part):
        return lambda n, t: (part, jnp.maximum((n * nt + t) * rb - 1, 0), 0)

    return pl.pallas_call(
        _conv_prompt_body,
        out_shape=[jax.ShapeDtypeStruct((n_seq * seq_len, c), BF16),
                   jax.ShapeDtypeStruct((n_seq, SUBLANES, c), F32)],
        grid=(n_seq, nt),
        in_specs=[pl.BlockSpec((ns, tq, LANES), cur(0)), pl.BlockSpec((ns, tq, LANES), cur(1)),
                  pl.BlockSpec((ns, tq, LANES), cur(2)),
                  pl.BlockSpec((ns, SUBLANES, LANES), prev(1)), pl.BlockSpec((ns, SUBLANES, LANES), prev(2)),
                  pl.BlockSpec((CONV_W, c), lambda n, t: (0, 0))],
        out_specs=[pl.BlockSpec((tq, c), lambda n, t: (n * nt + t, 0)),
                   pl.BlockSpec((1, SUBLANES, c), lambda n, t: (n, 0, 0))],
        compiler_params=_params("parallel", "arbitrary"),
        name="conv_prompt",
    )(slabs, slabs, slabs, slabs, slabs, conv_w)


def _conv_step_body(p_ref, prev_ref, w_ref, a_ref, st_ref):
    s_len = p_ref.shape[1]
    w = w_ref[...]
    hist = [prev_ref[k] for k in range(CONV_W - 1)] + [p_ref[1, s] * p_ref[2, s] for s in range(s_len)]
    for s in range(s_len):
        y = (w[0:1] * hist[s] + w[1:2] * hist[s + 1]) + w[2:3] * hist[s + 2]
        a_ref[s] = p_ref[0, s] * y
    for k in range(CONV_W - 1):
        st_ref[k] = hist[s_len + k]


def _conv_step(p3, prev, conv_w):
    _, s_len, n, c = p3.shape
    return pl.pallas_call(
        _conv_step_body,
        out_shape=[jax.ShapeDtypeStruct((s_len, n, c), F32), jax.ShapeDtypeStruct((CONV_W - 1, n, c), F32)],
        name="conv_step",
        compiler_params=pltpu.CompilerParams(vmem_limit_bytes=VMEM_LIMIT_BYTES),
    )(p3, prev, conv_w)


def _step_body(*refs, n_blk, qr, tok_div, w, dil, cw, bsz, has_sink, want_lse):
    q_ref, kvt_ref, cache_ref = refs[:3]
    pos = 3
    sink_ref = refs[pos] if has_sink else None
    pos += int(has_sink)
    o_ref = refs[pos]
    pos += 1
    lse_ref = refs[pos] if want_lse else None
    pos += int(want_lse)
    cout_ref = refs[pos]
    r_dim = n_blk * HEAD_DIM
    nrb = n_blk * qr
    n_chunks = w // cw
    scale = HEAD_DIM ** -0.5
    step = pl.program_id(0)
    lane_r = lax.broadcasted_iota(I32, (1, r_dim), 1)
    blk_masks = [(lane_r >= j * HEAD_DIM) & (lane_r < (j + 1) * HEAD_DIM) for j in range(n_blk)]
    row = lax.broadcasted_iota(I32, (nrb, 1), 0)
    tok = (row % qr) // tok_div
    lane_c = lax.broadcasted_iota(I32, (1, cw), 1)
    lane_n = lax.broadcasted_iota(I32, (1, LANES), 1)

    def valid(pos_l):
        ok = (pos_l >= tok) & (pos_l <= w + tok)
        if dil > 1:
            ok = ok & (((pos_l - tok) & (dil - 1)) == 0)
        return ok

    new_lane0 = LANES - 4

    def shift_window(b, kv, new_tile):
        n_cols = w // LANES
        prev = pltpu.roll(cache_ref[b, kv, :, 0:LANES], new_lane0, 1)
        for j in range(n_cols):
            if j + 1 < n_cols:
                nxt = pltpu.roll(cache_ref[b, kv, :, (j + 1) * LANES:(j + 2) * LANES], new_lane0, 1)
            else:
                nxt = new_tile
            cout_ref[b, kv, :, j * LANES:(j + 1) * LANES] = jnp.where(lane_n < new_lane0, prev, nxt)
            prev = nxt

    for b in range(bsz):
        off = ((step * bsz + b) * 4) % LANES
        shift = (new_lane0 + LANES - off) % LANES
        new_k = pltpu.roll(kvt_ref[0], shift, 1)
        new_v = pltpu.roll(kvt_ref[1], shift, 1)
        shift_window(b, 0, new_k)
        shift_window(b, 1, new_v)
        q = q_ref[b] * scale
        qbd = jnp.concatenate([jnp.where(mk, q, 0.0) for mk in blk_masks], axis=0).astype(BF16)
        scores = []
        for c in range(n_chunks):
            lo, hi = c * cw, (c + 1) * cw
            kc = cache_ref[b, 0, :, lo:hi]
            sc = jnp.dot(qbd, kc.astype(BF16), preferred_element_type=F32)
            scores.append(jnp.where(valid(lane_c + lo), sc, -jnp.inf))
        sc = jnp.dot(qbd, new_k.astype(BF16), preferred_element_type=F32)
        scores.append(jnp.where(valid(lane_n + (w - new_lane0)) & (lane_n >= new_lane0), sc, -jnp.inf))
        m = functools.reduce(jnp.maximum, [jnp.max(s, axis=-1, keepdims=True) for s in scores])
        if has_sink:
            m = jnp.maximum(m, sink_ref[...])
        probs = [jnp.exp(s - m) for s in scores]
        den = functools.reduce(lambda a, c: a + c, [jnp.sum(p, axis=-1, keepdims=True) for p in probs])
        if has_sink:
            den = den + jnp.exp(sink_ref[...] - m)
        pv = lax.dot_general(probs[-1].astype(BF16), new_v.astype(BF16), NT_DIMS, preferred_element_type=F32)
        for c in range(n_chunks):
            lo, hi = c * cw, (c + 1) * cw
            vc = cache_ref[b, 1, :, lo:hi]
            pv = pv + lax.dot_general(probs[c].astype(BF16), vc.astype(BF16), NT_DIMS,
                                      preferred_element_type=F32)
        o = jnp.zeros((qr, r_dim), F32)
        m_e = jnp.zeros((qr, r_dim), F32)
        den_e = jnp.zeros((qr, r_dim), F32)
        for j, mk in enumerate(blk_masks):
            rows = slice(j * qr, (j + 1) * qr)
            o = jnp.where(mk, pv[rows], o)
            m_e = jnp.where(mk, m[rows], m_e)
            den_e = jnp.where(mk, den[rows], den_e)
        o_ref[b] = o / den_e
        if want_lse:
            lse_ref[b] = m_e + jnp.log(den_e)


def _step_attn(q, kvt, cache, *, dil, tok_div, sink_col=None, want_lse, bsz, name):
    n, qr, r_dim = q.shape
    w = cache.shape[-1]
    n_blk = r_dim // HEAD_DIM
    cw = min(w, 512)
    in_specs = [pl.BlockSpec((bsz, qr, r_dim), lambda i: (i, 0, 0)),
                pl.BlockSpec((2, r_dim, LANES), lambda i: (0, 0, (i * bsz * 4) // LANES)),
                pl.BlockSpec((bsz, 2, r_dim, w), lambda i: (i, 0, 0, 0))]
    args = [q, kvt, cache]
    if sink_col is not None:
        in_specs.append(pl.BlockSpec((n_blk * qr, 1), lambda i: (0, 0)))
        args.append(sink_col)
    o_spec = pl.BlockSpec((bsz, qr, r_dim), lambda i: (i, 0, 0))
    out_shape = [jax.ShapeDtypeStruct((n, qr, r_dim), F32)]
    out_specs = [o_spec]
    if want_lse:
        out_shape.append(jax.ShapeDtypeStruct((n, qr, r_dim), F32))
        out_specs.append(o_spec)
    out_shape.append(jax.ShapeDtypeStruct(cache.shape, F32))
    out_specs.append(pl.BlockSpec((bsz, 2, r_dim, w), lambda i: (i, 0, 0, 0)))
    return pl.pallas_call(
        functools.partial(_step_body, n_blk=n_blk, qr=qr, tok_div=tok_div, w=w, dil=dil, cw=cw, bsz=bsz,
                          has_sink=sink_col is not None, want_lse=want_lse),
        out_shape=out_shape,
        grid=(n // bsz,),
        in_specs=in_specs,
        out_specs=out_specs,
        compiler_params=_params("parallel"),
        name=name,
    )(*args)


def _rows_to_tiles(ref, x):
    rows = x.shape[0]
    for s in range(SUBLANES):
        ref[pl.ds(s, rows, stride=SUBLANES), :] = x[:, s * LANES:(s + 1) * LANES]


def _tiles_to_rows(ref, rows):
    return jnp.concatenate([ref[pl.ds(s, rows, stride=SUBLANES), :] for s in range(SUBLANES)], axis=1)


def _router_body(h_ref, g_ref, wr_ref, xn_ref, im_ref, gm_ref, cnt_ref, carry_ref, *, n_exp):
    i = pl.program_id(0)

    @pl.when(i == 0)
    def _():
        carry_ref[...] = jnp.zeros_like(carry_ref)

    xn = _rms(h_ref[...], g_ref[...])
    _rows_to_tiles(xn_ref, xn)
    tm = xn.shape[0]
    wr = wr_ref[...]
    xh = xn.astype(BF16)
    xl = (xn - xh.astype(F32)).astype(BF16)
    wh = wr.astype(BF16)
    wl = (wr - wh.astype(F32)).astype(BF16)
    lg = jnp.dot(xh, wh, preferred_element_type=F32) + (
        jnp.dot(xh, wl, preferred_element_type=F32) + jnp.dot(xl, wh, preferred_element_type=F32))
    lane = lax.broadcasted_iota(I32, (tm, LANES), 1)
    lane_f = lane.astype(F32)
    lg = jnp.where(lane < n_exp, lg, -jnp.inf)
    m1 = jnp.max(lg, axis=-1, keepdims=True)
    i1 = jnp.min(jnp.where(lg == m1, lane_f, float(LANES)), axis=-1, keepdims=True)
    lg2 = jnp.where(lane_f == i1, -jnp.inf, lg)
    m2 = jnp.max(lg2, axis=-1, keepdims=True)
    i2 = jnp.min(jnp.where(lg2 == m2, lane_f, float(LANES)), axis=-1, keepdims=True)
    e = jnp.exp(m2 - m1)
    g1 = 1.0 / (1.0 + e)
    g2 = e / (1.0 + e)
    sel1 = lane_f == i1
    sel2 = lane_f == i2
    onehot = jnp.where(sel1 | sel2, 1.0, 0.0)
    r_i = lax.broadcasted_iota(I32, (tm, tm), 0)
    c_i = lax.broadcasted_iota(I32, (tm, tm), 1)
    tri = jnp.where(c_i < r_i, 1.0, 0.0).astype(BF16)
    before = jnp.dot(tri, onehot.astype(BF16), preferred_element_type=F32) + carry_ref[0:1]
    r1 = jnp.sum(jnp.where(sel1, before, 0.0), axis=-1, keepdims=True)
    r2 = jnp.sum(jnp.where(sel2, before, 0.0), axis=-1, keepdims=True)
    total = carry_ref[0:1] + jnp.sum(onehot, axis=0, keepdims=True)
    carry_ref[...] = jnp.broadcast_to(total, carry_ref.shape)
    cnt_ref[...] = jnp.broadcast_to(total, cnt_ref.shape).astype(I32)
    meta = jnp.where(lane == 0, i1, jnp.where(lane == 1, i2, jnp.where(lane == 2, r1, jnp.where(lane == 3, r2, 0.0))))
    im_ref[...] = meta.T[:SUBLANES].astype(I32)
    gm_ref[...] = jnp.where(lane == 0, g1, jnp.where(lane == 1, g2, 0.0))


def _router(h, gain, w_router):
    t, d = h.shape
    n_exp = w_router.shape[1]
    tm = _pick(t, (512, 256, 128, 64, 32, 16, 8))
    wr = jnp.zeros((d, LANES), F32).at[:, :n_exp].set(w_router)
    return pl.pallas_call(
        functools.partial(_router_body, n_exp=n_exp),
        out_shape=[jax.ShapeDtypeStruct((t * SUBLANES, LANES), F32), jax.ShapeDtypeStruct((SUBLANES, t), I32),
                   jax.ShapeDtypeStruct((t, LANES), F32), jax.ShapeDtypeStruct((SUBLANES, LANES), I32)],
        grid=(t // tm,),
        in_specs=[pl.BlockSpec((tm, d), lambda i: (i, 0)),
                  pl.BlockSpec((1, d), lambda i: (0, 0)),
                  pl.BlockSpec((d, LANES), lambda i: (0, 0))],
        out_specs=[pl.BlockSpec((tm * SUBLANES, LANES), lambda i: (i, 0)),
                   pl.BlockSpec((SUBLANES, tm), lambda i: (0, i)),
                   pl.BlockSpec((tm, LANES), lambda i: (i, 0)),
                   pl.BlockSpec((SUBLANES, LANES), lambda i: (0, 0))],
        scratch_shapes=[pltpu.VMEM((SUBLANES, LANES), F32)],
        compiler_params=_params("arbitrary"),
        name="moe_router",
    )(h, gain.reshape(1, -1), wr)


DISPATCH_SLOTS = 3


def _dispatch_body(s1_ref, s2_ref, x_hbm, zero_ref, out_ref, xbuf, load_sem, store_sem, *, tm, n_steps):
    del zero_ref
    i = pl.program_id(0)
    cur = i % DISPATCH_SLOTS

    def load(step):
        slot = step % DISPATCH_SLOTS
        return pltpu.make_async_copy(x_hbm.at[pl.ds(step * tm, tm)], xbuf.at[slot], load_sem.at[slot])

    def drain(slot):
        for k in range(TOP_K):
            pltpu.make_async_copy(xbuf.at[slot], out_ref.at[pl.ds(0, tm)], store_sem.at[slot, k]).wait()

    @pl.when(i == 0)
    def _():
        load(0).start()
        if n_steps > 1:
            load(1).start()

    load(i).wait()

    def issue(r, carry):
        src = xbuf.at[cur, r]
        pltpu.make_async_copy(src, out_ref.at[s1_ref[0, 0, r]], store_sem.at[cur, 0]).start(0)
        pltpu.make_async_copy(src, out_ref.at[s2_ref[0, 0, r]], store_sem.at[cur, 1]).start(1)
        return carry

    lax.fori_loop(0, tm, issue, 0, unroll=8)

    @pl.when(i >= 1)
    def _():
        drain((i - 1) % DISPATCH_SLOTS)

    @pl.when(i + 2 < n_steps)
    def _():
        load(i + 2).start()

    @pl.when(i == n_steps - 1)
    def _():
        drain(cur)


def _dispatch(xn, slot1, slot2, n_slots):
    t = xn.shape[0]
    tm = _pick(t, (512, 256, 128, 64, 32, 16, 8))
    nt = t // tm
    tile = xn.shape[1:]
    smem = pl.BlockSpec((1, 1, tm), lambda i: (i, 0, 0), memory_space=pltpu.SMEM)
    anywhere = pl.BlockSpec(memory_space=pl.ANY)
    return pl.pallas_call(
        functools.partial(_dispatch_body, tm=tm, n_steps=nt),
        out_shape=jax.ShapeDtypeStruct((n_slots,) + tile, F32),
        grid=(nt,),
        in_specs=[smem, smem, anywhere, anywhere],
        out_specs=pl.BlockSpec(memory_space=pl.ANY),
        scratch_shapes=[pltpu.VMEM((DISPATCH_SLOTS, tm) + tile, F32), pltpu.SemaphoreType.DMA((DISPATCH_SLOTS,)),
                        pltpu.SemaphoreType.DMA((DISPATCH_SLOTS, TOP_K))],
        input_output_aliases={3: 0},
        compiler_params=_params("arbitrary"),
        name="moe_dispatch",
    )(slot1.reshape(nt, 1, tm), slot2.reshape(nt, 1, tm), xn, jnp.zeros((n_slots,) + tile, F32))


def _experts_body(te_ref, nu_ref, x_ref, wg_hbm, wu_hbm, wd_hbm, o_ref, wg_ref, wu_ref, wd_ref, stage_in, stage_out,
                  sem, *, tm, tf):
    i = pl.program_id(0)
    active = i < nu_ref[0]
    expert = te_ref[i]
    changed = (i == 0) | (expert != te_ref[jnp.maximum(i - 1, 0)])

    @pl.when(active & changed)
    def _():
        _load_bf16(wg_hbm.at[expert], wg_ref, stage_in, sem)
        _load_bf16(wu_hbm.at[expert], wu_ref, stage_in, sem)
        _load_bf16(wd_hbm.at[expert], wd_ref, stage_out, sem)

    @pl.when(active)
    def _():
        xb = _tiles_to_rows(x_ref, tm).astype(BF16)
        acc = None
        for c in range(wg_ref.shape[1] // tf):
            cols = slice(c * tf, (c + 1) * tf)
            gate = jnp.dot(xb, wg_ref[:, cols], preferred_element_type=F32)
            up = jnp.dot(xb, wu_ref[:, cols], preferred_element_type=F32)
            act = (gate * jax.nn.sigmoid(gate) * up).astype(BF16)
            d = jnp.dot(act, wd_ref[cols, :], preferred_element_type=F32)
            acc = d if acc is None else acc + d
        _rows_to_tiles(o_ref, acc)

    @pl.when(jnp.logical_not(active))
    def _():
        o_ref[...] = jnp.zeros_like(o_ref)


def _experts(xs, tile_expert, n_used, w_gate, w_up, w_down, tm):
    n_slots = xs.shape[0] // SUBLANES
    d = SUBLANES * LANES
    f = w_gate.shape[2]
    tf = _pick(f, (1792, 1024, 512, 256, 128))
    n_tiles = n_slots // tm
    anywhere = pl.BlockSpec(memory_space=pl.ANY)
    grid_spec = pltpu.PrefetchScalarGridSpec(
        num_scalar_prefetch=2,
        grid=(n_tiles,),
        in_specs=[pl.BlockSpec((tm * SUBLANES, LANES), lambda i, te, nu: (jnp.minimum(i, nu[0] - 1), 0)),
                  anywhere, anywhere, anywhere],
        out_specs=pl.BlockSpec((tm * SUBLANES, LANES), lambda i, te, nu: (i, 0)),
        scratch_shapes=[pltpu.VMEM((d, f), BF16), pltpu.VMEM((d, f), BF16), pltpu.VMEM((f, d), BF16),
                        pltpu.VMEM((STAGE_SLOTS, _stage_rows(d, f), f), F32),
                        pltpu.VMEM((STAGE_SLOTS, _stage_rows(f, d), d), F32),
                        pltpu.SemaphoreType.DMA((STAGE_SLOTS,))],
    )
    return pl.pallas_call(
        functools.partial(_experts_body, tm=tm, tf=tf),
        out_shape=jax.ShapeDtypeStruct((n_slots * SUBLANES, LANES), F32),
        grid_spec=grid_spec,
        compiler_params=_params("arbitrary"),
        name="moe_experts",
    )(tile_expert, n_used, xs, w_gate, w_up, w_down)


def _gather_norm_body(s1_ref, s2_ref, n1_ref, n2_ref, h_ref, gm_ref, g_ref, ys_ref, oa_ref, ob_ref, ya_ref, yb_ref,
                      sem, *, tm, head_tiles, n_steps):
    i = pl.program_id(0)
    cur = i % 2

    def gather(a_ref, b_ref, buf):
        def issue(r, carry):
            dst = pl.ds(pl.multiple_of(r * SUBLANES, SUBLANES), SUBLANES)
            pltpu.make_async_copy(ys_ref.at[a_ref[0, 0, r]], ya_ref.at[buf, dst], sem.at[buf, 0]).start(0)
            pltpu.make_async_copy(ys_ref.at[b_ref[0, 0, r]], yb_ref.at[buf, dst], sem.at[buf, 1]).start(1)
            return carry

        lax.fori_loop(0, tm, issue, 0, unroll=8)

    @pl.when(i == 0)
    def _():
        gather(s1_ref, s2_ref, 0)

    @pl.when(i + 1 < n_steps)
    def _():
        gather(n1_ref, n2_ref, 1 - cur)

    pltpu.make_async_copy(ya_ref.at[cur], ya_ref.at[cur], sem.at[cur, 0]).wait()
    pltpu.make_async_copy(yb_ref.at[cur], yb_ref.at[cur], sem.at[cur, 1]).wait()
    gm = gm_ref[...]
    y = _rms(h_ref[...] + (gm[:, 0:1] * _tiles_to_rows(ya_ref.at[cur], tm)
                           + gm[:, 1:2] * _tiles_to_rows(yb_ref.at[cur], tm)), g_ref[...])

    @pl.when(i < head_tiles)
    def _():
        oa_ref[...] = y

    @pl.when(i >= head_tiles)
    def _():
        ob_ref[...] = y


def _gather_norm(h, gates, slot1, slot2, ys, gain, split):
    t, d = h.shape
    tm = _pick(split, (512, 256, 128, 64, 32, 16, 8))
    assert t % tm == 0
    nt = t // tm
    head_tiles = split // tm
    smem = pl.BlockSpec((1, 1, tm), lambda i: (i, 0, 0), memory_space=pltpu.SMEM)
    smem_next = pl.BlockSpec((1, 1, tm), lambda i: (jnp.minimum(i + 1, nt - 1), 0, 0), memory_space=pltpu.SMEM)
    s1, s2 = slot1.reshape(nt, 1, tm), slot2.reshape(nt, 1, tm)
    return pl.pallas_call(
        functools.partial(_gather_norm_body, tm=tm, head_tiles=head_tiles, n_steps=nt),
        out_shape=[jax.ShapeDtypeStruct((split, d), F32), jax.ShapeDtypeStruct((t - split, d), F32)],
        grid=(nt,),
        in_specs=[smem, smem, smem_next, smem_next, pl.BlockSpec((tm, d), lambda i: (i, 0)),
                  pl.BlockSpec((tm, LANES), lambda i: (i, 0)),
                  pl.BlockSpec((1, d), lambda i: (0, 0)),
                  pl.BlockSpec(memory_space=pl.ANY)],
        out_specs=[pl.BlockSpec((tm, d), lambda i: (jnp.minimum(i, head_tiles - 1), 0)),
                   pl.BlockSpec((tm, d), lambda i: (jnp.maximum(i - head_tiles, 0), 0))],
        scratch_shapes=[pltpu.VMEM((2, tm * SUBLANES, LANES), F32), pltpu.VMEM((2, tm * SUBLANES, LANES), F32),
                        pltpu.SemaphoreType.DMA((2, 2))],
        compiler_params=_params("arbitrary"),
        name="moe_gather_norm",
    )(s1, s2, s1, s2, h, gates, gain.reshape(1, -1), ys)


def _moe(h, gain, w_router, w_gate, w_up, w_down, final_gain, split):
    t, _ = h.shape
    n_exp = w_router.shape[1]
    tm_e = 512
    xn, imeta, gates, counts = _router(h, gain, w_router)
    idx1, idx2, rank1, rank2 = imeta[0], imeta[1], imeta[2], imeta[3]
    cnt = counts[0, :n_exp]
    padded = ((cnt + tm_e - 1) // tm_e) * tm_e
    ends = jnp.cumsum(padded)
    starts = ends - padded
    slot1 = starts[idx1] + rank1
    slot2 = starts[idx2] + rank2
    n_tiles = (TOP_K * t + n_exp * (tm_e - 1)) // tm_e
    n_used = (ends[-1] // tm_e).astype(I32)
    tile_start = jnp.arange(n_tiles, dtype=I32) * tm_e
    tile_expert = jnp.sum((tile_start[:, None] >= ends[None, :]).astype(I32), axis=1)
    tile_expert = jnp.minimum(tile_expert, n_exp - 1)
    assert h.shape[1] == SUBLANES * LANES
    n_slots = n_tiles * tm_e
    xs = _dispatch(xn.reshape(t, SUBLANES, LANES), slot1, slot2, n_slots)
    ys = _experts(xs.reshape(n_slots * SUBLANES, LANES), tile_expert, n_used.reshape(1), w_gate, w_up, w_down, tm_e)
    return _gather_norm(h, gates, slot1, slot2, ys.reshape(n_slots, SUBLANES, LANES), final_gain, split)


def _cache_to_slabs(cache):
    n, w, two, heads, hd = cache.shape
    return jnp.transpose(cache, (0, 2, 3, 4, 1)).reshape(n, two, heads * hd, w)


def _slabs_to_cache(slabs, heads):
    n, two, _, w = slabs.shape
    return jnp.transpose(slabs.reshape(n, two, heads, HEAD_DIM, w), (0, 4, 1, 2, 3))


def kernel(x_prompt, x_sample, cache_conv, cache_swa_kv, cache_dil_kv0, cache_dil_kv1, cache_dil_kv2, norm_mix0, w_in0, conv_w, swa_sink, w_out0, norm_ffn0, w_gate0, w_up0, w_down0, norm_mix1, w_in1, w_out1, norm_ffn1, w_router, w_gate1, w_up1, w_down1, norm_final):
    n_p, seq, d = x_prompt.shape
    n_s, s_len, _ = x_sample.shape
    tp, ts = n_p * seq, n_s * s_len
    d_conv = conv_w.shape[2]
    kvh, grp = swa_sink.shape[1], swa_sink.shape[2]
    hq0 = kvh * grp
    h1 = cache_dil_kv0.shape[4]
    c_q0, c_kv0 = hq0 * HEAD_DIM, kvh * HEAD_DIM
    q0_col = 3 * d_conv
    k0_col = q0_col + c_q0
    c_g = 3 * h1 * HEAD_DIM
    dil_caches = (cache_dil_kv0, cache_dil_kv1, cache_dil_kv2)

    x_rows = (x_prompt.reshape(tp, d), x_sample.reshape(ts, d))

    assert c_kv0 == LANES and grp % 2 == 0 and d_conv % LANES == 0
    nsc = d_conv // LANES
    q0_slab = 3 * nsc
    k0_slab = q0_slab + c_q0 // LANES
    proj0 = _dense([x_rows], w_in0[0], gain=norm_mix0[0], out_dtype=F32, name="in_proj0", slab_out=True)
    a_p, conv_tail = _conv_prompt(proj0, conv_w[0], n_seq=n_p, seq_len=seq, c=d_conv)
    att_p = _band_attn(proj0, n_seq=n_p, seq_len=seq, dil=1, q_slab=q0_slab, k_slab=k0_slab, v_slab=k0_slab + 1,
                       n_q_slabs=c_q0 // LANES, shared_kv=True, sink=swa_sink[0], want_lse=False,
                       name="swa_prompt")[0]
    proj0_s = proj0[:, tp:]
    p3 = proj0_s[:q0_slab].reshape(3, nsc, n_s, s_len, LANES)
    p3 = jnp.transpose(p3, (0, 3, 2, 1, 4)).reshape(3, s_len, n_s, d_conv)
    a_s, conv_new = _conv_step(p3, jnp.transpose(cache_conv[0], (1, 0, 2)), conv_w[0])
    a_s = jnp.transpose(a_s, (1, 0, 2)).reshape(ts, d_conv)
    kvt0 = _proj_t(jnp.transpose(w_in0[0][:, k0_col:]), x_rows[1], norm_mix0[0], "kv_t0")
    q_s = proj0_s[q0_slab:k0_slab].reshape(kvh, grp // 2, n_s, s_len, 2, HEAD_DIM)
    q_s = jnp.transpose(q_s, (2, 3, 1, 4, 0, 5)).reshape(n_s, s_len * grp, c_kv0)
    sink_col = jnp.broadcast_to(swa_sink[0][:, None, :], (kvh, s_len, grp)).reshape(kvh * s_len * grp, 1)
    o_s, swa_new = _step_attn(q_s, kvt0.reshape(2, c_kv0, ts), _cache_to_slabs(cache_swa_kv[0]), dil=1,
                              tok_div=grp, sink_col=sink_col, want_lse=False, bsz=8, name="swa_step")
    att_s = jnp.transpose(o_s.reshape(n_s, s_len, grp, kvh, HEAD_DIM), (0, 1, 3, 2, 4)).reshape(ts, c_q0)
    h = _dense([(a_p, a_s.astype(BF16)), (att_p, att_s.astype(BF16))], w_out0[0], res=x_rows, out_dtype=F32,
               name="out_proj0")
    h = _ffn(h, norm_ffn0[0], w_gate0[0], w_up0[0], w_down0[0], "ffn0")

    proj1 = _dense([h], w_in1[0], gain=norm_mix1[0], out_dtype=F32, name="in_proj1", slab_out=True)
    proj1_s = proj1[:, tp:]
    c_h = h1 * HEAD_DIM
    nsh = c_h // LANES
    pairs_p, pairs_s, dil_p, dil_s = [], [], [], []
    for g, dil in enumerate(DILATIONS):
        window = dil_caches[g].shape[2]
        s0 = g * 3 * nsh
        pairs_p.append(tuple(_band_attn(proj1, n_seq=n_p, seq_len=seq, dil=dil, q_slab=s0, k_slab=s0 + nsh,
                                        v_slab=s0 + 2 * nsh, n_q_slabs=nsh, want_lse=True,
                                        name=f"dil{g}_prompt")))
        keep = min(window, seq)
        state = _state_slabs(proj1, first_slab=s0 + nsh, n_slabs=2 * nsh, n_seq=n_p, seq_len=seq, keep=keep,
                             name=f"dil{g}_state")
        dil_p.append(_slabs_to_cache(state.reshape(n_p, 2, c_h, keep), h1)[None])
        lo = g * c_g
        q_g = jnp.transpose(proj1_s[s0:s0 + nsh].reshape(nsh, n_s, s_len, LANES), (1, 2, 0, 3))
        q_g = q_g.reshape(n_s, s_len, c_h)
        q_g = jnp.concatenate([q_g, jnp.zeros((n_s, SUBLANES - s_len, c_h), F32)], axis=1)
        kvt = _proj_t(jnp.transpose(w_in1[0][:, lo + c_h:lo + c_g]), h[tp:], norm_mix1[0], f"kv_t1_{g}")
        o_sg, l_sg, cache_new = _step_attn(q_g, kvt.reshape(2, c_h, ts), _cache_to_slabs(dil_caches[g][0]),
                                           dil=dil, tok_div=1, want_lse=True,
                                           bsz=max(1, 2048 // window), name=f"dil{g}_step")
        pairs_s.append((o_sg.reshape(n_s * SUBLANES, c_h), l_sg.reshape(n_s * SUBLANES, c_h)))
        dil_s.append(_slabs_to_cache(cache_new, h1)[None])
    comb_s = _combine(pairs_s, "combine_step").reshape(n_s, SUBLANES, c_h)[:, :s_len].reshape(ts, c_h)
    h = _combine_proj(pairs_p, comb_s, w_out1[0], h, "out_proj1")
    y_p, y_s = _moe(h, norm_ffn1[0], w_router[0], w_gate1[0], w_up1[0], w_down1[0], norm_final, tp)

    y_prompt = y_p.reshape(n_p, seq, d)
    y_sample = y_s.reshape(n_s, s_len, d)
    new_conv_prompt = conv_tail[:, SUBLANES - (CONV_W - 1):][None]
    new_conv_sample = jnp.transpose(conv_new, (1, 0, 2))[None]
    keep0 = min(SPAN, seq)
    swa_state = _state_slabs(proj0, first_slab=k0_slab, n_slabs=2, n_seq=n_p, seq_len=seq, keep=keep0,
                             name="swa_state")
    new_swa_kv_prompt = _slabs_to_cache(swa_state, kvh)[None]
    new_swa_kv_sample = _slabs_to_cache(swa_new, kvh)[None]
    return (y_prompt, y_sample, new_conv_prompt, new_conv_sample, new_swa_kv_prompt, new_swa_kv_sample,
            dil_p[0], dil_s[0], dil_p[1], dil_s[1], dil_p[2], dil_s[2])
```

```python
import functools

import jax
import jax.numpy as jnp
from jax import lax
from jax.experimental import pallas as pl
from jax.experimental.pallas import tpu as pltpu

F32 = jnp.float32
BF16 = jnp.bfloat16
I32 = jnp.int32

EPS = 1e-5
HEAD_DIM = 64
SPAN = 128
CONV_W = 3
DILATIONS = (1, 4, 16)
TOP_K = 2
LANES = 128
SUBLANES = 8
VMEM_LIMIT_BYTES = 56 * 1024 * 1024
NT_DIMS = (((1,), (1,)), ((), ()))
ROW_TILES = (512, 256, 128, 64, 32, 16, 8)
ATTN_TILE = 512
STEP_CHUNK = 512
STEP_POSITIONS = 2048
F32_BYTES = 4


def _params(*sem):
    return pltpu.CompilerParams(dimension_semantics=sem, vmem_limit_bytes=VMEM_LIMIT_BYTES)


def _pick(n, candidates):
    for c in candidates:
        if n % c == 0:
            return c
    raise ValueError(f"no tile for {n} in {candidates}")


def _rms(x, g):
    y = x * lax.rsqrt(jnp.mean(x * x, axis=-1, keepdims=True) + EPS)
    return y * g


STAGE_SLOTS = 4
STAGE_BYTES = 2 * 1024 * 1024


def _stage_rows(k, n):
    rows = k
    while rows * n * F32_BYTES > STAGE_BYTES and rows % 32 == 0:
        rows //= 2
    return rows


def _load_bf16(w_hbm, w_vmem, stage, sem):
    n_slots, chunk = stage.shape[0], stage.shape[1]
    n_chunks = w_hbm.shape[0] // chunk

    def copy(c):
        return pltpu.make_async_copy(w_hbm.at[pl.ds(c * chunk, chunk), :], stage.at[c % n_slots],
                                     sem.at[c % n_slots])

    for c in range(min(n_slots - 1, n_chunks)):
        copy(c).start(c % 2)
    for c in range(n_chunks):
        if c + n_slots - 1 < n_chunks:
            copy(c + n_slots - 1).start((c + n_slots - 1) % 2)
        copy(c).wait()
        w_vmem[c * chunk:(c + 1) * chunk, :] = stage[c % n_slots].astype(BF16)


def _dense_body(*refs, ks, split, head_tiles, has_gain, has_res, slab_out, cn):
    n_in = len(ks)
    step = pl.program_id(0)
    refs = list(refs)

    def take(is_split):
        count = 2 if is_split else 1
        parts = tuple(refs[:count])
        del refs[:count]
        return parts

    def read(parts, cols=slice(None)):
        if len(parts) == 1:
            return parts[0][:, cols]
        return jnp.where(step < head_tiles, parts[0][:, cols], parts[1][:, cols])

    xs = [take(s) for s in split[:n_in]]
    g_ref = refs.pop(0) if has_gain else None
    w_hbm = refs.pop(0)
    res = take(split[n_in]) if has_res else None
    o_ref, w_ref, stage, sem = refs

    @pl.when(step == 0)
    def _():
        _load_bf16(w_hbm, w_ref, stage, sem)

    if has_gain:
        lhs = [_rms(read(xs[0]), g_ref[...]).astype(BF16)]
    else:
        lhs = [read(x).astype(BF16) for x in xs]
    n = w_ref.shape[1]
    for c in range(n // cn):
        cols = slice(c * cn, (c + 1) * cn)
        acc = None
        row0 = 0
        for a, k in zip(lhs, ks):
            d = jnp.dot(a, w_ref[row0:row0 + k, cols], preferred_element_type=F32)
            acc = d if acc is None else acc + d
            row0 += k
        if has_res:
            acc = read(res, cols) + acc
        if slab_out:
            for s in range(cn // LANES):
                o_ref[c * (cn // LANES) + s] = acc[:, s * LANES:(s + 1) * LANES].astype(o_ref.dtype)
        else:
            o_ref[:, cols] = acc.astype(o_ref.dtype)


def _dense(xs, w, *, gain=None, res=None, out_dtype, name, slab_out=False):
    def n_rows(a):
        return sum(p.shape[0] for p in a) if isinstance(a, tuple) else a.shape[0]

    def width(a):
        return a[0].shape[1] if isinstance(a, tuple) else a.shape[1]

    operands = list(xs) + ([res] if res is not None else [])
    pairs = [a for a in operands if isinstance(a, tuple)]
    t = n_rows(xs[0])
    k_all, n = w.shape
    ks = tuple(width(x) for x in xs)
    assert sum(ks) == k_all
    gcd_rows = t
    for a in pairs:
        gcd_rows = min(gcd_rows, a[1].shape[0])
    tm = _pick(gcd_rows, ROW_TILES)
    assert t % tm == 0
    head_tiles = pairs[0][0].shape[0] // tm if pairs else 0
    for a in pairs:
        assert a[0].shape[0] == head_tiles * tm
    cn = _pick(n, (512, 384, 256, 128))
    in_specs, args = [], []

    def add_rows(a, k):
        if isinstance(a, tuple):
            in_specs.append(pl.BlockSpec((tm, k), lambda i: (jnp.minimum(i, head_tiles - 1), 0)))
            in_specs.append(pl.BlockSpec((tm, k), lambda i: (jnp.maximum(i - head_tiles, 0), 0)))
            args.extend(a)
        else:
            in_specs.append(pl.BlockSpec((tm, k), lambda i: (i, 0)))
            args.append(a)

    for x, k in zip(xs, ks):
        add_rows(x, k)
    if gain is not None:
        in_specs.append(pl.BlockSpec((1, ks[0]), lambda i: (0, 0)))
        args.append(gain.reshape(1, -1))
    in_specs.append(pl.BlockSpec(memory_space=pl.ANY))
    args.append(w)
    if res is not None:
        add_rows(res, n)
    split = tuple(isinstance(a, tuple) for a in operands)
    if slab_out:
        out_shape = jax.ShapeDtypeStruct((n // LANES, t, LANES), out_dtype)
        out_spec = pl.BlockSpec((n // LANES, tm, LANES), lambda i: (0, i, 0))
    else:
        out_shape = jax.ShapeDtypeStruct((t, n), out_dtype)
        out_spec = pl.BlockSpec((tm, n), lambda i: (i, 0))
    return pl.pallas_call(
        functools.partial(_dense_body, ks=ks, split=split, head_tiles=head_tiles, has_gain=gain is not None,
                          has_res=res is not None, slab_out=slab_out, cn=cn),
        out_shape=out_shape,
        grid=(t // tm,),
        in_specs=in_specs,
        out_specs=out_spec,
        scratch_shapes=[pltpu.VMEM((k_all, n), BF16), pltpu.VMEM((STAGE_SLOTS, _stage_rows(k_all, n), n), F32),
                        pltpu.SemaphoreType.DMA((STAGE_SLOTS,))],
        compiler_params=_params("arbitrary"),
        name=name,
    )(*args)


def _proj_t_body(w_ref, x_ref, g_ref, o_ref):
    xn = _rms(x_ref[...], g_ref[...]).astype(BF16)
    o_ref[...] = lax.dot_general(w_ref[...].astype(BF16), xn, NT_DIMS, preferred_element_type=F32)


def _proj_t(w_t, x, gain, name):
    c, k = w_t.shape
    rows = x.shape[0]
    tc = _pick(c, (512, 256, 128))
    return pl.pallas_call(
        _proj_t_body,
        out_shape=jax.ShapeDtypeStruct((c, rows), F32),
        grid=(c // tc,),
        in_specs=[pl.BlockSpec((tc, k), lambda i: (i, 0)),
                  pl.BlockSpec((rows, k), lambda i: (0, 0)),
                  pl.BlockSpec((1, k), lambda i: (0, 0))],
        out_specs=pl.BlockSpec((tc, rows), lambda i: (i, 0)),
        compiler_params=_params("parallel"),
        name=name,
    )(w_t, x, gain.reshape(1, -1))


def _ffn_body(x_ref, g_ref, wg_hbm, wu_hbm, wd_hbm, o_ref, wg_ref, wu_ref, wd_ref, stage_in, stage_out, sem, *,
              tf):
    @pl.when(pl.program_id(0) == 0)
    def _():
        _load_bf16(wg_hbm, wg_ref, stage_in, sem)
        _load_bf16(wu_hbm, wu_ref, stage_in, sem)
        _load_bf16(wd_hbm, wd_ref, stage_out, sem)

    x = x_ref[...]
    xn = _rms(x, g_ref[...]).astype(BF16)
    acc = None
    for c in range(wg_ref.shape[1] // tf):
        cols = slice(c * tf, (c + 1) * tf)
        gate = jnp.dot(xn, wg_ref[:, cols], preferred_element_type=F32)
        up = jnp.dot(xn, wu_ref[:, cols], preferred_element_type=F32)
        act = (gate * jax.nn.sigmoid(gate) * up).astype(BF16)
        d = jnp.dot(act, wd_ref[cols, :], preferred_element_type=F32)
        acc = d if acc is None else acc + d
    o_ref[...] = x + acc


def _ffn(x, gain, w_gate, w_up, w_down, name):
    t, d = x.shape
    f = w_gate.shape[1]
    tm = _pick(t, ROW_TILES)
    tf = f
    anywhere = pl.BlockSpec(memory_space=pl.ANY)
    return pl.pallas_call(
        functools.partial(_ffn_body, tf=tf),
        out_shape=jax.ShapeDtypeStruct((t, d), F32),
        grid=(t // tm,),
        in_specs=[pl.BlockSpec((tm, d), lambda i: (i, 0)),
                  pl.BlockSpec((1, d), lambda i: (0, 0)),
                  anywhere, anywhere, anywhere],
        out_specs=pl.BlockSpec((tm, d), lambda i: (i, 0)),
        scratch_shapes=[pltpu.VMEM((d, f), BF16), pltpu.VMEM((d, f), BF16), pltpu.VMEM((f, d), BF16),
                        pltpu.VMEM((STAGE_SLOTS, _stage_rows(d, f), f), F32),
                        pltpu.VMEM((STAGE_SLOTS, _stage_rows(f, d), d), F32),
                        pltpu.SemaphoreType.DMA((STAGE_SLOTS,))],
        compiler_params=_params("arbitrary"),
        name=name,
    )(x, gain.reshape(1, -1), w_gate, w_up, w_down)


def _band_body(*refs, dil, mb, shared_kv, has_sink, want_lse):
    q_ref, kc_ref, kp_ref, vc_ref, vp_ref = refs[:5]
    pos = 5
    sink_ref = refs[pos] if has_sink else None
    pos += int(has_sink)
    o_ref = refs[pos]
    lse_ref = refs[pos + 1] if want_lse else None
    t = pl.program_id(1)
    scale = HEAD_DIM ** -0.5
    qi = lax.broadcasted_iota(I32, (2 * SPAN, 2 * SPAN), 0) & (SPAN - 1)
    kj = lax.broadcasted_iota(I32, (2 * SPAN, 2 * SPAN), 1)
    band = (kj >= qi) & (kj <= qi + SPAN)
    band_first = band & (kj >= jnp.where(t == 0, SPAN, 0))
    lane = lax.broadcasted_iota(I32, (1, LANES), 1)
    halves = [lane < HEAD_DIM, lane >= HEAD_DIM]

    def rows_of(ref, s, r, count):
        if dil == 1:
            return ref[s, 0:count, :]
        return ref[s, pl.ds(r, count, stride=dil), :]

    def pair(q_a, mask_a, q_b, mask_b, kw, vw, msk, heads):
        q2 = jnp.concatenate([jnp.where(mask_a, q_a, 0.0), jnp.where(mask_b, q_b, 0.0)], axis=0).astype(BF16)
        s = lax.dot_general(q2, kw, NT_DIMS, preferred_element_type=F32)
        s = jnp.where(msk, s, -jnp.inf)
        m = jnp.max(s, axis=-1, keepdims=True)
        if has_sink:
            sink = jnp.concatenate([jnp.full((SPAN, 1), sink_ref[h], F32) for h in heads], axis=0)
            m = jnp.maximum(m, sink)
        p = jnp.exp(s - m)
        den = jnp.sum(p, axis=-1, keepdims=True)
        if has_sink:
            den = den + jnp.exp(sink - m)
        o = jnp.dot(p.astype(BF16), vw, preferred_element_type=F32) / den
        return o, m + jnp.log(den)

    def store(s, r, b, o_tile, lse_tile):
        if dil == 1:
            o_ref[b * SPAN:(b + 1) * SPAN, s * LANES:(s + 1) * LANES] = o_tile.astype(o_ref.dtype)
            if want_lse:
                lse_ref[b * SPAN:(b + 1) * SPAN, s * LANES:(s + 1) * LANES] = lse_tile
        else:
            o_ref[s, pl.ds(r + dil * b * SPAN, SPAN, stride=dil), :] = o_tile
            if want_lse:
                lse_ref[s, pl.ds(r + dil * b * SPAN, SPAN, stride=dil), :] = lse_tile

    n_slabs = q_ref.shape[0]
    first_head = 2 * n_slabs * pl.program_id(2)
    top, bot = slice(0, SPAN), slice(SPAN, 2 * SPAN)
    for r in range(dil):
        qs = [rows_of(q_ref, s, r, mb * SPAN) * scale for s in range(n_slabs)]
        kf = [jnp.concatenate([rows_of(kp_ref, s, r, SPAN), rows_of(kc_ref, s, r, mb * SPAN)], axis=0)
              for s in range(kc_ref.shape[0])]
        vf = [jnp.concatenate([rows_of(vp_ref, s, r, SPAN), rows_of(vc_ref, s, r, mb * SPAN)], axis=0)
              for s in range(vc_ref.shape[0])]
        ks = [k.astype(BF16) for k in kf]
        vs = [v.astype(BF16) for v in vf]
        if shared_kv:
            k_sw = pltpu.roll(kf[0], HEAD_DIM, 1).astype(BF16)
            v_sw = pltpu.roll(vf[0], HEAD_DIM, 1).astype(BF16)
        for b in range(mb):
            blk = slice(b * SPAN, (b + 1) * SPAN)
            keys = slice(b * SPAN, (b + 2) * SPAN)
            msk = band_first if b == 0 else band
            if not shared_kv:
                for s in range(n_slabs):
                    o, lse = pair(qs[s][blk], halves[0], qs[s][blk], halves[1], ks[s][keys], vs[s][keys], msk,
                                  (first_head + 2 * s, first_head + 2 * s + 1))
                    store(s, r, b, jnp.where(halves[0], o[top], o[bot]), jnp.where(halves[0], lse[top], lse[bot]))
            else:
                for kvh in range(2):
                    sa, sb = 2 * kvh, 2 * kvh + 1
                    o_al, l_al = pair(qs[sa][blk], halves[kvh], qs[sb][blk], halves[kvh], ks[0][keys],
                                      vs[0][keys], msk, (2 * sa + kvh, 2 * sb + kvh))
                    o_sw, l_sw = pair(qs[sa][blk], halves[1 - kvh], qs[sb][blk], halves[1 - kvh], k_sw[keys],
                                      v_sw[keys], msk, (2 * sa + 1 - kvh, 2 * sb + 1 - kvh))
                    store(sa, r, b, jnp.where(halves[kvh], o_al[top], o_sw[top]),
                          jnp.where(halves[kvh], l_al[top], l_sw[top]))
                    store(sb, r, b, jnp.where(halves[kvh], o_al[bot], o_sw[bot]),
                          jnp.where(halves[kvh], l_al[bot], l_sw[bot]))


def _band_attn(slabs, *, n_seq, seq_len, dil, q_slab, k_slab, v_slab, n_q_slabs, shared_kv=False, sink=None,
               want_lse, name):
    mb = max(1, ATTN_TILE // (dil * SPAN))
    tp = dil * SPAN * mb
    nt = seq_len // tp
    assert seq_len % tp == 0 and (not shared_kv or (n_q_slabs == 4 and dil == 1))
    prev_rows = dil * SPAN
    per_step = n_q_slabs if (shared_kv or tp <= ATTN_TILE) else 1
    kv_block = 1 if shared_kv else per_step
    kv_step = 0 if shared_kv else 1

    def cur(base, step):
        return lambda n, t, s: (base + s * step, n * nt + t, 0)

    def prev(base, step):
        return lambda n, t, s: (base + s * step, jnp.maximum((n * nt + t) * mb - 1, 0), 0)

    in_specs = [pl.BlockSpec((per_step, tp, LANES), cur(q_slab // per_step, 1)),
                pl.BlockSpec((kv_block, tp, LANES), cur(k_slab // kv_block, kv_step)),
                pl.BlockSpec((kv_block, prev_rows, LANES), prev(k_slab // kv_block, kv_step)),
                pl.BlockSpec((kv_block, tp, LANES), cur(v_slab // kv_block, kv_step)),
                pl.BlockSpec((kv_block, prev_rows, LANES), prev(v_slab // kv_block, kv_step))]
    assert q_slab % per_step == 0 and k_slab % kv_block == 0 and v_slab % kv_block == 0
    args = [slabs] * 5
    if sink is not None:
        in_specs.append(pl.BlockSpec(memory_space=pltpu.SMEM))
        args.append(sink.reshape(-1).astype(F32))
    rows = n_seq * seq_len
    if dil == 1:
        spec = pl.BlockSpec((tp, per_step * LANES), lambda n, t, s: (n * nt + t, s))
        out_shape = [jax.ShapeDtypeStruct((rows, n_q_slabs * LANES), BF16)]
        lse_shape = jax.ShapeDtypeStruct((rows, n_q_slabs * LANES), F32)
    else:
        spec = pl.BlockSpec((per_step, tp, LANES), lambda n, t, s: (s, n * nt + t, 0))
        out_shape = [jax.ShapeDtypeStruct((n_q_slabs, rows, LANES), F32)]
        lse_shape = jax.ShapeDtypeStruct((n_q_slabs, rows, LANES), F32)
    out_specs = [spec]
    if want_lse:
        out_shape.append(lse_shape)
        out_specs.append(spec)
    return pl.pallas_call(
        functools.partial(_band_body, dil=dil, mb=mb, shared_kv=shared_kv, has_sink=sink is not None,
                          want_lse=want_lse),
        out_shape=out_shape,
        grid=(n_seq, nt, n_q_slabs // per_step),
        in_specs=in_specs,
        out_specs=out_specs,
        compiler_params=_params("parallel", "parallel", "parallel"),
        name=name,
    )(*args)


def _combine_body(o0, l0, o1, l1, o2, l2, out_ref):
    def tile(ref):
        return ref[...].reshape(ref.shape[-2:]).astype(F32)

    a0, a1, a2 = tile(l0), tile(l1), tile(l2)
    m = jnp.maximum(jnp.maximum(a0, a1), a2)
    e0, e1, e2 = jnp.exp(a0 - m), jnp.exp(a1 - m), jnp.exp(a2 - m)
    num = e0 * tile(o0) + e1 * tile(o1) + e2 * tile(o2)
    out_ref[...] = (num / (e0 + e1 + e2)).astype(out_ref.dtype)


def _combine(pairs, name):
    first = pairs[0][0]
    rows, c = first.shape if first.ndim == 2 else (first.shape[1], first.shape[0] * LANES)
    tm = _pick(rows, (1024,) + ROW_TILES)
    flat = pl.BlockSpec((tm, LANES), lambda i, s: (i, s))
    slab = pl.BlockSpec((1, tm, LANES), lambda i, s: (s, i, 0))
    args = [a for pair in pairs for a in pair]
    return pl.pallas_call(
        _combine_body,
        out_shape=jax.ShapeDtypeStruct((rows, c), BF16),
        grid=(rows // tm, c // LANES),
        in_specs=[flat if a.ndim == 2 else slab for a in args],
        out_specs=flat,
        compiler_params=_params("parallel", "parallel"),
        name=name,
    )(*args)


def _combine_proj_body(o0, l0, o1, l1, o2, l2, tail_ref, res_ref, w_hbm, out_ref, w_ref, stage, sem, *, head_tiles):
    step = pl.program_id(0)

    @pl.when(step == 0)
    def _():
        _load_bf16(w_hbm, w_ref, stage, sem)

    parts = []
    for s in range(o1.shape[0]):
        lanes = slice(s * LANES, (s + 1) * LANES)
        a0, a1, a2 = l0[:, lanes], l1[s], l2[s]
        m = jnp.maximum(jnp.maximum(a0, a1), a2)
        e0, e1, e2 = jnp.exp(a0 - m), jnp.exp(a1 - m), jnp.exp(a2 - m)
        num = e0 * o0[:, lanes].astype(F32) + e1 * o1[s] + e2 * o2[s]
        parts.append(num / (e0 + e1 + e2))
    comb = jnp.concatenate(parts, axis=1).astype(BF16)
    comb = jnp.where(step < head_tiles, comb, tail_ref[...])
    out_ref[...] = res_ref[...] + jnp.dot(comb, w_ref[...], preferred_element_type=F32)


def _combine_proj(pairs, tail, w, res, name):
    (o0, l0), (o1, l1), (o2, l2) = pairs
    head, c = o0.shape
    t, n = res.shape
    tm = _pick(tail.shape[0], ROW_TILES)
    assert head % tm == 0 and t == head + tail.shape[0] and o1.ndim == 3 and o2.ndim == 3
    head_tiles = head // tm

    def first(i):
        return jnp.minimum(i, head_tiles - 1)

    flat = pl.BlockSpec((tm, c), lambda i: (first(i), 0))
    slab = pl.BlockSpec((c // LANES, tm, LANES), lambda i: (0, first(i), 0))
    return pl.pallas_call(
        functools.partial(_combine_proj_body, head_tiles=head_tiles),
        out_shape=jax.ShapeDtypeStruct((t, n), F32),
        grid=(t // tm,),
        in_specs=[flat, flat, slab, slab, slab, slab,
                  pl.BlockSpec((tm, c), lambda i: (jnp.maximum(i - head_tiles, 0), 0)),
                  pl.BlockSpec((tm, n), lambda i: (i, 0)),
                  pl.BlockSpec(memory_space=pl.ANY)],
        out_specs=pl.BlockSpec((tm, n), lambda i: (i, 0)),
        scratch_shapes=[pltpu.VMEM((c, n), BF16), pltpu.VMEM((STAGE_SLOTS, _stage_rows(c, n), n), F32),
                        pltpu.SemaphoreType.DMA((STAGE_SLOTS,))],
        compiler_params=_params("arbitrary"),
        name=name,
    )(o0, l0, o1, l1, o2, l2, tail, res, w)


def _state_body(x_ref, o_ref):
    o_ref[0, 0] = x_ref[0].T


def _state_slabs(slabs, *, first_slab, n_slabs, n_seq, seq_len, keep, name):
    assert seq_len % keep == 0
    per_seq = seq_len // keep
    return pl.pallas_call(
        _state_body,
        out_shape=jax.ShapeDtypeStruct((n_seq, n_slabs, LANES, keep), F32),
        grid=(n_seq, n_slabs),
        in_specs=[pl.BlockSpec((1, keep, LANES), lambda n, s: (first_slab + s, (n + 1) * per_seq - 1, 0))],
        out_specs=pl.BlockSpec((1, 1, LANES, keep), lambda n, s: (n, s, 0, 0)),
        compiler_params=_params("parallel", "parallel"),
        name=name,
    )(slabs)


def _conv_prompt_body(gb_ref, gc_ref, xa_ref, gcp_ref, xap_ref, w_ref, a_ref, st_ref):
    t = pl.program_id(1)
    w = w_ref[...]
    for s in range(gb_ref.shape[0]):
        lanes = slice(s * LANES, (s + 1) * LANES)
        u = gc_ref[s] * xa_ref[s]
        up = jnp.where(t == 0, 0.0, gcp_ref[s] * xap_ref[s])
        ext = jnp.concatenate([up, u], axis=0)
        y = (w[0:1, lanes] * pltpu.roll(ext, 2, 0)[SUBLANES:]
             + w[1:2, lanes] * pltpu.roll(ext, 1, 0)[SUBLANES:]) + w[2:3, lanes] * u
        a_ref[:, lanes] = (gb_ref[s] * y).astype(a_ref.dtype)
        st_ref[0, :, lanes] = u[u.shape[0] - SUBLANES:]


def _conv_prompt(slabs, conv_w, *, n_seq, seq_len, c):
    tq = _pick(seq_len, (512, 256, 128))
    nt = seq_len // tq
    rb = tq // SUBLANES
    ns = c // LANES

    def cur(part):
        return lambda n, t: (part, n * nt + t, 0)

    def prev(part):
        return lambda n, t: (part, jnp.maximum((n * nt + t) * rb - 1, 0), 0)

    return pl.pallas_call(
        _conv_prompt_body,
        out_shape=[jax.ShapeDtypeStruct((n_seq * seq_len, c), BF16),
                   jax.ShapeDtypeStruct((n_seq, SUBLANES, c), F32)],
        grid=(n_seq, nt),
        in_specs=[pl.BlockSpec((ns, tq, LANES), cur(0)), pl.BlockSpec((ns, tq, LANES), cur(1)),
                  pl.BlockSpec((ns, tq, LANES), cur(2)),
                  pl.BlockSpec((ns, SUBLANES, LANES), prev(1)), pl.BlockSpec((ns, SUBLANES, LANES), prev(2)),
                  pl.BlockSpec((CONV_W, c), lambda n, t: (0, 0))],
        out_specs=[pl.BlockSpec((tq, c), lambda n, t: (n * nt + t, 0)),
                   pl.BlockSpec((1, SUBLANES, c), lambda n, t: (n, 0, 0))],
        compiler_params=_params("parallel", "arbitrary"),
        name="conv_prompt",
    )(slabs, slabs, slabs, slabs, slabs, conv_w)


def _conv_step_body(p_ref, prev_ref, w_ref, a_ref, st_ref):
    s_len = p_ref.shape[1]
    w = w_ref[...]
    hist = [prev_ref[k] for k in range(CONV_W - 1)] + [p_ref[1, s] * p_ref[2, s] for s in range(s_len)]
    for s in range(s_len):
        y = (w[0:1] * hist[s] + w[1:2] * hist[s + 1]) + w[2:3] * hist[s + 2]
        a_ref[s] = p_ref[0, s] * y
    for k in range(CONV_W - 1):
        st_ref[k] = hist[s_len + k]


def _conv_step(p3, prev, conv_w):
    _, s_len, n, c = p3.shape
    return pl.pallas_call(
        _conv_step_body,
        out_shape=[jax.ShapeDtypeStruct((s_len, n, c), F32), jax.ShapeDtypeStruct((CONV_W - 1, n, c), F32)],
        name="conv_step",
        compiler_params=pltpu.CompilerParams(vmem_limit_bytes=VMEM_LIMIT_BYTES),
    )(p3, prev, conv_w)


def _step_body(*refs, n_blk, qr, tok_div, n_new, w, dil, cw, bsz, has_sink, want_lse):
    q_ref, kvt_ref, cache_ref = refs[:3]
    pos = 3
    sink_ref = refs[pos] if has_sink else None
    pos += int(has_sink)
    o_ref = refs[pos]
    pos += 1
    lse_ref = refs[pos] if want_lse else None
    pos += int(want_lse)
    cout_ref = refs[pos]
    r_dim = n_blk * HEAD_DIM
    nrb = n_blk * qr
    n_chunks = w // cw
    scale = HEAD_DIM ** -0.5
    step = pl.program_id(0)
    lane_r = lax.broadcasted_iota(I32, (1, r_dim), 1)
    blk_masks = [(lane_r >= j * HEAD_DIM) & (lane_r < (j + 1) * HEAD_DIM) for j in range(n_blk)]
    row = lax.broadcasted_iota(I32, (nrb, 1), 0)
    tok = (row % qr) // tok_div
    lane_c = lax.broadcasted_iota(I32, (1, cw), 1)
    lane_n = lax.broadcasted_iota(I32, (1, LANES), 1)

    def valid(pos_l):
        ok = (pos_l >= tok) & (pos_l <= w + tok)
        if dil > 1:
            ok = ok & (((pos_l - tok) & (dil - 1)) == 0)
        return ok

    new_lane0 = LANES - n_new

    def shift_window(b, kv, new_tile):
        n_cols = w // LANES
        prev = pltpu.roll(cache_ref[b, kv, :, 0:LANES], new_lane0, 1)
        for j in range(n_cols):
            if j + 1 < n_cols:
                nxt = pltpu.roll(cache_ref[b, kv, :, (j + 1) * LANES:(j + 2) * LANES], new_lane0, 1)
            else:
                nxt = new_tile
            cout_ref[b, kv, :, j * LANES:(j + 1) * LANES] = jnp.where(lane_n < new_lane0, prev, nxt)
            prev = nxt

    for b in range(bsz):
        off = ((step * bsz + b) * n_new) % LANES
        shift = (new_lane0 + LANES - off) % LANES
        new_k = pltpu.roll(kvt_ref[0], shift, 1)
        new_v = pltpu.roll(kvt_ref[1], shift, 1)
        shift_window(b, 0, new_k)
        shift_window(b, 1, new_v)
        q = q_ref[b] * scale
        qbd = jnp.concatenate([jnp.where(mk, q, 0.0) for mk in blk_masks], axis=0).astype(BF16)
        scores = []
        for c in range(n_chunks):
            lo, hi = c * cw, (c + 1) * cw
            kc = cache_ref[b, 0, :, lo:hi]
            sc = jnp.dot(qbd, kc.astype(BF16), preferred_element_type=F32)
            scores.append(jnp.where(valid(lane_c + lo), sc, -jnp.inf))
        sc = jnp.dot(qbd, new_k.astype(BF16), preferred_element_type=F32)
        scores.append(jnp.where(valid(lane_n + (w - new_lane0)) & (lane_n >= new_lane0), sc, -jnp.inf))
        m = functools.reduce(jnp.maximum, [jnp.max(s, axis=-1, keepdims=True) for s in scores])
        if has_sink:
            m = jnp.maximum(m, sink_ref[...])
        probs = [jnp.exp(s - m) for s in scores]
        den = functools.reduce(lambda a, c: a + c, [jnp.sum(p, axis=-1, keepdims=True) for p in probs])
        if has_sink:
            den = den + jnp.exp(sink_ref[...] - m)
        pv = lax.dot_general(probs[-1].astype(BF16), new_v.astype(BF16), NT_DIMS, preferred_element_type=F32)
        for c in range(n_chunks):
            lo, hi = c * cw, (c + 1) * cw
            vc = cache_ref[b, 1, :, lo:hi]
            pv = pv + lax.dot_general(probs[c].astype(BF16), vc.astype(BF16), NT_DIMS,
                                      preferred_element_type=F32)
        o = jnp.zeros((qr, r_dim), F32)
        m_e = jnp.zeros((qr, r_dim), F32)
        den_e = jnp.zeros((qr, r_dim), F32)
        for j, mk in enumerate(blk_masks):
            rows = slice(j * qr, (j + 1) * qr)
            o = jnp.where(mk, pv[rows], o)
            m_e = jnp.where(mk, m[rows], m_e)
            den_e = jnp.where(mk, den[rows], den_e)
        o_ref[b] = o / den_e
        if want_lse:
            lse_ref[b] = m_e + jnp.log(den_e)


def _step_attn(q, kvt, cache, *, dil, tok_div, sink_col=None, want_lse, name):
    n, qr, r_dim = q.shape
    w = cache.shape[-1]
    n_new = kvt.shape[2] // n
    n_blk = r_dim // HEAD_DIM
    cw = min(w, STEP_CHUNK)
    bsz = max(1, STEP_POSITIONS // w)
    assert LANES % (bsz * n_new) == 0 and n % bsz == 0
    in_specs = [pl.BlockSpec((bsz, qr, r_dim), lambda i: (i, 0, 0)),
                pl.BlockSpec((2, r_dim, LANES), lambda i: (0, 0, (i * bsz * n_new) // LANES)),
                pl.BlockSpec((bsz, 2, r_dim, w), lambda i: (i, 0, 0, 0))]
    args = [q, kvt, cache]
    if sink_col is not None:
        in_specs.append(pl.BlockSpec((n_blk * qr, 1), lambda i: (0, 0)))
        args.append(sink_col)
    o_spec = pl.BlockSpec((bsz, qr, r_dim), lambda i: (i, 0, 0))
    out_shape = [jax.ShapeDtypeStruct((n, qr, r_dim), F32)]
    out_specs = [o_spec]
    if want_lse:
        out_shape.append(jax.ShapeDtypeStruct((n, qr, r_dim), F32))
        out_specs.append(o_spec)
    out_shape.append(jax.ShapeDtypeStruct(cache.shape, F32))
    out_specs.append(pl.BlockSpec((bsz, 2, r_dim, w), lambda i: (i, 0, 0, 0)))
    return pl.pallas_call(
        functools.partial(_step_body, n_blk=n_blk, qr=qr, tok_div=tok_div, n_new=n_new, w=w, dil=dil, cw=cw, bsz=bsz,
                          has_sink=sink_col is not None, want_lse=want_lse),
        out_shape=out_shape,
        grid=(n // bsz,),
        in_specs=in_specs,
        out_specs=out_specs,
        compiler_params=_params("parallel"),
        name=name,
    )(*args)


def _rows_to_tiles(ref, x):
    rows = x.shape[0]
    for s in range(SUBLANES):
        ref[pl.ds(s, rows, stride=SUBLANES), :] = x[:, s * LANES:(s + 1) * LANES]


def _tiles_to_rows(ref, rows):
    return jnp.concatenate([ref[pl.ds(s, rows, stride=SUBLANES), :] for s in range(SUBLANES)], axis=1)


def _router_body(h_ref, g_ref, wr_ref, xn_ref, im_ref, gm_ref, cnt_ref, carry_ref, *, n_exp):
    i = pl.program_id(0)

    @pl.when(i == 0)
    def _():
        carry_ref[...] = jnp.zeros_like(carry_ref)

    xn = _rms(h_ref[...], g_ref[...])
    _rows_to_tiles(xn_ref, xn)
    tm = xn.shape[0]
    wr = wr_ref[...]
    xh = xn.astype(BF16)
    xl = (xn - xh.astype(F32)).astype(BF16)
    wh = wr.astype(BF16)
    wl = (wr - wh.astype(F32)).astype(BF16)
    lg = jnp.dot(xh, wh, preferred_element_type=F32) + (
        jnp.dot(xh, wl, preferred_element_type=F32) + jnp.dot(xl, wh, preferred_element_type=F32))
    lane = lax.broadcasted_iota(I32, (tm, LANES), 1)
    lane_f = lane.astype(F32)
    lg = jnp.where(lane < n_exp, lg, -jnp.inf)
    m1 = jnp.max(lg, axis=-1, keepdims=True)
    i1 = jnp.min(jnp.where(lg == m1, lane_f, float(LANES)), axis=-1, keepdims=True)
    lg2 = jnp.where(lane_f == i1, -jnp.inf, lg)
    m2 = jnp.max(lg2, axis=-1, keepdims=True)
    i2 = jnp.min(jnp.where(lg2 == m2, lane_f, float(LANES)), axis=-1, keepdims=True)
    e = jnp.exp(m2 - m1)
    g1 = 1.0 / (1.0 + e)
    g2 = e / (1.0 + e)
    sel1 = lane_f == i1
    sel2 = lane_f == i2
    onehot = jnp.where(sel1 | sel2, 1.0, 0.0)
    r_i = lax.broadcasted_iota(I32, (tm, tm), 0)
    c_i = lax.broadcasted_iota(I32, (tm, tm), 1)
    tri = jnp.where(c_i < r_i, 1.0, 0.0).astype(BF16)
    before = jnp.dot(tri, onehot.astype(BF16), preferred_element_type=F32) + carry_ref[0:1]
    r1 = jnp.sum(jnp.where(sel1, before, 0.0), axis=-1, keepdims=True)
    r2 = jnp.sum(jnp.where(sel2, before, 0.0), axis=-1, keepdims=True)
    total = carry_ref[0:1] + jnp.sum(onehot, axis=0, keepdims=True)
    carry_ref[...] = jnp.broadcast_to(total, carry_ref.shape)
    cnt_ref[...] = jnp.broadcast_to(total, cnt_ref.shape).astype(I32)
    meta = jnp.where(lane == 0, i1, jnp.where(lane == 1, i2, jnp.where(lane == 2, r1, jnp.where(lane == 3, r2, 0.0))))
    im_ref[...] = meta.T[:SUBLANES].astype(I32)
    gm_ref[...] = jnp.where(lane == 0, g1, jnp.where(lane == 1, g2, 0.0))


def _router(h, gain, w_router):
    t, d = h.shape
    n_exp = w_router.shape[1]
    tm = _pick(t, ROW_TILES)
    wr = jnp.zeros((d, LANES), F32).at[:, :n_exp].set(w_router)
    return pl.pallas_call(
        functools.partial(_router_body, n_exp=n_exp),
        out_shape=[jax.ShapeDtypeStruct((t * SUBLANES, LANES), F32), jax.ShapeDtypeStruct((SUBLANES, t), I32),
                   jax.ShapeDtypeStruct((t, LANES), F32), jax.ShapeDtypeStruct((SUBLANES, LANES), I32)],
        grid=(t // tm,),
        in_specs=[pl.BlockSpec((tm, d), lambda i: (i, 0)),
                  pl.BlockSpec((1, d), lambda i: (0, 0)),
                  pl.BlockSpec((d, LANES), lambda i: (0, 0))],
        out_specs=[pl.BlockSpec((tm * SUBLANES, LANES), lambda i: (i, 0)),
                   pl.BlockSpec((SUBLANES, tm), lambda i: (0, i)),
                   pl.BlockSpec((tm, LANES), lambda i: (i, 0)),
                   pl.BlockSpec((SUBLANES, LANES), lambda i: (0, 0))],
        scratch_shapes=[pltpu.VMEM((SUBLANES, LANES), F32)],
        compiler_params=_params("arbitrary"),
        name="moe_router",
    )(h, gain.reshape(1, -1), wr)


DISPATCH_SLOTS = 3


def _dispatch_body(s1_ref, s2_ref, x_hbm, zero_ref, out_ref, xbuf, load_sem, store_sem, *, tm, n_steps):
    del zero_ref
    i = pl.program_id(0)
    cur = i % DISPATCH_SLOTS

    def load(step):
        slot = step % DISPATCH_SLOTS
        return pltpu.make_async_copy(x_hbm.at[pl.ds(step * tm, tm)], xbuf.at[slot], load_sem.at[slot])

    def drain(slot):
        for k in range(TOP_K):
            pltpu.make_async_copy(xbuf.at[slot], out_ref.at[pl.ds(0, tm)], store_sem.at[slot, k]).wait()

    @pl.when(i == 0)
    def _():
        load(0).start()
        if n_steps > 1:
            load(1).start()

    load(i).wait()

    def issue(r, carry):
        src = xbuf.at[cur, r]
        pltpu.make_async_copy(src, out_ref.at[s1_ref[0, 0, r]], store_sem.at[cur, 0]).start(0)
        pltpu.make_async_copy(src, out_ref.at[s2_ref[0, 0, r]], store_sem.at[cur, 1]).start(1)
        return carry

    lax.fori_loop(0, tm, issue, 0, unroll=8)

    @pl.when(i >= 1)
    def _():
        drain((i - 1) % DISPATCH_SLOTS)

    @pl.when(i + 2 < n_steps)
    def _():
        load(i + 2).start()

    @pl.when(i == n_steps - 1)
    def _():
        drain(cur)


def _dispatch(xn, slot1, slot2, n_slots):
    t = xn.shape[0]
    tm = _pick(t, ROW_TILES)
    nt = t // tm
    tile = xn.shape[1:]
    smem = pl.BlockSpec((1, 1, tm), lambda i: (i, 0, 0), memory_space=pltpu.SMEM)
    anywhere = pl.BlockSpec(memory_space=pl.ANY)
    return pl.pallas_call(
        functools.partial(_dispatch_body, tm=tm, n_steps=nt),
        out_shape=jax.ShapeDtypeStruct((n_slots,) + tile, F32),
        grid=(nt,),
        in_specs=[smem, smem, anywhere, anywhere],
        out_specs=pl.BlockSpec(memory_space=pl.ANY),
        scratch_shapes=[pltpu.VMEM((DISPATCH_SLOTS, tm) + tile, F32), pltpu.SemaphoreType.DMA((DISPATCH_SLOTS,)),
                        pltpu.SemaphoreType.DMA((DISPATCH_SLOTS, TOP_K))],
        input_output_aliases={3: 0},
        compiler_params=_params("arbitrary"),
        name="moe_dispatch",
    )(slot1.reshape(nt, 1, tm), slot2.reshape(nt, 1, tm), xn, jnp.zeros((n_slots,) + tile, F32))


def _experts_body(te_ref, nu_ref, x_ref, wg_hbm, wu_hbm, wd_hbm, o_ref, wg_ref, wu_ref, wd_ref, stage_in, stage_out,
                  sem, *, tm, tf):
    i = pl.program_id(0)
    active = i < nu_ref[0]
    expert = te_ref[i]
    changed = (i == 0) | (expert != te_ref[jnp.maximum(i - 1, 0)])

    @pl.when(active & changed)
    def _():
        _load_bf16(wg_hbm.at[expert], wg_ref, stage_in, sem)
        _load_bf16(wu_hbm.at[expert], wu_ref, stage_in, sem)
        _load_bf16(wd_hbm.at[expert], wd_ref, stage_out, sem)

    @pl.when(active)
    def _():
        xb = _tiles_to_rows(x_ref, tm).astype(BF16)
        acc = None
        for c in range(wg_ref.shape[1] // tf):
            cols = slice(c * tf, (c + 1) * tf)
            gate = jnp.dot(xb, wg_ref[:, cols], preferred_element_type=F32)
            up = jnp.dot(xb, wu_ref[:, cols], preferred_element_type=F32)
            act = (gate * jax.nn.sigmoid(gate) * up).astype(BF16)
            d = jnp.dot(act, wd_ref[cols, :], preferred_element_type=F32)
            acc = d if acc is None else acc + d
        _rows_to_tiles(o_ref, acc)

    @pl.when(jnp.logical_not(active))
    def _():
        o_ref[...] = jnp.zeros_like(o_ref)


def _experts(xs, tile_expert, n_used, w_gate, w_up, w_down, tm):
    n_slots = xs.shape[0] // SUBLANES
    d = SUBLANES * LANES
    f = w_gate.shape[2]
    tf = _pick(f, (1792, 1024, 512, 256, 128))
    n_tiles = n_slots // tm
    anywhere = pl.BlockSpec(memory_space=pl.ANY)
    grid_spec = pltpu.PrefetchScalarGridSpec(
        num_scalar_prefetch=2,
        grid=(n_tiles,),
        in_specs=[pl.BlockSpec((tm * SUBLANES, LANES), lambda i, te, nu: (jnp.minimum(i, nu[0] - 1), 0)),
                  anywhere, anywhere, anywhere],
        out_specs=pl.BlockSpec((tm * SUBLANES, LANES), lambda i, te, nu: (i, 0)),
        scratch_shapes=[pltpu.VMEM((d, f), BF16), pltpu.VMEM((d, f), BF16), pltpu.VMEM((f, d), BF16),
                        pltpu.VMEM((STAGE_SLOTS, _stage_rows(d, f), f), F32),
                        pltpu.VMEM((STAGE_SLOTS, _stage_rows(f, d), d), F32),
                        pltpu.SemaphoreType.DMA((STAGE_SLOTS,))],
    )
    return pl.pallas_call(
        functools.partial(_experts_body, tm=tm, tf=tf),
        out_shape=jax.ShapeDtypeStruct((n_slots * SUBLANES, LANES), F32),
        grid_spec=grid_spec,
        compiler_params=_params("arbitrary"),
        name="moe_experts",
    )(tile_expert, n_used, xs, w_gate, w_up, w_down)


def _gather_norm_body(s1_ref, s2_ref, n1_ref, n2_ref, h_ref, gm_ref, g_ref, ys_ref, oa_ref, ob_ref, ya_ref, yb_ref,
                      sem, *, tm, head_tiles, n_steps):
    i = pl.program_id(0)
    cur = i % 2

    def gather(a_ref, b_ref, buf):
        def issue(r, carry):
            dst = pl.ds(pl.multiple_of(r * SUBLANES, SUBLANES), SUBLANES)
            pltpu.make_async_copy(ys_ref.at[a_ref[0, 0, r]], ya_ref.at[buf, dst], sem.at[buf, 0]).start(0)
            pltpu.make_async_copy(ys_ref.at[b_ref[0, 0, r]], yb_ref.at[buf, dst], sem.at[buf, 1]).start(1)
            return carry

        lax.fori_loop(0, tm, issue, 0, unroll=8)

    @pl.when(i == 0)
    def _():
        gather(s1_ref, s2_ref, 0)

    @pl.when(i + 1 < n_steps)
    def _():
        gather(n1_ref, n2_ref, 1 - cur)

    pltpu.make_async_copy(ya_ref.at[cur], ya_ref.at[cur], sem.at[cur, 0]).wait()
    pltpu.make_async_copy(yb_ref.at[cur], yb_ref.at[cur], sem.at[cur, 1]).wait()
    gm = gm_ref[...]
    y = _rms(h_ref[...] + (gm[:, 0:1] * _tiles_to_rows(ya_ref.at[cur], tm)
                           + gm[:, 1:2] * _tiles_to_rows(yb_ref.at[cur], tm)), g_ref[...])

    @pl.when(i < head_tiles)
    def _():
        oa_ref[...] = y

    @pl.when(i >= head_tiles)
    def _():
        ob_ref[...] = y


def _gather_norm(h, gates, slot1, slot2, ys, gain, split):
    t, d = h.shape
    tm = _pick(split, ROW_TILES)
    assert t % tm == 0
    nt = t // tm
    head_tiles = split // tm
    smem = pl.BlockSpec((1, 1, tm), lambda i: (i, 0, 0), memory_space=pltpu.SMEM)
    smem_next = pl.BlockSpec((1, 1, tm), lambda i: (jnp.minimum(i + 1, nt - 1), 0, 0), memory_space=pltpu.SMEM)
    s1, s2 = slot1.reshape(nt, 1, tm), slot2.reshape(nt, 1, tm)
    return pl.pallas_call(
        functools.partial(_gather_norm_body, tm=tm, head_tiles=head_tiles, n_steps=nt),
        out_shape=[jax.ShapeDtypeStruct((split, d), F32), jax.ShapeDtypeStruct((t - split, d), F32)],
        grid=(nt,),
        in_specs=[smem, smem, smem_next, smem_next, pl.BlockSpec((tm, d), lambda i: (i, 0)),
                  pl.BlockSpec((tm, LANES), lambda i: (i, 0)),
                  pl.BlockSpec((1, d), lambda i: (0, 0)),
                  pl.BlockSpec(memory_space=pl.ANY)],
        out_specs=[pl.BlockSpec((tm, d), lambda i: (jnp.minimum(i, head_tiles - 1), 0)),
                   pl.BlockSpec((tm, d), lambda i: (jnp.maximum(i - head_tiles, 0), 0))],
        scratch_shapes=[pltpu.VMEM((2, tm * SUBLANES, LANES), F32), pltpu.VMEM((2, tm * SUBLANES, LANES), F32),
                        pltpu.SemaphoreType.DMA((2, 2))],
        compiler_params=_params("arbitrary"),
        name="moe_gather_norm",
    )(s1, s2, s1, s2, h, gates, gain.reshape(1, -1), ys)


def _moe(h, gain, w_router, w_gate, w_up, w_down, final_gain, split):
    t, _ = h.shape
    n_exp = w_router.shape[1]
    tm_e = 512
    xn, imeta, gates, counts = _router(h, gain, w_router)
    idx1, idx2, rank1, rank2 = imeta[0], imeta[1], imeta[2], imeta[3]
    cnt = counts[0, :n_exp]
    padded = ((cnt + tm_e - 1) // tm_e) * tm_e
    ends = jnp.cumsum(padded)
    starts = ends - padded
    slot1 = starts[idx1] + rank1
    slot2 = starts[idx2] + rank2
    n_tiles = (TOP_K * t + n_exp * (tm_e - 1)) // tm_e
    n_used = (ends[-1] // tm_e).astype(I32)
    tile_start = jnp.arange(n_tiles, dtype=I32) * tm_e
    tile_expert = jnp.sum((tile_start[:, None] >= ends[None, :]).astype(I32), axis=1)
    tile_expert = jnp.minimum(tile_expert, n_exp - 1)
    assert h.shape[1] == SUBLANES * LANES
    n_slots = n_tiles * tm_e
    xs = _dispatch(xn.reshape(t, SUBLANES, LANES), slot1, slot2, n_slots)
    ys = _experts(xs.reshape(n_slots * SUBLANES, LANES), tile_expert, n_used.reshape(1), w_gate, w_up, w_down, tm_e)
    return _gather_norm(h, gates, slot1, slot2, ys.reshape(n_slots, SUBLANES, LANES), final_gain, split)


def _cache_to_slabs(cache):
    n, w, two, heads, hd = cache.shape
    return jnp.transpose(cache, (0, 2, 3, 4, 1)).reshape(n, two, heads * hd, w)


def _slabs_to_cache(slabs, heads):
    n, two, _, w = slabs.shape
    return jnp.transpose(slabs.reshape(n, two, heads, HEAD_DIM, w), (0, 4, 1, 2, 3))


def kernel(x_prompt, x_sample, cache_conv, cache_swa_kv, cache_dil_kv0, cache_dil_kv1, cache_dil_kv2, norm_mix0, w_in0, conv_w, swa_sink, w_out0, norm_ffn0, w_gate0, w_up0, w_down0, norm_mix1, w_in1, w_out1, norm_ffn1, w_router, w_gate1, w_up1, w_down1, norm_final):
    n_p, seq, d = x_prompt.shape
    n_s, s_len, _ = x_sample.shape
    tp, ts = n_p * seq, n_s * s_len
    d_conv = conv_w.shape[2]
    kvh, grp = swa_sink.shape[1], swa_sink.shape[2]
    hq0 = kvh * grp
    h1 = cache_dil_kv0.shape[4]
    c_q0, c_kv0 = hq0 * HEAD_DIM, kvh * HEAD_DIM
    q0_col = 3 * d_conv
    k0_col = q0_col + c_q0
    c_g = 3 * h1 * HEAD_DIM
    dil_caches = (cache_dil_kv0, cache_dil_kv1, cache_dil_kv2)

    x_rows = (x_prompt.reshape(tp, d), x_sample.reshape(ts, d))

    assert c_kv0 == LANES and grp % 2 == 0 and d_conv % LANES == 0
    nsc = d_conv // LANES
    q0_slab = 3 * nsc
    k0_slab = q0_slab + c_q0 // LANES
    proj0 = _dense([x_rows], w_in0[0], gain=norm_mix0[0], out_dtype=F32, name="in_proj0", slab_out=True)
    a_p, conv_tail = _conv_prompt(proj0, conv_w[0], n_seq=n_p, seq_len=seq, c=d_conv)
    att_p = _band_attn(proj0, n_seq=n_p, seq_len=seq, dil=1, q_slab=q0_slab, k_slab=k0_slab, v_slab=k0_slab + 1,
                       n_q_slabs=c_q0 // LANES, shared_kv=True, sink=swa_sink[0], want_lse=False,
                       name="swa_prompt")[0]
    proj0_s = proj0[:, tp:]
    p3 = proj0_s[:q0_slab].reshape(3, nsc, n_s, s_len, LANES)
    p3 = jnp.transpose(p3, (0, 3, 2, 1, 4)).reshape(3, s_len, n_s, d_conv)
    a_s, conv_new = _conv_step(p3, jnp.transpose(cache_conv[0], (1, 0, 2)), conv_w[0])
    a_s = jnp.transpose(a_s, (1, 0, 2)).reshape(ts, d_conv)
    kvt0 = _proj_t(jnp.transpose(w_in0[0][:, k0_col:]), x_rows[1], norm_mix0[0], "kv_t0")
    q_s = proj0_s[q0_slab:k0_slab].reshape(kvh, grp // 2, n_s, s_len, 2, HEAD_DIM)
    q_s = jnp.transpose(q_s, (2, 3, 1, 4, 0, 5)).reshape(n_s, s_len * grp, c_kv0)
    sink_col = jnp.broadcast_to(swa_sink[0][:, None, :], (kvh, s_len, grp)).reshape(kvh * s_len * grp, 1)
    o_s, swa_new = _step_attn(q_s, kvt0.reshape(2, c_kv0, ts), _cache_to_slabs(cache_swa_kv[0]), dil=1,
                              tok_div=grp, sink_col=sink_col, want_lse=False, name="swa_step")
    att_s = jnp.transpose(o_s.reshape(n_s, s_len, grp, kvh, HEAD_DIM), (0, 1, 3, 2, 4)).reshape(ts, c_q0)
    h = _dense([(a_p, a_s.astype(BF16)), (att_p, att_s.astype(BF16))], w_out0[0], res=x_rows, out_dtype=F32,
               name="out_proj0")
    h = _ffn(h, norm_ffn0[0], w_gate0[0], w_up0[0], w_down0[0], "ffn0")

    proj1 = _dense([h], w_in1[0], gain=norm_mix1[0], out_dtype=F32, name="in_proj1", slab_out=True)
    proj1_s = proj1[:, tp:]
    c_h = h1 * HEAD_DIM
    nsh = c_h // LANES
    pairs_p, pairs_s, dil_p, dil_s = [], [], [], []
    for g, dil in enumerate(DILATIONS):
        window = dil_caches[g].shape[2]
        s0 = g * 3 * nsh
        pairs_p.append(tuple(_band_attn(proj1, n_seq=n_p, seq_len=seq, dil=dil, q_slab=s0, k_slab=s0 + nsh,
                                        v_slab=s0 + 2 * nsh, n_q_slabs=nsh, want_lse=True,
                                        name=f"dil{g}_prompt")))
        keep = min(window, seq)
        state = _state_slabs(proj1, first_slab=s0 + nsh, n_slabs=2 * nsh, n_seq=n_p, seq_len=seq, keep=keep,
                             name=f"dil{g}_state")
        dil_p.append(_slabs_to_cache(state.reshape(n_p, 2, c_h, keep), h1)[None])
        lo = g * c_g
        q_g = jnp.transpose(proj1_s[s0:s0 + nsh].reshape(nsh, n_s, s_len, LANES), (1, 2, 0, 3))
        q_g = q_g.reshape(n_s, s_len, c_h)
        q_g = jnp.concatenate([q_g, jnp.zeros((n_s, SUBLANES - s_len, c_h), F32)], axis=1)
        kvt = _proj_t(jnp.transpose(w_in1[0][:, lo + c_h:lo + c_g]), h[tp:], norm_mix1[0], f"kv_t1_{g}")
        o_sg, l_sg, cache_new = _step_attn(q_g, kvt.reshape(2, c_h, ts), _cache_to_slabs(dil_caches[g][0]),
                                           dil=dil, tok_div=1, want_lse=True, name=f"dil{g}_step")
        pairs_s.append((o_sg.reshape(n_s * SUBLANES, c_h), l_sg.reshape(n_s * SUBLANES, c_h)))
        dil_s.append(_slabs_to_cache(cache_new, h1)[None])
    comb_s = _combine(pairs_s, "combine_step").reshape(n_s, SUBLANES, c_h)[:, :s_len].reshape(ts, c_h)
    h = _combine_proj(pairs_p, comb_s, w_out1[0], h, "out_proj1")
    y_p, y_s = _moe(h, norm_ffn1[0], w_router[0], w_gate1[0], w_up1[0], w_down1[0], norm_final, tp)

    y_prompt = y_p.reshape(n_p, seq, d)
    y_sample = y_s.reshape(n_s, s_len, d)
    new_conv_prompt = conv_tail[:, SUBLANES - (CONV_W - 1):][None]
    new_conv_sample = jnp.transpose(conv_new, (1, 0, 2))[None]
    keep0 = min(SPAN, seq)
    swa_state = _state_slabs(proj0, first_slab=k0_slab, n_slabs=2, n_seq=n_p, seq_len=seq, keep=keep0,
                             name="swa_state")
    new_swa_kv_prompt = _slabs_to_cache(swa_state, kvh)[None]
    new_swa_kv_sample = _slabs_to_cache(swa_new, kvh)[None]
    return (y_prompt, y_sample, new_conv_prompt, new_conv_sample, new_swa_kv_prompt, new_swa_kv_sample,
            dil_p[0], dil_s[0], dil_p[1], dil_s[1], dil_p[2], dil_s[2])
```

```python
import functools

import jax
import jax.numpy as jnp
from jax import lax
from jax.experimental import pallas as pl
from jax.experimental.pallas import tpu as pltpu

F32 = jnp.float32
BF16 = jnp.bfloat16
I32 = jnp.int32

EPS = 1e-5
HEAD_DIM = 64
SPAN = 128
CONV_W = 3
DILATIONS = (1, 4, 16)
TOP_K = 2
LANES = 128
SUBLANES = 8
VMEM_LIMIT_BYTES = 56 * 1024 * 1024
NT_DIMS = (((1,), (1,)), ((), ()))
ROW_TILES = (512, 256, 128, 64, 32, 16, 8)
ATTN_TILE = 1024
STEP_CHUNK = 512
STEP_POSITIONS = 2048
F32_BYTES = 4


def _params(*sem):
    return pltpu.CompilerParams(dimension_semantics=sem, vmem_limit_bytes=VMEM_LIMIT_BYTES)


def _pick(n, candidates):
    for c in candidates:
        if n % c == 0:
            return c
    raise ValueError(f"no tile for {n} in {candidates}")


def _rms(x, g):
    y = x * lax.rsqrt(jnp.mean(x * x, axis=-1, keepdims=True) + EPS)
    return y * g


STAGE_SLOTS = 4
STAGE_BYTES = 2 * 1024 * 1024


def _stage_rows(k, n):
    rows = k
    while rows * n * F32_BYTES > STAGE_BYTES and rows % 32 == 0:
        rows //= 2
    return rows


def _load_bf16(w_hbm, w_vmem, stage, sem):
    n_slots, chunk = stage.shape[0], stage.shape[1]
    n_chunks = w_hbm.shape[0] // chunk

    def copy(c):
        return pltpu.make_async_copy(w_hbm.at[pl.ds(c * chunk, chunk), :], stage.at[c % n_slots],
                                     sem.at[c % n_slots])

    for c in range(min(n_slots - 1, n_chunks)):
        copy(c).start(c % 2)
    for c in range(n_chunks):
        if c + n_slots - 1 < n_chunks:
            copy(c + n_slots - 1).start((c + n_slots - 1) % 2)
        copy(c).wait()
        w_vmem[c * chunk:(c + 1) * chunk, :] = stage[c % n_slots].astype(BF16)


def _dense_body(*refs, ks, split, head_tiles, has_gain, has_res, slab_out, cn):
    n_in = len(ks)
    step = pl.program_id(0)
    refs = list(refs)

    def take(is_split):
        count = 2 if is_split else 1
        parts = tuple(refs[:count])
        del refs[:count]
        return parts

    def read(parts, cols=slice(None)):
        if len(parts) == 1:
            return parts[0][:, cols]
        return jnp.where(step < head_tiles, parts[0][:, cols], parts[1][:, cols])

    xs = [take(s) for s in split[:n_in]]
    g_ref = refs.pop(0) if has_gain else None
    w_hbm = refs.pop(0)
    res = take(split[n_in]) if has_res else None
    o_ref, w_ref, stage, sem = refs

    @pl.when(step == 0)
    def _():
        _load_bf16(w_hbm, w_ref, stage, sem)

    if has_gain:
        lhs = [_rms(read(xs[0]), g_ref[...]).astype(BF16)]
    else:
        lhs = [read(x).astype(BF16) for x in xs]
    n = w_ref.shape[1]
    for c in range(n // cn):
        cols = slice(c * cn, (c + 1) * cn)
        acc = None
        row0 = 0
        for a, k in zip(lhs, ks):
            d = jnp.dot(a, w_ref[row0:row0 + k, cols], preferred_element_type=F32)
            acc = d if acc is None else acc + d
            row0 += k
        if has_res:
            acc = read(res, cols) + acc
        if slab_out:
            for s in range(cn // LANES):
                o_ref[c * (cn // LANES) + s] = acc[:, s * LANES:(s + 1) * LANES].astype(o_ref.dtype)
        else:
            o_ref[:, cols] = acc.astype(o_ref.dtype)


def _dense(xs, w, *, gain=None, res=None, out_dtype, name, slab_out=False):
    def n_rows(a):
        return sum(p.shape[0] for p in a) if isinstance(a, tuple) else a.shape[0]

    def width(a):
        return a[0].shape[1] if isinstance(a, tuple) else a.shape[1]

    operands = list(xs) + ([res] if res is not None else [])
    pairs = [a for a in operands if isinstance(a, tuple)]
    t = n_rows(xs[0])
    k_all, n = w.shape
    ks = tuple(width(x) for x in xs)
    assert sum(ks) == k_all
    gcd_rows = t
    for a in pairs:
        gcd_rows = min(gcd_rows, a[1].shape[0])
    tm = _pick(gcd_rows, ROW_TILES)
    assert t % tm == 0
    head_tiles = pairs[0][0].shape[0] // tm if pairs else 0
    for a in pairs:
        assert a[0].shape[0] == head_tiles * tm
    cn = _pick(n, (512, 384, 256, 128))
    in_specs, args = [], []

    def add_rows(a, k):
        if isinstance(a, tuple):
            in_specs.append(pl.BlockSpec((tm, k), lambda i: (jnp.minimum(i, head_tiles - 1), 0)))
            in_specs.append(pl.BlockSpec((tm, k), lambda i: (jnp.maximum(i - head_tiles, 0), 0)))
            args.extend(a)
        else:
            in_specs.append(pl.BlockSpec((tm, k), lambda i: (i, 0)))
            args.append(a)

    for x, k in zip(xs, ks):
        add_rows(x, k)
    if gain is not None:
        in_specs.append(pl.BlockSpec((1, ks[0]), lambda i: (0, 0)))
        args.append(gain.reshape(1, -1))
    in_specs.append(pl.BlockSpec(memory_space=pl.ANY))
    args.append(w)
    if res is not None:
        add_rows(res, n)
    split = tuple(isinstance(a, tuple) for a in operands)
    if slab_out:
        out_shape = jax.ShapeDtypeStruct((n // LANES, t, LANES), out_dtype)
        out_spec = pl.BlockSpec((n // LANES, tm, LANES), lambda i: (0, i, 0))
    else:
        out_shape = jax.ShapeDtypeStruct((t, n), out_dtype)
        out_spec = pl.BlockSpec((tm, n), lambda i: (i, 0))
    return pl.pallas_call(
        functools.partial(_dense_body, ks=ks, split=split, head_tiles=head_tiles, has_gain=gain is not None,
                          has_res=res is not None, slab_out=slab_out, cn=cn),
        out_shape=out_shape,
        grid=(t // tm,),
        in_specs=in_specs,
        out_specs=out_spec,
        scratch_shapes=[pltpu.VMEM((k_all, n), BF16), pltpu.VMEM((STAGE_SLOTS, _stage_rows(k_all, n), n), F32),
                        pltpu.SemaphoreType.DMA((STAGE_SLOTS,))],
        compiler_params=_params("arbitrary"),
        name=name,
    )(*args)


def _proj_t_body(w_ref, x_ref, g_ref, o_ref):
    xn = _rms(x_ref[...], g_ref[...]).astype(BF16)
    o_ref[...] = lax.dot_general(w_ref[...].astype(BF16), xn, NT_DIMS, preferred_element_type=F32)


def _proj_t(w_t, x, gain, name):
    c, k = w_t.shape
    rows = x.shape[0]
    tc = _pick(c, (512, 256, 128))
    return pl.pallas_call(
        _proj_t_body,
        out_shape=jax.ShapeDtypeStruct((c, rows), F32),
        grid=(c // tc,),
        in_specs=[pl.BlockSpec((tc, k), lambda i: (i, 0)),
                  pl.BlockSpec((rows, k), lambda i: (0, 0)),
                  pl.BlockSpec((1, k), lambda i: (0, 0))],
        out_specs=pl.BlockSpec((tc, rows), lambda i: (i, 0)),
        compiler_params=_params("parallel"),
        name=name,
    )(w_t, x, gain.reshape(1, -1))


def _ffn_body(x_ref, g_ref, wg_hbm, wu_hbm, wd_hbm, o_ref, wg_ref, wu_ref, wd_ref, stage_in, stage_out, sem, *,
              tf):
    @pl.when(pl.program_id(0) == 0)
    def _():
        _load_bf16(wg_hbm, wg_ref, stage_in, sem)
        _load_bf16(wu_hbm, wu_ref, stage_in, sem)
        _load_bf16(wd_hbm, wd_ref, stage_out, sem)

    x = x_ref[...]
    xn = _rms(x, g_ref[...]).astype(BF16)
    acc = None
    for c in range(wg_ref.shape[1] // tf):
        cols = slice(c * tf, (c + 1) * tf)
        gate = jnp.dot(xn, wg_ref[:, cols], preferred_element_type=F32)
        up = jnp.dot(xn, wu_ref[:, cols], preferred_element_type=F32)
        act = (gate * jax.nn.sigmoid(gate) * up).astype(BF16)
        d = jnp.dot(act, wd_ref[cols, :], preferred_element_type=F32)
        acc = d if acc is None else acc + d
    o_ref[...] = x + acc


def _ffn(x, gain, w_gate, w_up, w_down, name):
    t, d = x.shape
    f = w_gate.shape[1]
    tm = _pick(t, ROW_TILES)
    tf = f
    anywhere = pl.BlockSpec(memory_space=pl.ANY)
    return pl.pallas_call(
        functools.partial(_ffn_body, tf=tf),
        out_shape=jax.ShapeDtypeStruct((t, d), F32),
        grid=(t // tm,),
        in_specs=[pl.BlockSpec((tm, d), lambda i: (i, 0)),
                  pl.BlockSpec((1, d), lambda i: (0, 0)),
                  anywhere, anywhere, anywhere],
        out_specs=pl.BlockSpec((tm, d), lambda i: (i, 0)),
        scratch_shapes=[pltpu.VMEM((d, f), BF16), pltpu.VMEM((d, f), BF16), pltpu.VMEM((f, d), BF16),
                        pltpu.VMEM((STAGE_SLOTS, _stage_rows(d, f), f), F32),
                        pltpu.VMEM((STAGE_SLOTS, _stage_rows(f, d), d), F32),
                        pltpu.SemaphoreType.DMA((STAGE_SLOTS,))],
        compiler_params=_params("arbitrary"),
        name=name,
    )(x, gain.reshape(1, -1), w_gate, w_up, w_down)


def _band_body(*refs, dil, mb, shared_kv, has_sink, want_lse):
    q_ref, kc_ref, kp_ref, vc_ref, vp_ref = refs[:5]
    pos = 5
    sink_ref = refs[pos] if has_sink else None
    pos += int(has_sink)
    o_ref = refs[pos]
    lse_ref = refs[pos + 1] if want_lse else None
    t = pl.program_id(1)
    scale = HEAD_DIM ** -0.5
    qi = lax.broadcasted_iota(I32, (2 * SPAN, 2 * SPAN), 0) & (SPAN - 1)
    kj = lax.broadcasted_iota(I32, (2 * SPAN, 2 * SPAN), 1)
    band = (kj >= qi) & (kj <= qi + SPAN)
    band_first = band & (kj >= jnp.where(t == 0, SPAN, 0))
    lane = lax.broadcasted_iota(I32, (1, LANES), 1)
    halves = [lane < HEAD_DIM, lane >= HEAD_DIM]

    def rows_of(ref, s, r, count):
        if dil == 1:
            return ref[s, 0:count, :]
        return ref[s, pl.ds(r, count, stride=dil), :]

    def pair(q_a, mask_a, q_b, mask_b, kw, vw, msk, heads):
        q2 = jnp.concatenate([jnp.where(mask_a, q_a, 0.0), jnp.where(mask_b, q_b, 0.0)], axis=0).astype(BF16)
        s = lax.dot_general(q2, kw, NT_DIMS, preferred_element_type=F32)
        s = jnp.where(msk, s, -jnp.inf)
        m = jnp.max(s, axis=-1, keepdims=True)
        if has_sink:
            sink = jnp.concatenate([jnp.full((SPAN, 1), sink_ref[h], F32) for h in heads], axis=0)
            m = jnp.maximum(m, sink)
        p = jnp.exp(s - m)
        den = jnp.sum(p, axis=-1, keepdims=True)
        if has_sink:
            den = den + jnp.exp(sink - m)
        o = jnp.dot(p.astype(BF16), vw, preferred_element_type=F32) / den
        return o, m + jnp.log(den)

    def store(s, r, b, o_tile, lse_tile):
        if dil == 1:
            o_ref[b * SPAN:(b + 1) * SPAN, s * LANES:(s + 1) * LANES] = o_tile.astype(o_ref.dtype)
            if want_lse:
                lse_ref[b * SPAN:(b + 1) * SPAN, s * LANES:(s + 1) * LANES] = lse_tile
        else:
            o_ref[s, pl.ds(r + dil * b * SPAN, SPAN, stride=dil), :] = o_tile
            if want_lse:
                lse_ref[s, pl.ds(r + dil * b * SPAN, SPAN, stride=dil), :] = lse_tile

    n_slabs = q_ref.shape[0]
    first_head = 2 * n_slabs * pl.program_id(2)
    top, bot = slice(0, SPAN), slice(SPAN, 2 * SPAN)
    for r in range(dil):
        qs = [rows_of(q_ref, s, r, mb * SPAN) * scale for s in range(n_slabs)]
        kf = [jnp.concatenate([rows_of(kp_ref, s, r, SPAN), rows_of(kc_ref, s, r, mb * SPAN)], axis=0)
              for s in range(kc_ref.shape[0])]
        vf = [jnp.concatenate([rows_of(vp_ref, s, r, SPAN), rows_of(vc_ref, s, r, mb * SPAN)], axis=0)
              for s in range(vc_ref.shape[0])]
        ks = [k.astype(BF16) for k in kf]
        vs = [v.astype(BF16) for v in vf]
        if shared_kv:
            k_sw = pltpu.roll(kf[0], HEAD_DIM, 1).astype(BF16)
            v_sw = pltpu.roll(vf[0], HEAD_DIM, 1).astype(BF16)
        for b in range(mb):
            blk = slice(b * SPAN, (b + 1) * SPAN)
            keys = slice(b * SPAN, (b + 2) * SPAN)
            msk = band_first if b == 0 else band
            if not shared_kv:
                for s in range(n_slabs):
                    o, lse = pair(qs[s][blk], halves[0], qs[s][blk], halves[1], ks[s][keys], vs[s][keys], msk,
                                  (first_head + 2 * s, first_head + 2 * s + 1))
                    store(s, r, b, jnp.where(halves[0], o[top], o[bot]), jnp.where(halves[0], lse[top], lse[bot]))
            else:
                for kvh in range(2):
                    sa, sb = 2 * kvh, 2 * kvh + 1
                    o_al, l_al = pair(qs[sa][blk], halves[kvh], qs[sb][blk], halves[kvh], ks[0][keys],
                                      vs[0][keys], msk, (2 * sa + kvh, 2 * sb + kvh))
                    o_sw, l_sw = pair(qs[sa][blk], halves[1 - kvh], qs[sb][blk], halves[1 - kvh], k_sw[keys],
                                      v_sw[keys], msk, (2 * sa + 1 - kvh, 2 * sb + 1 - kvh))
                    store(sa, r, b, jnp.where(halves[kvh], o_al[top], o_sw[top]),
                          jnp.where(halves[kvh], l_al[top], l_sw[top]))
                    store(sb, r, b, jnp.where(halves[kvh], o_al[bot], o_sw[bot]),
                          jnp.where(halves[kvh], l_al[bot], l_sw[bot]))


def _band_attn(slabs, *, n_seq, seq_len, dil, q_slab, k_slab, v_slab, n_q_slabs, shared_kv=False, sink=None,
               want_lse, name):
    mb = max(1, ATTN_TILE // (dil * SPAN))
    tp = dil * SPAN * mb
    nt = seq_len // tp
    assert seq_len % tp == 0 and (not shared_kv or (n_q_slabs == 4 and dil == 1))
    prev_rows = dil * SPAN
    per_step = n_q_slabs if (shared_kv or tp <= ATTN_TILE) else 1
    kv_block = 1 if shared_kv else per_step
    kv_step = 0 if shared_kv else 1

    def cur(base, step):
        return lambda n, t, s: (base + s * step, n * nt + t, 0)

    def prev(base, step):
        return lambda n, t, s: (base + s * step, jnp.maximum((n * nt + t) * mb - 1, 0), 0)

    in_specs = [pl.BlockSpec((per_step, tp, LANES), cur(q_slab // per_step, 1)),
                pl.BlockSpec((kv_block, tp, LANES), cur(k_slab // kv_block, kv_step)),
                pl.BlockSpec((kv_block, prev_rows, LANES), prev(k_slab // kv_block, kv_step)),
                pl.BlockSpec((kv_block, tp, LANES), cur(v_slab // kv_block, kv_step)),
                pl.BlockSpec((kv_block, prev_rows, LANES), prev(v_slab // kv_block, kv_step))]
    assert q_slab % per_step == 0 and k_slab % kv_block == 0 and v_slab % kv_block == 0
    args = [slabs] * 5
    if sink is not None:
        in_specs.append(pl.BlockSpec(memory_space=pltpu.SMEM))
        args.append(sink.reshape(-1).astype(F32))
    rows = n_seq * seq_len
    if dil == 1:
        spec = pl.BlockSpec((tp, per_step * LANES), lambda n, t, s: (n * nt + t, s))
        out_shape = [jax.ShapeDtypeStruct((rows, n_q_slabs * LANES), BF16)]
        lse_shape = jax.ShapeDtypeStruct((rows, n_q_slabs * LANES), F32)
    else:
        spec = pl.BlockSpec((per_step, tp, LANES), lambda n, t, s: (s, n * nt + t, 0))
        out_shape = [jax.ShapeDtypeStruct((n_q_slabs, rows, LANES), F32)]
        lse_shape = jax.ShapeDtypeStruct((n_q_slabs, rows, LANES), F32)
    out_specs = [spec]
    if want_lse:
        out_shape.append(lse_shape)
        out_specs.append(spec)
    return pl.pallas_call(
        functools.partial(_band_body, dil=dil, mb=mb, shared_kv=shared_kv, has_sink=sink is not None,
                          want_lse=want_lse),
        out_shape=out_shape,
        grid=(n_seq, nt, n_q_slabs // per_step),
        in_specs=in_specs,
        out_specs=out_specs,
        compiler_params=_params("parallel", "parallel", "parallel"),
        name=name,
    )(*args)


def _combine_body(o0, l0, o1, l1, o2, l2, out_ref):
    def tile(ref):
        return ref[...].reshape(ref.shape[-2:]).astype(F32)

    a0, a1, a2 = tile(l0), tile(l1), tile(l2)
    m = jnp.maximum(jnp.maximum(a0, a1), a2)
    e0, e1, e2 = jnp.exp(a0 - m), jnp.exp(a1 - m), jnp.exp(a2 - m)
    num = e0 * tile(o0) + e1 * tile(o1) + e2 * tile(o2)
    out_ref[...] = (num / (e0 + e1 + e2)).astype(out_ref.dtype)


def _combine(pairs, name):
    first = pairs[0][0]
    rows, c = first.shape if first.ndim == 2 else (first.shape[1], first.shape[0] * LANES)
    tm = _pick(rows, (1024,) + ROW_TILES)
    flat = pl.BlockSpec((tm, LANES), lambda i, s: (i, s))
    slab = pl.BlockSpec((1, tm, LANES), lambda i, s: (s, i, 0))
    args = [a for pair in pairs for a in pair]
    return pl.pallas_call(
        _combine_body,
        out_shape=jax.ShapeDtypeStruct((rows, c), BF16),
        grid=(rows // tm, c // LANES),
        in_specs=[flat if a.ndim == 2 else slab for a in args],
        out_specs=flat,
        compiler_params=_params("parallel", "parallel"),
        name=name,
    )(*args)


def _combine_proj_body(o0, l0, o1, l1, o2, l2, tail_ref, res_ref, w_hbm, out_ref, w_ref, stage, sem, *, head_tiles):
    step = pl.program_id(0)

    @pl.when(step == 0)
    def _():
        _load_bf16(w_hbm, w_ref, stage, sem)

    parts = []
    for s in range(o1.shape[0]):
        lanes = slice(s * LANES, (s + 1) * LANES)
        a0, a1, a2 = l0[:, lanes], l1[s], l2[s]
        m = jnp.maximum(jnp.maximum(a0, a1), a2)
        e0, e1, e2 = jnp.exp(a0 - m), jnp.exp(a1 - m), jnp.exp(a2 - m)
        num = e0 * o0[:, lanes].astype(F32) + e1 * o1[s] + e2 * o2[s]
        parts.append(num / (e0 + e1 + e2))
    comb = jnp.concatenate(parts, axis=1).astype(BF16)
    comb = jnp.where(step < head_tiles, comb, tail_ref[...])
    out_ref[...] = res_ref[...] + jnp.dot(comb, w_ref[...], preferred_element_type=F32)


def _combine_proj(pairs, tail, w, res, name):
    (o0, l0), (o1, l1), (o2, l2) = pairs
    head, c = o0.shape
    t, n = res.shape
    tm = _pick(tail.shape[0], ROW_TILES)
    assert head % tm == 0 and t == head + tail.shape[0] and o1.ndim == 3 and o2.ndim == 3
    head_tiles = head // tm

    def first(i):
        return jnp.minimum(i, head_tiles - 1)

    flat = pl.BlockSpec((tm, c), lambda i: (first(i), 0))
    slab = pl.BlockSpec((c // LANES, tm, LANES), lambda i: (0, first(i), 0))
    return pl.pallas_call(
        functools.partial(_combine_proj_body, head_tiles=head_tiles),
        out_shape=jax.ShapeDtypeStruct((t, n), F32),
        grid=(t // tm,),
        in_specs=[flat, flat, slab, slab, slab, slab,
                  pl.BlockSpec((tm, c), lambda i: (jnp.maximum(i - head_tiles, 0), 0)),
                  pl.BlockSpec((tm, n), lambda i: (i, 0)),
                  pl.BlockSpec(memory_space=pl.ANY)],
        out_specs=pl.BlockSpec((tm, n), lambda i: (i, 0)),
        scratch_shapes=[pltpu.VMEM((c, n), BF16), pltpu.VMEM((STAGE_SLOTS, _stage_rows(c, n), n), F32),
                        pltpu.SemaphoreType.DMA((STAGE_SLOTS,))],
        compiler_params=_params("arbitrary"),
        name=name,
    )(o0, l0, o1, l1, o2, l2, tail, res, w)


def _state_body(x_ref, o_ref):
    o_ref[0, 0] = x_ref[0].T


def _state_slabs(slabs, *, first_slab, n_slabs, n_seq, seq_len, keep, name):
    assert seq_len % keep == 0
    per_seq = seq_len // keep
    return pl.pallas_call(
        _state_body,
        out_shape=jax.ShapeDtypeStruct((n_seq, n_slabs, LANES, keep), F32),
        grid=(n_seq, n_slabs),
        in_specs=[pl.BlockSpec((1, keep, LANES), lambda n, s: (first_slab + s, (n + 1) * per_seq - 1, 0))],
        out_specs=pl.BlockSpec((1, 1, LANES, keep), lambda n, s: (n, s, 0, 0)),
        compiler_params=_params("parallel", "parallel"),
        name=name,
    )(slabs)


def _conv_prompt_body(gb_ref, gc_ref, xa_ref, gcp_ref, xap_ref, w_ref, a_ref, st_ref):
    t = pl.program_id(1)
    w = w_ref[...]
    for s in range(gb_ref.shape[0]):
        lanes = slice(s * LANES, (s + 1) * LANES)
        u = gc_ref[s] * xa_ref[s]
        up = jnp.where(t == 0, 0.0, gcp_ref[s] * xap_ref[s])
        ext = jnp.concatenate([up, u], axis=0)
        y = (w[0:1, lanes] * pltpu.roll(ext, 2, 0)[SUBLANES:]
             + w[1:2, lanes] * pltpu.roll(ext, 1, 0)[SUBLANES:]) + w[2:3, lanes] * u
        a_ref[:, lanes] = (gb_ref[s] * y).astype(a_ref.dtype)
        st_ref[0, :, lanes] = u[u.shape[0] - SUBLANES:]


def _conv_prompt(slabs, conv_w, *, n_seq, seq_len, c):
    tq = _pick(seq_len, (512, 256, 128))
    nt = seq_len // tq
    rb = tq // SUBLANES
    ns = c // LANES

    def cur(part):
        return lambda n, t: (part, n * nt + t, 0)

    def prev(part):
        return lambda n, t: (part, jnp.maximum((n * nt + t) * rb - 1, 0), 0)

    return pl.pallas_call(
        _conv_prompt_body,
        out_shape=[jax.ShapeDtypeStruct((n_seq * seq_len, c), BF16),
                   jax.ShapeDtypeStruct((n_seq, SUBLANES, c), F32)],
        grid=(n_seq, nt),
        in_specs=[pl.BlockSpec((ns, tq, LANES), cur(0)), pl.BlockSpec((ns, tq, LANES), cur(1)),
                  pl.BlockSpec((ns, tq, LANES), cur(2)),
                  pl.BlockSpec((ns, SUBLANES, LANES), prev(1)), pl.BlockSpec((ns, SUBLANES, LANES), prev(2)),
                  pl.BlockSpec((CONV_W, c), lambda n, t: (0, 0))],
        out_specs=[pl.BlockSpec((tq, c), lambda n, t: (n * nt + t, 0)),
                   pl.BlockSpec((1, SUBLANES, c), lambda n, t: (n, 0, 0))],
        compiler_params=_params("parallel", "arbitrary"),
        name="conv_prompt",
    )(slabs, slabs, slabs, slabs, slabs, conv_w)


def _conv_step_body(p_ref, prev_ref, w_ref, a_ref, st_ref):
    s_len = p_ref.shape[1]
    w = w_ref[...]
    hist = [prev_ref[k] for k in range(CONV_W - 1)] + [p_ref[1, s] * p_ref[2, s] for s in range(s_len)]
    for s in range(s_len):
        y = (w[0:1] * hist[s] + w[1:2] * hist[s + 1]) + w[2:3] * hist[s + 2]
        a_ref[s] = p_ref[0, s] * y
    for k in range(CONV_W - 1):
        st_ref[k] = hist[s_len + k]


def _conv_step(p3, prev, conv_w):
    _, s_len, n, c = p3.shape
    return pl.pallas_call(
        _conv_step_body,
        out_shape=[jax.ShapeDtypeStruct((s_len, n, c), F32), jax.ShapeDtypeStruct((CONV_W - 1, n, c), F32)],
        name="conv_step",
        compiler_params=pltpu.CompilerParams(vmem_limit_bytes=VMEM_LIMIT_BYTES),
    )(p3, prev, conv_w)


def _step_body(*refs, n_blk, qr, tok_div, n_new, w, dil, cw, bsz, has_sink, want_lse):
    q_ref, kvt_ref, cache_ref = refs[:3]
    pos = 3
    sink_ref = refs[pos] if has_sink else None
    pos += int(has_sink)
    o_ref = refs[pos]
    pos += 1
    lse_ref = refs[pos] if want_lse else None
    pos += int(want_lse)
    cout_ref = refs[pos]
    r_dim = n_blk * HEAD_DIM
    nrb = n_blk * qr
    n_chunks = w // cw
    scale = HEAD_DIM ** -0.5
    step = pl.program_id(0)
    lane_r = lax.broadcasted_iota(I32, (1, r_dim), 1)
    blk_masks = [(lane_r >= j * HEAD_DIM) & (lane_r < (j + 1) * HEAD_DIM) for j in range(n_blk)]
    row = lax.broadcasted_iota(I32, (nrb, 1), 0)
    tok = (row % qr) // tok_div
    lane_c = lax.broadcasted_iota(I32, (1, cw), 1)
    lane_n = lax.broadcasted_iota(I32, (1, LANES), 1)

    def valid(pos_l):
        ok = (pos_l >= tok) & (pos_l <= w + tok)
        if dil > 1:
            ok = ok & (((pos_l - tok) & (dil - 1)) == 0)
        return ok

    new_lane0 = LANES - n_new

    def shift_window(b, kv, new_tile):
        n_cols = w // LANES
        prev = pltpu.roll(cache_ref[b, kv, :, 0:LANES], new_lane0, 1)
        for j in range(n_cols):
            if j + 1 < n_cols:
                nxt = pltpu.roll(cache_ref[b, kv, :, (j + 1) * LANES:(j + 2) * LANES], new_lane0, 1)
            else:
                nxt = new_tile
            cout_ref[b, kv, :, j * LANES:(j + 1) * LANES] = jnp.where(lane_n < new_lane0, prev, nxt)
            prev = nxt

    for b in range(bsz):
        off = ((step * bsz + b) * n_new) % LANES
        shift = (new_lane0 + LANES - off) % LANES
        new_k = pltpu.roll(kvt_ref[0], shift, 1)
        new_v = pltpu.roll(kvt_ref[1], shift, 1)
        shift_window(b, 0, new_k)
        shift_window(b, 1, new_v)
        q = q_ref[b] * scale
        qbd = jnp.concatenate([jnp.where(mk, q, 0.0) for mk in blk_masks], axis=0).astype(BF16)
        scores = []
        for c in range(n_chunks):
            lo, hi = c * cw, (c + 1) * cw
            kc = cache_ref[b, 0, :, lo:hi]
            sc = jnp.dot(qbd, kc.astype(BF16), preferred_element_type=F32)
            scores.append(jnp.where(valid(lane_c + lo), sc, -jnp.inf))
        sc = jnp.dot(qbd, new_k.astype(BF16), preferred_element_type=F32)
        scores.append(jnp.where(valid(lane_n + (w - new_lane0)) & (lane_n >= new_lane0), sc, -jnp.inf))
        m = functools.reduce(jnp.maximum, [jnp.max(s, axis=-1, keepdims=True) for s in scores])
        if has_sink:
            m = jnp.maximum(m, sink_ref[...])
        probs = [jnp.exp(s - m) for s in scores]
        den = functools.reduce(lambda a, c: a + c, [jnp.sum(p, axis=-1, keepdims=True) for p in probs])
        if has_sink:
            den = den + jnp.exp(sink_ref[...] - m)
        pv = lax.dot_general(probs[-1].astype(BF16), new_v.astype(BF16), NT_DIMS, preferred_element_type=F32)
        for c in range(n_chunks):
            lo, hi = c * cw, (c + 1) * cw
            vc = cache_ref[b, 1, :, lo:hi]
            pv = pv + lax.dot_general(probs[c].astype(BF16), vc.astype(BF16), NT_DIMS,
                                      preferred_element_type=F32)
        o = jnp.zeros((qr, r_dim), F32)
        m_e = jnp.zeros((qr, r_dim), F32)
        den_e = jnp.zeros((qr, r_dim), F32)
        for j, mk in enumerate(blk_masks):
            rows = slice(j * qr, (j + 1) * qr)
            o = jnp.where(mk, pv[rows], o)
            m_e = jnp.where(mk, m[rows], m_e)
            den_e = jnp.where(mk, den[rows], den_e)
        o_ref[b] = o / den_e
        if want_lse:
            lse_ref[b] = m_e + jnp.log(den_e)


def _step_attn(q, kvt, cache, *, dil, tok_div, sink_col=None, want_lse, name):
    n, qr, r_dim = q.shape
    w = cache.shape[-1]
    n_new = kvt.shape[2] // n
    n_blk = r_dim // HEAD_DIM
    cw = min(w, STEP_CHUNK)
    bsz = max(1, STEP_POSITIONS // w)
    assert LANES % (bsz * n_new) == 0 and n % bsz == 0
    in_specs = [pl.BlockSpec((bsz, qr, r_dim), lambda i: (i, 0, 0)),
                pl.BlockSpec((2, r_dim, LANES), lambda i: (0, 0, (i * bsz * n_new) // LANES)),
                pl.BlockSpec((bsz, 2, r_dim, w), lambda i: (i, 0, 0, 0))]
    args = [q, kvt, cache]
    if sink_col is not None:
        in_specs.append(pl.BlockSpec((n_blk * qr, 1), lambda i: (0, 0)))
        args.append(sink_col)
    o_spec = pl.BlockSpec((bsz, qr, r_dim), lambda i: (i, 0, 0))
    out_shape = [jax.ShapeDtypeStruct((n, qr, r_dim), F32)]
    out_specs = [o_spec]
    if want_lse:
        out_shape.append(jax.ShapeDtypeStruct((n, qr, r_dim), F32))
        out_specs.append(o_spec)
    out_shape.append(jax.ShapeDtypeStruct(cache.shape, F32))
    out_specs.append(pl.BlockSpec((bsz, 2, r_dim, w), lambda i: (i, 0, 0, 0)))
    return pl.pallas_call(
        functools.partial(_step_body, n_blk=n_blk, qr=qr, tok_div=tok_div, n_new=n_new, w=w, dil=dil, cw=cw, bsz=bsz,
                          has_sink=sink_col is not None, want_lse=want_lse),
        out_shape=out_shape,
        grid=(n // bsz,),
        in_specs=in_specs,
        out_specs=out_specs,
        compiler_params=_params("parallel"),
        name=name,
    )(*args)


def _rows_to_tiles(ref, x):
    rows = x.shape[0]
    for s in range(SUBLANES):
        ref[pl.ds(s, rows, stride=SUBLANES), :] = x[:, s * LANES:(s + 1) * LANES]


def _tiles_to_rows(ref, rows):
    return jnp.concatenate([ref[pl.ds(s, rows, stride=SUBLANES), :] for s in range(SUBLANES)], axis=1)


def _router_body(h_ref, g_ref, wr_ref, xn_ref, im_ref, gm_ref, cnt_ref, carry_ref, *, n_exp):
    i = pl.program_id(0)

    @pl.when(i == 0)
    def _():
        carry_ref[...] = jnp.zeros_like(carry_ref)

    xn = _rms(h_ref[...], g_ref[...])
    _rows_to_tiles(xn_ref, xn)
    tm = xn.shape[0]
    wr = wr_ref[...]
    xh = xn.astype(BF16)
    xl = (xn - xh.astype(F32)).astype(BF16)
    wh = wr.astype(BF16)
    wl = (wr - wh.astype(F32)).astype(BF16)
    lg = jnp.dot(xh, wh, preferred_element_type=F32) + (
        jnp.dot(xh, wl, preferred_element_type=F32) + jnp.dot(xl, wh, preferred_element_type=F32))
    lane = lax.broadcasted_iota(I32, (tm, LANES), 1)
    lane_f = lane.astype(F32)
    lg = jnp.where(lane < n_exp, lg, -jnp.inf)
    m1 = jnp.max(lg, axis=-1, keepdims=True)
    i1 = jnp.min(jnp.where(lg == m1, lane_f, float(LANES)), axis=-1, keepdims=True)
    lg2 = jnp.where(lane_f == i1, -jnp.inf, lg)
    m2 = jnp.max(lg2, axis=-1, keepdims=True)
    i2 = jnp.min(jnp.where(lg2 == m2, lane_f, float(LANES)), axis=-1, keepdims=True)
    e = jnp.exp(m2 - m1)
    g1 = 1.0 / (1.0 + e)
    g2 = e / (1.0 + e)
    sel1 = lane_f == i1
    sel2 = lane_f == i2
    onehot = jnp.where(sel1 | sel2, 1.0, 0.0)
    r_i = lax.broadcasted_iota(I32, (tm, tm), 0)
    c_i = lax.broadcasted_iota(I32, (tm, tm), 1)
    tri = jnp.where(c_i < r_i, 1.0, 0.0).astype(BF16)
    before = jnp.dot(tri, onehot.astype(BF16), preferred_element_type=F32) + carry_ref[0:1]
    r1 = jnp.sum(jnp.where(sel1, before, 0.0), axis=-1, keepdims=True)
    r2 = jnp.sum(jnp.where(sel2, before, 0.0), axis=-1, keepdims=True)
    total = carry_ref[0:1] + jnp.sum(onehot, axis=0, keepdims=True)
    carry_ref[...] = jnp.broadcast_to(total, carry_ref.shape)
    cnt_ref[...] = jnp.broadcast_to(total, cnt_ref.shape).astype(I32)
    meta = jnp.where(lane == 0, i1, jnp.where(lane == 1, i2, jnp.where(lane == 2, r1, jnp.where(lane == 3, r2, 0.0))))
    im_ref[...] = meta.T[:SUBLANES].astype(I32)
    gm_ref[...] = jnp.where(lane == 0, g1, jnp.where(lane == 1, g2, 0.0))


def _router(h, gain, w_router):
    t, d = h.shape
    n_exp = w_router.shape[1]
    tm = _pick(t, ROW_TILES)
    wr = jnp.zeros((d, LANES), F32).at[:, :n_exp].set(w_router)
    return pl.pallas_call(
        functools.partial(_router_body, n_exp=n_exp),
        out_shape=[jax.ShapeDtypeStruct((t * SUBLANES, LANES), F32), jax.ShapeDtypeStruct((SUBLANES, t), I32),
                   jax.ShapeDtypeStruct((t, LANES), F32), jax.ShapeDtypeStruct((SUBLANES, LANES), I32)],
        grid=(t // tm,),
        in_specs=[pl.BlockSpec((tm, d), lambda i: (i, 0)),
                  pl.BlockSpec((1, d), lambda i: (0, 0)),
                  pl.BlockSpec((d, LANES), lambda i: (0, 0))],
        out_specs=[pl.BlockSpec((tm * SUBLANES, LANES), lambda i: (i, 0)),
                   pl.BlockSpec((SUBLANES, tm), lambda i: (0, i)),
                   pl.BlockSpec((tm, LANES), lambda i: (i, 0)),
                   pl.BlockSpec((SUBLANES, LANES), lambda i: (0, 0))],
        scratch_shapes=[pltpu.VMEM((SUBLANES, LANES), F32)],
        compiler_params=_params("arbitrary"),
        name="moe_router",
    )(h, gain.reshape(1, -1), wr)


DISPATCH_SLOTS = 3


def _dispatch_body(s1_ref, s2_ref, x_hbm, zero_ref, out_ref, xbuf, load_sem, store_sem, *, tm, n_steps):
    del zero_ref
    i = pl.program_id(0)
    cur = i % DISPATCH_SLOTS

    def load(step):
        slot = step % DISPATCH_SLOTS
        return pltpu.make_async_copy(x_hbm.at[pl.ds(step * tm, tm)], xbuf.at[slot], load_sem.at[slot])

    def drain(slot):
        for k in range(TOP_K):
            pltpu.make_async_copy(xbuf.at[slot], out_ref.at[pl.ds(0, tm)], store_sem.at[slot, k]).wait()

    @pl.when(i == 0)
    def _():
        load(0).start()
        if n_steps > 1:
            load(1).start()

    load(i).wait()

    def issue(r, carry):
        src = xbuf.at[cur, r]
        pltpu.make_async_copy(src, out_ref.at[s1_ref[0, 0, r]], store_sem.at[cur, 0]).start(0)
        pltpu.make_async_copy(src, out_ref.at[s2_ref[0, 0, r]], store_sem.at[cur, 1]).start(1)
        return carry

    lax.fori_loop(0, tm, issue, 0, unroll=8)

    @pl.when(i >= 1)
    def _():
        drain((i - 1) % DISPATCH_SLOTS)

    @pl.when(i + 2 < n_steps)
    def _():
        load(i + 2).start()

    @pl.when(i == n_steps - 1)
    def _():
        drain(cur)


def _dispatch(xn, slot1, slot2, n_slots):
    t = xn.shape[0]
    tm = _pick(t, ROW_TILES)
    nt = t // tm
    tile = xn.shape[1:]
    smem = pl.BlockSpec((1, 1, tm), lambda i: (i, 0, 0), memory_space=pltpu.SMEM)
    anywhere = pl.BlockSpec(memory_space=pl.ANY)
    return pl.pallas_call(
        functools.partial(_dispatch_body, tm=tm, n_steps=nt),
        out_shape=jax.ShapeDtypeStruct((n_slots,) + tile, F32),
        grid=(nt,),
        in_specs=[smem, smem, anywhere, anywhere],
        out_specs=pl.BlockSpec(memory_space=pl.ANY),
        scratch_shapes=[pltpu.VMEM((DISPATCH_SLOTS, tm) + tile, F32), pltpu.SemaphoreType.DMA((DISPATCH_SLOTS,)),
                        pltpu.SemaphoreType.DMA((DISPATCH_SLOTS, TOP_K))],
        input_output_aliases={3: 0},
        compiler_params=_params("arbitrary"),
        name="moe_dispatch",
    )(slot1.reshape(nt, 1, tm), slot2.reshape(nt, 1, tm), xn, jnp.zeros((n_slots,) + tile, F32))


def _experts_body(te_ref, nu_ref, x_ref, wg_hbm, wu_hbm, wd_hbm, o_ref, wg_ref, wu_ref, wd_ref, stage_in, stage_out,
                  sem, *, tm, tf):
    i = pl.program_id(0)
    active = i < nu_ref[0]
    expert = te_ref[i]
    changed = (i == 0) | (expert != te_ref[jnp.maximum(i - 1, 0)])

    @pl.when(active & changed)
    def _():
        _load_bf16(wg_hbm.at[expert], wg_ref, stage_in, sem)
        _load_bf16(wu_hbm.at[expert], wu_ref, stage_in, sem)
        _load_bf16(wd_hbm.at[expert], wd_ref, stage_out, sem)

    @pl.when(active)
    def _():
        xb = _tiles_to_rows(x_ref, tm).astype(BF16)
        acc = None
        for c in range(wg_ref.shape[1] // tf):
            cols = slice(c * tf, (c + 1) * tf)
            gate = jnp.dot(xb, wg_ref[:, cols], preferred_element_type=F32)
            up = jnp.dot(xb, wu_ref[:, cols], preferred_element_type=F32)
            act = (gate * jax.nn.sigmoid(gate) * up).astype(BF16)
            d = jnp.dot(act, wd_ref[cols, :], preferred_element_type=F32)
            acc = d if acc is None else acc + d
        _rows_to_tiles(o_ref, acc)

    @pl.when(jnp.logical_not(active))
    def _():
        o_ref[...] = jnp.zeros_like(o_ref)


def _experts(xs, tile_expert, n_used, w_gate, w_up, w_down, tm):
    n_slots = xs.shape[0] // SUBLANES
    d = SUBLANES * LANES
    f = w_gate.shape[2]
    tf = _pick(f, (1792, 1024, 512, 256, 128))
    n_tiles = n_slots // tm
    anywhere = pl.BlockSpec(memory_space=pl.ANY)
    grid_spec = pltpu.PrefetchScalarGridSpec(
        num_scalar_prefetch=2,
        grid=(n_tiles,),
        in_specs=[pl.BlockSpec((tm * SUBLANES, LANES), lambda i, te, nu: (jnp.minimum(i, nu[0] - 1), 0)),
                  anywhere, anywhere, anywhere],
        out_specs=pl.BlockSpec((tm * SUBLANES, LANES), lambda i, te, nu: (i, 0)),
        scratch_shapes=[pltpu.VMEM((d, f), BF16), pltpu.VMEM((d, f), BF16), pltpu.VMEM((f, d), BF16),
                        pltpu.VMEM((STAGE_SLOTS, _stage_rows(d, f), f), F32),
                        pltpu.VMEM((STAGE_SLOTS, _stage_rows(f, d), d), F32),
                        pltpu.SemaphoreType.DMA((STAGE_SLOTS,))],
    )
    return pl.pallas_call(
        functools.partial(_experts_body, tm=tm, tf=tf),
        out_shape=jax.ShapeDtypeStruct((n_slots * SUBLANES, LANES), F32),
        grid_spec=grid_spec,
        compiler_params=_params("arbitrary"),
        name="moe_experts",
    )(tile_expert, n_used, xs, w_gate, w_up, w_down)


def _gather_norm_body(s1_ref, s2_ref, n1_ref, n2_ref, h_ref, gm_ref, g_ref, ys_ref, oa_ref, ob_ref, ya_ref, yb_ref,
                      sem, *, tm, head_tiles, n_steps):
    i = pl.program_id(0)
    cur = i % 2

    def gather(a_ref, b_ref, buf):
        def issue(r, carry):
            dst = pl.ds(pl.multiple_of(r * SUBLANES, SUBLANES), SUBLANES)
            pltpu.make_async_copy(ys_ref.at[a_ref[0, 0, r]], ya_ref.at[buf, dst], sem.at[buf, 0]).start(0)
            pltpu.make_async_copy(ys_ref.at[b_ref[0, 0, r]], yb_ref.at[buf, dst], sem.at[buf, 1]).start(1)
            return carry

        lax.fori_loop(0, tm, issue, 0, unroll=8)

    @pl.when(i == 0)
    def _():
        gather(s1_ref, s2_ref, 0)

    @pl.when(i + 1 < n_steps)
    def _():
        gather(n1_ref, n2_ref, 1 - cur)

    pltpu.make_async_copy(ya_ref.at[cur], ya_ref.at[cur], sem.at[cur, 0]).wait()
    pltpu.make_async_copy(yb_ref.at[cur], yb_ref.at[cur], sem.at[cur, 1]).wait()
    gm = gm_ref[...]
    y = _rms(h_ref[...] + (gm[:, 0:1] * _tiles_to_rows(ya_ref.at[cur], tm)
                           + gm[:, 1:2] * _tiles_to_rows(yb_ref.at[cur], tm)), g_ref[...])

    @pl.when(i < head_tiles)
    def _():
        oa_ref[...] = y

    @pl.when(i >= head_tiles)
    def _():
        ob_ref[...] = y


def _gather_norm(h, gates, slot1, slot2, ys, gain, split):
    t, d = h.shape
    tm = _pick(split, ROW_TILES)
    assert t % tm == 0
    nt = t // tm
    head_tiles = split // tm
    smem = pl.BlockSpec((1, 1, tm), lambda i: (i, 0, 0), memory_space=pltpu.SMEM)
    smem_next = pl.BlockSpec((1, 1, tm), lambda i: (jnp.minimum(i + 1, nt - 1), 0, 0), memory_space=pltpu.SMEM)
    s1, s2 = slot1.reshape(nt, 1, tm), slot2.reshape(nt, 1, tm)
    return pl.pallas_call(
        functools.partial(_gather_norm_body, tm=tm, head_tiles=head_tiles, n_steps=nt),
        out_shape=[jax.ShapeDtypeStruct((split, d), F32), jax.ShapeDtypeStruct((t - split, d), F32)],
        grid=(nt,),
        in_specs=[smem, smem, smem_next, smem_next, pl.BlockSpec((tm, d), lambda i: (i, 0)),
                  pl.BlockSpec((tm, LANES), lambda i: (i, 0)),
                  pl.BlockSpec((1, d), lambda i: (0, 0)),
                  pl.BlockSpec(memory_space=pl.ANY)],
        out_specs=[pl.BlockSpec((tm, d), lambda i: (jnp.minimum(i, head_tiles - 1), 0)),
                   pl.BlockSpec((tm, d), lambda i: (jnp.maximum(i - head_tiles, 0), 0))],
        scratch_shapes=[pltpu.VMEM((2, tm * SUBLANES, LANES), F32), pltpu.VMEM((2, tm * SUBLANES, LANES), F32),
                        pltpu.SemaphoreType.DMA((2, 2))],
        compiler_params=_params("arbitrary"),
        name="moe_gather_norm",
    )(s1, s2, s1, s2, h, gates, gain.reshape(1, -1), ys)


def _moe(h, gain, w_router, w_gate, w_up, w_down, final_gain, split):
    t, _ = h.shape
    n_exp = w_router.shape[1]
    tm_e = 512
    xn, imeta, gates, counts = _router(h, gain, w_router)
    idx1, idx2, rank1, rank2 = imeta[0], imeta[1], imeta[2], imeta[3]
    cnt = counts[0, :n_exp]
    padded = ((cnt + tm_e - 1) // tm_e) * tm_e
    ends = jnp.cumsum(padded)
    starts = ends - padded
    slot1 = starts[idx1] + rank1
    slot2 = starts[idx2] + rank2
    n_tiles = (TOP_K * t + n_exp * (tm_e - 1)) // tm_e
    n_used = (ends[-1] // tm_e).astype(I32)
    tile_start = jnp.arange(n_tiles, dtype=I32) * tm_e
    tile_expert = jnp.sum((tile_start[:, None] >= ends[None, :]).astype(I32), axis=1)
    tile_expert = jnp.minimum(tile_expert, n_exp - 1)
    assert h.shape[1] == SUBLANES * LANES
    n_slots = n_tiles * tm_e
    xs = _dispatch(xn.reshape(t, SUBLANES, LANES), slot1, slot2, n_slots)
    ys = _experts(xs.reshape(n_slots * SUBLANES, LANES), tile_expert, n_used.reshape(1), w_gate, w_up, w_down, tm_e)
    return _gather_norm(h, gates, slot1, slot2, ys.reshape(n_slots, SUBLANES, LANES), final_gain, split)


def _cache_to_slabs(cache):
    n, w, two, heads, hd = cache.shape
    return jnp.transpose(cache, (0, 2, 3, 4, 1)).reshape(n, two, heads * hd, w)


def _slabs_to_cache(slabs, heads):
    n, two, _, w = slabs.shape
    return jnp.transpose(slabs.reshape(n, two, heads, HEAD_DIM, w), (0, 4, 1, 2, 3))


def kernel(x_prompt, x_sample, cache_conv, cache_swa_kv, cache_dil_kv0, cache_dil_kv1, cache_dil_kv2, norm_mix0, w_in0, conv_w, swa_sink, w_out0, norm_ffn0, w_gate0, w_up0, w_down0, norm_mix1, w_in1, w_out1, norm_ffn1, w_router, w_gate1, w_up1, w_down1, norm_final):
    n_p, seq, d = x_prompt.shape
    n_s, s_len, _ = x_sample.shape
    tp, ts = n_p * seq, n_s * s_len
    d_conv = conv_w.shape[2]
    kvh, grp = swa_sink.shape[1], swa_sink.shape[2]
    hq0 = kvh * grp
    h1 = cache_dil_kv0.shape[4]
    c_q0, c_kv0 = hq0 * HEAD_DIM, kvh * HEAD_DIM
    q0_col = 3 * d_conv
    k0_col = q0_col + c_q0
    c_g = 3 * h1 * HEAD_DIM
    dil_caches = (cache_dil_kv0, cache_dil_kv1, cache_dil_kv2)

    x_rows = (x_prompt.reshape(tp, d), x_sample.reshape(ts, d))

    assert c_kv0 == LANES and grp % 2 == 0 and d_conv % LANES == 0
    nsc = d_conv // LANES
    q0_slab = 3 * nsc
    k0_slab = q0_slab + c_q0 // LANES
    proj0 = _dense([x_rows], w_in0[0], gain=norm_mix0[0], out_dtype=F32, name="in_proj0", slab_out=True)
    a_p, conv_tail = _conv_prompt(proj0, conv_w[0], n_seq=n_p, seq_len=seq, c=d_conv)
    att_p = _band_attn(proj0, n_seq=n_p, seq_len=seq, dil=1, q_slab=q0_slab, k_slab=k0_slab, v_slab=k0_slab + 1,
                       n_q_slabs=c_q0 // LANES, shared_kv=True, sink=swa_sink[0], want_lse=False,
                       name="swa_prompt")[0]
    proj0_s = proj0[:, tp:]
    p3 = proj0_s[:q0_slab].reshape(3, nsc, n_s, s_len, LANES)
    p3 = jnp.transpose(p3, (0, 3, 2, 1, 4)).reshape(3, s_len, n_s, d_conv)
    a_s, conv_new = _conv_step(p3, jnp.transpose(cache_conv[0], (1, 0, 2)), conv_w[0])
    a_s = jnp.transpose(a_s, (1, 0, 2)).reshape(ts, d_conv)
    kvt0 = _proj_t(jnp.transpose(w_in0[0][:, k0_col:]), x_rows[1], norm_mix0[0], "kv_t0")
    q_s = proj0_s[q0_slab:k0_slab].reshape(kvh, grp // 2, n_s, s_len, 2, HEAD_DIM)
    q_s = jnp.transpose(q_s, (2, 3, 1, 4, 0, 5)).reshape(n_s, s_len * grp, c_kv0)
    sink_col = jnp.broadcast_to(swa_sink[0][:, None, :], (kvh, s_len, grp)).reshape(kvh * s_len * grp, 1)
    o_s, swa_new = _step_attn(q_s, kvt0.reshape(2, c_kv0, ts), _cache_to_slabs(cache_swa_kv[0]), dil=1,
                              tok_div=grp, sink_col=sink_col, want_lse=False, name="swa_step")
    att_s = jnp.transpose(o_s.reshape(n_s, s_len, grp, kvh, HEAD_DIM), (0, 1, 3, 2, 4)).reshape(ts, c_q0)
    h = _dense([(a_p, a_s.astype(BF16)), (att_p, att_s.astype(BF16))], w_out0[0], res=x_rows, out_dtype=F32,
               name="out_proj0")
    h = _ffn(h, norm_ffn0[0], w_gate0[0], w_up0[0], w_down0[0], "ffn0")

    proj1 = _dense([h], w_in1[0], gain=norm_mix1[0], out_dtype=F32, name="in_proj1", slab_out=True)
    proj1_s = proj1[:, tp:]
    c_h = h1 * HEAD_DIM
    nsh = c_h // LANES
    pairs_p, pairs_s, dil_p, dil_s = [], [], [], []
    for g, dil in enumerate(DILATIONS):
        window = dil_caches[g].shape[2]
        s0 = g * 3 * nsh
        pairs_p.append(tuple(_band_attn(proj1, n_seq=n_p, seq_len=seq, dil=dil, q_slab=s0, k_slab=s0 + nsh,
                                        v_slab=s0 + 2 * nsh, n_q_slabs=nsh, want_lse=True,
                                        name=f"dil{g}_prompt")))
        keep = min(window, seq)
        state = _state_slabs(proj1, first_slab=s0 + nsh, n_slabs=2 * nsh, n_seq=n_p, seq_len=seq, keep=keep,
                             name=f"dil{g}_state")
        dil_p.append(_slabs_to_cache(state.reshape(n_p, 2, c_h, keep), h1)[None])
        lo = g * c_g
        q_g = jnp.transpose(proj1_s[s0:s0 + nsh].reshape(nsh, n_s, s_len, LANES), (1, 2, 0, 3))
        q_g = q_g.reshape(n_s, s_len, c_h)
        q_g = jnp.concatenate([q_g, jnp.zeros((n_s, SUBLANES - s_len, c_h), F32)], axis=1)
        kvt = _proj_t(jnp.transpose(w_in1[0][:, lo + c_h:lo + c_g]), h[tp:], norm_mix1[0], f"kv_t1_{g}")
        o_sg, l_sg, cache_new = _step_attn(q_g, kvt.reshape(2, c_h, ts), _cache_to_slabs(dil_caches[g][0]),
                                           dil=dil, tok_div=1, want_lse=True, name=f"dil{g}_step")
        pairs_s.append((o_sg.reshape(n_s * SUBLANES, c_h), l_sg.reshape(n_s * SUBLANES, c_h)))
        dil_s.append(_slabs_to_cache(cache_new, h1)[None])
    comb_s = _combine(pairs_s, "combine_step").reshape(n_s, SUBLANES, c_h)[:, :s_len].reshape(ts, c_h)
    h = _combine_proj(pairs_p, comb_s, w_out1[0], h, "out_proj1")
    y_p, y_s = _moe(h, norm_ffn1[0], w_router[0], w_gate1[0], w_up1[0], w_down1[0], norm_final, tp)

    y_prompt = y_p.reshape(n_p, seq, d)
    y_sample = y_s.reshape(n_s, s_len, d)
    new_conv_prompt = conv_tail[:, SUBLANES - (CONV_W - 1):][None]
    new_conv_sample = jnp.transpose(conv_new, (1, 0, 2))[None]
    keep0 = min(SPAN, seq)
    swa_state = _state_slabs(proj0, first_slab=k0_slab, n_slabs=2, n_seq=n_p, seq_len=seq, keep=keep0,
                             name="swa_state")
    new_swa_kv_prompt = _slabs_to_cache(swa_state, kvh)[None]
    new_swa_kv_sample = _slabs_to_cache(swa_new, kvh)[None]
    return (y_prompt, y_sample, new_conv_prompt, new_conv_sample, new_swa_kv_prompt, new_swa_kv_sample,
            dil_p[0], dil_s[0], dil_p[1], dil_s[1], dil_p[2], dil_s[2])
```
